```python
import jax, jax.numpy as jnp
from jax import lax
import numpy as np

D_MODEL = 1024
BATCH = 1
SEQ = 16384
DEPTH = 1

GRID_W = 64
D_CONV = 512
CONV_K = 31
NA_HEADS = 8
NA_HEAD_DIM = 64
D_NA = NA_HEADS * NA_HEAD_DIM
D_IN = 2 * D_CONV + 3 * D_NA
D_MIX = D_CONV + D_NA
WIN_H = 8
WIN_W = 16
QB = WIN_W
KBW = 2 * WIN_W
N_COL_BLOCKS = GRID_W // QB
N_EXPERTS = 32
TOP_K = 4
D_FF = D_MODEL
SWIGLU_LIMIT = 7.0
SWIGLU_ALPHA = 1.702
MOE_BLOCK = 128
EPS = 1e-5

kernel_name = "hybrid_conformer_conv_natten2d_moe_encoder"


def rmsnorm(x, g):
    xf = x.astype(jnp.float32)
    y = xf * lax.rsqrt(jnp.mean(xf * xf, axis=-1, keepdims=True) + EPS) * g.astype(jnp.float32)
    return y.astype(x.dtype)


def conformer_conv(a, g, conv_w, conv_b, ln_g, ln_b):
    u = a * jax.nn.sigmoid(g)
    u = lax.conv_general_dilated(
        u, conv_w[:, None, :].astype(u.dtype), window_strides=(1,),
        padding=[(CONV_K // 2, CONV_K // 2)],
        dimension_numbers=("NWC", "WIO", "NWC"),
        feature_group_count=D_CONV) + conv_b
    uf = u.astype(jnp.float32)
    mu = jnp.mean(uf, axis=-1, keepdims=True)
    var = jnp.mean(jnp.square(uf - mu), axis=-1, keepdims=True)
    un = (uf - mu) * lax.rsqrt(var + EPS) * ln_g.astype(jnp.float32) + ln_b.astype(jnp.float32)
    return jax.nn.silu(un).astype(a.dtype)


def neighbourhood_attention(q, k, v, rpb):
    B, S, H, Dh = q.shape
    rows = S // GRID_W
    kh = min(WIN_H, rows)
    scale = Dh ** -0.5
    qg = q.reshape(B, rows, N_COL_BLOCKS, QB, H, Dh)
    kg = k.reshape(B, rows, GRID_W, H, Dh)
    vg = v.reshape(B, rows, GRID_W, H, Dh)

    col = jnp.arange(GRID_W, dtype=jnp.int32).reshape(N_COL_BLOCKS, QB)
    c0 = jnp.clip(col - WIN_W // 2, 0, GRID_W - WIN_W)
    blk_start = jnp.clip(jnp.arange(N_COL_BLOCKS, dtype=jnp.int32) * QB - WIN_W // 2, 0, GRID_W - KBW)
    key_col = blk_start[:, None] + jnp.arange(KBW, dtype=jnp.int32)[None, :]
    col_valid = (key_col[:, None, :] >= c0[:, :, None]) & (key_col[:, None, :] < c0[:, :, None] + WIN_W)
    dcol_idx = jnp.clip(key_col[:, None, :] - col[:, :, None], -(WIN_W - 1), WIN_W - 1) + (WIN_W - 1)
    valid = col_valid[None, :, None, :, None, :]

    def row_fn(args):
        i, q_row = args
        r0 = jnp.clip(i - kh // 2, 0, rows - kh)
        k_slab = lax.dynamic_slice_in_dim(kg, r0, kh, axis=1)
        v_slab = lax.dynamic_slice_in_dim(vg, r0, kh, axis=1)
        k_blk = k_slab[:, :, key_col]
        v_blk = v_slab[:, :, key_col]
        s = jnp.einsum("bnqhd,brnkhd->bnhqrk", q_row, k_blk,
                       preferred_element_type=jnp.float32) * scale
        drow_idx = r0 + jnp.arange(kh, dtype=jnp.int32) - i + (WIN_H - 1)
        bias = rpb[:, drow_idx[None, None, :, None], dcol_idx[:, :, None, :]]
        s = s + jnp.moveaxis(bias, 0, 1)[None].astype(jnp.float32)
        s = jnp.where(valid, s, jnp.float32(-1e30))
        p = jax.nn.softmax(s.reshape(B, N_COL_BLOCKS, H, QB, kh * KBW), axis=-1)
        p = p.reshape(B, N_COL_BLOCKS, H, QB, kh, KBW).astype(v.dtype)
        return jnp.einsum("bnhqrk,brnkhd->bnqhd", p, v_blk)

    out = lax.map(row_fn, (jnp.arange(rows, dtype=jnp.int32), jnp.moveaxis(qg, 1, 0)))
    return jnp.moveaxis(out, 0, 1).reshape(B, S, H * Dh)


def mixer(xn, w_in, b_in, conv_w, conv_b, ln_g, ln_b, rpb, w_out, b_out):
    B, S, _ = xn.shape
    proj = jnp.einsum("bsd,de->bse", xn, w_in) + b_in
    glu_a, glu_g, q, k, v = jnp.split(
        proj, [D_CONV, 2 * D_CONV, 2 * D_CONV + D_NA, 2 * D_CONV + 2 * D_NA], axis=-1)
    conv_out = conformer_conv(glu_a, glu_g, conv_w, conv_b, ln_g, ln_b)
    na_out = neighbourhood_attention(q.reshape(B, S, NA_HEADS, NA_HEAD_DIM),
                                     k.reshape(B, S, NA_HEADS, NA_HEAD_DIM),
                                     v.reshape(B, S, NA_HEADS, NA_HEAD_DIM), rpb)
    mixed = jnp.concatenate([conv_out, na_out], axis=-1)
    return jnp.einsum("bse,ed->bsd", mixed, w_out) + b_out


def moe(xn, w_router, b_router, w_gate, b_gate, w_up, b_up, w_down, b_down):
    B, S, D = xn.shape
    T = B * S
    xt = xn.reshape(T, D)
    logits = (xt @ w_router + b_router).astype(jnp.float32)
    top_vals, top_idx = lax.top_k(logits, TOP_K)
    gates = jax.nn.softmax(top_vals, axis=-1)
    tk = T * TOP_K
    exp_flat = top_idx.reshape(tk)
    tok_flat = jnp.arange(tk, dtype=jnp.int32) // TOP_K
    gate_flat = gates.reshape(tk)
    order = jnp.argsort(exp_flat)
    s_exp, s_tok, s_gate = exp_flat[order], tok_flat[order], gate_flat[order]
    sizes = jnp.bincount(exp_flat, length=N_EXPERTS).astype(jnp.int32)
    offsets = jnp.cumsum(sizes) - sizes
    padded = (sizes + MOE_BLOCK - 1) // MOE_BLOCK * MOE_BLOCK
    pad_ends = jnp.cumsum(padded)
    pad_offsets = pad_ends - padded
    dest = pad_offsets[s_exp] + jnp.arange(tk, dtype=jnp.int32) - offsets[s_exp]
    cap = tk + N_EXPERTS * MOE_BLOCK
    n_blocks = cap // MOE_BLOCK
    tok_pad = jnp.zeros((cap,), jnp.int32).at[dest].set(s_tok)
    gate_pad = jnp.zeros((cap,), jnp.float32).at[dest].set(s_gate)
    blk_exp = jnp.minimum(
        jnp.searchsorted(pad_ends, jnp.arange(n_blocks, dtype=jnp.int32) * MOE_BLOCK, side="right"),
        N_EXPERTS - 1).astype(jnp.int32)
    x_blocks = xt[tok_pad].reshape(n_blocks, MOE_BLOCK, D)

    def expert_block(args):
        xb, e = args
        gt = jnp.minimum(xb @ w_gate[e] + b_gate[e], SWIGLU_LIMIT)
        up = jnp.clip(xb @ w_up[e] + b_up[e], -SWIGLU_LIMIT, SWIGLU_LIMIT)
        hdn = (up + 1.0) * (gt * jax.nn.sigmoid(SWIGLU_ALPHA * gt))
        return hdn @ w_down[e] + b_down[e]

    y = lax.map(expert_block, (x_blocks, blk_exp)).reshape(cap, D)
    y = y * gate_pad[:, None].astype(y.dtype)
    out = jax.ops.segment_sum(y, tok_pad, num_segments=T)
    return out.reshape(B, S, D)


def setup_inputs(seed: int = 0) -> dict:
    key = jax.random.key(seed)
    ks = jax.random.split(key, 22)

    def nrm(k, shape, scale):
        return jax.random.normal(k, shape, jnp.float32) * scale

    return {
        "x": nrm(ks[0], (BATCH, SEQ, D_MODEL), 1.0),
        "g_mix": 1.0 + nrm(ks[1], (DEPTH, D_MODEL), 0.02),
        "w_in": nrm(ks[2], (DEPTH, D_MODEL, D_IN), D_MODEL ** -0.5),
        "b_in": nrm(ks[3], (DEPTH, D_IN), 0.01),
        "conv_w": nrm(ks[4], (DEPTH, CONV_K, D_CONV), CONV_K ** -0.5),
        "conv_b": nrm(ks[5], (DEPTH, D_CONV), 0.01),
        "ln_g": 1.0 + nrm(ks[6], (DEPTH, D_CONV), 0.02),
        "ln_b": nrm(ks[7], (DEPTH, D_CONV), 0.01),
        "rpb": nrm(ks[8], (DEPTH, NA_HEADS, 2 * WIN_H - 1, 2 * WIN_W - 1), 0.02),
        "w_out": nrm(ks[9], (DEPTH, D_MIX, D_MODEL), D_MIX ** -0.5),
        "b_out": nrm(ks[10], (DEPTH, D_MODEL), 0.01),
        "g_ffn": 1.0 + nrm(ks[11], (DEPTH, D_MODEL), 0.02),
        "w_router": nrm(ks[12], (DEPTH, D_MODEL, N_EXPERTS), D_MODEL ** -0.5),
        "b_router": nrm(ks[13], (DEPTH, N_EXPERTS), 0.01),
        "w_gate": nrm(ks[14], (DEPTH, N_EXPERTS, D_MODEL, D_FF), D_MODEL ** -0.5),
        "b_gate": nrm(ks[15], (DEPTH, N_EXPERTS, D_FF), 0.01),
        "w_up": nrm(ks[16], (DEPTH, N_EXPERTS, D_MODEL, D_FF), D_MODEL ** -0.5),
        "b_up": nrm(ks[17], (DEPTH, N_EXPERTS, D_FF), 0.01),
        "w_down": nrm(ks[18], (DEPTH, N_EXPERTS, D_FF, D_MODEL), D_FF ** -0.5),
        "b_down": nrm(ks[19], (DEPTH, N_EXPERTS, D_MODEL), 0.01),
        "g_final": 1.0 + nrm(ks[20], (D_MODEL,), 0.02),
    }


def reference(x, g_mix, w_in, b_in, conv_w, conv_b, ln_g, ln_b, rpb, w_out, b_out,
              g_ffn, w_router, b_router, w_gate, b_gate, w_up, b_up, w_down, b_down, g_final):
    h = x
    for l in range(DEPTH):
        xn = rmsnorm(h, g_mix[l])
        h = h + mixer(xn, w_in[l], b_in[l], conv_w[l], conv_b[l], ln_g[l], ln_b[l],
                      rpb[l], w_out[l], b_out[l])
        xn = rmsnorm(h, g_ffn[l])
        h = h + moe(xn, w_router[l], b_router[l], w_gate[l], b_gate[l], w_up[l], b_up[l],
                    w_down[l], b_down[l])
    return rmsnorm(h, g_final)
```

```python
import functools

import jax
import jax.numpy as jnp
from jax import lax
from jax.experimental import pallas as pl
from jax.experimental.pallas import tpu as pltpu

F32 = jnp.float32
BF16 = jnp.bfloat16

D_MODEL = 1024
GRID_W = 64
D_CONV = 512
CONV_K = 31
NA_HEADS = 8
NA_HEAD_DIM = 64
D_NA = NA_HEADS * NA_HEAD_DIM
D_IN = 2 * D_CONV + 3 * D_NA
WIN_H = 8
WIN_W = 16
N_EXPERTS = 32
TOP_K = 4
D_FF = D_MODEL
SWIGLU_LIMIT = 7.0
SWIGLU_ALPHA = 1.702
EPS = 1e-5
MASK_VALUE = -1e30

V7X_VMEM_BYTES = 64 * 1024 * 1024
LANES = 128

PROJ_TM = 512
CONV_TT = 256
CONV_HALO = 16
CONV_CH = 32
ATTN_RB = 16
EXP_BM = 256
COMB_TM = 256


def _vmem_limit(nbytes):
    return int(min(nbytes, V7X_VMEM_BYTES - 6 * 1024 * 1024))


def _inproj_kernel(x_ref, g_ref, w_ref, b_ref, u_ref, q_ref, k_ref, v_ref):
    x = x_ref[...]
    ms = jnp.mean(x * x, axis=-1, keepdims=True)
    xn = (x * lax.rsqrt(ms + EPS) * g_ref[...]).astype(BF16)

    def proj(c):
        sl = slice(c * D_CONV, (c + 1) * D_CONV)
        return jnp.dot(xn, w_ref[:, sl], preferred_element_type=F32) + b_ref[:, sl]

    u_ref[...] = proj(0) * jax.nn.sigmoid(proj(1))
    q_ref[...] = (proj(2) * (NA_HEAD_DIM ** -0.5)).astype(BF16)
    k_ref[...] = proj(3).astype(BF16)
    v_ref[...] = proj(4).astype(BF16)


def _inproj(x2, g_mix, w_in, b_in):
    t = x2.shape[0]
    tm = min(PROJ_TM, t)
    tok = lambda i: (i, 0)
    const = lambda i: (0, 0)
    return pl.pallas_call(
        _inproj_kernel,
        grid=(t // tm,),
        in_specs=[
            pl.BlockSpec((tm, D_MODEL), tok),
            pl.BlockSpec((1, D_MODEL), const),
            pl.BlockSpec((D_MODEL, D_IN), const),
            pl.BlockSpec((1, D_IN), const),
        ],
        out_specs=[
            pl.BlockSpec((tm, D_CONV), tok),
            pl.BlockSpec((tm, D_NA), tok),
            pl.BlockSpec((tm, D_NA), tok),
            pl.BlockSpec((tm, D_NA), tok),
        ],
        out_shape=[
            jax.ShapeDtypeStruct((t, D_CONV), F32),
            jax.ShapeDtypeStruct((t, D_NA), BF16),
            jax.ShapeDtypeStruct((t, D_NA), BF16),
            jax.ShapeDtypeStruct((t, D_NA), BF16),
        ],
        compiler_params=pltpu.CompilerParams(
            dimension_semantics=("arbitrary",), vmem_limit_bytes=_vmem_limit(48 << 20)),
        name="inproj",
    )(x2, g_mix.reshape(1, D_MODEL), w_in.astype(BF16), b_in.reshape(1, D_IN))


def _conv_kernel(prev_ref, cur_ref, next_ref, w_ref, cb_ref, lg_ref, lb_ref, o_ref, ext_ref):
    i = pl.program_id(0)
    n = pl.num_programs(0)
    tt = cur_ref.shape[0]
    ext_ref[0:CONV_HALO, :] = jnp.where(i > 0, prev_ref[...], 0.0)
    ext_ref[CONV_HALO:CONV_HALO + tt, :] = cur_ref[...]
    ext_ref[CONV_HALO + tt:2 * CONV_HALO + tt, :] = jnp.where(i < n - 1, next_ref[...], 0.0)
    base = CONV_HALO - CONV_K // 2
    for c in range(tt // CONV_CH):
        acc = jnp.zeros((CONV_CH, D_CONV), F32)
        for j in range(CONV_K):
            r = c * CONV_CH + base + j
            acc = acc + ext_ref[r:r + CONV_CH, :] * w_ref[j:j + 1, :]
        acc = acc + cb_ref[...]
        mu = jnp.mean(acc, axis=-1, keepdims=True)
        d = acc - mu
        var = jnp.mean(d * d, axis=-1, keepdims=True)
        un = d * lax.rsqrt(var + EPS) * lg_ref[...] + lb_ref[...]
        o_ref[c * CONV_CH:(c + 1) * CONV_CH, :] = (un * jax.nn.sigmoid(un)).astype(o_ref.dtype)


def _conv(u, conv_w, conv_b, ln_g, ln_b):
    t = u.shape[0]
    tt = min(CONV_TT, t)
    hb = tt // CONV_HALO
    nhb = t // CONV_HALO
    const = lambda i: (0, 0)
    w = jnp.zeros((CONV_K + 1, D_CONV), F32).at[:CONV_K].set(conv_w)
    return pl.pallas_call(
        _conv_kernel,
        grid=(t // tt,),
        in_specs=[
            pl.BlockSpec((CONV_HALO, D_CONV), lambda i: (jnp.maximum(i * hb - 1, 0), 0)),
            pl.BlockSpec((tt, D_CONV), lambda i: (i, 0)),
            pl.BlockSpec((CONV_HALO, D_CONV), lambda i: (jnp.minimum((i + 1) * hb, nhb - 1), 0)),
            pl.BlockSpec((CONV_K + 1, D_CONV), const),
            pl.BlockSpec((1, D_CONV), const),
            pl.BlockSpec((1, D_CONV), const),
            pl.BlockSpec((1, D_CONV), const),
        ],
        out_specs=pl.BlockSpec((tt, D_CONV), lambda i: (i, 0)),
        out_shape=jax.ShapeDtypeStruct((t, D_CONV), BF16),
        scratch_shapes=[pltpu.VMEM((tt + 2 * CONV_HALO, D_CONV), F32)],
        compiler_params=pltpu.CompilerParams(dimension_semantics=("arbitrary",)),
        name="conv",
    )(u, u, u, w, conv_b.reshape(1, D_CONV), ln_g.reshape(1, D_CONV), ln_b.reshape(1, D_CONV))


def _bias_table(rpb):
    col = jnp.arange(GRID_W, dtype=jnp.int32)
    c0 = jnp.clip(col - WIN_W // 2, 0, GRID_W - WIN_W)
    kc = col
    valid = (kc[None, :] >= c0[:, None]) & (kc[None, :] < c0[:, None] + WIN_W)
    dcol = jnp.clip(kc[None, :] - col[:, None], -(WIN_W - 1), WIN_W - 1) + (WIN_W - 1)
    cls = jnp.arange(WIN_H, dtype=jnp.int32)
    rr = jnp.arange(WIN_H, dtype=jnp.int32)
    drow = rr[None, :] - cls[:, None] + (WIN_H - 1)
    b = rpb[:, drow[:, :, None, None], dcol[None, None, :, :]]
    b = jnp.where(valid[None, None, None], b.astype(F32), MASK_VALUE)
    b = jnp.transpose(b, (0, 1, 3, 2, 4))
    return b.reshape(NA_HEADS, WIN_H, GRID_W, WIN_H * GRID_W)


def _attn_kernel(q_ref, k_ref, v_ref, bt_ref, o_ref, *, rb, rows):
    jb = pl.program_id(1)
    lane = lax.broadcasted_iota(jnp.int32, (GRID_W, LANES), 1)
    first_head = lane < NA_HEAD_DIM
    nkeys = WIN_H * GRID_W

    def row(r, carry):
        i = jb * rb + r
        r0 = jnp.clip(i - WIN_H // 2, 0, rows - WIN_H)
        c = i - r0
        q2 = q_ref[pl.ds(pl.multiple_of(r * GRID_W, GRID_W), GRID_W), :]
        zero = jnp.zeros_like(q2)
        qst = jnp.concatenate([jnp.where(first_head, q2, zero), jnp.where(first_head, zero, q2)], axis=0)
        start = pl.multiple_of(r0 * GRID_W, GRID_W)
        ks = k_ref[pl.ds(start, nkeys), :]
        vs = v_ref[pl.ds(start, nkeys), :]
        s = lax.dot_general(qst, ks, (((1,), (1,)), ((), ())), preferred_element_type=F32)
        bias = jnp.concatenate([bt_ref[0, c], bt_ref[1, c]], axis=0)
        s = jnp.where(bias > 0.5 * MASK_VALUE, s + bias, MASK_VALUE)
        m = jnp.max(s, axis=-1, keepdims=True)
        p = jnp.exp(s - m)
        l = jnp.sum(p, axis=-1, keepdims=True)
        o = jnp.dot(p.astype(BF16), vs, preferred_element_type=F32) / l
        o2 = jnp.where(first_head, o[:GRID_W], o[GRID_W:])
        o_ref[pl.ds(pl.multiple_of(r * GRID_W, GRID_W), GRID_W), :] = o2.astype(o_ref.dtype)
        return carry

    lax.fori_loop(0, rb, row, 0)


def _attention(q, k, v, rpb):
    t = q.shape[0]
    rows = t // GRID_W
    rb = min(ATTN_RB, rows)
    npairs = D_NA // LANES
    bt = _bias_table(rpb)
    kern = functools.partial(_attn_kernel, rb=rb, rows=rows)
    return pl.pallas_call(
        kern,
        grid=(npairs, rows // rb),
        in_specs=[
            pl.BlockSpec((rb * GRID_W, LANES), lambda p, j: (j, p)),
            pl.BlockSpec((t, LANES), lambda p, j: (0, p)),
            pl.BlockSpec((t, LANES), lambda p, j: (0, p)),
            pl.BlockSpec((2, WIN_H, GRID_W, WIN_H * GRID_W), lambda p, j: (p, 0, 0, 0)),
        ],
        out_specs=pl.BlockSpec((rb * GRID_W, LANES), lambda p, j: (j, p)),
        out_shape=jax.ShapeDtypeStruct((t, D_NA), BF16),
        compiler_params=pltpu.CompilerParams(
            dimension_semantics=("arbitrary", "arbitrary"), vmem_limit_bytes=_vmem_limit(40 << 20)),
        name="attention",
    )(q, k, v, bt)


def _outproj_kernel(conv_ref, na_ref, x_ref, wo_ref, bo_ref, g_ref, wr_ref, br_ref,
                    h_ref, xn_ref, idx_ref, gate_ref):
    mixed = jnp.concatenate([conv_ref[...], na_ref[...]], axis=-1)
    h = x_ref[...] + jnp.dot(mixed, wo_ref[...], preferred_element_type=F32) + bo_ref[...]
    h_ref[...] = h
    ms = jnp.mean(h * h, axis=-1, keepdims=True)
    xn = h * lax.rsqrt(ms + EPS) * g_ref[...]
    xn_ref[...] = xn
    logits = lax.dot_general(wr_ref[...], xn, (((1,), (1,)), ((), ())),
                             preferred_element_type=F32, precision=lax.Precision.HIGHEST) + br_ref[...]
    ids = lax.broadcasted_iota(jnp.int32, logits.shape, 0)
    vals, sels = [], []
    l = logits
    for _ in range(TOP_K):
        m = jnp.max(l, axis=0, keepdims=True)
        sel = jnp.min(jnp.where(l == m, ids, N_EXPERTS), axis=0, keepdims=True)
        vals.append(m)
        sels.append(sel)
        l = jnp.where(ids == sel, -jnp.inf, l)
    es = [jnp.exp(vk - vals[0]) for vk in vals]
    tot = es[0] + es[1] + es[2] + es[3]
    idx_ref[...] = jnp.concatenate(sels, axis=0)
    gate_ref[...] = jnp.concatenate([e / tot for e in es], axis=0)


def _outproj_router(conv_out, na_out, x2, w_out, b_out, g_ffn, w_router, b_router):
    t = x2.shape[0]
    tm = min(PROJ_TM, t)
    tok = lambda i: (i, 0)
    const = lambda i: (0, 0)
    return pl.pallas_call(
        _outproj_kernel,
        grid=(t // tm,),
        in_specs=[
            pl.BlockSpec((tm, D_CONV), tok),
            pl.BlockSpec((tm, D_NA), tok),
            pl.BlockSpec((tm, D_MODEL), tok),
            pl.BlockSpec((D_MODEL, D_MODEL), const),
            pl.BlockSpec((1, D_MODEL), const),
            pl.BlockSpec((1, D_MODEL), const),
            pl.BlockSpec((N_EXPERTS, D_MODEL), const),
            pl.BlockSpec((N_EXPERTS, 1), const),
        ],
        out_specs=[
            pl.BlockSpec((tm, D_MODEL), tok),
            pl.BlockSpec((tm, D_MODEL), tok),
            pl.BlockSpec((TOP_K, tm), lambda i: (0, i)),
            pl.BlockSpec((TOP_K, tm), lambda i: (0, i)),
        ],
        out_shape=[
            jax.ShapeDtypeStruct((t, D_MODEL), F32),
            jax.ShapeDtypeStruct((t, D_MODEL), F32),
            jax.ShapeDtypeStruct((TOP_K, t), jnp.int32),
            jax.ShapeDtypeStruct((TOP_K, t), F32),
        ],
        compiler_params=pltpu.CompilerParams(
            dimension_semantics=("arbitrary",), vmem_limit_bytes=_vmem_limit(48 << 20)),
        name="outproj_router",
    )(conv_out, na_out, x2, w_out.astype(BF16), b_out.reshape(1, D_MODEL), g_ffn.reshape(1, D_MODEL),
      w_router.T, b_router.reshape(N_EXPERTS, 1))


def _route_plan(idx_t, t):
    tk = t * TOP_K
    exp_flat = idx_t.T.reshape(tk)
    order = jnp.argsort(exp_flat).astype(jnp.int32)
    s_exp = exp_flat[order]
    sizes = jnp.bincount(exp_flat, length=N_EXPERTS).astype(jnp.int32)
    offsets = jnp.cumsum(sizes) - sizes
    padded = (sizes + EXP_BM - 1) // EXP_BM * EXP_BM
    pad_ends = jnp.cumsum(padded)
    pad_offsets = pad_ends - padded
    dest = pad_offsets[s_exp] + jnp.arange(tk, dtype=jnp.int32) - offsets[s_exp]
    cap = tk + N_EXPERTS * EXP_BM
    nb = cap // EXP_BM
    tok = order // TOP_K
    slot = order % TOP_K
    tok_pad = jnp.zeros((cap,), jnp.int32).at[dest].set(tok)
    dst_pad = jnp.zeros((cap,), jnp.int32).at[dest].set(slot * t + tok)
    blk_start = jnp.arange(nb, dtype=jnp.int32) * EXP_BM
    blk_exp = jnp.minimum(jnp.searchsorted(pad_ends, blk_start, side="right"), N_EXPERTS - 1).astype(jnp.int32)
    in_use = blk_start < pad_ends[N_EXPERTS - 1]
    valid_end = pad_offsets[blk_exp] + sizes[blk_exp]
    blk_valid = jnp.where(in_use, jnp.clip(valid_end - blk_start, 0, EXP_BM), 0).astype(jnp.int32)
    return tok_pad.reshape(nb, 1, EXP_BM), dst_pad.reshape(nb, 1, EXP_BM), blk_exp, blk_valid


DMA_UNROLL = 8


def _for_rows(n, fn):
    groups = n // DMA_UNROLL

    def group(g, c):
        for j in range(DMA_UNROLL):
            fn(g * DMA_UNROLL + j)
        return c
    lax.fori_loop(0, groups, group, 0)

    def single(r, c):
        fn(r)
        return c
    lax.fori_loop(groups * DMA_UNROLL, n, single, 0)


def _expert_kernel(be_ref, nv_ref, tokc_ref, tokn_ref, dst_ref, xn_hbm, wg_ref, wu_ref, wd_ref,
                   bg_ref, bu_ref, bd_ref, out_hbm, xbuf, ybuf, wbf, gsem, ssem):
    b = pl.program_id(0)
    nb = pl.num_programs(0)
    slot = b % 2
    bm = xbuf.shape[1]

    def gather_copy(tok, r, s):
        return pltpu.make_async_copy(xn_hbm.at[pl.ds(tok, 1)], xbuf.at[s, pl.ds(r, 1)], gsem.at[s])

    def scatter_copy(dst, r, s):
        return pltpu.make_async_copy(ybuf.at[s, pl.ds(r, 1)], out_hbm.at[pl.ds(dst, 1)], ssem.at[s])

    def start_gather(tok_ref, s):
        def body(r, c):
            gather_copy(tok_ref[0, 0, r], r, s).start()
            return c
        lax.fori_loop(0, bm, body, 0, unroll=DMA_UNROLL)

    def wait_scatter(step, s):
        _for_rows(nv_ref[step], lambda r: scatter_copy(0, r, s).wait())

    @pl.when(b == 0)
    def _():
        start_gather(tokc_ref, 0)

    @pl.when(b + 1 < nb)
    def _():
        start_gather(tokn_ref, 1 - slot)

    @pl.when((b == 0) | (be_ref[b] != be_ref[jnp.maximum(b - 1, 0)]))
    def _():
        wbf[0] = wg_ref[0].astype(BF16)
        wbf[1] = wu_ref[0].astype(BF16)
        wbf[2] = wd_ref[0].astype(BF16)

    def wait_gather(r, c):
        gather_copy(0, r, slot).wait()
        return c
    lax.fori_loop(0, bm, wait_gather, 0, unroll=DMA_UNROLL)

    @pl.when(b >= 2)
    def _():
        wait_scatter(b - 2, slot)

    @pl.when(nv_ref[b] > 0)
    def _():
        x = xbuf[slot].astype(BF16)
        gt = jnp.minimum(jnp.dot(x, wbf[0], preferred_element_type=F32) + bg_ref[0], SWIGLU_LIMIT)
        up = jnp.clip(jnp.dot(x, wbf[1], preferred_element_type=F32) + bu_ref[0],
                      -SWIGLU_LIMIT, SWIGLU_LIMIT)
        hdn = (up + 1.0) * (gt * jax.nn.sigmoid(SWIGLU_ALPHA * gt))
        ybuf[slot] = jnp.dot(hdn.astype(BF16), wbf[2], preferred_element_type=F32) + bd_ref[0]
        _for_rows(nv_ref[b], lambda r: scatter_copy(dst_ref[0, 0, r], r, slot).start())

    @pl.when(b == nb - 1)
    def _():
        wait_scatter(b, slot)

        @pl.when(b >= 1)
        def _():
            wait_scatter(b - 1, 1 - slot)


def _experts(xn2, tok_pad, dst_pad, blk_exp, blk_valid, w_gate, b_gate, w_up, b_up, w_down, b_down):
    t = xn2.shape[0]
    nb = tok_pad.shape[0]
    blk = lambda b, be, nv: (b, 0, 0)
    nxt = lambda b, be, nv: (jnp.minimum(b + 1, nb - 1), 0, 0)
    wsel = lambda b, be, nv: (be[b], 0, 0)
    smem_blk = lambda im: pl.BlockSpec((1, 1, EXP_BM), im, memory_space=pltpu.SMEM)
    wspec = pl.BlockSpec((1, D_MODEL, D_FF), wsel)
    bspec = pl.BlockSpec((1, 1, D_FF), wsel)
    grid_spec = pltpu.PrefetchScalarGridSpec(
        num_scalar_prefetch=2,
        grid=(nb,),
        in_specs=[
            smem_blk(blk), smem_blk(nxt), smem_blk(blk),
            pl.BlockSpec(memory_space=pl.ANY),
            wspec, wspec, wspec, bspec, bspec, bspec,
        ],
        out_specs=pl.BlockSpec(memory_space=pl.ANY),
        scratch_shapes=[
            pltpu.VMEM((2, EXP_BM, D_MODEL), F32),
            pltpu.VMEM((2, EXP_BM, D_MODEL), F32),
            pltpu.VMEM((3, D_MODEL, D_FF), BF16),
            pltpu.SemaphoreType.DMA((2,)),
            pltpu.SemaphoreType.DMA((2,)),
        ],
    )
    return pl.pallas_call(
        _expert_kernel,
        grid_spec=grid_spec,
        out_shape=jax.ShapeDtypeStruct((TOP_K * t, D_MODEL), F32),
        compiler_params=pltpu.CompilerParams(
            dimension_semantics=("arbitrary",), vmem_limit_bytes=_vmem_limit(56 << 20)),
        name="experts",
    )(blk_exp, blk_valid, tok_pad, tok_pad, dst_pad, xn2, w_gate, w_up, w_down,
      b_gate.reshape(N_EXPERTS, 1, D_FF), b_up.reshape(N_EXPERTS, 1, D_FF),
      b_down.reshape(N_EXPERTS, 1, D_MODEL))


def _combine_kernel(h_ref, y0_ref, y1_ref, y2_ref, y3_ref, gate_ref, g_ref, o_ref):
    g = gate_ref[...]
    out = h_ref[...]
    for k, y_ref in enumerate((y0_ref, y1_ref, y2_ref, y3_ref)):
        out = out + g[:, k:k + 1] * y_ref[0]
    ms = jnp.mean(out * out, axis=-1, keepdims=True)
    o_ref[...] = out * lax.rsqrt(ms + EPS) * g_ref[...]


def _combine(h, y4, gates, g_final):
    t = h.shape[0]
    tm = min(COMB_TM, t)
    y3d = y4.reshape(TOP_K, t, D_MODEL)
    tok = lambda i: (i, 0)
    yspec = lambda k: pl.BlockSpec((1, tm, D_MODEL), lambda i: (k, i, 0))
    return pl.pallas_call(
        _combine_kernel,
        grid=(t // tm,),
        in_specs=[
            pl.BlockSpec((tm, D_MODEL), tok),
            yspec(0), yspec(1), yspec(2), yspec(3),
            pl.BlockSpec((tm, TOP_K), tok),
            pl.BlockSpec((1, D_MODEL), lambda i: (0, 0)),
        ],
        out_specs=pl.BlockSpec((tm, D_MODEL), tok),
        out_shape=jax.ShapeDtypeStruct((t, D_MODEL), F32),
        compiler_params=pltpu.CompilerParams(dimension_semantics=("arbitrary",)),
        name="combine",
    )(h, y3d, y3d, y3d, y3d, gates, g_final.reshape(1, D_MODEL))


def kernel(x, g_mix, w_in, b_in, conv_w, conv_b, ln_g, ln_b, rpb, w_out, b_out, g_ffn, w_router,
           b_router, w_gate, b_gate, w_up, b_up, w_down, b_down, g_final):
    bsz, seq, d = x.shape
    assert bsz == 1 and d == D_MODEL and g_mix.shape[0] == 1
    assert seq % (GRID_W * WIN_H) == 0
    x2 = x.reshape(seq, d)
    u, q, k, v = _inproj(x2, g_mix[0], w_in[0], b_in[0])
    conv_out = _conv(u, conv_w[0], conv_b[0], ln_g[0], ln_b[0])
    na_out = _attention(q, k, v, rpb[0])
    h, xn2, idx_t, gate_t = _outproj_router(conv_out, na_out, x2, w_out[0], b_out[0], g_ffn[0],
                                            w_router[0], b_router[0])
    tok_pad, dst_pad, blk_exp, blk_valid = _route_plan(idx_t, seq)
    y4 = _experts(xn2, tok_pad, dst_pad, blk_exp, blk_valid, w_gate[0], b_gate[0], w_up[0], b_up[0],
                  w_down[0], b_down[0])
    out = _combine(h, y4, gate_t.T, g_final)
    return out.reshape(bsz, seq, d)
```

```python
import functools

import numpy as np
import jax
import jax.numpy as jnp
from jax import lax
from jax.experimental import pallas as pl
from jax.experimental.pallas import tpu as pltpu

F32 = jnp.float32
BF16 = jnp.bfloat16

D_MODEL = 1024
GRID_W = 64
D_CONV = 512
CONV_K = 31
NA_HEADS = 8
NA_HEAD_DIM = 64
D_NA = NA_HEADS * NA_HEAD_DIM
D_IN = 2 * D_CONV + 3 * D_NA
WIN_H = 8
WIN_W = 16
N_EXPERTS = 32
TOP_K = 4
D_FF = D_MODEL
SWIGLU_LIMIT = 7.0
SWIGLU_ALPHA = 1.702
EPS = 1e-5
MASK_VALUE = -1e30

V7X_VMEM_BYTES = 64 * 1024 * 1024
LANES = 128

PROJ_TM = 512
CONV_TT = 256
CONV_HALO = 16
CONV_CH = 32
ATTN_RB = 16
EXP_BM = 256
COMB_TM = 256


def _vmem_limit(nbytes):
    return int(min(nbytes, V7X_VMEM_BYTES - 6 * 1024 * 1024))


def _inproj_kernel(x_ref, g_ref, w_ref, b_ref, u_ref, q_ref, k_ref, v_ref):
    x = x_ref[...]
    ms = jnp.mean(x * x, axis=-1, keepdims=True)
    xn = (x * lax.rsqrt(ms + EPS) * g_ref[...]).astype(BF16)

    def proj(c):
        sl = slice(c * D_CONV, (c + 1) * D_CONV)
        return jnp.dot(xn, w_ref[:, sl], preferred_element_type=F32) + b_ref[:, sl]

    u_ref[...] = proj(0) * jax.nn.sigmoid(proj(1))
    q_ref[...] = (proj(2) * (NA_HEAD_DIM ** -0.5)).astype(BF16)
    k_ref[...] = proj(3).astype(BF16)
    v_ref[...] = proj(4).astype(BF16)


def _inproj(x2, g_mix, w_in, b_in):
    t = x2.shape[0]
    tm = min(PROJ_TM, t)
    tok = lambda i: (i, 0)
    const = lambda i: (0, 0)
    return pl.pallas_call(
        _inproj_kernel,
        grid=(t // tm,),
        in_specs=[
            pl.BlockSpec((tm, D_MODEL), tok),
            pl.BlockSpec((1, D_MODEL), const),
            pl.BlockSpec((D_MODEL, D_IN), const),
            pl.BlockSpec((1, D_IN), const),
        ],
        out_specs=[
            pl.BlockSpec((tm, D_CONV), tok),
            pl.BlockSpec((tm, D_NA), tok),
            pl.BlockSpec((tm, D_NA), tok),
            pl.BlockSpec((tm, D_NA), tok),
        ],
        out_shape=[
            jax.ShapeDtypeStruct((t, D_CONV), F32),
            jax.ShapeDtypeStruct((t, D_NA), BF16),
            jax.ShapeDtypeStruct((t, D_NA), BF16),
            jax.ShapeDtypeStruct((t, D_NA), BF16),
        ],
        compiler_params=pltpu.CompilerParams(
            dimension_semantics=("arbitrary",), vmem_limit_bytes=_vmem_limit(48 << 20)),
        name="inproj",
    )(x2, g_mix.reshape(1, D_MODEL), w_in.astype(BF16), b_in.reshape(1, D_IN))


def _conv_kernel(prev_ref, cur_ref, next_ref, w_ref, cb_ref, lg_ref, lb_ref, o_ref, ext_ref):
    i = pl.program_id(0)
    n = pl.num_programs(0)
    tt = cur_ref.shape[0]
    ext_ref[0:CONV_HALO, :] = jnp.where(i > 0, prev_ref[...], 0.0)
    ext_ref[CONV_HALO:CONV_HALO + tt, :] = cur_ref[...]
    ext_ref[CONV_HALO + tt:2 * CONV_HALO + tt, :] = jnp.where(i < n - 1, next_ref[...], 0.0)
    base = CONV_HALO - CONV_K // 2
    for c in range(tt // CONV_CH):
        acc = jnp.zeros((CONV_CH, D_CONV), F32)
        for j in range(CONV_K):
            r = c * CONV_CH + base + j
            acc = acc + ext_ref[r:r + CONV_CH, :] * w_ref[j:j + 1, :]
        acc = acc + cb_ref[...]
        mu = jnp.mean(acc, axis=-1, keepdims=True)
        d = acc - mu
        var = jnp.mean(d * d, axis=-1, keepdims=True)
        un = d * lax.rsqrt(var + EPS) * lg_ref[...] + lb_ref[...]
        o_ref[c * CONV_CH:(c + 1) * CONV_CH, :] = (un * jax.nn.sigmoid(un)).astype(o_ref.dtype)


def _conv(u, conv_w, conv_b, ln_g, ln_b):
    t = u.shape[0]
    tt = min(CONV_TT, t)
    hb = tt // CONV_HALO
    nhb = t // CONV_HALO
    const = lambda i: (0, 0)
    w = jnp.zeros((CONV_K + 1, D_CONV), F32).at[:CONV_K].set(conv_w)
    return pl.pallas_call(
        _conv_kernel,
        grid=(t // tt,),
        in_specs=[
            pl.BlockSpec((CONV_HALO, D_CONV), lambda i: (jnp.maximum(i * hb - 1, 0), 0)),
            pl.BlockSpec((tt, D_CONV), lambda i: (i, 0)),
            pl.BlockSpec((CONV_HALO, D_CONV), lambda i: (jnp.minimum((i + 1) * hb, nhb - 1), 0)),
            pl.BlockSpec((CONV_K + 1, D_CONV), const),
            pl.BlockSpec((1, D_CONV), const),
            pl.BlockSpec((1, D_CONV), const),
            pl.BlockSpec((1, D_CONV), const),
        ],
        out_specs=pl.BlockSpec((tt, D_CONV), lambda i: (i, 0)),
        out_shape=jax.ShapeDtypeStruct((t, D_CONV), BF16),
        scratch_shapes=[pltpu.VMEM((tt + 2 * CONV_HALO, D_CONV), F32)],
        compiler_params=pltpu.CompilerParams(dimension_semantics=("arbitrary",)),
        name="conv",
    )(u, u, u, w, conv_b.reshape(1, D_CONV), ln_g.reshape(1, D_CONV), ln_b.reshape(1, D_CONV))


def _bias_table(rpb):
    col = np.arange(GRID_W)
    c0 = np.clip(col - WIN_W // 2, 0, GRID_W - WIN_W)
    valid = (col[None, :] >= c0[:, None]) & (col[None, :] < c0[:, None] + WIN_W)
    dcol = np.clip(col[None, :] - col[:, None], -(WIN_W - 1), WIN_W - 1) + (WIN_W - 1)
    onehot = ((dcol[:, :, None] == np.arange(2 * WIN_W - 1)) & valid[:, :, None]).astype(np.float32)
    tcol = jnp.einsum("hdj,ckj->hdck", rpb.astype(F32), onehot, precision=lax.Precision.HIGHEST)
    tcol = jnp.where(valid[None, None], tcol, MASK_VALUE)
    b = jnp.stack([tcol[:, WIN_H - 1 - c:2 * WIN_H - 1 - c] for c in range(WIN_H)], axis=1)
    b = jnp.transpose(b, (0, 1, 3, 2, 4))
    return b.reshape(NA_HEADS, WIN_H, GRID_W, WIN_H * GRID_W)


def _attn_kernel(q_ref, k_ref, v_ref, bt_ref, o_ref, *, rb, rows):
    jb = pl.program_id(1)
    lane = lax.broadcasted_iota(jnp.int32, (GRID_W, LANES), 1)
    first_head = lane < NA_HEAD_DIM
    nkeys = WIN_H * GRID_W

    def row(r, carry):
        i = jb * rb + r
        r0 = jnp.clip(i - WIN_H // 2, 0, rows - WIN_H)
        c = i - r0
        q2 = q_ref[pl.ds(pl.multiple_of(r * GRID_W, GRID_W), GRID_W), :]
        zero = jnp.zeros_like(q2)
        qst = jnp.concatenate([jnp.where(first_head, q2, zero), jnp.where(first_head, zero, q2)], axis=0)
        start = pl.multiple_of(r0 * GRID_W, GRID_W)
        ks = k_ref[pl.ds(start, nkeys), :]
        vs = v_ref[pl.ds(start, nkeys), :]
        s = lax.dot_general(qst, ks, (((1,), (1,)), ((), ())), preferred_element_type=F32)
        bias = jnp.concatenate([bt_ref[0, c], bt_ref[1, c]], axis=0)
        s = jnp.where(bias > 0.5 * MASK_VALUE, s + bias, MASK_VALUE)
        m = jnp.max(s, axis=-1, keepdims=True)
        p = jnp.exp(s - m)
        l = jnp.sum(p, axis=-1, keepdims=True)
        o = jnp.dot(p.astype(BF16), vs, preferred_element_type=F32) / l
        o2 = jnp.where(first_head, o[:GRID_W], o[GRID_W:])
        o_ref[pl.ds(pl.multiple_of(r * GRID_W, GRID_W), GRID_W), :] = o2.astype(o_ref.dtype)
        return carry

    lax.fori_loop(0, rb, row, 0)


def _attention(q, k, v, rpb):
    t = q.shape[0]
    rows = t // GRID_W
    rb = min(ATTN_RB, rows)
    npairs = D_NA // LANES
    bt = _bias_table(rpb)
    kern = functools.partial(_attn_kernel, rb=rb, rows=rows)
    return pl.pallas_call(
        kern,
        grid=(npairs, rows // rb),
        in_specs=[
            pl.BlockSpec((rb * GRID_W, LANES), lambda p, j: (j, p)),
            pl.BlockSpec((t, LANES), lambda p, j: (0, p)),
            pl.BlockSpec((t, LANES), lambda p, j: (0, p)),
            pl.BlockSpec((2, WIN_H, GRID_W, WIN_H * GRID_W), lambda p, j: (p, 0, 0, 0)),
        ],
        out_specs=pl.BlockSpec((rb * GRID_W, LANES), lambda p, j: (j, p)),
        out_shape=jax.ShapeDtypeStruct((t, D_NA), BF16),
        compiler_params=pltpu.CompilerParams(
            dimension_semantics=("arbitrary", "arbitrary"), vmem_limit_bytes=_vmem_limit(40 << 20)),
        name="attention",
    )(q, k, v, bt)


def _outproj_kernel(conv_ref, na_ref, x_ref, wo_ref, bo_ref, g_ref, wr_ref, br_ref,
                    h_ref, xn_ref, idx_ref, gate_ref):
    mixed = jnp.concatenate([conv_ref[...], na_ref[...]], axis=-1)
    h = x_ref[...] + jnp.dot(mixed, wo_ref[...], preferred_element_type=F32) + bo_ref[...]
    h_ref[...] = h
    ms = jnp.mean(h * h, axis=-1, keepdims=True)
    xn = h * lax.rsqrt(ms + EPS) * g_ref[...]
    xn_ref[...] = xn
    logits = lax.dot_general(wr_ref[...], xn, (((1,), (1,)), ((), ())),
                             preferred_element_type=F32, precision=lax.Precision.HIGHEST) + br_ref[...]
    ids = lax.broadcasted_iota(jnp.int32, logits.shape, 0)
    vals, sels = [], []
    l = logits
    for _ in range(TOP_K):
        m = jnp.max(l, axis=0, keepdims=True)
        sel = jnp.min(jnp.where(l == m, ids, N_EXPERTS), axis=0, keepdims=True)
        vals.append(m)
        sels.append(sel)
        l = jnp.where(ids == sel, -jnp.inf, l)
    es = [jnp.exp(vk - vals[0]) for vk in vals]
    tot = es[0] + es[1] + es[2] + es[3]
    idx_ref[...] = jnp.concatenate(sels, axis=0)
    gate_ref[...] = jnp.concatenate([e / tot for e in es], axis=0)


def _outproj_router(conv_out, na_out, x2, w_out, b_out, g_ffn, w_router, b_router):
    t = x2.shape[0]
    tm = min(PROJ_TM, t)
    tok = lambda i: (i, 0)
    const = lambda i: (0, 0)
    return pl.pallas_call(
        _outproj_kernel,
        grid=(t // tm,),
        in_specs=[
            pl.BlockSpec((tm, D_CONV), tok),
            pl.BlockSpec((tm, D_NA), tok),
            pl.BlockSpec((tm, D_MODEL), tok),
            pl.BlockSpec((D_MODEL, D_MODEL), const),
            pl.BlockSpec((1, D_MODEL), const),
            pl.BlockSpec((1, D_MODEL), const),
            pl.BlockSpec((N_EXPERTS, D_MODEL), const),
            pl.BlockSpec((N_EXPERTS, 1), const),
        ],
        out_specs=[
            pl.BlockSpec((tm, D_MODEL), tok),
            pl.BlockSpec((tm, D_MODEL), tok),
            pl.BlockSpec((TOP_K, tm), lambda i: (0, i)),
            pl.BlockSpec((TOP_K, tm), lambda i: (0, i)),
        ],
        out_shape=[
            jax.ShapeDtypeStruct((t, D_MODEL), F32),
            jax.ShapeDtypeStruct((t, D_MODEL), F32),
            jax.ShapeDtypeStruct((TOP_K, t), jnp.int32),
            jax.ShapeDtypeStruct((TOP_K, t), F32),
        ],
        compiler_params=pltpu.CompilerParams(
            dimension_semantics=("arbitrary",), vmem_limit_bytes=_vmem_limit(48 << 20)),
        name="outproj_router",
    )(conv_out, na_out, x2, w_out.astype(BF16), b_out.reshape(1, D_MODEL), g_ffn.reshape(1, D_MODEL),
      w_router.T, b_router.reshape(N_EXPERTS, 1))


def _route_plan(idx_t, t):
    tk = t * TOP_K
    flat_bits = 17
    assert tk < (1 << flat_bits) - 1
    pad_mark = (1 << flat_bits) - 1
    experts = jnp.arange(N_EXPERTS, dtype=jnp.int32)
    exp_flat = idx_t.T.reshape(tk)
    sizes = jnp.sum(exp_flat[:, None] == experts[None, :], axis=0, dtype=jnp.int32)
    padded = (sizes + EXP_BM - 1) // EXP_BM * EXP_BM
    pad_ends = jnp.cumsum(padded)
    pad_offsets = pad_ends - padded
    cap = tk + N_EXPERTS * EXP_BM
    nb = cap // EXP_BM
    real = (exp_flat * 2 << flat_bits) | jnp.arange(tk, dtype=jnp.int32)
    pad_j = jnp.arange(EXP_BM, dtype=jnp.int32)[None, :]
    pad_key = jnp.where(pad_j < (padded - sizes)[:, None], experts[:, None] * 2 + 1, 2 * N_EXPERTS)
    pads = ((pad_key << flat_bits) | pad_mark).reshape(N_EXPERTS * EXP_BM)
    flat = jnp.sort(jnp.concatenate([real, pads])) & pad_mark
    flat = jnp.where(flat == pad_mark, 0, flat)
    tok_pad = flat // TOP_K
    dst_pad = (flat % TOP_K) * t + tok_pad
    blk_start = jnp.arange(nb, dtype=jnp.int32) * EXP_BM
    blk_sel = blk_start[:, None] >= pad_ends[None, :]
    blk_exp = jnp.minimum(jnp.sum(blk_sel, axis=1, dtype=jnp.int32), N_EXPERTS - 1)
    valid_end = jnp.sum(jnp.where(blk_exp[:, None] == experts[None, :], (pad_offsets + sizes)[None, :], 0),
                        axis=1, dtype=jnp.int32)
    in_use = blk_start < pad_ends[N_EXPERTS - 1]
    blk_valid = jnp.where(in_use, jnp.clip(valid_end - blk_start, 0, EXP_BM), 0).astype(jnp.int32)
    return tok_pad.reshape(nb, 1, EXP_BM), dst_pad.reshape(nb, 1, EXP_BM), blk_exp, blk_valid


DMA_UNROLL = 8


def _for_rows(n, fn):
    groups = n // DMA_UNROLL

    def group(g, c):
        for j in range(DMA_UNROLL):
            fn(g * DMA_UNROLL + j)
        return c
    lax.fori_loop(0, groups, group, 0)

    def single(r, c):
        fn(r)
        return c
    lax.fori_loop(groups * DMA_UNROLL, n, single, 0)


def _expert_kernel(be_ref, nv_ref, tokc_ref, tokn_ref, dst_ref, xn_hbm, wg_ref, wu_ref, wd_ref,
                   bg_ref, bu_ref, bd_ref, out_hbm, xbuf, ybuf, wbf, gsem, ssem):
    b = pl.program_id(0)
    nb = pl.num_programs(0)
    slot = b % 2
    bm = xbuf.shape[1]

    def gather_copy(tok, r, s):
        return pltpu.make_async_copy(xn_hbm.at[pl.ds(tok, 1)], xbuf.at[s, pl.ds(r, 1)], gsem.at[s])

    def scatter_copy(dst, r, s):
        return pltpu.make_async_copy(ybuf.at[s, pl.ds(r, 1)], out_hbm.at[pl.ds(dst, 1)], ssem.at[s])

    def start_gather(tok_ref, s):
        def body(r, c):
            gather_copy(tok_ref[0, 0, r], r, s).start()
            return c
        lax.fori_loop(0, bm, body, 0, unroll=DMA_UNROLL)

    def wait_scatter(step, s):
        nv = nv_ref[step]

        @pl.when(nv == bm)
        def _():
            pltpu.make_async_copy(ybuf.at[s], out_hbm.at[pl.ds(0, bm)], ssem.at[s]).wait()

        @pl.when(nv < bm)
        def _():
            _for_rows(nv, lambda r: scatter_copy(0, r, s).wait())

    @pl.when(b == 0)
    def _():
        start_gather(tokc_ref, 0)

    @pl.when(b + 1 < nb)
    def _():
        start_gather(tokn_ref, 1 - slot)

    @pl.when((b == 0) | (be_ref[b] != be_ref[jnp.maximum(b - 1, 0)]))
    def _():
        wbf[0] = wg_ref[0].astype(BF16)
        wbf[1] = wu_ref[0].astype(BF16)
        wbf[2] = wd_ref[0].astype(BF16)

    pltpu.make_async_copy(xn_hbm.at[pl.ds(0, bm)], xbuf.at[slot], gsem.at[slot]).wait()

    @pl.when(b >= 2)
    def _():
        wait_scatter(b - 2, slot)

    @pl.when(nv_ref[b] > 0)
    def _():
        x = xbuf[slot].astype(BF16)
        gt = jnp.minimum(jnp.dot(x, wbf[0], preferred_element_type=F32) + bg_ref[0], SWIGLU_LIMIT)
        up = jnp.clip(jnp.dot(x, wbf[1], preferred_element_type=F32) + bu_ref[0],
                      -SWIGLU_LIMIT, SWIGLU_LIMIT)
        hdn = (up + 1.0) * (gt * jax.nn.sigmoid(SWIGLU_ALPHA * gt))
        ybuf[slot] = jnp.dot(hdn.astype(BF16), wbf[2], preferred_element_type=F32) + bd_ref[0]
        _for_rows(nv_ref[b], lambda r: scatter_copy(dst_ref[0, 0, r], r, slot).start())

    @pl.when(b == nb - 1)
    def _():
        wait_scatter(b, slot)

        @pl.when(b >= 1)
        def _():
            wait_scatter(b - 1, 1 - slot)


def _experts(xn2, tok_pad, dst_pad, blk_exp, blk_valid, w_gate, b_gate, w_up, b_up, w_down, b_down):
    t = xn2.shape[0]
    nb = tok_pad.shape[0]
    blk = lambda b, be, nv: (b, 0, 0)
    nxt = lambda b, be, nv: (jnp.minimum(b + 1, nb - 1), 0, 0)
    wsel = lambda b, be, nv: (be[b], 0, 0)
    smem_blk = lambda im: pl.BlockSpec((1, 1, EXP_BM), im, memory_space=pltpu.SMEM)
    wspec = pl.BlockSpec((1, D_MODEL, D_FF), wsel)
    bspec = pl.BlockSpec((1, 1, D_FF), wsel)
    grid_spec = pltpu.PrefetchScalarGridSpec(
        num_scalar_prefetch=2,
        grid=(nb,),
        in_specs=[
            smem_blk(blk), smem_blk(nxt), smem_blk(blk),
            pl.BlockSpec(memory_space=pl.ANY),
            wspec, wspec, wspec, bspec, bspec, bspec,
        ],
        out_specs=pl.BlockSpec(memory_space=pl.ANY),
        scratch_shapes=[
            pltpu.VMEM((2, EXP_BM, D_MODEL), F32),
            pltpu.VMEM((2, EXP_BM, D_MODEL), F32),
            pltpu.VMEM((3, D_MODEL, D_FF), BF16),
            pltpu.SemaphoreType.DMA((2,)),
            pltpu.SemaphoreType.DMA((2,)),
        ],
    )
    return pl.pallas_call(
        _expert_kernel,
        grid_spec=grid_spec,
        out_shape=jax.ShapeDtypeStruct((TOP_K * t, D_MODEL), F32),
        compiler_params=pltpu.CompilerParams(
            dimension_semantics=("arbitrary",), vmem_limit_bytes=_vmem_limit(56 << 20)),
        name="experts",
    )(blk_exp, blk_valid, tok_pad, tok_pad, dst_pad, xn2, w_gate, w_up, w_down,
      b_gate.reshape(N_EXPERTS, 1, D_FF), b_up.reshape(N_EXPERTS, 1, D_FF),
      b_down.reshape(N_EXPERTS, 1, D_MODEL))


def _combine_kernel(h_ref, y0_ref, y1_ref, y2_ref, y3_ref, gate_ref, g_ref, o_ref):
    g = gate_ref[...]
    out = h_ref[...]
    for k, y_ref in enumerate((y0_ref, y1_ref, y2_ref, y3_ref)):
        out = out + g[:, k:k + 1] * y_ref[0]
    ms = jnp.mean(out * out, axis=-1, keepdims=True)
    o_ref[...] = out * lax.rsqrt(ms + EPS) * g_ref[...]


def _combine(h, y4, gates, g_final):
    t = h.shape[0]
    tm = min(COMB_TM, t)
    y3d = y4.reshape(TOP_K, t, D_MODEL)
    tok = lambda i: (i, 0)
    yspec = lambda k: pl.BlockSpec((1, tm, D_MODEL), lambda i: (k, i, 0))
    return pl.pallas_call(
        _combine_kernel,
        grid=(t // tm,),
        in_specs=[
            pl.BlockSpec((tm, D_MODEL), tok),
            yspec(0), yspec(1), yspec(2), yspec(3),
            pl.BlockSpec((tm, TOP_K), tok),
            pl.BlockSpec((1, D_MODEL), lambda i: (0, 0)),
        ],
        out_specs=pl.BlockSpec((tm, D_MODEL), tok),
        out_shape=jax.ShapeDtypeStruct((t, D_MODEL), F32),
        compiler_params=pltpu.CompilerParams(dimension_semantics=("arbitrary",)),
        name="combine",
    )(h, y3d, y3d, y3d, y3d, gates, g_final.reshape(1, D_MODEL))


def kernel(x, g_mix, w_in, b_in, conv_w, conv_b, ln_g, ln_b, rpb, w_out, b_out, g_ffn, w_router,
           b_router, w_gate, b_gate, w_up, b_up, w_down, b_down, g_final):
    bsz, seq, d = x.shape
    assert bsz == 1 and d == D_MODEL and g_mix.shape[0] == 1
    assert seq % (GRID_W * WIN_H) == 0
    x2 = x.reshape(seq, d)
    u, q, k, v = _inproj(x2, g_mix[0], w_in[0], b_in[0])
    conv_out = _conv(u, conv_w[0], conv_b[0], ln_g[0], ln_b[0])
    na_out = _attention(q, k, v, rpb[0])
    h, xn2, idx_t, gate_t = _outproj_router(conv_out, na_out, x2, w_out[0], b_out[0], g_ffn[0],
                                            w_router[0], b_router[0])
    tok_pad, dst_pad, blk_exp, blk_valid = _route_plan(idx_t, seq)
    y4 = _experts(xn2, tok_pad, dst_pad, blk_exp, blk_valid, w_gate[0], b_gate[0], w_up[0], b_up[0],
                  w_down[0], b_down[0])
    out = _combine(h, y4, gate_t.T, g_final)
    return out.reshape(bsz, seq, d)
```

```python
import functools

import numpy as np
import jax
import jax.numpy as jnp
from jax import lax
from jax.experimental import pallas as pl
from jax.experimental.pallas import tpu as pltpu

F32 = jnp.float32
BF16 = jnp.bfloat16
U32 = jnp.uint32
I32 = jnp.int32

D_MODEL = 1024
GRID_W = 64
D_CONV = 512
CONV_K = 31
NA_HEADS = 8
NA_HEAD_DIM = 64
D_NA = NA_HEADS * NA_HEAD_DIM
D_IN = 2 * D_CONV + 3 * D_NA
WIN_H = 8
WIN_W = 16
N_EXPERTS = 32
TOP_K = 4
D_FF = D_MODEL
SWIGLU_LIMIT = 7.0
SWIGLU_ALPHA = 1.702
EPS = 1e-5
MASK_VALUE = -1e30

V7X_VMEM_BYTES = 64 * 1024 * 1024
LANES = 128

PROJ_TM = 512
CONV_TT = 256
CONV_HALO = 16
CONV_CH = 32
ATTN_RB = 16
ROUTE_TM = 128
SUBLANES = 8
ROUTE_RP = 40
EXP_BM = 256
HALF = D_MODEL // 2


def _vmem_limit(nbytes):
    return int(min(nbytes, V7X_VMEM_BYTES - 6 * 1024 * 1024))


def _pack_rows(x):
    lo = lax.bitcast_convert_type(x[:, :HALF], U32) >> 16
    hi = lax.bitcast_convert_type(x[:, HALF:], U32) & jnp.uint32(0xFFFF0000)
    return lo | hi


def _unpack_rows(w):
    lo = lax.bitcast_convert_type(w << 16, F32)
    hi = lax.bitcast_convert_type(w & jnp.uint32(0xFFFF0000), F32)
    return jnp.concatenate([lo, hi], axis=1).astype(BF16)


def _inproj_kernel(x_ref, g_ref, w_ref, b_ref, u_ref, q_ref, k_ref, v_ref):
    x = x_ref[...]
    ms = jnp.mean(x * x, axis=-1, keepdims=True)
    xn = (x * lax.rsqrt(ms + EPS) * g_ref[...]).astype(BF16)

    def proj(c):
        sl = slice(c * D_CONV, (c + 1) * D_CONV)
        return jnp.dot(xn, w_ref[:, sl], preferred_element_type=F32) + b_ref[:, sl]

    u_ref[...] = proj(0) * jax.nn.sigmoid(proj(1))
    q_ref[...] = (proj(2) * (NA_HEAD_DIM ** -0.5)).astype(BF16)
    k_ref[...] = proj(3).astype(BF16)
    v_ref[...] = proj(4).astype(BF16)


def _inproj(x2, g_mix, w_in, b_in):
    t = x2.shape[0]
    tm = min(PROJ_TM, t)
    tok = lambda i: (i, 0)
    const = lambda i: (0, 0)
    return pl.pallas_call(
        _inproj_kernel,
        grid=(t // tm,),
        in_specs=[
            pl.BlockSpec((tm, D_MODEL), tok),
            pl.BlockSpec((1, D_MODEL), const),
            pl.BlockSpec((D_MODEL, D_IN), const),
            pl.BlockSpec((1, D_IN), const),
        ],
        out_specs=[
            pl.BlockSpec((tm, D_CONV), tok),
            pl.BlockSpec((tm, D_NA), tok),
            pl.BlockSpec((tm, D_NA), tok),
            pl.BlockSpec((tm, D_NA), tok),
        ],
        out_shape=[
            jax.ShapeDtypeStruct((t, D_CONV), F32),
            jax.ShapeDtypeStruct((t, D_NA), BF16),
            jax.ShapeDtypeStruct((t, D_NA), BF16),
            jax.ShapeDtypeStruct((t, D_NA), BF16),
        ],
        compiler_params=pltpu.CompilerParams(
            dimension_semantics=("arbitrary",), vmem_limit_bytes=_vmem_limit(48 << 20)),
        name="inproj",
    )(x2, g_mix.reshape(1, D_MODEL), w_in.astype(BF16), b_in.reshape(1, D_IN))


def _conv_kernel(prev_ref, cur_ref, next_ref, w_ref, cb_ref, lg_ref, lb_ref, o_ref, ext_ref):
    i = pl.program_id(0)
    n = pl.num_programs(0)
    tt = cur_ref.shape[0]
    ext_ref[0:CONV_HALO, :] = jnp.where(i > 0, prev_ref[...], 0.0)
    ext_ref[CONV_HALO:CONV_HALO + tt, :] = cur_ref[...]
    ext_ref[CONV_HALO + tt:2 * CONV_HALO + tt, :] = jnp.where(i < n - 1, next_ref[...], 0.0)
    base = CONV_HALO - CONV_K // 2
    for c in range(tt // CONV_CH):
        acc = jnp.zeros((CONV_CH, D_CONV), F32)
        for j in range(CONV_K):
            r = c * CONV_CH + base + j
            acc = acc + ext_ref[r:r + CONV_CH, :] * w_ref[j:j + 1, :]
        acc = acc + cb_ref[...]
        mu = jnp.mean(acc, axis=-1, keepdims=True)
        d = acc - mu
        var = jnp.mean(d * d, axis=-1, keepdims=True)
        un = d * lax.rsqrt(var + EPS) * lg_ref[...] + lb_ref[...]
        o_ref[c * CONV_CH:(c + 1) * CONV_CH, :] = (un * jax.nn.sigmoid(un)).astype(o_ref.dtype)


def _conv(u, conv_w, conv_b, ln_g, ln_b):
    t = u.shape[0]
    tt = min(CONV_TT, t)
    hb = tt // CONV_HALO
    nhb = t // CONV_HALO
    const = lambda i: (0, 0)
    w = jnp.zeros((CONV_K + 1, D_CONV), F32).at[:CONV_K].set(conv_w)
    return pl.pallas_call(
        _conv_kernel,
        grid=(t // tt,),
        in_specs=[
            pl.BlockSpec((CONV_HALO, D_CONV), lambda i: (jnp.maximum(i * hb - 1, 0), 0)),
            pl.BlockSpec((tt, D_CONV), lambda i: (i, 0)),
            pl.BlockSpec((CONV_HALO, D_CONV), lambda i: (jnp.minimum((i + 1) * hb, nhb - 1), 0)),
            pl.BlockSpec((CONV_K + 1, D_CONV), const),
            pl.BlockSpec((1, D_CONV), const),
            pl.BlockSpec((1, D_CONV), const),
            pl.BlockSpec((1, D_CONV), const),
        ],
        out_specs=pl.BlockSpec((tt, D_CONV), lambda i: (i, 0)),
        out_shape=jax.ShapeDtypeStruct((t, D_CONV), BF16),
        scratch_shapes=[pltpu.VMEM((tt + 2 * CONV_HALO, D_CONV), F32)],
        compiler_params=pltpu.CompilerParams(dimension_semantics=("arbitrary",)),
        name="conv",
    )(u, u, u, w, conv_b.reshape(1, D_CONV), ln_g.reshape(1, D_CONV), ln_b.reshape(1, D_CONV))


def _bias_table(rpb):
    col = np.arange(GRID_W)
    c0 = np.clip(col - WIN_W // 2, 0, GRID_W - WIN_W)
    valid = (col[None, :] >= c0[:, None]) & (col[None, :] < c0[:, None] + WIN_W)
    dcol = np.clip(col[None, :] - col[:, None], -(WIN_W - 1), WIN_W - 1) + (WIN_W - 1)
    onehot = ((dcol[:, :, None] == np.arange(2 * WIN_W - 1)) & valid[:, :, None]).astype(np.float32)
    tcol = jnp.einsum("hdj,ckj->hdck", rpb.astype(F32), onehot, precision=lax.Precision.HIGHEST)
    tcol = jnp.where(valid[None, None], tcol, MASK_VALUE)
    b = jnp.stack([tcol[:, WIN_H - 1 - c:2 * WIN_H - 1 - c] for c in range(WIN_H)], axis=1)
    b = jnp.transpose(b, (0, 1, 3, 2, 4))
    return b.reshape(NA_HEADS, WIN_H, GRID_W, WIN_H * GRID_W)


def _attn_kernel(q_ref, k_ref, v_ref, bt_ref, o_ref, *, rb, rows):
    jb = pl.program_id(1)
    lane = lax.broadcasted_iota(jnp.int32, (GRID_W, LANES), 1)
    first_head = lane < NA_HEAD_DIM
    nkeys = WIN_H * GRID_W

    def row(r, carry):
        i = jb * rb + r
        r0 = jnp.clip(i - WIN_H // 2, 0, rows - WIN_H)
        c = i - r0
        q2 = q_ref[pl.ds(pl.multiple_of(r * GRID_W, GRID_W), GRID_W), :]
        zero = jnp.zeros_like(q2)
        qst = jnp.concatenate([jnp.where(first_head, q2, zero), jnp.where(first_head, zero, q2)], axis=0)
        start = pl.multiple_of(r0 * GRID_W, GRID_W)
        ks = k_ref[pl.ds(start, nkeys), :]
        vs = v_ref[pl.ds(start, nkeys), :]
        s = lax.dot_general(qst, ks, (((1,), (1,)), ((), ())), preferred_element_type=F32)
        bias = jnp.concatenate([bt_ref[0, c], bt_ref[1, c]], axis=0)
        s = jnp.where(bias > 0.5 * MASK_VALUE, s + bias, MASK_VALUE)
        m = jnp.max(s, axis=-1, keepdims=True)
        p = jnp.exp(s - m)
        l = jnp.sum(p, axis=-1, keepdims=True)
        o = jnp.dot(p.astype(BF16), vs, preferred_element_type=F32) / l
        o2 = jnp.where(first_head, o[:GRID_W], o[GRID_W:])
        o_ref[pl.ds(pl.multiple_of(r * GRID_W, GRID_W), GRID_W), :] = o2.astype(o_ref.dtype)
        return carry

    lax.fori_loop(0, rb, row, 0)


def _attention(q, k, v, rpb):
    t = q.shape[0]
    rows = t // GRID_W
    rb = min(ATTN_RB, rows)
    npairs = D_NA // LANES
    bt = _bias_table(rpb)
    kern = functools.partial(_attn_kernel, rb=rb, rows=rows)
    return pl.pallas_call(
        kern,
        grid=(npairs, rows // rb),
        in_specs=[
            pl.BlockSpec((rb * GRID_W, LANES), lambda p, j: (j, p)),
            pl.BlockSpec((t, LANES), lambda p, j: (0, p)),
            pl.BlockSpec((t, LANES), lambda p, j: (0, p)),
            pl.BlockSpec((2, WIN_H, GRID_W, WIN_H * GRID_W), lambda p, j: (p, 0, 0, 0)),
        ],
        out_specs=pl.BlockSpec((rb * GRID_W, LANES), lambda p, j: (j, p)),
        out_shape=jax.ShapeDtypeStruct((t, D_NA), BF16),
        compiler_params=pltpu.CompilerParams(
            dimension_semantics=("arbitrary", "arbitrary"), vmem_limit_bytes=_vmem_limit(40 << 20)),
        name="attention",
    )(q, k, v, bt)


def _outproj_kernel(conv_ref, na_ref, x_ref, wo_ref, bo_ref, g_ref, wr_ref, br_ref,
                    h_ref, xn_ref, idx_ref, gate_ref):
    mixed = jnp.concatenate([conv_ref[...], na_ref[...]], axis=-1)
    h = x_ref[...] + jnp.dot(mixed, wo_ref[...], preferred_element_type=F32) + bo_ref[...]
    h_ref[...] = h
    ms = jnp.mean(h * h, axis=-1, keepdims=True)
    xn = h * lax.rsqrt(ms + EPS) * g_ref[...]
    xn_ref[...] = xn.astype(xn_ref.dtype)
    logits = lax.dot_general(wr_ref[...], xn, (((1,), (1,)), ((), ())),
                             preferred_element_type=F32, precision=lax.Precision.HIGHEST) + br_ref[...]
    ids = lax.broadcasted_iota(jnp.int32, logits.shape, 0)
    vals, sels = [], []
    l = logits
    for _ in range(TOP_K):
        m = jnp.max(l, axis=0, keepdims=True)
        sel = jnp.min(jnp.where(l == m, ids, N_EXPERTS), axis=0, keepdims=True)
        vals.append(m)
        sels.append(sel)
        l = jnp.where(ids == sel, -jnp.inf, l)
    es = [jnp.exp(vk - vals[0]) for vk in vals]
    tot = es[0] + es[1] + es[2] + es[3]
    idx_ref[...] = jnp.concatenate(sels, axis=0)
    gate_ref[...] = jnp.concatenate([e / tot for e in es], axis=0)


def _outproj_router(conv_out, na_out, x2, w_out, b_out, g_ffn, w_router, b_router):
    t = x2.shape[0]
    tm = min(PROJ_TM, t)
    tok = lambda i: (i, 0)
    const = lambda i: (0, 0)
    return pl.pallas_call(
        _outproj_kernel,
        grid=(t // tm,),
        in_specs=[
            pl.BlockSpec((tm, D_CONV), tok),
            pl.BlockSpec((tm, D_NA), tok),
            pl.BlockSpec((tm, D_MODEL), tok),
            pl.BlockSpec((D_MODEL, D_MODEL), const),
            pl.BlockSpec((1, D_MODEL), const),
            pl.BlockSpec((1, D_MODEL), const),
            pl.BlockSpec((N_EXPERTS, D_MODEL), const),
            pl.BlockSpec((N_EXPERTS, 1), const),
        ],
        out_specs=[
            pl.BlockSpec((tm, D_MODEL), tok),
            pl.BlockSpec((tm, D_MODEL), tok),
            pl.BlockSpec((TOP_K, tm), lambda i: (0, i)),
            pl.BlockSpec((TOP_K, tm), lambda i: (0, i)),
        ],
        out_shape=[
            jax.ShapeDtypeStruct((t, D_MODEL), F32),
            jax.ShapeDtypeStruct((t, D_MODEL), BF16),
            jax.ShapeDtypeStruct((TOP_K, t), jnp.int32),
            jax.ShapeDtypeStruct((TOP_K, t), F32),
        ],
        compiler_params=pltpu.CompilerParams(
            dimension_semantics=("arbitrary",), vmem_limit_bytes=_vmem_limit(48 << 20)),
        name="outproj_router",
    )(conv_out, na_out, x2, w_out.astype(BF16), b_out.reshape(1, D_MODEL), g_ffn.reshape(1, D_MODEL),
      w_router.T, b_router.reshape(N_EXPERTS, 1))


def _route_plan(idx_t, t):
    nt = t // ROUTE_TM
    experts = jnp.arange(N_EXPERTS, dtype=I32)
    onehot = idx_t.reshape(TOP_K, nt, ROUTE_TM, 1) == experts
    cnt = jnp.sum(onehot, axis=(0, 2), dtype=I32)
    sizes = jnp.sum(cnt, axis=0)
    padded = (sizes + ROUTE_RP + EXP_BM - 1) // EXP_BM * EXP_BM
    pad_ends = jnp.cumsum(padded)
    pad_off = pad_ends - padded
    tbase = pad_off[None, :] + jnp.cumsum(cnt, axis=0) - cnt
    cap = _sorted_rows(t)
    nb = cap // EXP_BM
    n_used = pad_ends[N_EXPERTS - 1] // EXP_BM
    blk_src = jnp.minimum(jnp.arange(nb, dtype=I32), n_used - 1)
    blk_exp = jnp.minimum(
        jnp.sum(blk_src[:, None] * EXP_BM >= pad_ends[None, :], axis=1, dtype=I32), N_EXPERTS - 1)
    return dict(cnt=cnt.reshape(-1), tbase=tbase.reshape(-1).astype(I32),
                last_blk=(pad_ends - EXP_BM).astype(I32), two_blk=(padded >= 2 * EXP_BM).astype(I32),
                blk_src=blk_src, blk_exp=blk_exp, n_used=n_used.reshape(1).astype(I32))


def _sorted_rows(t):
    cap = t * TOP_K + N_EXPERTS * (EXP_BM + ROUTE_RP)
    return (cap + EXP_BM - 1) // EXP_BM * EXP_BM


def _chunk_geometry(tb, n):
    head = tb % SUBLANES
    start = pl.multiple_of(tb - head, SUBLANES)
    nchunks = (head + n + ROUTE_RP - 1) // ROUTE_RP
    return start, head, nchunks


def _dispatch_kernel(tb_ref, cnt_ref, lb_ref, two_ref, nu_ref, idx_ref, tbv_ref, xn_ref, xs_hbm, loc_ref,
                     stage, ostage, zbuf, head_ref, cs_ref, ms_ref, sem, osem, zsem):
    i = pl.program_id(0)
    nt = pl.num_programs(0)
    slot = i % 2
    tm = idx_ref.shape[1]
    groups = ROUTE_RP // SUBLANES

    def geometry(e):
        return _chunk_geometry(tb_ref[i * N_EXPERTS + e], cnt_ref[i * N_EXPERTS + e])

    def chunk_copy(e, s):
        start, _, _ = geometry(e)
        return pltpu.make_async_copy(stage.at[s, pl.ds(e * ROUTE_RP, ROUTE_RP)],
                                     xs_hbm.at[pl.ds(start, ROUTE_RP)], sem.at[s])

    def zero_copy(start):
        return pltpu.make_async_copy(zbuf, xs_hbm.at[pl.ds(pl.multiple_of(start, EXP_BM), EXP_BM)], zsem)

    @pl.when(i == 0)
    def _():
        zbuf[...] = jnp.zeros(zbuf.shape, U32)
        head_ref[...] = jnp.zeros(head_ref.shape, U32)
        for phase in ("start", "wait"):
            for e in range(N_EXPERTS):
                getattr(zero_copy(lb_ref[e]), phase)()

                @pl.when(two_ref[e] > 0)
                def _():
                    getattr(zero_copy(lb_ref[e] - EXP_BM), phase)()

        def unused_block(bi, carry):
            cp = zero_copy(bi * EXP_BM)
            cp.start()
            cp.wait()
            return carry
        lax.fori_loop(nu_ref[0], xs_hbm.shape[0] // EXP_BM, unused_block, 0)

    idx = idx_ref[...]
    eio = lax.broadcasted_iota(I32, (N_EXPERTS, tm), 0)
    member = jnp.zeros((N_EXPERTS, tm), F32)
    for k in range(TOP_K):
        member = member + (idx[k:k + 1, :] == eio).astype(F32)
    tri = (lax.broadcasted_iota(I32, (tm, tm), 0) < lax.broadcasted_iota(I32, (tm, tm), 1)).astype(BF16)
    rank = jnp.dot(member.astype(BF16), tri, preferred_element_type=F32)
    pos = rank + (tbv_ref[0] % SUBLANES).astype(F32)
    loc_ref[...] = jnp.concatenate(
        [jnp.sum(jnp.where(idx[k:k + 1, :] == eio, pos, 0.0), axis=0, keepdims=True) for k in range(TOP_K)],
        axis=0).astype(I32)
    cs_ref[...] = pos
    ms_ref[...] = member

    jio = lax.broadcasted_iota(I32, (ROUTE_RP, tm), 0).astype(F32)
    sel = [jnp.where((jio == pos[e:e + 1, :]) & (member[e:e + 1, :] > 0.0), 1.0, 0.0).astype(BF16)
           for e in range(N_EXPERTS)]
    rows = jnp.dot(jnp.concatenate(sel, axis=0), xn_ref[...], preferred_element_type=F32)

    @pl.when(i > 0)
    def _():
        for e in range(N_EXPERTS):
            chunk_copy(e, 1 - slot).wait()

    stage[slot] = _pack_rows(rows)
    sub = lax.broadcasted_iota(I32, (SUBLANES, HALF), 0)
    for e in range(N_EXPERTS):
        _, head, _ = geometry(e)
        first = pl.ds(e * ROUTE_RP, SUBLANES)
        stage[slot, first, :] = jnp.where(sub < head, head_ref[e], stage[slot, first, :])
        chunk_copy(e, slot).start()
        g = jnp.minimum((head + cnt_ref[i * N_EXPERTS + e]) // SUBLANES, groups - 1)
        head_ref[e] = stage[slot, pl.ds(pl.multiple_of(e * ROUTE_RP + g * SUBLANES, SUBLANES), SUBLANES), :]

    def per_expert(e, carry):
        start, head, nchunks = geometry(e)
        end = head + cnt_ref[i * N_EXPERTS + e]

        def per_chunk(c, carry2):
            want = jio + (c * ROUTE_RP).astype(F32)
            pick = jnp.where((want == cs_ref[pl.ds(e, 1), :]) & (ms_ref[pl.ds(e, 1), :] > 0.0), 1.0, 0.0)
            ostage[...] = _pack_rows(jnp.dot(pick.astype(BF16), xn_ref[...], preferred_element_type=F32))
            cp = pltpu.make_async_copy(
                ostage, xs_hbm.at[pl.ds(pl.multiple_of(start + c * ROUTE_RP, SUBLANES), ROUTE_RP)], osem)
            cp.start()
            cp.wait()

            @pl.when(end // ROUTE_RP == c)
            def _():
                g = (end - c * ROUTE_RP) // SUBLANES
                head_ref[e] = ostage[pl.ds(pl.multiple_of(g * SUBLANES, SUBLANES), SUBLANES), :]
            return carry2
        return lax.fori_loop(1, nchunks, per_chunk, carry)
    lax.fori_loop(0, N_EXPERTS, per_expert, 0)

    @pl.when(i == nt - 1)
    def _():
        for e in range(N_EXPERTS):
            chunk_copy(e, slot).wait()


def _dispatch(plan, idx_t, xn2):
    t = xn2.shape[0]
    tm = ROUTE_TM
    cap = _sorted_rows(t)
    grid_spec = pltpu.PrefetchScalarGridSpec(
        num_scalar_prefetch=5,
        grid=(t // tm,),
        in_specs=[
            pl.BlockSpec((TOP_K, tm), lambda i, *_: (0, i)),
            pl.BlockSpec((1, N_EXPERTS, 1), lambda i, *_: (i, 0, 0)),
            pl.BlockSpec((tm, D_MODEL), lambda i, *_: (i, 0)),
        ],
        out_specs=[
            pl.BlockSpec(memory_space=pl.ANY),
            pl.BlockSpec((TOP_K, tm), lambda i, *_: (0, i)),
        ],
        scratch_shapes=[
            pltpu.VMEM((2, N_EXPERTS * ROUTE_RP, HALF), U32),
            pltpu.VMEM((ROUTE_RP, HALF), U32),
            pltpu.VMEM((EXP_BM, HALF), U32),
            pltpu.VMEM((N_EXPERTS, SUBLANES, HALF), U32),
            pltpu.VMEM((N_EXPERTS, tm), F32),
            pltpu.VMEM((N_EXPERTS, tm), F32),
            pltpu.SemaphoreType.DMA((2,)),
            pltpu.SemaphoreType.DMA(()),
            pltpu.SemaphoreType.DMA(()),
        ],
    )
    return pl.pallas_call(
        _dispatch_kernel,
        grid_spec=grid_spec,
        out_shape=[
            jax.ShapeDtypeStruct((cap, HALF), U32),
            jax.ShapeDtypeStruct((TOP_K, t), I32),
        ],
        compiler_params=pltpu.CompilerParams(
            dimension_semantics=("arbitrary",), vmem_limit_bytes=_vmem_limit(32 << 20)),
        name="dispatch",
    )(plan["tbase"], plan["cnt"], plan["last_blk"], plan["two_blk"], plan["n_used"], idx_t,
      plan["tbase"].reshape(t // tm, N_EXPERTS, 1), xn2)


def _expert_kernel(be_ref, src_ref, nu_ref, xs_ref, wg_ref, wu_ref, wd_ref, bg_ref, bu_ref, bd_ref,
                   ys_ref, wbf):
    b = pl.program_id(0)

    @pl.when(b < nu_ref[0])
    def _():
        @pl.when((b == 0) | (be_ref[b] != be_ref[jnp.maximum(b - 1, 0)]))
        def _():
            wbf[0] = wg_ref[0].astype(BF16)
            wbf[1] = wu_ref[0].astype(BF16)
            wbf[2] = wd_ref[0].astype(BF16)

        x = _unpack_rows(xs_ref[...])
        gt = jnp.minimum(jnp.dot(x, wbf[0], preferred_element_type=F32) + bg_ref[0], SWIGLU_LIMIT)
        up = jnp.clip(jnp.dot(x, wbf[1], preferred_element_type=F32) + bu_ref[0],
                      -SWIGLU_LIMIT, SWIGLU_LIMIT)
        hdn = (up + 1.0) * (gt * jax.nn.sigmoid(SWIGLU_ALPHA * gt))
        y = jnp.dot(hdn.astype(BF16), wbf[2], preferred_element_type=F32) + bd_ref[0]
        ys_ref[...] = _pack_rows(y.astype(BF16).astype(F32))

    @pl.when(b >= nu_ref[0])
    def _():
        ys_ref[...] = jnp.zeros(ys_ref.shape, U32)


def _experts(plan, xs, w_gate, b_gate, w_up, b_up, w_down, b_down):
    cap = xs.shape[0]
    nb = cap // EXP_BM
    rows = lambda b, be, src, nu: (src[b], 0)
    wsel = lambda b, be, src, nu: (be[b], 0, 0)
    wspec = pl.BlockSpec((1, D_MODEL, D_FF), wsel)
    bspec = pl.BlockSpec((1, 1, D_FF), wsel)
    grid_spec = pltpu.PrefetchScalarGridSpec(
        num_scalar_prefetch=3,
        grid=(nb,),
        in_specs=[pl.BlockSpec((EXP_BM, HALF), rows), wspec, wspec, wspec, bspec, bspec, bspec],
        out_specs=pl.BlockSpec((EXP_BM, HALF), lambda b, be, src, nu: (b, 0)),
        scratch_shapes=[pltpu.VMEM((3, D_MODEL, D_FF), BF16)],
    )
    return pl.pallas_call(
        _expert_kernel,
        grid_spec=grid_spec,
        out_shape=jax.ShapeDtypeStruct((cap, HALF), U32),
        compiler_params=pltpu.CompilerParams(
            dimension_semantics=("arbitrary",), vmem_limit_bytes=_vmem_limit(56 << 20)),
        name="experts",
    )(plan["blk_exp"], plan["blk_src"], plan["n_used"], xs, w_gate, w_up, w_down,
      b_gate.reshape(N_EXPERTS, 1, D_FF), b_up.reshape(N_EXPERTS, 1, D_FF),
      b_down.reshape(N_EXPERTS, 1, D_MODEL))


def _combine_kernel(tb_ref, cnt_ref, h_ref, idx_ref, loc_ref, gate_ref, ys_hbm, g_ref, o_ref,
                    ybuf, obuf, acc_ref, sem, osem):
    i = pl.program_id(0)
    nt = pl.num_programs(0)
    slot = i % 2
    tm = h_ref.shape[0]

    def geometry(step, e):
        return _chunk_geometry(tb_ref[step * N_EXPERTS + e], cnt_ref[step * N_EXPERTS + e])

    def chunk_copy(step, e, s):
        start, _, _ = geometry(step, e)
        return pltpu.make_async_copy(ys_hbm.at[pl.ds(start, ROUTE_RP)],
                                     ybuf.at[s, pl.ds(e * ROUTE_RP, ROUTE_RP)], sem.at[s])

    @pl.when(i == 0)
    def _():
        for e in range(N_EXPERTS):
            chunk_copy(0, e, 0).start()

    @pl.when(i + 1 < nt)
    def _():
        for e in range(N_EXPERTS):
            chunk_copy(i + 1, e, 1 - slot).start()

    idx = idx_ref[...]
    loc = loc_ref[...]
    gate = gate_ref[...]
    g_hi = gate.astype(BF16).astype(F32)
    g_lo = gate - g_hi

    def gate_matrix(ncols, col_of):
        colio = lax.broadcasted_iota(I32, (tm, ncols), 1)
        hi = jnp.zeros((tm, ncols), F32)
        lo = jnp.zeros((tm, ncols), F32)
        for k in range(TOP_K):
            hit = colio == col_of[:, k:k + 1]
            hi = jnp.where(hit, g_hi[:, k:k + 1], hi)
            lo = jnp.where(hit, g_lo[:, k:k + 1], lo)
        return jnp.concatenate([hi, lo], axis=0).astype(BF16)

    for e in range(N_EXPERTS):
        chunk_copy(i, e, slot).wait()
    col = jnp.where(loc < ROUTE_RP, idx * ROUTE_RP + loc, -1)
    z = jnp.dot(gate_matrix(N_EXPERTS * ROUTE_RP, col), _unpack_rows(ybuf[slot]),
                preferred_element_type=F32)
    acc_ref[...] = z[:tm] + z[tm:]

    def per_expert(e, carry):
        start, _, nchunks = geometry(i, e)

        def per_chunk(c, carry2):
            cp = pltpu.make_async_copy(
                ys_hbm.at[pl.ds(pl.multiple_of(start + c * ROUTE_RP, SUBLANES), ROUTE_RP)], obuf, osem)
            cp.start()
            cp.wait()
            ccol = jnp.where(idx == e, loc - c * ROUTE_RP, -1)
            zc = jnp.dot(gate_matrix(ROUTE_RP, ccol), _unpack_rows(obuf[...]), preferred_element_type=F32)
            acc_ref[...] += zc[:tm] + zc[tm:]
            return carry2
        return lax.fori_loop(1, nchunks, per_chunk, carry)
    lax.fori_loop(0, N_EXPERTS, per_expert, 0)

    out = h_ref[...] + acc_ref[...]
    ms = jnp.mean(out * out, axis=-1, keepdims=True)
    o_ref[...] = out * lax.rsqrt(ms + EPS) * g_ref[...]


def _combine(plan, h, ys, idx_tm, loc_tm, gate_tm, g_final):
    t = h.shape[0]
    tm = ROUTE_TM
    tok = lambda i, *_: (i, 0)
    grid_spec = pltpu.PrefetchScalarGridSpec(
        num_scalar_prefetch=2,
        grid=(t // tm,),
        in_specs=[
            pl.BlockSpec((tm, D_MODEL), tok),
            pl.BlockSpec((tm, TOP_K), tok),
            pl.BlockSpec((tm, TOP_K), tok),
            pl.BlockSpec((tm, TOP_K), tok),
            pl.BlockSpec(memory_space=pl.ANY),
            pl.BlockSpec((1, D_MODEL), lambda i, *_: (0, 0)),
        ],
        out_specs=pl.BlockSpec((tm, D_MODEL), tok),
        scratch_shapes=[
            pltpu.VMEM((2, N_EXPERTS * ROUTE_RP, HALF), U32),
            pltpu.VMEM((ROUTE_RP, HALF), U32),
            pltpu.VMEM((tm, D_MODEL), F32),
            pltpu.SemaphoreType.DMA((2,)),
            pltpu.SemaphoreType.DMA(()),
        ],
    )
    return pl.pallas_call(
        _combine_kernel,
        grid_spec=grid_spec,
        out_shape=jax.ShapeDtypeStruct((t, D_MODEL), F32),
        compiler_params=pltpu.CompilerParams(
            dimension_semantics=("arbitrary",), vmem_limit_bytes=_vmem_limit(32 << 20)),
        name="combine",
    )(plan["tbase"], plan["cnt"], h, idx_tm, loc_tm, gate_tm, ys, g_final.reshape(1, D_MODEL))


def _moe(h, xn2, idx_t, gate_t, w_gate, b_gate, w_up, b_up, w_down, b_down, g_final):
    t = h.shape[0]
    plan = _route_plan(idx_t, t)
    xs, loc_t = _dispatch(plan, idx_t, xn2)
    ys = _experts(plan, xs, w_gate, b_gate, w_up, b_up, w_down, b_down)
    return _combine(plan, h, ys, idx_t.T, loc_t.T, gate_t.T, g_final)


def kernel(x, g_mix, w_in, b_in, conv_w, conv_b, ln_g, ln_b, rpb, w_out, b_out, g_ffn, w_router,
           b_router, w_gate, b_gate, w_up, b_up, w_down, b_down, g_final):
    bsz, seq, d = x.shape
    assert bsz == 1 and d == D_MODEL and g_mix.shape[0] == 1
    assert seq % (GRID_W * WIN_H) == 0
    x2 = x.reshape(seq, d)
    u, q, k, v = _inproj(x2, g_mix[0], w_in[0], b_in[0])
    conv_out = _conv(u, conv_w[0], conv_b[0], ln_g[0], ln_b[0])
    na_out = _attention(q, k, v, rpb[0])
    h, xn2, idx_t, gate_t = _outproj_router(conv_out, na_out, x2, w_out[0], b_out[0], g_ffn[0],
                                            w_router[0], b_router[0])
    out = _moe(h, xn2, idx_t, gate_t, w_gate[0], b_gate[0], w_up[0], b_up[0], w_down[0], b_down[0],
               g_final)
    return out.reshape(bsz, seq, d)
```

```python
import functools

import numpy as np
import jax
import jax.numpy as jnp
from jax import lax
from jax.experimental import pallas as pl
from jax.experimental.pallas import tpu as pltpu

F32 = jnp.float32
BF16 = jnp.bfloat16
U32 = jnp.uint32
I32 = jnp.int32

D_MODEL = 1024
GRID_W = 64
D_CONV = 512
CONV_K = 31
NA_HEADS = 8
NA_HEAD_DIM = 64
D_NA = NA_HEADS * NA_HEAD_DIM
D_IN = 2 * D_CONV + 3 * D_NA
WIN_H = 8
WIN_W = 16
N_EXPERTS = 32
TOP_K = 4
D_FF = D_MODEL
SWIGLU_LIMIT = 7.0
SWIGLU_ALPHA = 1.702
EPS = 1e-5
MASK_VALUE = -1e30

V7X_VMEM_BYTES = 64 * 1024 * 1024
LANES = 128

PROJ_TM = 512
CONV_TT = 256
CONV_HALO = 16
CONV_CH = 32
ATTN_RB = 16
ROUTE_TM = 128
SUBLANES = 8
ROUTE_RP = 40
EXP_BM = 256
HALF = D_MODEL // 2


def _vmem_limit(nbytes):
    return int(min(nbytes, V7X_VMEM_BYTES - 6 * 1024 * 1024))


def _pack_rows(x):
    lo = lax.bitcast_convert_type(x[:, :HALF], U32) >> 16
    hi = lax.bitcast_convert_type(x[:, HALF:], U32) & jnp.uint32(0xFFFF0000)
    return lo | hi


def _unpack_rows(w):
    lo = lax.bitcast_convert_type(w << 16, F32)
    hi = lax.bitcast_convert_type(w & jnp.uint32(0xFFFF0000), F32)
    return jnp.concatenate([lo, hi], axis=1).astype(BF16)


def _inproj_kernel(x_ref, g_ref, w_ref, b_ref, u_ref, q_ref, k_ref, v_ref):
    x = x_ref[...]
    ms = jnp.mean(x * x, axis=-1, keepdims=True)
    xn = (x * lax.rsqrt(ms + EPS) * g_ref[...]).astype(BF16)

    def proj(c):
        sl = slice(c * D_CONV, (c + 1) * D_CONV)
        return jnp.dot(xn, w_ref[:, sl], preferred_element_type=F32) + b_ref[:, sl]

    u_ref[...] = proj(0) * jax.nn.sigmoid(proj(1))
    q_ref[...] = (proj(2) * (NA_HEAD_DIM ** -0.5)).astype(BF16)
    k_ref[...] = proj(3).astype(BF16)
    v_ref[...] = proj(4).astype(BF16)


def _inproj(x2, g_mix, w_in, b_in):
    t = x2.shape[0]
    tm = min(PROJ_TM, t)
    tok = lambda i: (i, 0)
    const = lambda i: (0, 0)
    return pl.pallas_call(
        _inproj_kernel,
        grid=(t // tm,),
        in_specs=[
            pl.BlockSpec((tm, D_MODEL), tok),
            pl.BlockSpec((1, D_MODEL), const),
            pl.BlockSpec((D_MODEL, D_IN), const),
            pl.BlockSpec((1, D_IN), const),
        ],
        out_specs=[
            pl.BlockSpec((tm, D_CONV), tok),
            pl.BlockSpec((tm, D_NA), tok),
            pl.BlockSpec((tm, D_NA), tok),
            pl.BlockSpec((tm, D_NA), tok),
        ],
        out_shape=[
            jax.ShapeDtypeStruct((t, D_CONV), F32),
            jax.ShapeDtypeStruct((t, D_NA), BF16),
            jax.ShapeDtypeStruct((t, D_NA), BF16),
            jax.ShapeDtypeStruct((t, D_NA), BF16),
        ],
        compiler_params=pltpu.CompilerParams(
            dimension_semantics=("arbitrary",), vmem_limit_bytes=_vmem_limit(48 << 20)),
        name="inproj",
    )(x2, g_mix.reshape(1, D_MODEL), w_in.astype(BF16), b_in.reshape(1, D_IN))


def _conv_kernel(prev_ref, cur_ref, next_ref, w_ref, cb_ref, lg_ref, lb_ref, o_ref, ext_ref, sh_ref):
    i = pl.program_id(0)
    n = pl.num_programs(0)
    tt = cur_ref.shape[0]
    ext_ref[0:CONV_HALO, :] = jnp.where(i > 0, prev_ref[...], 0.0)
    ext_ref[CONV_HALO:CONV_HALO + tt, :] = cur_ref[...]
    ext_ref[CONV_HALO + tt:2 * CONV_HALO + tt, :] = jnp.where(i < n - 1, next_ref[...], 0.0)
    base = CONV_HALO - CONV_K // 2
    span = (CONV_K - 1) // SUBLANES * SUBLANES
    for b in range(SUBLANES):
        sh_ref[b] = ext_ref[base + b:base + b + tt + span, :]
    for c in range(tt // CONV_CH):
        acc = jnp.zeros((CONV_CH, D_CONV), F32)
        for b in range(SUBLANES):
            for a in range((CONV_K - 1 - b) // SUBLANES + 1):
                r = c * CONV_CH + SUBLANES * a
                acc = acc + sh_ref[b, r:r + CONV_CH, :] * w_ref[SUBLANES * a + b:SUBLANES * a + b + 1, :]
        acc = acc + cb_ref[...]
        mu = jnp.mean(acc, axis=-1, keepdims=True)
        d = acc - mu
        var = jnp.mean(d * d, axis=-1, keepdims=True)
        un = d * lax.rsqrt(var + EPS) * lg_ref[...] + lb_ref[...]
        o_ref[c * CONV_CH:(c + 1) * CONV_CH, :] = (un * jax.nn.sigmoid(un)).astype(o_ref.dtype)


def _conv(u, conv_w, conv_b, ln_g, ln_b):
    t = u.shape[0]
    tt = min(CONV_TT, t)
    hb = tt // CONV_HALO
    nhb = t // CONV_HALO
    const = lambda i: (0, 0)
    w = jnp.zeros((CONV_K + 1, D_CONV), F32).at[:CONV_K].set(conv_w)
    return pl.pallas_call(
        _conv_kernel,
        grid=(t // tt,),
        in_specs=[
            pl.BlockSpec((CONV_HALO, D_CONV), lambda i: (jnp.maximum(i * hb - 1, 0), 0)),
            pl.BlockSpec((tt, D_CONV), lambda i: (i, 0)),
            pl.BlockSpec((CONV_HALO, D_CONV), lambda i: (jnp.minimum((i + 1) * hb, nhb - 1), 0)),
            pl.BlockSpec((CONV_K + 1, D_CONV), const),
            pl.BlockSpec((1, D_CONV), const),
            pl.BlockSpec((1, D_CONV), const),
            pl.BlockSpec((1, D_CONV), const),
        ],
        out_specs=pl.BlockSpec((tt, D_CONV), lambda i: (i, 0)),
        out_shape=jax.ShapeDtypeStruct((t, D_CONV), BF16),
        scratch_shapes=[pltpu.VMEM((tt + 2 * CONV_HALO, D_CONV), F32),
                        pltpu.VMEM((SUBLANES, tt + (CONV_K - 1) // SUBLANES * SUBLANES, D_CONV), F32)],
        compiler_params=pltpu.CompilerParams(dimension_semantics=("arbitrary",)),
        name="conv",
    )(u, u, u, w, conv_b.reshape(1, D_CONV), ln_g.reshape(1, D_CONV), ln_b.reshape(1, D_CONV))


def _bias_table(rpb):
    col = np.arange(GRID_W)
    c0 = np.clip(col - WIN_W // 2, 0, GRID_W - WIN_W)
    valid = (col[None, :] >= c0[:, None]) & (col[None, :] < c0[:, None] + WIN_W)
    dcol = np.clip(col[None, :] - col[:, None], -(WIN_W - 1), WIN_W - 1) + (WIN_W - 1)
    onehot = ((dcol[:, :, None] == np.arange(2 * WIN_W - 1)) & valid[:, :, None]).astype(np.float32)
    tcol = jnp.einsum("hdj,ckj->hdck", rpb.astype(F32), onehot, precision=lax.Precision.HIGHEST)
    tcol = jnp.where(valid[None, None], tcol, MASK_VALUE)
    b = jnp.stack([tcol[:, WIN_H - 1 - c:2 * WIN_H - 1 - c] for c in range(WIN_H)], axis=1)
    b = jnp.transpose(b, (0, 1, 3, 2, 4))
    return b.reshape(NA_HEADS, WIN_H, GRID_W, WIN_H * GRID_W)


def _attn_kernel(q_ref, k_ref, v_ref, bt_ref, o_ref, s_ref, *, rb, rows):
    jb = pl.program_id(1)
    lane = lax.broadcasted_iota(jnp.int32, (GRID_W, LANES), 1)
    first_head = lane < NA_HEAD_DIM
    nkeys = WIN_H * GRID_W

    def window(r):
        i = jb * rb + r
        r0 = jnp.clip(i - WIN_H // 2, 0, rows - WIN_H)
        return i - r0, pl.multiple_of(r0 * GRID_W, GRID_W)

    def scores(r):
        _, start = window(r)
        q2 = q_ref[pl.ds(pl.multiple_of(r * GRID_W, GRID_W), GRID_W), :]
        zero = jnp.zeros_like(q2)
        qst = jnp.concatenate([jnp.where(first_head, q2, zero), jnp.where(first_head, zero, q2)], axis=0)
        ks = k_ref[pl.ds(start, nkeys), :]
        return lax.dot_general(qst, ks, (((1,), (1,)), ((), ())), preferred_element_type=F32)

    def finish(r, s):
        c, start = window(r)
        vs = v_ref[pl.ds(start, nkeys), :]
        bias = jnp.concatenate([bt_ref[0, c], bt_ref[1, c]], axis=0)
        s = jnp.where(bias > 0.5 * MASK_VALUE, s + bias, MASK_VALUE)
        m = jnp.max(s, axis=-1, keepdims=True)
        p = jnp.exp(s - m)
        l = jnp.sum(p, axis=-1, keepdims=True)
        o = jnp.dot(p.astype(BF16), vs, preferred_element_type=F32) / l
        o2 = jnp.where(first_head, o[:GRID_W], o[GRID_W:])
        o_ref[pl.ds(pl.multiple_of(r * GRID_W, GRID_W), GRID_W), :] = o2.astype(o_ref.dtype)

    s_ref[0] = scores(0)

    def pair(p, carry):
        r = 2 * p
        s_ref[1] = scores(r + 1)
        finish(r, s_ref[0])
        s_ref[0] = scores(jnp.minimum(r + 2, rb - 1))
        finish(r + 1, s_ref[1])
        return carry

    lax.fori_loop(0, rb // 2, pair, 0)


def _attention(q, k, v, rpb):
    t = q.shape[0]
    rows = t // GRID_W
    rb = min(ATTN_RB, rows)
    npairs = D_NA // LANES
    bt = _bias_table(rpb)
    kern = functools.partial(_attn_kernel, rb=rb, rows=rows)
    return pl.pallas_call(
        kern,
        grid=(npairs, rows // rb),
        in_specs=[
            pl.BlockSpec((rb * GRID_W, LANES), lambda p, j: (j, p)),
            pl.BlockSpec((t, LANES), lambda p, j: (0, p)),
            pl.BlockSpec((t, LANES), lambda p, j: (0, p)),
            pl.BlockSpec((2, WIN_H, GRID_W, WIN_H * GRID_W), lambda p, j: (p, 0, 0, 0)),
        ],
        out_specs=pl.BlockSpec((rb * GRID_W, LANES), lambda p, j: (j, p)),
        out_shape=jax.ShapeDtypeStruct((t, D_NA), BF16),
        scratch_shapes=[pltpu.VMEM((2, 2 * GRID_W, WIN_H * GRID_W), F32)],
        compiler_params=pltpu.CompilerParams(
            dimension_semantics=("arbitrary", "arbitrary"), vmem_limit_bytes=_vmem_limit(40 << 20)),
        name="attention",
    )(q, k, v, bt)


def _outproj_kernel(conv_ref, na_ref, x_ref, wo_ref, bo_ref, g_ref, wr_ref, br_ref,
                    h_ref, xn_ref, idx_ref, gate_ref):
    mixed = jnp.concatenate([conv_ref[...], na_ref[...]], axis=-1)
    h = x_ref[...] + jnp.dot(mixed, wo_ref[...], preferred_element_type=F32) + bo_ref[...]
    h_ref[...] = h
    ms = jnp.mean(h * h, axis=-1, keepdims=True)
    xn = h * lax.rsqrt(ms + EPS) * g_ref[...]
    xn_ref[...] = xn.astype(xn_ref.dtype)
    logits = lax.dot_general(wr_ref[...], xn, (((1,), (1,)), ((), ())),
                             preferred_element_type=F32, precision=lax.Precision.HIGHEST) + br_ref[...]
    ids = lax.broadcasted_iota(jnp.int32, logits.shape, 0)
    vals, sels = [], []
    l = logits
    for _ in range(TOP_K):
        m = jnp.max(l, axis=0, keepdims=True)
        sel = jnp.min(jnp.where(l == m, ids, N_EXPERTS), axis=0, keepdims=True)
        vals.append(m)
        sels.append(sel)
        l = jnp.where(ids == sel, -jnp.inf, l)
    es = [jnp.exp(vk - vals[0]) for vk in vals]
    tot = es[0] + es[1] + es[2] + es[3]
    idx_ref[...] = jnp.concatenate(sels, axis=0)
    gate_ref[...] = jnp.concatenate([e / tot for e in es], axis=0)


def _outproj_router(conv_out, na_out, x2, w_out, b_out, g_ffn, w_router, b_router):
    t = x2.shape[0]
    tm = min(PROJ_TM, t)
    tok = lambda i: (i, 0)
    const = lambda i: (0, 0)
    return pl.pallas_call(
        _outproj_kernel,
        grid=(t // tm,),
        in_specs=[
            pl.BlockSpec((tm, D_CONV), tok),
            pl.BlockSpec((tm, D_NA), tok),
            pl.BlockSpec((tm, D_MODEL), tok),
            pl.BlockSpec((D_MODEL, D_MODEL), const),
            pl.BlockSpec((1, D_MODEL), const),
            pl.BlockSpec((1, D_MODEL), const),
            pl.BlockSpec((N_EXPERTS, D_MODEL), const),
            pl.BlockSpec((N_EXPERTS, 1), const),
        ],
        out_specs=[
            pl.BlockSpec((tm, D_MODEL), tok),
            pl.BlockSpec((tm, D_MODEL), tok),
            pl.BlockSpec((TOP_K, tm), lambda i: (0, i)),
            pl.BlockSpec((TOP_K, tm), lambda i: (0, i)),
        ],
        out_shape=[
            jax.ShapeDtypeStruct((t, D_MODEL), F32),
            jax.ShapeDtypeStruct((t, D_MODEL), BF16),
            jax.ShapeDtypeStruct((TOP_K, t), jnp.int32),
            jax.ShapeDtypeStruct((TOP_K, t), F32),
        ],
        compiler_params=pltpu.CompilerParams(
            dimension_semantics=("arbitrary",), vmem_limit_bytes=_vmem_limit(48 << 20)),
        name="outproj_router",
    )(conv_out, na_out, x2, w_out.astype(BF16), b_out.reshape(1, D_MODEL), g_ffn.reshape(1, D_MODEL),
      w_router.T, b_router.reshape(N_EXPERTS, 1))


def _route_plan(idx_t, t):
    nt = t // ROUTE_TM
    experts = jnp.arange(N_EXPERTS, dtype=I32)
    onehot = idx_t.reshape(TOP_K, nt, ROUTE_TM, 1) == experts
    cnt = jnp.sum(onehot, axis=(0, 2), dtype=I32)
    sizes = jnp.sum(cnt, axis=0)
    padded = (sizes + ROUTE_RP + EXP_BM - 1) // EXP_BM * EXP_BM
    pad_ends = jnp.cumsum(padded)
    pad_off = pad_ends - padded
    tbase = pad_off[None, :] + jnp.cumsum(cnt, axis=0) - cnt
    cap = _sorted_rows(t)
    nb = cap // EXP_BM
    n_used = pad_ends[N_EXPERTS - 1] // EXP_BM
    blk_src = jnp.minimum(jnp.arange(nb, dtype=I32), n_used - 1)
    blk_exp = jnp.minimum(
        jnp.sum(blk_src[:, None] * EXP_BM >= pad_ends[None, :], axis=1, dtype=I32), N_EXPERTS - 1)
    multi = jnp.any(tbase % SUBLANES + cnt > ROUTE_RP, axis=1).astype(I32)
    return dict(cnt=cnt.reshape(-1), tbase=tbase.reshape(-1).astype(I32), multi=multi,
                last_blk=(pad_ends - EXP_BM).astype(I32), two_blk=(padded >= 2 * EXP_BM).astype(I32),
                blk_src=blk_src, blk_exp=blk_exp, n_used=n_used.reshape(1).astype(I32))


def _sorted_rows(t):
    cap = t * TOP_K + N_EXPERTS * (EXP_BM + ROUTE_RP)
    return (cap + EXP_BM - 1) // EXP_BM * EXP_BM


def _chunk_geometry(tb, n):
    head = tb % SUBLANES
    start = pl.multiple_of(tb - head, SUBLANES)
    nchunks = (head + n + ROUTE_RP - 1) // ROUTE_RP
    return start, head, nchunks


def _dispatch_kernel(tb_ref, cnt_ref, mt_ref, lb_ref, two_ref, nu_ref, idx_ref, tbv_ref, xn_ref, xs_hbm, loc_ref,
                     stage, ostage, zbuf, head_ref, cs_ref, ms_ref, sem, osem, zsem):
    i = pl.program_id(0)
    nt = pl.num_programs(0)
    slot = i % 2
    tm = idx_ref.shape[1]
    groups = ROUTE_RP // SUBLANES

    def geometry(e):
        return _chunk_geometry(tb_ref[i * N_EXPERTS + e], cnt_ref[i * N_EXPERTS + e])

    def chunk_copy(e, s):
        start, _, _ = geometry(e)
        return pltpu.make_async_copy(stage.at[s, pl.ds(e * ROUTE_RP, ROUTE_RP)],
                                     xs_hbm.at[pl.ds(start, ROUTE_RP)], sem.at[s])

    def zero_copy(start):
        return pltpu.make_async_copy(zbuf, xs_hbm.at[pl.ds(pl.multiple_of(start, EXP_BM), EXP_BM)], zsem)

    @pl.when(i == 0)
    def _():
        zbuf[...] = jnp.zeros(zbuf.shape, U32)
        head_ref[...] = jnp.zeros(head_ref.shape, U32)
        for phase in ("start", "wait"):
            for e in range(N_EXPERTS):
                getattr(zero_copy(lb_ref[e]), phase)()

                @pl.when(two_ref[e] > 0)
                def _():
                    getattr(zero_copy(lb_ref[e] - EXP_BM), phase)()

        def unused_block(bi, carry):
            cp = zero_copy(bi * EXP_BM)
            cp.start()
            cp.wait()
            return carry
        lax.fori_loop(nu_ref[0], xs_hbm.shape[0] // EXP_BM, unused_block, 0)

    idx = idx_ref[...]
    eio = lax.broadcasted_iota(I32, (N_EXPERTS, tm), 0)
    member = jnp.zeros((N_EXPERTS, tm), F32)
    for k in range(TOP_K):
        member = member + (idx[k:k + 1, :] == eio).astype(F32)
    tri = (lax.broadcasted_iota(I32, (tm, tm), 0) < lax.broadcasted_iota(I32, (tm, tm), 1)).astype(BF16)
    rank = jnp.dot(member.astype(BF16), tri, preferred_element_type=F32)
    pos = rank + (tbv_ref[0] % SUBLANES).astype(F32)
    loc_ref[...] = jnp.concatenate(
        [jnp.sum(jnp.where(idx[k:k + 1, :] == eio, pos, 0.0), axis=0, keepdims=True) for k in range(TOP_K)],
        axis=0).astype(I32)
    cs_ref[...] = pos
    ms_ref[...] = member

    jio = lax.broadcasted_iota(I32, (ROUTE_RP, tm), 0).astype(F32)
    sel = [jnp.where((jio == pos[e:e + 1, :]) & (member[e:e + 1, :] > 0.0), 1.0, 0.0).astype(BF16)
           for e in range(N_EXPERTS)]
    rows = jnp.dot(jnp.concatenate(sel, axis=0), xn_ref[...], preferred_element_type=F32)

    @pl.when(i > 0)
    def _():
        for e in range(N_EXPERTS):
            chunk_copy(e, 1 - slot).wait()

    stage[slot] = _pack_rows(rows)
    sub = lax.broadcasted_iota(I32, (SUBLANES, HALF), 0)
    for e in range(N_EXPERTS):
        _, head, _ = geometry(e)
        first = pl.ds(e * ROUTE_RP, SUBLANES)
        stage[slot, first, :] = jnp.where(sub < head, head_ref[e], stage[slot, first, :])
        chunk_copy(e, slot).start()
        g = jnp.minimum((head + cnt_ref[i * N_EXPERTS + e]) // SUBLANES, groups - 1)
        head_ref[e] = stage[slot, pl.ds(pl.multiple_of(e * ROUTE_RP + g * SUBLANES, SUBLANES), SUBLANES), :]

    def per_expert(e, carry):
        start, head, nchunks = geometry(e)
        end = head + cnt_ref[i * N_EXPERTS + e]

        def per_chunk(c, carry2):
            want = jio + (c * ROUTE_RP).astype(F32)
            pick = jnp.where((want == cs_ref[pl.ds(e, 1), :]) & (ms_ref[pl.ds(e, 1), :] > 0.0), 1.0, 0.0)
            ostage[...] = _pack_rows(jnp.dot(pick.astype(BF16), xn_ref[...], preferred_element_type=F32))
            cp = pltpu.make_async_copy(
                ostage, xs_hbm.at[pl.ds(pl.multiple_of(start + c * ROUTE_RP, SUBLANES), ROUTE_RP)], osem)
            cp.start()
            cp.wait()

            @pl.when(end // ROUTE_RP == c)
            def _():
                g = (end - c * ROUTE_RP) // SUBLANES
                head_ref[e] = ostage[pl.ds(pl.multiple_of(g * SUBLANES, SUBLANES), SUBLANES), :]
            return carry2
        return lax.fori_loop(1, nchunks, per_chunk, carry)

    @pl.when(mt_ref[i] > 0)
    def _():
        lax.fori_loop(0, N_EXPERTS, per_expert, 0)

    @pl.when(i == nt - 1)
    def _():
        for e in range(N_EXPERTS):
            chunk_copy(e, slot).wait()


def _dispatch(plan, idx_t, xn2):
    t = xn2.shape[0]
    tm = ROUTE_TM
    cap = _sorted_rows(t)
    grid_spec = pltpu.PrefetchScalarGridSpec(
        num_scalar_prefetch=6,
        grid=(t // tm,),
        in_specs=[
            pl.BlockSpec((TOP_K, tm), lambda i, *_: (0, i)),
            pl.BlockSpec((1, N_EXPERTS, 1), lambda i, *_: (i, 0, 0)),
            pl.BlockSpec((tm, D_MODEL), lambda i, *_: (i, 0)),
        ],
        out_specs=[
            pl.BlockSpec(memory_space=pl.ANY),
            pl.BlockSpec((TOP_K, tm), lambda i, *_: (0, i)),
        ],
        scratch_shapes=[
            pltpu.VMEM((2, N_EXPERTS * ROUTE_RP, HALF), U32),
            pltpu.VMEM((ROUTE_RP, HALF), U32),
            pltpu.VMEM((EXP_BM, HALF), U32),
            pltpu.VMEM((N_EXPERTS, SUBLANES, HALF), U32),
            pltpu.VMEM((N_EXPERTS, tm), F32),
            pltpu.VMEM((N_EXPERTS, tm), F32),
            pltpu.SemaphoreType.DMA((2,)),
            pltpu.SemaphoreType.DMA(()),
            pltpu.SemaphoreType.DMA(()),
        ],
    )
    return pl.pallas_call(
        _dispatch_kernel,
        grid_spec=grid_spec,
        out_shape=[
            jax.ShapeDtypeStruct((cap, HALF), U32),
            jax.ShapeDtypeStruct((TOP_K, t), I32),
        ],
        compiler_params=pltpu.CompilerParams(
            dimension_semantics=("arbitrary",), vmem_limit_bytes=_vmem_limit(32 << 20)),
        name="dispatch",
    )(plan["tbase"], plan["cnt"], plan["multi"], plan["last_blk"], plan["two_blk"], plan["n_used"], idx_t,
      plan["tbase"].reshape(t // tm, N_EXPERTS, 1), xn2)


def _expert_kernel(be_ref, src_ref, nu_ref, xs_ref, wg_hbm, wu_hbm, wd_hbm, bg_ref, bu_ref, bd_ref,
                   ys_ref, wf32, wbf, wsem):
    b = pl.program_id(0)
    e = be_ref[b]

    def weight_copies(expert, par):
        return [pltpu.make_async_copy(w.at[expert], wf32.at[par, m], wsem.at[par])
                for m, w in enumerate((wg_hbm, wu_hbm, wd_hbm))]

    @pl.when(b == 0)
    def _():
        for cp in weight_copies(e, e % 2):
            cp.start()

    @pl.when(b < nu_ref[0])
    def _():
        @pl.when((b == 0) | (e != be_ref[jnp.maximum(b - 1, 0)]))
        def _():
            par = e % 2
            for cp in weight_copies(e, par):
                cp.wait()

            @pl.when(e + 1 < N_EXPERTS)
            def _():
                for cp in weight_copies(e + 1, 1 - par):
                    cp.start()

            for m in range(3):
                wbf[m] = wf32[par, m].astype(BF16)

        x = _unpack_rows(xs_ref[...])
        gt = jnp.minimum(jnp.dot(x, wbf[0], preferred_element_type=F32) + bg_ref[0], SWIGLU_LIMIT)
        up = jnp.clip(jnp.dot(x, wbf[1], preferred_element_type=F32) + bu_ref[0],
                      -SWIGLU_LIMIT, SWIGLU_LIMIT)
        hdn = (up + 1.0) * (gt * jax.nn.sigmoid(SWIGLU_ALPHA * gt))
        y = jnp.dot(hdn.astype(BF16), wbf[2], preferred_element_type=F32) + bd_ref[0]
        ys_ref[...] = _pack_rows(y.astype(BF16).astype(F32))

    @pl.when(b >= nu_ref[0])
    def _():
        ys_ref[...] = jnp.zeros(ys_ref.shape, U32)


def _experts(plan, xs, w_gate, b_gate, w_up, b_up, w_down, b_down):
    cap = xs.shape[0]
    nb = cap // EXP_BM
    rows = lambda b, be, src, nu: (src[b], 0)
    wsel = lambda b, be, src, nu: (be[b], 0, 0)
    wspec = pl.BlockSpec(memory_space=pl.ANY)
    bspec = pl.BlockSpec((1, 1, D_FF), wsel)
    grid_spec = pltpu.PrefetchScalarGridSpec(
        num_scalar_prefetch=3,
        grid=(nb,),
        in_specs=[pl.BlockSpec((EXP_BM, HALF), rows), wspec, wspec, wspec, bspec, bspec, bspec],
        out_specs=pl.BlockSpec((EXP_BM, HALF), lambda b, be, src, nu: (b, 0)),
        scratch_shapes=[pltpu.VMEM((2, 3, D_MODEL, D_FF), F32),
                        pltpu.VMEM((3, D_MODEL, D_FF), BF16),
                        pltpu.SemaphoreType.DMA((2,))],
    )
    return pl.pallas_call(
        _expert_kernel,
        grid_spec=grid_spec,
        out_shape=jax.ShapeDtypeStruct((cap, HALF), U32),
        compiler_params=pltpu.CompilerParams(
            dimension_semantics=("arbitrary",), vmem_limit_bytes=_vmem_limit(56 << 20)),
        name="experts",
    )(plan["blk_exp"], plan["blk_src"], plan["n_used"], xs, w_gate, w_up, w_down,
      b_gate.reshape(N_EXPERTS, 1, D_FF), b_up.reshape(N_EXPERTS, 1, D_FF),
      b_down.reshape(N_EXPERTS, 1, D_MODEL))


def _combine_kernel(tb_ref, cnt_ref, mt_ref, h_ref, idx_ref, loc_ref, gate_ref, ys_hbm, g_ref, o_ref,
                    ybuf, obuf, acc_ref, sem, osem):
    i = pl.program_id(0)
    nt = pl.num_programs(0)
    slot = i % 2
    tm = h_ref.shape[0]

    def geometry(step, e):
        return _chunk_geometry(tb_ref[step * N_EXPERTS + e], cnt_ref[step * N_EXPERTS + e])

    def chunk_copy(step, e, s):
        start, _, _ = geometry(step, e)
        return pltpu.make_async_copy(ys_hbm.at[pl.ds(start, ROUTE_RP)],
                                     ybuf.at[s, pl.ds(e * ROUTE_RP, ROUTE_RP)], sem.at[s])

    @pl.when(i == 0)
    def _():
        for e in range(N_EXPERTS):
            chunk_copy(0, e, 0).start()

    @pl.when(i + 1 < nt)
    def _():
        for e in range(N_EXPERTS):
            chunk_copy(i + 1, e, 1 - slot).start()

    idx = idx_ref[...]
    loc = loc_ref[...]
    gate = gate_ref[...]
    g_hi = gate.astype(BF16).astype(F32)
    g_lo = gate - g_hi

    def gate_matrix(ncols, col_of):
        colio = lax.broadcasted_iota(I32, (tm, ncols), 1)
        hi = jnp.zeros((tm, ncols), F32)
        lo = jnp.zeros((tm, ncols), F32)
        for k in range(TOP_K):
            hit = colio == col_of[:, k:k + 1]
            hi = jnp.where(hit, g_hi[:, k:k + 1], hi)
            lo = jnp.where(hit, g_lo[:, k:k + 1], lo)
        return jnp.concatenate([hi, lo], axis=0).astype(BF16)

    for e in range(N_EXPERTS):
        chunk_copy(i, e, slot).wait()
    col = jnp.where(loc < ROUTE_RP, idx * ROUTE_RP + loc, -1)
    z = jnp.dot(gate_matrix(N_EXPERTS * ROUTE_RP, col), _unpack_rows(ybuf[slot]),
                preferred_element_type=F32)
    acc_ref[...] = z[:tm] + z[tm:]

    def per_expert(e, carry):
        start, _, nchunks = geometry(i, e)

        def per_chunk(c, carry2):
            cp = pltpu.make_async_copy(
                ys_hbm.at[pl.ds(pl.multiple_of(start + c * ROUTE_RP, SUBLANES), ROUTE_RP)], obuf, osem)
            cp.start()
            cp.wait()
            ccol = jnp.where(idx == e, loc - c * ROUTE_RP, -1)
            zc = jnp.dot(gate_matrix(ROUTE_RP, ccol), _unpack_rows(obuf[...]), preferred_element_type=F32)
            acc_ref[...] += zc[:tm] + zc[tm:]
            return carry2
        return lax.fori_loop(1, nchunks, per_chunk, carry)

    @pl.when(mt_ref[i] > 0)
    def _():
        lax.fori_loop(0, N_EXPERTS, per_expert, 0)

    out = h_ref[...] + acc_ref[...]
    ms = jnp.mean(out * out, axis=-1, keepdims=True)
    o_ref[...] = out * lax.rsqrt(ms + EPS) * g_ref[...]


def _combine(plan, h, ys, idx_tm, loc_tm, gate_tm, g_final):
    t = h.shape[0]
    tm = ROUTE_TM
    tok = lambda i, *_: (i, 0)
    grid_spec = pltpu.PrefetchScalarGridSpec(
        num_scalar_prefetch=3,
        grid=(t // tm,),
        in_specs=[
            pl.BlockSpec((tm, D_MODEL), tok),
            pl.BlockSpec((tm, TOP_K), tok),
            pl.BlockSpec((tm, TOP_K), tok),
            pl.BlockSpec((tm, TOP_K), tok),
            pl.BlockSpec(memory_space=pl.ANY),
            pl.BlockSpec((1, D_MODEL), lambda i, *_: (0, 0)),
        ],
        out_specs=pl.BlockSpec((tm, D_MODEL), tok),
        scratch_shapes=[
            pltpu.VMEM((2, N_EXPERTS * ROUTE_RP, HALF), U32),
            pltpu.VMEM((ROUTE_RP, HALF), U32),
            pltpu.VMEM((tm, D_MODEL), F32),
            pltpu.SemaphoreType.DMA((2,)),
            pltpu.SemaphoreType.DMA(()),
        ],
    )
    return pl.pallas_call(
        _combine_kernel,
        grid_spec=grid_spec,
        out_shape=jax.ShapeDtypeStruct((t, D_MODEL), F32),
        compiler_params=pltpu.CompilerParams(
            dimension_semantics=("arbitrary",), vmem_limit_bytes=_vmem_limit(32 << 20)),
        name="combine",
    )(plan["tbase"], plan["cnt"], plan["multi"], h, idx_tm, loc_tm, gate_tm, ys,
      g_final.reshape(1, D_MODEL))


def _moe(h, xn2, idx_t, gate_t, w_gate, b_gate, w_up, b_up, w_down, b_down, g_final):
    t = h.shape[0]
    plan = _route_plan(idx_t, t)
    xs, loc_t = _dispatch(plan, idx_t, xn2)
    ys = _experts(plan, xs, w_gate, b_gate, w_up, b_up, w_down, b_down)
    return _combine(plan, h, ys, idx_t.T, loc_t.T, gate_t.T, g_final)


def kernel(x, g_mix, w_in, b_in, conv_w, conv_b, ln_g, ln_b, rpb, w_out, b_out, g_ffn, w_router,
           b_router, w_gate, b_gate, w_up, b_up, w_down, b_down, g_final):
    bsz, seq, d = x.shape
    assert bsz == 1 and d == D_MODEL and g_mix.shape[0] == 1
    assert seq % (GRID_W * WIN_H) == 0
    x2 = x.reshape(seq, d)
    u, q, k, v = _inproj(x2, g_mix[0], w_in[0], b_in[0])
    conv_out = _conv(u, conv_w[0], conv_b[0], ln_g[0], ln_b[0])
    na_out = _attention(q, k, v, rpb[0])
    h, xn2, idx_t, gate_t = _outproj_router(conv_out, na_out, x2, w_out[0], b_out[0], g_ffn[0],
                                            w_router[0], b_router[0])
    out = _moe(h, xn2, idx_t, gate_t, w_gate[0], b_gate[0], w_up[0], b_up[0], w_down[0], b_down[0],
               g_final)
    return out.reshape(bsz, seq, d)
```

```python
import functools

import numpy as np
import jax
import jax.numpy as jnp
from jax import lax
from jax.experimental import pallas as pl
from jax.experimental.pallas import tpu as pltpu

F32 = jnp.float32
BF16 = jnp.bfloat16
U32 = jnp.uint32
I32 = jnp.int32

D_MODEL = 1024
GRID_W = 64
D_CONV = 512
CONV_K = 31
NA_HEADS = 8
NA_HEAD_DIM = 64
D_NA = NA_HEADS * NA_HEAD_DIM
D_IN = 2 * D_CONV + 3 * D_NA
WIN_H = 8
WIN_W = 16
N_EXPERTS = 32
TOP_K = 4
D_FF = D_MODEL
SWIGLU_LIMIT = 7.0
SWIGLU_ALPHA = 1.702
EPS = 1e-5
MASK_VALUE = -1e30

V7X_VMEM_BYTES = 64 * 1024 * 1024
LANES = 128

PROJ_TM = 512
ROUTER_SUB = 256
CONV_TT = 256
CONV_HALO = 16
CONV_CH = 32
ATTN_RB = 32
ATTN_UNROLL = 4
ROUTE_TM = 128
SUBLANES = 8
ROUTE_RP = 40
EXP_BM = 256
HALF = D_MODEL // 2


def _vmem_limit(nbytes):
    return int(min(nbytes, V7X_VMEM_BYTES - 6 * 1024 * 1024))


def _pack_rows(x):
    lo = lax.bitcast_convert_type(x[:, :HALF], U32) >> 16
    hi = lax.bitcast_convert_type(x[:, HALF:], U32) & jnp.uint32(0xFFFF0000)
    return lo | hi


def _unpack_rows(w):
    lo = lax.bitcast_convert_type(w << 16, F32)
    hi = lax.bitcast_convert_type(w & jnp.uint32(0xFFFF0000), F32)
    return jnp.concatenate([lo, hi], axis=1).astype(BF16)


def _inproj_kernel(x_ref, g_ref, w_ref, b_ref, u_ref, q_ref, k_ref, v_ref):
    x = x_ref[...]
    ms = jnp.mean(x * x, axis=-1, keepdims=True)
    xn = (x * lax.rsqrt(ms + EPS) * g_ref[...]).astype(BF16)

    def proj(c):
        sl = slice(c * D_CONV, (c + 1) * D_CONV)
        return jnp.dot(xn, w_ref[:, sl], preferred_element_type=F32) + b_ref[:, sl]

    u_ref[...] = proj(0) * jax.nn.sigmoid(proj(1))
    q_ref[...] = (proj(2) * (NA_HEAD_DIM ** -0.5)).astype(BF16)
    k_ref[...] = proj(3).astype(BF16)
    v_ref[...] = proj(4).astype(BF16)


def _inproj(x2, g_mix, w_in, b_in):
    t = x2.shape[0]
    tm = min(PROJ_TM, t)
    tok = lambda i: (i, 0)
    const = lambda i: (0, 0)
    return pl.pallas_call(
        _inproj_kernel,
        grid=(t // tm,),
        in_specs=[
            pl.BlockSpec((tm, D_MODEL), tok),
            pl.BlockSpec((1, D_MODEL), const),
            pl.BlockSpec((D_MODEL, D_IN), const),
            pl.BlockSpec((1, D_IN), const),
        ],
        out_specs=[
            pl.BlockSpec((tm, D_CONV), tok),
            pl.BlockSpec((tm, D_NA), tok),
            pl.BlockSpec((tm, D_NA), tok),
            pl.BlockSpec((tm, D_NA), tok),
        ],
        out_shape=[
            jax.ShapeDtypeStruct((t, D_CONV), F32),
            jax.ShapeDtypeStruct((t, D_NA), BF16),
            jax.ShapeDtypeStruct((t, D_NA), BF16),
            jax.ShapeDtypeStruct((t, D_NA), BF16),
        ],
        compiler_params=pltpu.CompilerParams(
            dimension_semantics=("arbitrary",), vmem_limit_bytes=_vmem_limit(48 << 20)),
        name="inproj",
    )(x2, g_mix.reshape(1, D_MODEL), w_in.astype(BF16), b_in.reshape(1, D_IN))


def _conv_kernel(prev_ref, cur_ref, next_ref, w_ref, cb_ref, lg_ref, lb_ref, o_ref, ext_ref, sh_ref):
    i = pl.program_id(0)
    n = pl.num_programs(0)
    tt = cur_ref.shape[0]
    ext_ref[0:CONV_HALO, :] = jnp.where(i > 0, prev_ref[...], 0.0)
    ext_ref[CONV_HALO:CONV_HALO + tt, :] = cur_ref[...]
    ext_ref[CONV_HALO + tt:2 * CONV_HALO + tt, :] = jnp.where(i < n - 1, next_ref[...], 0.0)
    base = CONV_HALO - CONV_K // 2
    span = (CONV_K - 1) // SUBLANES * SUBLANES
    for b in range(SUBLANES):
        sh_ref[b] = ext_ref[base + b:base + b + tt + span, :]
    for c in range(tt // CONV_CH):
        acc = jnp.zeros((CONV_CH, D_CONV), F32)
        for b in range(SUBLANES):
            for a in range((CONV_K - 1 - b) // SUBLANES + 1):
                r = c * CONV_CH + SUBLANES * a
                acc = acc + sh_ref[b, r:r + CONV_CH, :] * w_ref[SUBLANES * a + b:SUBLANES * a + b + 1, :]
        acc = acc + cb_ref[...]
        mu = jnp.mean(acc, axis=-1, keepdims=True)
        d = acc - mu
        var = jnp.mean(d * d, axis=-1, keepdims=True)
        un = d * lax.rsqrt(var + EPS) * lg_ref[...] + lb_ref[...]
        o_ref[c * CONV_CH:(c + 1) * CONV_CH, :] = (un * jax.nn.sigmoid(un)).astype(o_ref.dtype)


def _conv(u, conv_w, conv_b, ln_g, ln_b):
    t = u.shape[0]
    tt = min(CONV_TT, t)
    hb = tt // CONV_HALO
    nhb = t // CONV_HALO
    const = lambda i: (0, 0)
    w = jnp.zeros((CONV_K + 1, D_CONV), F32).at[:CONV_K].set(conv_w)
    return pl.pallas_call(
        _conv_kernel,
        grid=(t // tt,),
        in_specs=[
            pl.BlockSpec((CONV_HALO, D_CONV), lambda i: (jnp.maximum(i * hb - 1, 0), 0)),
            pl.BlockSpec((tt, D_CONV), lambda i: (i, 0)),
            pl.BlockSpec((CONV_HALO, D_CONV), lambda i: (jnp.minimum((i + 1) * hb, nhb - 1), 0)),
            pl.BlockSpec((CONV_K + 1, D_CONV), const),
            pl.BlockSpec((1, D_CONV), const),
            pl.BlockSpec((1, D_CONV), const),
            pl.BlockSpec((1, D_CONV), const),
        ],
        out_specs=pl.BlockSpec((tt, D_CONV), lambda i: (i, 0)),
        out_shape=jax.ShapeDtypeStruct((t, D_CONV), BF16),
        scratch_shapes=[pltpu.VMEM((tt + 2 * CONV_HALO, D_CONV), F32),
                        pltpu.VMEM((SUBLANES, tt + (CONV_K - 1) // SUBLANES * SUBLANES, D_CONV), F32)],
        compiler_params=pltpu.CompilerParams(dimension_semantics=("arbitrary",)),
        name="conv",
    )(u, u, u, w, conv_b.reshape(1, D_CONV), ln_g.reshape(1, D_CONV), ln_b.reshape(1, D_CONV))


def _bias_table(rpb):
    col = np.arange(GRID_W)
    c0 = np.clip(col - WIN_W // 2, 0, GRID_W - WIN_W)
    valid = (col[None, :] >= c0[:, None]) & (col[None, :] < c0[:, None] + WIN_W)
    dcol = np.clip(col[None, :] - col[:, None], -(WIN_W - 1), WIN_W - 1) + (WIN_W - 1)
    onehot = ((dcol[:, :, None] == np.arange(2 * WIN_W - 1)) & valid[:, :, None]).astype(np.float32)
    tcol = jnp.einsum("hdj,ckj->hdck", rpb.astype(F32), onehot, precision=lax.Precision.HIGHEST)
    tcol = jnp.where(valid[None, None], tcol, MASK_VALUE)
    b = jnp.stack([tcol[:, WIN_H - 1 - c:2 * WIN_H - 1 - c] for c in range(WIN_H)], axis=1)
    b = jnp.transpose(b, (0, 1, 3, 2, 4))
    return b.reshape(NA_HEADS, WIN_H, GRID_W, WIN_H * GRID_W)


def _attn_kernel(q_ref, k_ref, v_ref, bt_ref, o_ref, s_ref, *, rb, rows):
    jb = pl.program_id(1)
    lane = lax.broadcasted_iota(jnp.int32, (GRID_W, LANES), 1)
    first_head = lane < NA_HEAD_DIM
    nkeys = WIN_H * GRID_W

    def window(r):
        i = jb * rb + r
        r0 = jnp.clip(i - WIN_H // 2, 0, rows - WIN_H)
        return i - r0, pl.multiple_of(r0 * GRID_W, GRID_W)

    def scores(r):
        _, start = window(r)
        q2 = q_ref[pl.ds(pl.multiple_of(r * GRID_W, GRID_W), GRID_W), :]
        zero = jnp.zeros_like(q2)
        qst = jnp.concatenate([jnp.where(first_head, q2, zero), jnp.where(first_head, zero, q2)], axis=0)
        ks = k_ref[pl.ds(start, nkeys), :]
        return lax.dot_general(qst, ks, (((1,), (1,)), ((), ())), preferred_element_type=F32)

    def finish(r, s):
        c, start = window(r)
        vs = v_ref[pl.ds(start, nkeys), :]
        bias = jnp.concatenate([bt_ref[0, c], bt_ref[1, c]], axis=0)
        s = jnp.where(bias > 0.5 * MASK_VALUE, s + bias, MASK_VALUE)
        m = jnp.max(s, axis=-1, keepdims=True)
        p = jnp.exp(s - m)
        l = jnp.sum(p, axis=-1, keepdims=True)
        o = jnp.dot(p.astype(BF16), vs, preferred_element_type=F32) / l
        o2 = jnp.where(first_head, o[:GRID_W], o[GRID_W:])
        o_ref[pl.ds(pl.multiple_of(r * GRID_W, GRID_W), GRID_W), :] = o2.astype(o_ref.dtype)

    s_ref[0] = scores(0)

    def pair(p, carry):
        r = 2 * p
        s_ref[1] = scores(r + 1)
        finish(r, s_ref[0])
        s_ref[0] = scores(jnp.minimum(r + 2, rb - 1))
        finish(r + 1, s_ref[1])
        return carry

    lax.fori_loop(0, rb // 2, pair, 0, unroll=ATTN_UNROLL)


def _attention(q, k, v, rpb):
    t = q.shape[0]
    rows = t // GRID_W
    rb = min(ATTN_RB, rows)
    npairs = D_NA // LANES
    bt = _bias_table(rpb)
    kern = functools.partial(_attn_kernel, rb=rb, rows=rows)
    return pl.pallas_call(
        kern,
        grid=(npairs, rows // rb),
        in_specs=[
            pl.BlockSpec((rb * GRID_W, LANES), lambda p, j: (j, p)),
            pl.BlockSpec((t, LANES), lambda p, j: (0, p)),
            pl.BlockSpec((t, LANES), lambda p, j: (0, p)),
            pl.BlockSpec((2, WIN_H, GRID_W, WIN_H * GRID_W), lambda p, j: (p, 0, 0, 0)),
        ],
        out_specs=pl.BlockSpec((rb * GRID_W, LANES), lambda p, j: (j, p)),
        out_shape=jax.ShapeDtypeStruct((t, D_NA), BF16),
        scratch_shapes=[pltpu.VMEM((2, 2 * GRID_W, WIN_H * GRID_W), F32)],
        compiler_params=pltpu.CompilerParams(
            dimension_semantics=("arbitrary", "arbitrary"), vmem_limit_bytes=_vmem_limit(40 << 20)),
        name="attention",
    )(q, k, v, bt)


def _outproj_kernel(conv_ref, na_ref, x_ref, wo_ref, bo_ref, g_ref, wr_ref, br_ref,
                    h_ref, xn_ref, idx_ref, gate_ref):
    for s in range(x_ref.shape[0] // ROUTER_SUB):
        rows = slice(s * ROUTER_SUB, (s + 1) * ROUTER_SUB)
        mixed = jnp.concatenate([conv_ref[rows, :], na_ref[rows, :]], axis=-1)
        h = x_ref[rows, :] + jnp.dot(mixed, wo_ref[...], preferred_element_type=F32) + bo_ref[...]
        h_ref[rows, :] = h
        ms = jnp.mean(h * h, axis=-1, keepdims=True)
        xn = (h * lax.rsqrt(ms + EPS) * g_ref[...]).astype(BF16)
        xn_ref[rows, :] = xn
        logits = lax.dot_general(wr_ref[...], xn, (((1,), (1,)), ((), ())),
                                 preferred_element_type=F32) + br_ref[...]
        ids = lax.broadcasted_iota(jnp.int32, logits.shape, 0)
        vals, sels = [], []
        l = logits
        for _ in range(TOP_K):
            m = jnp.max(l, axis=0, keepdims=True)
            sel = jnp.min(jnp.where(l == m, ids, N_EXPERTS), axis=0, keepdims=True)
            vals.append(m)
            sels.append(sel)
            l = jnp.where(ids == sel, -jnp.inf, l)
        es = [jnp.exp(vk - vals[0]) for vk in vals]
        tot = es[0] + es[1] + es[2] + es[3]
        idx_ref[:, rows] = jnp.concatenate(sels, axis=0)
        gate_ref[:, rows] = jnp.concatenate([e / tot for e in es], axis=0)


def _outproj_router(conv_out, na_out, x2, w_out, b_out, g_ffn, w_router, b_router):
    t = x2.shape[0]
    tm = min(PROJ_TM, t)
    tok = lambda i: (i, 0)
    const = lambda i: (0, 0)
    return pl.pallas_call(
        _outproj_kernel,
        grid=(t // tm,),
        in_specs=[
            pl.BlockSpec((tm, D_CONV), tok),
            pl.BlockSpec((tm, D_NA), tok),
            pl.BlockSpec((tm, D_MODEL), tok),
            pl.BlockSpec((D_MODEL, D_MODEL), const),
            pl.BlockSpec((1, D_MODEL), const),
            pl.BlockSpec((1, D_MODEL), const),
            pl.BlockSpec((N_EXPERTS, D_MODEL), const),
            pl.BlockSpec((N_EXPERTS, 1), const),
        ],
        out_specs=[
            pl.BlockSpec((tm, D_MODEL), tok),
            pl.BlockSpec((tm, D_MODEL), tok),
            pl.BlockSpec((TOP_K, tm), lambda i: (0, i)),
            pl.BlockSpec((TOP_K, tm), lambda i: (0, i)),
        ],
        out_shape=[
            jax.ShapeDtypeStruct((t, D_MODEL), F32),
            jax.ShapeDtypeStruct((t, D_MODEL), BF16),
            jax.ShapeDtypeStruct((TOP_K, t), jnp.int32),
            jax.ShapeDtypeStruct((TOP_K, t), F32),
        ],
        compiler_params=pltpu.CompilerParams(
            dimension_semantics=("arbitrary",), vmem_limit_bytes=_vmem_limit(48 << 20)),
        name="outproj_router",
    )(conv_out, na_out, x2, w_out.astype(BF16), b_out.reshape(1, D_MODEL), g_ffn.reshape(1, D_MODEL),
      w_router.T.astype(BF16), b_router.reshape(N_EXPERTS, 1))


def _route_plan(idx_t, t):
    nt = t // ROUTE_TM
    experts = jnp.arange(N_EXPERTS, dtype=I32)
    onehot = idx_t.reshape(TOP_K, nt, ROUTE_TM, 1) == experts
    cnt = jnp.sum(onehot, axis=(0, 2), dtype=I32)
    sizes = jnp.sum(cnt, axis=0)
    padded = (sizes + ROUTE_RP + EXP_BM - 1) // EXP_BM * EXP_BM
    pad_ends = jnp.cumsum(padded)
    pad_off = pad_ends - padded
    tbase = pad_off[None, :] + jnp.cumsum(cnt, axis=0) - cnt
    cap = _sorted_rows(t)
    nb = cap // EXP_BM
    n_used = pad_ends[N_EXPERTS - 1] // EXP_BM
    blk_src = jnp.minimum(jnp.arange(nb, dtype=I32), n_used - 1)
    blk_exp = jnp.minimum(
        jnp.sum(blk_src[:, None] * EXP_BM >= pad_ends[None, :], axis=1, dtype=I32), N_EXPERTS - 1)
    row_end = jnp.sum(jnp.where(blk_exp[:, None] == experts[None, :], (pad_off + sizes)[None, :], 0), axis=1)
    blk_live = jnp.clip(row_end - blk_src * EXP_BM, 0, EXP_BM).astype(I32)
    multi = jnp.any(tbase % SUBLANES + cnt > ROUTE_RP, axis=1).astype(I32)
    return dict(cnt=cnt.reshape(-1), tbase=tbase.reshape(-1).astype(I32), multi=multi,
                last_blk=(pad_ends - EXP_BM).astype(I32), two_blk=(padded >= 2 * EXP_BM).astype(I32),
                blk_src=blk_src, blk_exp=blk_exp, blk_live=blk_live, n_used=n_used.reshape(1).astype(I32))


def _sorted_rows(t):
    cap = t * TOP_K + N_EXPERTS * (EXP_BM + ROUTE_RP)
    return (cap + EXP_BM - 1) // EXP_BM * EXP_BM


def _chunk_geometry(tb, n):
    head = tb % SUBLANES
    start = pl.multiple_of(tb - head, SUBLANES)
    nchunks = (head + n + ROUTE_RP - 1) // ROUTE_RP
    return start, head, nchunks


def _dispatch_kernel(tb_ref, cnt_ref, mt_ref, lb_ref, two_ref, nu_ref, idx_ref, tbv_ref, xn_ref, xs_hbm, loc_ref,
                     stage, ostage, zbuf, head_ref, cs_ref, ms_ref, sem, osem, zsem):
    i = pl.program_id(0)
    nt = pl.num_programs(0)
    slot = i % 2
    tm = idx_ref.shape[1]
    groups = ROUTE_RP // SUBLANES

    def geometry(e):
        return _chunk_geometry(tb_ref[i * N_EXPERTS + e], cnt_ref[i * N_EXPERTS + e])

    def chunk_copy(e, s):
        start, _, _ = geometry(e)
        return pltpu.make_async_copy(stage.at[s, pl.ds(e * ROUTE_RP, ROUTE_RP)],
                                     xs_hbm.at[pl.ds(start, ROUTE_RP)], sem.at[s])

    def zero_copy(start):
        return pltpu.make_async_copy(zbuf, xs_hbm.at[pl.ds(pl.multiple_of(start, EXP_BM), EXP_BM)], zsem)

    @pl.when(i == 0)
    def _():
        zbuf[...] = jnp.zeros(zbuf.shape, U32)
        head_ref[...] = jnp.zeros(head_ref.shape, U32)
        for phase in ("start", "wait"):
            for e in range(N_EXPERTS):
                getattr(zero_copy(lb_ref[e]), phase)()

                @pl.when(two_ref[e] > 0)
                def _():
                    getattr(zero_copy(lb_ref[e] - EXP_BM), phase)()

        def unused_block(bi, carry):
            cp = zero_copy(bi * EXP_BM)
            cp.start()
            cp.wait()
            return carry
        lax.fori_loop(nu_ref[0], xs_hbm.shape[0] // EXP_BM, unused_block, 0)

    idx = idx_ref[...]
    eio = lax.broadcasted_iota(I32, (N_EXPERTS, tm), 0)
    member = jnp.zeros((N_EXPERTS, tm), F32)
    for k in range(TOP_K):
        member = member + (idx[k:k + 1, :] == eio).astype(F32)
    tri = (lax.broadcasted_iota(I32, (tm, tm), 0) < lax.broadcasted_iota(I32, (tm, tm), 1)).astype(BF16)
    rank = jnp.dot(member.astype(BF16), tri, preferred_element_type=F32)
    pos = rank + (tbv_ref[0] % SUBLANES).astype(F32)
    loc_ref[...] = jnp.concatenate(
        [jnp.sum(jnp.where(idx[k:k + 1, :] == eio, pos, 0.0), axis=0, keepdims=True) for k in range(TOP_K)],
        axis=0).astype(I32)
    cs_ref[...] = pos
    ms_ref[...] = member

    jio = lax.broadcasted_iota(I32, (ROUTE_RP, tm), 0).astype(F32)
    sel = [jnp.where((jio == pos[e:e + 1, :]) & (member[e:e + 1, :] > 0.0), 1.0, 0.0).astype(BF16)
           for e in range(N_EXPERTS)]
    rows = jnp.dot(jnp.concatenate(sel, axis=0), xn_ref[...], preferred_element_type=F32)

    @pl.when(i > 0)
    def _():
        for e in range(N_EXPERTS):
            chunk_copy(e, 1 - slot).wait()

    stage[slot] = _pack_rows(rows)
    sub = lax.broadcasted_iota(I32, (SUBLANES, HALF), 0)
    for e in range(N_EXPERTS):
        _, head, _ = geometry(e)
        first = pl.ds(e * ROUTE_RP, SUBLANES)
        stage[slot, first, :] = jnp.where(sub < head, head_ref[e], stage[slot, first, :])
        chunk_copy(e, slot).start()
        g = jnp.minimum((head + cnt_ref[i * N_EXPERTS + e]) // SUBLANES, groups - 1)
        head_ref[e] = stage[slot, pl.ds(pl.multiple_of(e * ROUTE_RP + g * SUBLANES, SUBLANES), SUBLANES), :]

    def per_expert(e, carry):
        start, head, nchunks = geometry(e)
        end = head + cnt_ref[i * N_EXPERTS + e]

        def per_chunk(c, carry2):
            want = jio + (c * ROUTE_RP).astype(F32)
            pick = jnp.where((want == cs_ref[pl.ds(e, 1), :]) & (ms_ref[pl.ds(e, 1), :] > 0.0), 1.0, 0.0)
            ostage[...] = _pack_rows(jnp.dot(pick.astype(BF16), xn_ref[...], preferred_element_type=F32))
            cp = pltpu.make_async_copy(
                ostage, xs_hbm.at[pl.ds(pl.multiple_of(start + c * ROUTE_RP, SUBLANES), ROUTE_RP)], osem)
            cp.start()
            cp.wait()

            @pl.when(end // ROUTE_RP == c)
            def _():
                g = (end - c * ROUTE_RP) // SUBLANES
                head_ref[e] = ostage[pl.ds(pl.multiple_of(g * SUBLANES, SUBLANES), SUBLANES), :]
            return carry2
        return lax.fori_loop(1, nchunks, per_chunk, carry)

    @pl.when(mt_ref[i] > 0)
    def _():
        lax.fori_loop(0, N_EXPERTS, per_expert, 0)

    @pl.when(i == nt - 1)
    def _():
        for e in range(N_EXPERTS):
            chunk_copy(e, slot).wait()


def _dispatch(plan, idx_t, xn2):
    t = xn2.shape[0]
    tm = ROUTE_TM
    cap = _sorted_rows(t)
    grid_spec = pltpu.PrefetchScalarGridSpec(
        num_scalar_prefetch=6,
        grid=(t // tm,),
        in_specs=[
            pl.BlockSpec((TOP_K, tm), lambda i, *_: (0, i)),
            pl.BlockSpec((1, N_EXPERTS, 1), lambda i, *_: (i, 0, 0)),
            pl.BlockSpec((tm, D_MODEL), lambda i, *_: (i, 0)),
        ],
        out_specs=[
            pl.BlockSpec(memory_space=pl.ANY),
            pl.BlockSpec((TOP_K, tm), lambda i, *_: (0, i)),
        ],
        scratch_shapes=[
            pltpu.VMEM((2, N_EXPERTS * ROUTE_RP, HALF), U32),
            pltpu.VMEM((ROUTE_RP, HALF), U32),
            pltpu.VMEM((EXP_BM, HALF), U32),
            pltpu.VMEM((N_EXPERTS, SUBLANES, HALF), U32),
            pltpu.VMEM((N_EXPERTS, tm), F32),
            pltpu.VMEM((N_EXPERTS, tm), F32),
            pltpu.SemaphoreType.DMA((2,)),
            pltpu.SemaphoreType.DMA(()),
            pltpu.SemaphoreType.DMA(()),
        ],
    )
    return pl.pallas_call(
        _dispatch_kernel,
        grid_spec=grid_spec,
        out_shape=[
            jax.ShapeDtypeStruct((cap, HALF), U32),
            jax.ShapeDtypeStruct((TOP_K, t), I32),
        ],
        compiler_params=pltpu.CompilerParams(
            dimension_semantics=("arbitrary",), vmem_limit_bytes=_vmem_limit(32 << 20)),
        name="dispatch",
    )(plan["tbase"], plan["cnt"], plan["multi"], plan["last_blk"], plan["two_blk"], plan["n_used"], idx_t,
      plan["tbase"].reshape(t // tm, N_EXPERTS, 1), xn2)


def _expert_kernel(be_ref, src_ref, nu_ref, live_ref, xs_ref, wg_hbm, wu_hbm, wd_hbm, bg_ref, bu_ref, bd_ref,
                   ys_ref, wf32, wbf, wsem):
    b = pl.program_id(0)
    e = be_ref[b]

    def weight_copies(expert, par):
        return [pltpu.make_async_copy(w.at[expert], wf32.at[par, m], wsem.at[par])
                for m, w in enumerate((wg_hbm, wu_hbm, wd_hbm))]

    @pl.when(b == 0)
    def _():
        for cp in weight_copies(e, e % 2):
            cp.start()

    @pl.when(b < nu_ref[0])
    def _():
        @pl.when((b == 0) | (e != be_ref[jnp.maximum(b - 1, 0)]))
        def _():
            par = e % 2
            for cp in weight_copies(e, par):
                cp.wait()

            @pl.when(e + 1 < N_EXPERTS)
            def _():
                for cp in weight_copies(e + 1, 1 - par):
                    cp.start()

            for m in range(3):
                wbf[m] = wf32[par, m].astype(BF16)

        def ffn(rows):
            x = _unpack_rows(xs_ref[rows, :])
            gt = jnp.minimum(jnp.dot(x, wbf[0], preferred_element_type=F32) + bg_ref[0], SWIGLU_LIMIT)
            up = jnp.clip(jnp.dot(x, wbf[1], preferred_element_type=F32) + bu_ref[0],
                          -SWIGLU_LIMIT, SWIGLU_LIMIT)
            hdn = (up + 1.0) * (gt * jax.nn.sigmoid(SWIGLU_ALPHA * gt))
            y = jnp.dot(hdn.astype(BF16), wbf[2], preferred_element_type=F32) + bd_ref[0]
            ys_ref[rows, :] = _pack_rows(y.astype(BF16).astype(F32))

        half = EXP_BM // 2

        @pl.when(live_ref[b] > half)
        def _():
            ffn(slice(0, EXP_BM))

        @pl.when(live_ref[b] <= half)
        def _():
            ffn(slice(0, half))
            ys_ref[half:, :] = jnp.zeros((EXP_BM - half, HALF), U32)

    @pl.when(b >= nu_ref[0])
    def _():
        ys_ref[...] = jnp.zeros(ys_ref.shape, U32)


def _experts(plan, xs, w_gate, b_gate, w_up, b_up, w_down, b_down):
    cap = xs.shape[0]
    nb = cap // EXP_BM
    rows = lambda b, be, src, nu, live: (src[b], 0)
    wsel = lambda b, be, src, nu, live: (be[b], 0, 0)
    wspec = pl.BlockSpec(memory_space=pl.ANY)
    bspec = pl.BlockSpec((1, 1, D_FF), wsel)
    grid_spec = pltpu.PrefetchScalarGridSpec(
        num_scalar_prefetch=4,
        grid=(nb,),
        in_specs=[pl.BlockSpec((EXP_BM, HALF), rows), wspec, wspec, wspec, bspec, bspec, bspec],
        out_specs=pl.BlockSpec((EXP_BM, HALF), lambda b, be, src, nu, live: (b, 0)),
        scratch_shapes=[pltpu.VMEM((2, 3, D_MODEL, D_FF), F32),
                        pltpu.VMEM((3, D_MODEL, D_FF), BF16),
                        pltpu.SemaphoreType.DMA((2,))],
    )
    return pl.pallas_call(
        _expert_kernel,
        grid_spec=grid_spec,
        out_shape=jax.ShapeDtypeStruct((cap, HALF), U32),
        compiler_params=pltpu.CompilerParams(
            dimension_semantics=("arbitrary",), vmem_limit_bytes=_vmem_limit(56 << 20)),
        name="experts",
    )(plan["blk_exp"], plan["blk_src"], plan["n_used"], plan["blk_live"], xs, w_gate, w_up, w_down,
      b_gate.reshape(N_EXPERTS, 1, D_FF), b_up.reshape(N_EXPERTS, 1, D_FF),
      b_down.reshape(N_EXPERTS, 1, D_MODEL))


def _combine_kernel(tb_ref, cnt_ref, mt_ref, h_ref, idx_ref, loc_ref, gate_ref, ys_hbm, g_ref, o_ref,
                    ybuf, obuf, acc_ref, sem, osem):
    i = pl.program_id(0)
    nt = pl.num_programs(0)
    slot = i % 2
    tm = h_ref.shape[0]

    def geometry(step, e):
        return _chunk_geometry(tb_ref[step * N_EXPERTS + e], cnt_ref[step * N_EXPERTS + e])

    def chunk_copy(step, e, s):
        start, _, _ = geometry(step, e)
        return pltpu.make_async_copy(ys_hbm.at[pl.ds(start, ROUTE_RP)],
                                     ybuf.at[s, pl.ds(e * ROUTE_RP, ROUTE_RP)], sem.at[s])

    @pl.when(i == 0)
    def _():
        for e in range(N_EXPERTS):
            chunk_copy(0, e, 0).start()

    @pl.when(i + 1 < nt)
    def _():
        for e in range(N_EXPERTS):
            chunk_copy(i + 1, e, 1 - slot).start()

    idx = idx_ref[...]
    loc = loc_ref[...]
    gate = gate_ref[...]
    g_hi = gate.astype(BF16).astype(F32)
    g_lo = gate - g_hi

    def gate_matrix(ncols, col_of):
        colio = lax.broadcasted_iota(I32, (tm, ncols), 1)
        hi = jnp.zeros((tm, ncols), F32)
        lo = jnp.zeros((tm, ncols), F32)
        for k in range(TOP_K):
            hit = colio == col_of[:, k:k + 1]
            hi = jnp.where(hit, g_hi[:, k:k + 1], hi)
            lo = jnp.where(hit, g_lo[:, k:k + 1], lo)
        return jnp.concatenate([hi, lo], axis=0).astype(BF16)

    for e in range(N_EXPERTS):
        chunk_copy(i, e, slot).wait()
    col = jnp.where(loc < ROUTE_RP, idx * ROUTE_RP + loc, -1)
    z = jnp.dot(gate_matrix(N_EXPERTS * ROUTE_RP, col), _unpack_rows(ybuf[slot]),
                preferred_element_type=F32)
    acc_ref[...] = z[:tm] + z[tm:]

    def per_expert(e, carry):
        start, _, nchunks = geometry(i, e)

        def per_chunk(c, carry2):
            cp = pltpu.make_async_copy(
                ys_hbm.at[pl.ds(pl.multiple_of(start + c * ROUTE_RP, SUBLANES), ROUTE_RP)], obuf, osem)
            cp.start()
            cp.wait()
            ccol = jnp.where(idx == e, loc - c * ROUTE_RP, -1)
            zc = jnp.dot(gate_matrix(ROUTE_RP, ccol), _unpack_rows(obuf[...]), preferred_element_type=F32)
            acc_ref[...] += zc[:tm] + zc[tm:]
            return carry2
        return lax.fori_loop(1, nchunks, per_chunk, carry)

    @pl.when(mt_ref[i] > 0)
    def _():
        lax.fori_loop(0, N_EXPERTS, per_expert, 0)

    out = h_ref[...] + acc_ref[...]
    ms = jnp.mean(out * out, axis=-1, keepdims=True)
    o_ref[...] = out * lax.rsqrt(ms + EPS) * g_ref[...]


def _combine(plan, h, ys, idx_tm, loc_tm, gate_tm, g_final):
    t = h.shape[0]
    tm = ROUTE_TM
    tok = lambda i, *_: (i, 0)
    grid_spec = pltpu.PrefetchScalarGridSpec(
        num_scalar_prefetch=3,
        grid=(t // tm,),
        in_specs=[
            pl.BlockSpec((tm, D_MODEL), tok),
            pl.BlockSpec((tm, TOP_K), tok),
            pl.BlockSpec((tm, TOP_K), tok),
            pl.BlockSpec((tm, TOP_K), tok),
            pl.BlockSpec(memory_space=pl.ANY),
            pl.BlockSpec((1, D_MODEL), lambda i, *_: (0, 0)),
        ],
        out_specs=pl.BlockSpec((tm, D_MODEL), tok),
        scratch_shapes=[
            pltpu.VMEM((2, N_EXPERTS * ROUTE_RP, HALF), U32),
            pltpu.VMEM((ROUTE_RP, HALF), U32),
            pltpu.VMEM((tm, D_MODEL), F32),
            pltpu.SemaphoreType.DMA((2,)),
            pltpu.SemaphoreType.DMA(()),
        ],
    )
    return pl.pallas_call(
        _combine_kernel,
        grid_spec=grid_spec,
        out_shape=jax.ShapeDtypeStruct((t, D_MODEL), F32),
        compiler_params=pltpu.CompilerParams(
            dimension_semantics=("arbitrary",), vmem_limit_bytes=_vmem_limit(32 << 20)),
        name="combine",
    )(plan["tbase"], plan["cnt"], plan["multi"], h, idx_tm, loc_tm, gate_tm, ys,
      g_final.reshape(1, D_MODEL))


def _moe(h, xn2, idx_t, gate_t, w_gate, b_gate, w_up, b_up, w_down, b_down, g_final):
    t = h.shape[0]
    plan = _route_plan(idx_t, t)
    xs, loc_t = _dispatch(plan, idx_t, xn2)
    ys = _experts(plan, xs, w_gate, b_gate, w_up, b_up, w_down, b_down)
    return _combine(plan, h, ys, idx_t.T, loc_t.T, gate_t.T, g_final)


def kernel(x, g_mix, w_in, b_in, conv_w, conv_b, ln_g, ln_b, rpb, w_out, b_out, g_ffn, w_router,
           b_router, w_gate, b_gate, w_up, b_up, w_down, b_down, g_final):
    bsz, seq, d = x.shape
    assert bsz == 1 and d == D_MODEL and g_mix.shape[0] == 1
    assert seq % (GRID_W * WIN_H) == 0
    x2 = x.reshape(seq, d)
    u, q, k, v = _inproj(x2, g_mix[0], w_in[0], b_in[0])
    conv_out = _conv(u, conv_w[0], conv_b[0], ln_g[0], ln_b[0])
    na_out = _attention(q, k, v, rpb[0])
    h, xn2, idx_t, gate_t = _outproj_router(conv_out, na_out, x2, w_out[0], b_out[0], g_ffn[0],
                                            w_router[0], b_router[0])
    out = _moe(h, xn2, idx_t, gate_t, w_gate[0], b_gate[0], w_up[0], b_up[0], w_down[0], b_down[0],
               g_final)
    return out.reshape(bsz, seq, d)
```

```python
import functools

import numpy as np
import jax
import jax.numpy as jnp
from jax import lax
from jax.experimental import pallas as pl
from jax.experimental.pallas import tpu as pltpu

F32 = jnp.float32
BF16 = jnp.bfloat16
U32 = jnp.uint32
I32 = jnp.int32

D_MODEL = 1024
GRID_W = 64
D_CONV = 512
CONV_K = 31
NA_HEADS = 8
NA_HEAD_DIM = 64
D_NA = NA_HEADS * NA_HEAD_DIM
D_IN = 2 * D_CONV + 3 * D_NA
WIN_H = 8
WIN_W = 16
N_EXPERTS = 32
TOP_K = 4
D_FF = D_MODEL
SWIGLU_LIMIT = 7.0
SWIGLU_ALPHA = 1.702
EPS = 1e-5
MASK_VALUE = -1e30

V7X_VMEM_BYTES = 64 * 1024 * 1024
LANES = 128

PROJ_TM = 512
ROUTER_SUB = 256
CONV_TT = 256
CONV_HALO = 16
CONV_CH = 32
ATTN_RB = 32
ATTN_UNROLL = 4
ROUTE_TM = 256
COMBINE_SUB = 128
SUBLANES = 8
ROUTE_RP = 56
EXP_BM = 256
HALF = D_MODEL // 2


def _vmem_limit(nbytes):
    return int(min(nbytes, V7X_VMEM_BYTES - 6 * 1024 * 1024))


def _pack_rows(x):
    lo = lax.bitcast_convert_type(x[:, :HALF], U32) >> 16
    hi = lax.bitcast_convert_type(x[:, HALF:], U32) & jnp.uint32(0xFFFF0000)
    return lo | hi


def _unpack_rows(w):
    lo = lax.bitcast_convert_type(w << 16, F32)
    hi = lax.bitcast_convert_type(w & jnp.uint32(0xFFFF0000), F32)
    return jnp.concatenate([lo, hi], axis=1).astype(BF16)


def _inproj_kernel(x_ref, g_ref, w_ref, b_ref, u_ref, q_ref, k_ref, v_ref):
    x = x_ref[...]
    ms = jnp.mean(x * x, axis=-1, keepdims=True)
    xn = (x * lax.rsqrt(ms + EPS) * g_ref[...]).astype(BF16)

    def proj(c):
        sl = slice(c * D_CONV, (c + 1) * D_CONV)
        return jnp.dot(xn, w_ref[:, sl], preferred_element_type=F32) + b_ref[:, sl]

    u_ref[...] = proj(0) * jax.nn.sigmoid(proj(1))
    q_ref[...] = (proj(2) * (NA_HEAD_DIM ** -0.5)).astype(BF16)
    k_ref[...] = proj(3).astype(BF16)
    v_ref[...] = proj(4).astype(BF16)


def _inproj(x2, g_mix, w_in, b_in):
    t = x2.shape[0]
    tm = min(PROJ_TM, t)
    tok = lambda i: (i, 0)
    const = lambda i: (0, 0)
    return pl.pallas_call(
        _inproj_kernel,
        grid=(t // tm,),
        in_specs=[
            pl.BlockSpec((tm, D_MODEL), tok),
            pl.BlockSpec((1, D_MODEL), const),
            pl.BlockSpec((D_MODEL, D_IN), const),
            pl.BlockSpec((1, D_IN), const),
        ],
        out_specs=[
            pl.BlockSpec((tm, D_CONV), tok),
            pl.BlockSpec((tm, D_NA), tok),
            pl.BlockSpec((tm, D_NA), tok),
            pl.BlockSpec((tm, D_NA), tok),
        ],
        out_shape=[
            jax.ShapeDtypeStruct((t, D_CONV), F32),
            jax.ShapeDtypeStruct((t, D_NA), BF16),
            jax.ShapeDtypeStruct((t, D_NA), BF16),
            jax.ShapeDtypeStruct((t, D_NA), BF16),
        ],
        compiler_params=pltpu.CompilerParams(
            dimension_semantics=("arbitrary",), vmem_limit_bytes=_vmem_limit(48 << 20)),
        name="inproj",
    )(x2, g_mix.reshape(1, D_MODEL), w_in.astype(BF16), b_in.reshape(1, D_IN))


def _conv_kernel(prev_ref, cur_ref, next_ref, w_ref, cb_ref, lg_ref, lb_ref, o_ref, ext_ref, sh_ref):
    i = pl.program_id(0)
    n = pl.num_programs(0)
    tt = cur_ref.shape[0]
    ext_ref[0:CONV_HALO, :] = jnp.where(i > 0, prev_ref[...], 0.0)
    ext_ref[CONV_HALO:CONV_HALO + tt, :] = cur_ref[...]
    ext_ref[CONV_HALO + tt:2 * CONV_HALO + tt, :] = jnp.where(i < n - 1, next_ref[...], 0.0)
    base = CONV_HALO - CONV_K // 2
    span = (CONV_K - 1) // SUBLANES * SUBLANES
    for b in range(SUBLANES):
        sh_ref[b] = ext_ref[base + b:base + b + tt + span, :]
    for c in range(tt // CONV_CH):
        acc = jnp.zeros((CONV_CH, D_CONV), F32)
        for b in range(SUBLANES):
            for a in range((CONV_K - 1 - b) // SUBLANES + 1):
                r = c * CONV_CH + SUBLANES * a
                acc = acc + sh_ref[b, r:r + CONV_CH, :] * w_ref[SUBLANES * a + b:SUBLANES * a + b + 1, :]
        acc = acc + cb_ref[...]
        mu = jnp.mean(acc, axis=-1, keepdims=True)
        d = acc - mu
        var = jnp.mean(d * d, axis=-1, keepdims=True)
        un = d * lax.rsqrt(var + EPS) * lg_ref[...] + lb_ref[...]
        o_ref[c * CONV_CH:(c + 1) * CONV_CH, :] = (un * jax.nn.sigmoid(un)).astype(o_ref.dtype)


def _conv(u, conv_w, conv_b, ln_g, ln_b):
    t = u.shape[0]
    tt = min(CONV_TT, t)
    hb = tt // CONV_HALO
    nhb = t // CONV_HALO
    const = lambda i: (0, 0)
    w = jnp.zeros((CONV_K + 1, D_CONV), F32).at[:CONV_K].set(conv_w)
    return pl.pallas_call(
        _conv_kernel,
        grid=(t // tt,),
        in_specs=[
            pl.BlockSpec((CONV_HALO, D_CONV), lambda i: (jnp.maximum(i * hb - 1, 0), 0)),
            pl.BlockSpec((tt, D_CONV), lambda i: (i, 0)),
            pl.BlockSpec((CONV_HALO, D_CONV), lambda i: (jnp.minimum((i + 1) * hb, nhb - 1), 0)),
            pl.BlockSpec((CONV_K + 1, D_CONV), const),
            pl.BlockSpec((1, D_CONV), const),
            pl.BlockSpec((1, D_CONV), const),
            pl.BlockSpec((1, D_CONV), const),
        ],
        out_specs=pl.BlockSpec((tt, D_CONV), lambda i: (i, 0)),
        out_shape=jax.ShapeDtypeStruct((t, D_CONV), BF16),
        scratch_shapes=[pltpu.VMEM((tt + 2 * CONV_HALO, D_CONV), F32),
                        pltpu.VMEM((SUBLANES, tt + (CONV_K - 1) // SUBLANES * SUBLANES, D_CONV), F32)],
        compiler_params=pltpu.CompilerParams(dimension_semantics=("arbitrary",)),
        name="conv",
    )(u, u, u, w, conv_b.reshape(1, D_CONV), ln_g.reshape(1, D_CONV), ln_b.reshape(1, D_CONV))


def _bias_table(rpb):
    col = np.arange(GRID_W)
    c0 = np.clip(col - WIN_W // 2, 0, GRID_W - WIN_W)
    valid = (col[None, :] >= c0[:, None]) & (col[None, :] < c0[:, None] + WIN_W)
    dcol = np.clip(col[None, :] - col[:, None], -(WIN_W - 1), WIN_W - 1) + (WIN_W - 1)
    onehot = ((dcol[:, :, None] == np.arange(2 * WIN_W - 1)) & valid[:, :, None]).astype(np.float32)
    tcol = jnp.einsum("hdj,ckj->hdck", rpb.astype(F32), onehot, precision=lax.Precision.HIGHEST)
    tcol = jnp.where(valid[None, None], tcol, MASK_VALUE)
    b = jnp.stack([tcol[:, WIN_H - 1 - c:2 * WIN_H - 1 - c] for c in range(WIN_H)], axis=1)
    b = jnp.transpose(b, (0, 1, 3, 2, 4))
    return b.reshape(NA_HEADS, WIN_H, GRID_W, WIN_H * GRID_W)


def _attn_kernel(q_ref, k_ref, v_ref, bt_ref, o_ref, s_ref, *, rb, rows):
    jb = pl.program_id(1)
    lane = lax.broadcasted_iota(jnp.int32, (GRID_W, LANES), 1)
    first_head = lane < NA_HEAD_DIM
    nkeys = WIN_H * GRID_W

    def window(r):
        i = jb * rb + r
        r0 = jnp.clip(i - WIN_H // 2, 0, rows - WIN_H)
        return i - r0, pl.multiple_of(r0 * GRID_W, GRID_W)

    def scores(r):
        _, start = window(r)
        q2 = q_ref[pl.ds(pl.multiple_of(r * GRID_W, GRID_W), GRID_W), :]
        zero = jnp.zeros_like(q2)
        qst = jnp.concatenate([jnp.where(first_head, q2, zero), jnp.where(first_head, zero, q2)], axis=0)
        ks = k_ref[pl.ds(start, nkeys), :]
        return lax.dot_general(qst, ks, (((1,), (1,)), ((), ())), preferred_element_type=F32)

    def finish(r, s):
        c, start = window(r)
        vs = v_ref[pl.ds(start, nkeys), :]
        bias = jnp.concatenate([bt_ref[0, c], bt_ref[1, c]], axis=0)
        s = jnp.where(bias > 0.5 * MASK_VALUE, s + bias, MASK_VALUE)
        m = jnp.max(s, axis=-1, keepdims=True)
        p = jnp.exp(s - m)
        l = jnp.sum(p, axis=-1, keepdims=True)
        o = jnp.dot(p.astype(BF16), vs, preferred_element_type=F32) / l
        o2 = jnp.where(first_head, o[:GRID_W], o[GRID_W:])
        o_ref[pl.ds(pl.multiple_of(r * GRID_W, GRID_W), GRID_W), :] = o2.astype(o_ref.dtype)

    s_ref[0] = scores(0)

    def pair(p, carry):
        r = 2 * p
        s_ref[1] = scores(r + 1)
        finish(r, s_ref[0])
        s_ref[0] = scores(jnp.minimum(r + 2, rb - 1))
        finish(r + 1, s_ref[1])
        return carry

    lax.fori_loop(0, rb // 2, pair, 0, unroll=ATTN_UNROLL)


def _attention(q, k, v, rpb):
    t = q.shape[0]
    rows = t // GRID_W
    rb = min(ATTN_RB, rows)
    npairs = D_NA // LANES
    bt = _bias_table(rpb)
    kern = functools.partial(_attn_kernel, rb=rb, rows=rows)
    return pl.pallas_call(
        kern,
        grid=(npairs, rows // rb),
        in_specs=[
            pl.BlockSpec((rb * GRID_W, LANES), lambda p, j: (j, p)),
            pl.BlockSpec((t, LANES), lambda p, j: (0, p)),
            pl.BlockSpec((t, LANES), lambda p, j: (0, p)),
            pl.BlockSpec((2, WIN_H, GRID_W, WIN_H * GRID_W), lambda p, j: (p, 0, 0, 0)),
        ],
        out_specs=pl.BlockSpec((rb * GRID_W, LANES), lambda p, j: (j, p)),
        out_shape=jax.ShapeDtypeStruct((t, D_NA), BF16),
        scratch_shapes=[pltpu.VMEM((2, 2 * GRID_W, WIN_H * GRID_W), F32)],
        compiler_params=pltpu.CompilerParams(
            dimension_semantics=("arbitrary", "arbitrary"), vmem_limit_bytes=_vmem_limit(40 << 20)),
        name="attention",
    )(q, k, v, bt)


def _outproj_kernel(conv_ref, na_ref, x_ref, wo_ref, bo_ref, g_ref, wr_ref, br_ref,
                    h_ref, xn_ref, idx_ref, gate_ref):
    for s in range(x_ref.shape[0] // ROUTER_SUB):
        rows = slice(s * ROUTER_SUB, (s + 1) * ROUTER_SUB)
        mixed = jnp.concatenate([conv_ref[rows, :], na_ref[rows, :]], axis=-1)
        h = x_ref[rows, :] + jnp.dot(mixed, wo_ref[...], preferred_element_type=F32) + bo_ref[...]
        h_ref[rows, :] = h
        ms = jnp.mean(h * h, axis=-1, keepdims=True)
        xn = (h * lax.rsqrt(ms + EPS) * g_ref[...]).astype(BF16)
        xn_ref[rows, :] = xn
        logits = lax.dot_general(wr_ref[...], xn, (((1,), (1,)), ((), ())),
                                 preferred_element_type=F32) + br_ref[...]
        ids = lax.broadcasted_iota(jnp.int32, logits.shape, 0)
        vals, sels = [], []
        l = logits
        for _ in range(TOP_K):
            m = jnp.max(l, axis=0, keepdims=True)
            sel = jnp.min(jnp.where(l == m, ids, N_EXPERTS), axis=0, keepdims=True)
            vals.append(m)
            sels.append(sel)
            l = jnp.where(ids == sel, -jnp.inf, l)
        es = [jnp.exp(vk - vals[0]) for vk in vals]
        tot = es[0] + es[1] + es[2] + es[3]
        idx_ref[:, rows] = jnp.concatenate(sels, axis=0)
        gate_ref[:, rows] = jnp.concatenate([e / tot for e in es], axis=0)


def _outproj_router(conv_out, na_out, x2, w_out, b_out, g_ffn, w_router, b_router):
    t = x2.shape[0]
    tm = min(PROJ_TM, t)
    tok = lambda i: (i, 0)
    const = lambda i: (0, 0)
    return pl.pallas_call(
        _outproj_kernel,
        grid=(t // tm,),
        in_specs=[
            pl.BlockSpec((tm, D_CONV), tok),
            pl.BlockSpec((tm, D_NA), tok),
            pl.BlockSpec((tm, D_MODEL), tok),
            pl.BlockSpec((D_MODEL, D_MODEL), const),
            pl.BlockSpec((1, D_MODEL), const),
            pl.BlockSpec((1, D_MODEL), const),
            pl.BlockSpec((N_EXPERTS, D_MODEL), const),
            pl.BlockSpec((N_EXPERTS, 1), const),
        ],
        out_specs=[
            pl.BlockSpec((tm, D_MODEL), tok),
            pl.BlockSpec((tm, D_MODEL), tok),
            pl.BlockSpec((TOP_K, tm), lambda i: (0, i)),
            pl.BlockSpec((TOP_K, tm), lambda i: (0, i)),
        ],
        out_shape=[
            jax.ShapeDtypeStruct((t, D_MODEL), F32),
            jax.ShapeDtypeStruct((t, D_MODEL), BF16),
            jax.ShapeDtypeStruct((TOP_K, t), jnp.int32),
            jax.ShapeDtypeStruct((TOP_K, t), F32),
        ],
        compiler_params=pltpu.CompilerParams(
            dimension_semantics=("arbitrary",), vmem_limit_bytes=_vmem_limit(48 << 20)),
        name="outproj_router",
    )(conv_out, na_out, x2, w_out.astype(BF16), b_out.reshape(1, D_MODEL), g_ffn.reshape(1, D_MODEL),
      w_router.T.astype(BF16), b_router.reshape(N_EXPERTS, 1))


def _route_plan(idx_t, t):
    nt = t // ROUTE_TM
    experts = jnp.arange(N_EXPERTS, dtype=I32)
    onehot = idx_t.reshape(TOP_K, nt, ROUTE_TM, 1) == experts
    cnt = jnp.sum(onehot, axis=(0, 2), dtype=I32)
    sizes = jnp.sum(cnt, axis=0)
    padded = (sizes + ROUTE_RP + EXP_BM - 1) // EXP_BM * EXP_BM
    pad_ends = jnp.cumsum(padded)
    pad_off = pad_ends - padded
    tbase = pad_off[None, :] + jnp.cumsum(cnt, axis=0) - cnt
    cap = _sorted_rows(t)
    nb = cap // EXP_BM
    n_used = pad_ends[N_EXPERTS - 1] // EXP_BM
    blk_src = jnp.minimum(jnp.arange(nb, dtype=I32), n_used - 1)
    blk_exp = jnp.minimum(
        jnp.sum(blk_src[:, None] * EXP_BM >= pad_ends[None, :], axis=1, dtype=I32), N_EXPERTS - 1)
    row_end = jnp.sum(jnp.where(blk_exp[:, None] == experts[None, :], (pad_off + sizes)[None, :], 0), axis=1)
    blk_live = jnp.clip(row_end - blk_src * EXP_BM, 0, EXP_BM).astype(I32)
    multi = jnp.any(tbase % SUBLANES + cnt > ROUTE_RP, axis=1).astype(I32)
    return dict(cnt=cnt.reshape(-1), tbase=tbase.reshape(-1).astype(I32), multi=multi,
                last_blk=(pad_ends - EXP_BM).astype(I32), two_blk=(padded >= 2 * EXP_BM).astype(I32),
                blk_src=blk_src, blk_exp=blk_exp, blk_live=blk_live, n_used=n_used.reshape(1).astype(I32))


def _sorted_rows(t):
    cap = t * TOP_K + N_EXPERTS * (EXP_BM + ROUTE_RP)
    return (cap + EXP_BM - 1) // EXP_BM * EXP_BM


def _chunk_geometry(tb, n):
    head = tb % SUBLANES
    start = pl.multiple_of(tb - head, SUBLANES)
    nchunks = (head + n + ROUTE_RP - 1) // ROUTE_RP
    return start, head, nchunks


def _dispatch_kernel(tb_ref, cnt_ref, mt_ref, lb_ref, two_ref, nu_ref, idx_ref, tbv_ref, xn_ref, xs_hbm, loc_ref,
                     stage, ostage, zbuf, head_ref, cs_ref, ms_ref, sem, osem, zsem):
    i = pl.program_id(0)
    nt = pl.num_programs(0)
    slot = i % 2
    tm = idx_ref.shape[1]
    groups = ROUTE_RP // SUBLANES

    def geometry(e):
        return _chunk_geometry(tb_ref[i * N_EXPERTS + e], cnt_ref[i * N_EXPERTS + e])

    def chunk_copy(e, s):
        start, _, _ = geometry(e)
        return pltpu.make_async_copy(stage.at[s, pl.ds(e * ROUTE_RP, ROUTE_RP)],
                                     xs_hbm.at[pl.ds(start, ROUTE_RP)], sem.at[s])

    def zero_copy(start):
        return pltpu.make_async_copy(zbuf, xs_hbm.at[pl.ds(pl.multiple_of(start, EXP_BM), EXP_BM)], zsem)

    @pl.when(i == 0)
    def _():
        zbuf[...] = jnp.zeros(zbuf.shape, U32)
        head_ref[...] = jnp.zeros(head_ref.shape, U32)
        for phase in ("start", "wait"):
            for e in range(N_EXPERTS):
                getattr(zero_copy(lb_ref[e]), phase)()

                @pl.when(two_ref[e] > 0)
                def _():
                    getattr(zero_copy(lb_ref[e] - EXP_BM), phase)()

        def unused_block(bi, carry):
            cp = zero_copy(bi * EXP_BM)
            cp.start()
            cp.wait()
            return carry
        lax.fori_loop(nu_ref[0], xs_hbm.shape[0] // EXP_BM, unused_block, 0)

    idx = idx_ref[...]
    eio = lax.broadcasted_iota(I32, (N_EXPERTS, tm), 0)
    member = jnp.zeros((N_EXPERTS, tm), F32)
    for k in range(TOP_K):
        member = member + (idx[k:k + 1, :] == eio).astype(F32)
    tri = (lax.broadcasted_iota(I32, (tm, tm), 0) < lax.broadcasted_iota(I32, (tm, tm), 1)).astype(BF16)
    rank = jnp.dot(member.astype(BF16), tri, preferred_element_type=F32)
    pos = rank + (tbv_ref[0] % SUBLANES).astype(F32)
    loc_ref[...] = jnp.concatenate(
        [jnp.sum(jnp.where(idx[k:k + 1, :] == eio, pos, 0.0), axis=0, keepdims=True) for k in range(TOP_K)],
        axis=0).astype(I32)
    cs_ref[...] = pos
    ms_ref[...] = member

    jio = lax.broadcasted_iota(I32, (ROUTE_RP, tm), 0).astype(F32)
    sel = [jnp.where((jio == pos[e:e + 1, :]) & (member[e:e + 1, :] > 0.0), 1.0, 0.0).astype(BF16)
           for e in range(N_EXPERTS)]
    rows = jnp.dot(jnp.concatenate(sel, axis=0), xn_ref[...], preferred_element_type=F32)

    @pl.when(i > 0)
    def _():
        for e in range(N_EXPERTS):
            chunk_copy(e, 1 - slot).wait()

    stage[slot] = _pack_rows(rows)
    sub = lax.broadcasted_iota(I32, (SUBLANES, HALF), 0)
    for e in range(N_EXPERTS):
        _, head, _ = geometry(e)
        first = pl.ds(e * ROUTE_RP, SUBLANES)
        stage[slot, first, :] = jnp.where(sub < head, head_ref[e], stage[slot, first, :])
        chunk_copy(e, slot).start()
        g = jnp.minimum((head + cnt_ref[i * N_EXPERTS + e]) // SUBLANES, groups - 1)
        head_ref[e] = stage[slot, pl.ds(pl.multiple_of(e * ROUTE_RP + g * SUBLANES, SUBLANES), SUBLANES), :]

    def per_expert(e, carry):
        start, head, nchunks = geometry(e)
        end = head + cnt_ref[i * N_EXPERTS + e]

        def per_chunk(c, carry2):
            want = jio + (c * ROUTE_RP).astype(F32)
            pick = jnp.where((want == cs_ref[pl.ds(e, 1), :]) & (ms_ref[pl.ds(e, 1), :] > 0.0), 1.0, 0.0)
            ostage[...] = _pack_rows(jnp.dot(pick.astype(BF16), xn_ref[...], preferred_element_type=F32))
            cp = pltpu.make_async_copy(
                ostage, xs_hbm.at[pl.ds(pl.multiple_of(start + c * ROUTE_RP, SUBLANES), ROUTE_RP)], osem)
            cp.start()
            cp.wait()

            @pl.when(end // ROUTE_RP == c)
            def _():
                g = (end - c * ROUTE_RP) // SUBLANES
                head_ref[e] = ostage[pl.ds(pl.multiple_of(g * SUBLANES, SUBLANES), SUBLANES), :]
            return carry2
        return lax.fori_loop(1, nchunks, per_chunk, carry)

    @pl.when(mt_ref[i] > 0)
    def _():
        lax.fori_loop(0, N_EXPERTS, per_expert, 0)

    @pl.when(i == nt - 1)
    def _():
        for e in range(N_EXPERTS):
            chunk_copy(e, slot).wait()


def _dispatch(plan, idx_t, xn2):
    t = xn2.shape[0]
    tm = ROUTE_TM
    cap = _sorted_rows(t)
    grid_spec = pltpu.PrefetchScalarGridSpec(
        num_scalar_prefetch=6,
        grid=(t // tm,),
        in_specs=[
            pl.BlockSpec((TOP_K, tm), lambda i, *_: (0, i)),
            pl.BlockSpec((1, N_EXPERTS, 1), lambda i, *_: (i, 0, 0)),
            pl.BlockSpec((tm, D_MODEL), lambda i, *_: (i, 0)),
        ],
        out_specs=[
            pl.BlockSpec(memory_space=pl.ANY),
            pl.BlockSpec((TOP_K, tm), lambda i, *_: (0, i)),
        ],
        scratch_shapes=[
            pltpu.VMEM((2, N_EXPERTS * ROUTE_RP, HALF), U32),
            pltpu.VMEM((ROUTE_RP, HALF), U32),
            pltpu.VMEM((EXP_BM, HALF), U32),
            pltpu.VMEM((N_EXPERTS, SUBLANES, HALF), U32),
            pltpu.VMEM((N_EXPERTS, tm), F32),
            pltpu.VMEM((N_EXPERTS, tm), F32),
            pltpu.SemaphoreType.DMA((2,)),
            pltpu.SemaphoreType.DMA(()),
            pltpu.SemaphoreType.DMA(()),
        ],
    )
    return pl.pallas_call(
        _dispatch_kernel,
        grid_spec=grid_spec,
        out_shape=[
            jax.ShapeDtypeStruct((cap, HALF), U32),
            jax.ShapeDtypeStruct((TOP_K, t), I32),
        ],
        compiler_params=pltpu.CompilerParams(
            dimension_semantics=("arbitrary",), vmem_limit_bytes=_vmem_limit(32 << 20)),
        name="dispatch",
    )(plan["tbase"], plan["cnt"], plan["multi"], plan["last_blk"], plan["two_blk"], plan["n_used"], idx_t,
      plan["tbase"].reshape(t // tm, N_EXPERTS, 1), xn2)


def _expert_kernel(be_ref, src_ref, nu_ref, live_ref, xs_ref, wg_hbm, wu_hbm, wd_hbm, bg_ref, bu_ref, bd_ref,
                   ys_ref, wf32, wbf, wsem):
    b = pl.program_id(0)
    e = be_ref[b]

    def weight_copies(expert, par):
        return [pltpu.make_async_copy(w.at[expert], wf32.at[par, m], wsem.at[par])
                for m, w in enumerate((wg_hbm, wu_hbm, wd_hbm))]

    @pl.when(b == 0)
    def _():
        for cp in weight_copies(e, e % 2):
            cp.start()

    @pl.when(b < nu_ref[0])
    def _():
        @pl.when((b == 0) | (e != be_ref[jnp.maximum(b - 1, 0)]))
        def _():
            par = e % 2
            for cp in weight_copies(e, par):
                cp.wait()

            @pl.when(e + 1 < N_EXPERTS)
            def _():
                for cp in weight_copies(e + 1, 1 - par):
                    cp.start()

            for m in range(3):
                wbf[m] = wf32[par, m].astype(BF16)

        def ffn(rows):
            x = _unpack_rows(xs_ref[rows, :])
            gt = jnp.minimum(jnp.dot(x, wbf[0], preferred_element_type=F32) + bg_ref[0], SWIGLU_LIMIT)
            up = jnp.clip(jnp.dot(x, wbf[1], preferred_element_type=F32) + bu_ref[0],
                          -SWIGLU_LIMIT, SWIGLU_LIMIT)
            hdn = (up + 1.0) * (gt * jax.nn.sigmoid(SWIGLU_ALPHA * gt))
            y = jnp.dot(hdn.astype(BF16), wbf[2], preferred_element_type=F32) + bd_ref[0]
            ys_ref[rows, :] = _pack_rows(y.astype(BF16).astype(F32))

        half = EXP_BM // 2

        @pl.when(live_ref[b] > half)
        def _():
            ffn(slice(0, EXP_BM))

        @pl.when(live_ref[b] <= half)
        def _():
            ffn(slice(0, half))
            ys_ref[half:, :] = jnp.zeros((EXP_BM - half, HALF), U32)

    @pl.when(b >= nu_ref[0])
    def _():
        ys_ref[...] = jnp.zeros(ys_ref.shape, U32)


def _experts(plan, xs, w_gate, b_gate, w_up, b_up, w_down, b_down):
    cap = xs.shape[0]
    nb = cap // EXP_BM
    rows = lambda b, be, src, nu, live: (src[b], 0)
    wsel = lambda b, be, src, nu, live: (be[b], 0, 0)
    wspec = pl.BlockSpec(memory_space=pl.ANY)
    bspec = pl.BlockSpec((1, 1, D_FF), wsel)
    grid_spec = pltpu.PrefetchScalarGridSpec(
        num_scalar_prefetch=4,
        grid=(nb,),
        in_specs=[pl.BlockSpec((EXP_BM, HALF), rows), wspec, wspec, wspec, bspec, bspec, bspec],
        out_specs=pl.BlockSpec((EXP_BM, HALF), lambda b, be, src, nu, live: (b, 0)),
        scratch_shapes=[pltpu.VMEM((2, 3, D_MODEL, D_FF), F32),
                        pltpu.VMEM((3, D_MODEL, D_FF), BF16),
                        pltpu.SemaphoreType.DMA((2,))],
    )
    return pl.pallas_call(
        _expert_kernel,
        grid_spec=grid_spec,
        out_shape=jax.ShapeDtypeStruct((cap, HALF), U32),
        compiler_params=pltpu.CompilerParams(
            dimension_semantics=("arbitrary",), vmem_limit_bytes=_vmem_limit(56 << 20)),
        name="experts",
    )(plan["blk_exp"], plan["blk_src"], plan["n_used"], plan["blk_live"], xs, w_gate, w_up, w_down,
      b_gate.reshape(N_EXPERTS, 1, D_FF), b_up.reshape(N_EXPERTS, 1, D_FF),
      b_down.reshape(N_EXPERTS, 1, D_MODEL))


def _combine_kernel(tb_ref, cnt_ref, mt_ref, h_ref, idx_ref, loc_ref, gate_ref, ys_hbm, g_ref, o_ref,
                    ybuf, obuf, acc_ref, sem, osem):
    i = pl.program_id(0)
    nt = pl.num_programs(0)
    slot = i % 2
    tm = h_ref.shape[0]

    def geometry(step, e):
        return _chunk_geometry(tb_ref[step * N_EXPERTS + e], cnt_ref[step * N_EXPERTS + e])

    def chunk_copy(step, e, s):
        start, _, _ = geometry(step, e)
        return pltpu.make_async_copy(ys_hbm.at[pl.ds(start, ROUTE_RP)],
                                     ybuf.at[s, pl.ds(e * ROUTE_RP, ROUTE_RP)], sem.at[s])

    @pl.when(i == 0)
    def _():
        for e in range(N_EXPERTS):
            chunk_copy(0, e, 0).start()

    @pl.when(i + 1 < nt)
    def _():
        for e in range(N_EXPERTS):
            chunk_copy(i + 1, e, 1 - slot).start()

    def gate_matrix(rows, ncols, col_of):
        n = rows.stop - rows.start
        gate = gate_ref[rows, :]
        g_hi = gate.astype(BF16).astype(F32)
        g_lo = gate - g_hi
        colio = lax.broadcasted_iota(I32, (n, ncols), 1)
        hi = jnp.zeros((n, ncols), F32)
        lo = jnp.zeros((n, ncols), F32)
        for k in range(TOP_K):
            hit = colio == col_of[:, k:k + 1]
            hi = jnp.where(hit, g_hi[:, k:k + 1], hi)
            lo = jnp.where(hit, g_lo[:, k:k + 1], lo)
        return jnp.concatenate([hi, lo], axis=0).astype(BF16)

    for e in range(N_EXPERTS):
        chunk_copy(i, e, slot).wait()
    y = _unpack_rows(ybuf[slot])
    for s in range(tm // COMBINE_SUB):
        rows = slice(s * COMBINE_SUB, (s + 1) * COMBINE_SUB)
        idx = idx_ref[rows, :]
        loc = loc_ref[rows, :]
        col = jnp.where(loc < ROUTE_RP, idx * ROUTE_RP + loc, -1)
        z = jnp.dot(gate_matrix(rows, N_EXPERTS * ROUTE_RP, col), y, preferred_element_type=F32)
        acc_ref[rows, :] = z[:COMBINE_SUB] + z[COMBINE_SUB:]

    def per_expert(e, carry):
        start, _, nchunks = geometry(i, e)

        def per_chunk(c, carry2):
            cp = pltpu.make_async_copy(
                ys_hbm.at[pl.ds(pl.multiple_of(start + c * ROUTE_RP, SUBLANES), ROUTE_RP)], obuf, osem)
            cp.start()
            cp.wait()
            ccol = jnp.where(idx_ref[...] == e, loc_ref[...] - c * ROUTE_RP, -1)
            zc = jnp.dot(gate_matrix(slice(0, tm), ROUTE_RP, ccol), _unpack_rows(obuf[...]),
                         preferred_element_type=F32)
            acc_ref[...] += zc[:tm] + zc[tm:]
            return carry2
        return lax.fori_loop(1, nchunks, per_chunk, carry)

    @pl.when(mt_ref[i] > 0)
    def _():
        lax.fori_loop(0, N_EXPERTS, per_expert, 0)

    out = h_ref[...] + acc_ref[...]
    ms = jnp.mean(out * out, axis=-1, keepdims=True)
    o_ref[...] = out * lax.rsqrt(ms + EPS) * g_ref[...]


def _combine(plan, h, ys, idx_tm, loc_tm, gate_tm, g_final):
    t = h.shape[0]
    tm = ROUTE_TM
    tok = lambda i, *_: (i, 0)
    grid_spec = pltpu.PrefetchScalarGridSpec(
        num_scalar_prefetch=3,
        grid=(t // tm,),
        in_specs=[
            pl.BlockSpec((tm, D_MODEL), tok),
            pl.BlockSpec((tm, TOP_K), tok),
            pl.BlockSpec((tm, TOP_K), tok),
            pl.BlockSpec((tm, TOP_K), tok),
            pl.BlockSpec(memory_space=pl.ANY),
            pl.BlockSpec((1, D_MODEL), lambda i, *_: (0, 0)),
        ],
        out_specs=pl.BlockSpec((tm, D_MODEL), tok),
        scratch_shapes=[
            pltpu.VMEM((2, N_EXPERTS * ROUTE_RP, HALF), U32),
            pltpu.VMEM((ROUTE_RP, HALF), U32),
            pltpu.VMEM((tm, D_MODEL), F32),
            pltpu.SemaphoreType.DMA((2,)),
            pltpu.SemaphoreType.DMA(()),
        ],
    )
    return pl.pallas_call(
        _combine_kernel,
        grid_spec=grid_spec,
        out_shape=jax.ShapeDtypeStruct((t, D_MODEL), F32),
        compiler_params=pltpu.CompilerParams(
            dimension_semantics=("arbitrary",), vmem_limit_bytes=_vmem_limit(32 << 20)),
        name="combine",
    )(plan["tbase"], plan["cnt"], plan["multi"], h, idx_tm, loc_tm, gate_tm, ys,
      g_final.reshape(1, D_MODEL))


def _moe(h, xn2, idx_t, gate_t, w_gate, b_gate, w_up, b_up, w_down, b_down, g_final):
    t = h.shape[0]
    plan = _route_plan(idx_t, t)
    xs, loc_t = _dispatch(plan, idx_t, xn2)
    ys = _experts(plan, xs, w_gate, b_gate, w_up, b_up, w_down, b_down)
    return _combine(plan, h, ys, idx_t.T, loc_t.T, gate_t.T, g_final)


def kernel(x, g_mix, w_in, b_in, conv_w, conv_b, ln_g, ln_b, rpb, w_out, b_out, g_ffn, w_router,
           b_router, w_gate, b_gate, w_up, b_up, w_down, b_down, g_final):
    bsz, seq, d = x.shape
    assert bsz == 1 and d == D_MODEL and g_mix.shape[0] == 1
    assert seq % (GRID_W * WIN_H) == 0
    x2 = x.reshape(seq, d)
    u, q, k, v = _inproj(x2, g_mix[0], w_in[0], b_in[0])
    conv_out = _conv(u, conv_w[0], conv_b[0], ln_g[0], ln_b[0])
    na_out = _attention(q, k, v, rpb[0])
    h, xn2, idx_t, gate_t = _outproj_router(conv_out, na_out, x2, w_out[0], b_out[0], g_ffn[0],
                                            w_router[0], b_router[0])
    out = _moe(h, xn2, idx_t, gate_t, w_gate[0], b_gate[0], w_up[0], b_up[0], w_down[0], b_down[0],
               g_final)
    return out.reshape(bsz, seq, d)
```

```python
import functools

import numpy as np
import jax
import jax.numpy as jnp
from jax import lax
from jax.experimental import pallas as pl
from jax.experimental.pallas import tpu as pltpu

F32 = jnp.float32
BF16 = jnp.bfloat16
U32 = jnp.uint32
I32 = jnp.int32

D_MODEL = 1024
GRID_W = 64
D_CONV = 512
CONV_K = 31
NA_HEADS = 8
NA_HEAD_DIM = 64
D_NA = NA_HEADS * NA_HEAD_DIM
D_IN = 2 * D_CONV + 3 * D_NA
WIN_H = 8
WIN_W = 16
N_EXPERTS = 32
TOP_K = 4
D_FF = D_MODEL
SWIGLU_LIMIT = 7.0
SWIGLU_ALPHA = 1.702
EPS = 1e-5
MASK_VALUE = -1e30

V7X_VMEM_BYTES = 64 * 1024 * 1024
LANES = 128

PROJ_TM = 512
ROUTER_SUB = 256
CONV_TT = 256
CONV_HALO = 16
CONV_CH = 32
ATTN_RB = 32
ATTN_UNROLL = 4
ROUTE_TM = 256
COMBINE_SUB = 256
SUBLANES = 8
ROUTE_RP = 56
EXP_BM = 512
EXP_PARTS = (512, 256, 128)
HALF = D_MODEL // 2


def _vmem_limit(nbytes):
    return int(min(nbytes, V7X_VMEM_BYTES - 6 * 1024 * 1024))


def _pack_rows(x):
    lo = lax.bitcast_convert_type(x[:, :HALF], U32) >> 16
    hi = lax.bitcast_convert_type(x[:, HALF:], U32) & jnp.uint32(0xFFFF0000)
    return lo | hi


def _unpack_rows(w):
    lo = lax.bitcast_convert_type(w << 16, F32)
    hi = lax.bitcast_convert_type(w & jnp.uint32(0xFFFF0000), F32)
    return jnp.concatenate([lo, hi], axis=1).astype(BF16)


def _inproj_kernel(x_ref, g_ref, w_ref, b_ref, u_ref, q_ref, k_ref, v_ref):
    x = x_ref[...]
    ms = jnp.mean(x * x, axis=-1, keepdims=True)
    xn = (x * lax.rsqrt(ms + EPS) * g_ref[...]).astype(BF16)

    def proj(c):
        sl = slice(c * D_CONV, (c + 1) * D_CONV)
        return jnp.dot(xn, w_ref[:, sl], preferred_element_type=F32) + b_ref[:, sl]

    u_ref[...] = proj(0) * jax.nn.sigmoid(proj(1))
    q_ref[...] = (proj(2) * (NA_HEAD_DIM ** -0.5)).astype(BF16)
    k_ref[...] = proj(3).astype(BF16)
    v_ref[...] = proj(4).astype(BF16)


def _inproj(x2, g_mix, w_in, b_in):
    t = x2.shape[0]
    tm = min(PROJ_TM, t)
    tok = lambda i: (i, 0)
    const = lambda i: (0, 0)
    return pl.pallas_call(
        _inproj_kernel,
        grid=(t // tm,),
        in_specs=[
            pl.BlockSpec((tm, D_MODEL), tok),
            pl.BlockSpec((1, D_MODEL), const),
            pl.BlockSpec((D_MODEL, D_IN), const),
            pl.BlockSpec((1, D_IN), const),
        ],
        out_specs=[
            pl.BlockSpec((tm, D_CONV), tok),
            pl.BlockSpec((tm, D_NA), tok),
            pl.BlockSpec((tm, D_NA), tok),
            pl.BlockSpec((tm, D_NA), tok),
        ],
        out_shape=[
            jax.ShapeDtypeStruct((t, D_CONV), F32),
            jax.ShapeDtypeStruct((t, D_NA), BF16),
            jax.ShapeDtypeStruct((t, D_NA), BF16),
            jax.ShapeDtypeStruct((t, D_NA), BF16),
        ],
        compiler_params=pltpu.CompilerParams(
            dimension_semantics=("arbitrary",), vmem_limit_bytes=_vmem_limit(48 << 20)),
        name="inproj",
    )(x2, g_mix.reshape(1, D_MODEL), w_in.astype(BF16), b_in.reshape(1, D_IN))


def _conv_kernel(prev_ref, cur_ref, next_ref, w_ref, cb_ref, lg_ref, lb_ref, o_ref, ext_ref, sh_ref):
    i = pl.program_id(0)
    n = pl.num_programs(0)
    tt = cur_ref.shape[0]
    ext_ref[0:CONV_HALO, :] = jnp.where(i > 0, prev_ref[...], 0.0)
    ext_ref[CONV_HALO:CONV_HALO + tt, :] = cur_ref[...]
    ext_ref[CONV_HALO + tt:2 * CONV_HALO + tt, :] = jnp.where(i < n - 1, next_ref[...], 0.0)
    base = CONV_HALO - CONV_K // 2
    span = (CONV_K - 1) // SUBLANES * SUBLANES
    for b in range(SUBLANES):
        sh_ref[b] = ext_ref[base + b:base + b + tt + span, :]
    for c in range(tt // CONV_CH):
        acc = jnp.zeros((CONV_CH, D_CONV), F32)
        for b in range(SUBLANES):
            for a in range((CONV_K - 1 - b) // SUBLANES + 1):
                r = c * CONV_CH + SUBLANES * a
                acc = acc + sh_ref[b, r:r + CONV_CH, :] * w_ref[SUBLANES * a + b:SUBLANES * a + b + 1, :]
        acc = acc + cb_ref[...]
        mu = jnp.mean(acc, axis=-1, keepdims=True)
        d = acc - mu
        var = jnp.mean(d * d, axis=-1, keepdims=True)
        un = d * lax.rsqrt(var + EPS) * lg_ref[...] + lb_ref[...]
        o_ref[c * CONV_CH:(c + 1) * CONV_CH, :] = (un * jax.nn.sigmoid(un)).astype(o_ref.dtype)


def _conv(u, conv_w, conv_b, ln_g, ln_b):
    t = u.shape[0]
    tt = min(CONV_TT, t)
    hb = tt // CONV_HALO
    nhb = t // CONV_HALO
    const = lambda i: (0, 0)
    w = jnp.zeros((CONV_K + 1, D_CONV), F32).at[:CONV_K].set(conv_w)
    return pl.pallas_call(
        _conv_kernel,
        grid=(t // tt,),
        in_specs=[
            pl.BlockSpec((CONV_HALO, D_CONV), lambda i: (jnp.maximum(i * hb - 1, 0), 0)),
            pl.BlockSpec((tt, D_CONV), lambda i: (i, 0)),
            pl.BlockSpec((CONV_HALO, D_CONV), lambda i: (jnp.minimum((i + 1) * hb, nhb - 1), 0)),
            pl.BlockSpec((CONV_K + 1, D_CONV), const),
            pl.BlockSpec((1, D_CONV), const),
            pl.BlockSpec((1, D_CONV), const),
            pl.BlockSpec((1, D_CONV), const),
        ],
        out_specs=pl.BlockSpec((tt, D_CONV), lambda i: (i, 0)),
        out_shape=jax.ShapeDtypeStruct((t, D_CONV), BF16),
        scratch_shapes=[pltpu.VMEM((tt + 2 * CONV_HALO, D_CONV), F32),
                        pltpu.VMEM((SUBLANES, tt + (CONV_K - 1) // SUBLANES * SUBLANES, D_CONV), F32)],
        compiler_params=pltpu.CompilerParams(dimension_semantics=("arbitrary",)),
        name="conv",
    )(u, u, u, w, conv_b.reshape(1, D_CONV), ln_g.reshape(1, D_CONV), ln_b.reshape(1, D_CONV))


def _bias_table(rpb):
    col = np.arange(GRID_W)
    c0 = np.clip(col - WIN_W // 2, 0, GRID_W - WIN_W)
    valid = (col[None, :] >= c0[:, None]) & (col[None, :] < c0[:, None] + WIN_W)
    dcol = np.clip(col[None, :] - col[:, None], -(WIN_W - 1), WIN_W - 1) + (WIN_W - 1)
    onehot = ((dcol[:, :, None] == np.arange(2 * WIN_W - 1)) & valid[:, :, None]).astype(np.float32)
    tcol = jnp.einsum("hdj,ckj->hdck", rpb.astype(F32), onehot, precision=lax.Precision.HIGHEST)
    tcol = jnp.where(valid[None, None], tcol, MASK_VALUE)
    b = jnp.stack([tcol[:, WIN_H - 1 - c:2 * WIN_H - 1 - c] for c in range(WIN_H)], axis=1)
    b = jnp.transpose(b, (0, 1, 3, 2, 4))
    return b.reshape(NA_HEADS, WIN_H, GRID_W, WIN_H * GRID_W)


def _attn_kernel(q_ref, k_ref, v_ref, bt_ref, o_ref, s_ref, *, rb, rows):
    jb = pl.program_id(1)
    lane = lax.broadcasted_iota(jnp.int32, (GRID_W, LANES), 1)
    first_head = lane < NA_HEAD_DIM
    nkeys = WIN_H * GRID_W

    def window(r):
        i = jb * rb + r
        r0 = jnp.clip(i - WIN_H // 2, 0, rows - WIN_H)
        return i - r0, pl.multiple_of(r0 * GRID_W, GRID_W)

    def scores(r):
        _, start = window(r)
        q2 = q_ref[pl.ds(pl.multiple_of(r * GRID_W, GRID_W), GRID_W), :]
        zero = jnp.zeros_like(q2)
        qst = jnp.concatenate([jnp.where(first_head, q2, zero), jnp.where(first_head, zero, q2)], axis=0)
        ks = k_ref[pl.ds(start, nkeys), :]
        return lax.dot_general(qst, ks, (((1,), (1,)), ((), ())), preferred_element_type=F32)

    def finish(r, s):
        c, start = window(r)
        vs = v_ref[pl.ds(start, nkeys), :]
        bias = jnp.concatenate([bt_ref[0, c], bt_ref[1, c]], axis=0)
        s = jnp.where(bias > 0.5 * MASK_VALUE, s + bias, MASK_VALUE)
        m = jnp.max(s, axis=-1, keepdims=True)
        p = jnp.exp(s - m)
        l = jnp.sum(p, axis=-1, keepdims=True)
        o = jnp.dot(p.astype(BF16), vs, preferred_element_type=F32) / l
        o2 = jnp.where(first_head, o[:GRID_W], o[GRID_W:])
        o_ref[pl.ds(pl.multiple_of(r * GRID_W, GRID_W), GRID_W), :] = o2.astype(o_ref.dtype)

    s_ref[0] = scores(0)

    def pair(p, carry):
        r = 2 * p
        s_ref[1] = scores(r + 1)
        finish(r, s_ref[0])
        s_ref[0] = scores(jnp.minimum(r + 2, rb - 1))
        finish(r + 1, s_ref[1])
        return carry

    lax.fori_loop(0, rb // 2, pair, 0, unroll=ATTN_UNROLL)


def _attention(q, k, v, rpb):
    t = q.shape[0]
    rows = t // GRID_W
    rb = min(ATTN_RB, rows)
    npairs = D_NA // LANES
    bt = _bias_table(rpb)
    kern = functools.partial(_attn_kernel, rb=rb, rows=rows)
    return pl.pallas_call(
        kern,
        grid=(npairs, rows // rb),
        in_specs=[
            pl.BlockSpec((rb * GRID_W, LANES), lambda p, j: (j, p)),
            pl.BlockSpec((t, LANES), lambda p, j: (0, p)),
            pl.BlockSpec((t, LANES), lambda p, j: (0, p)),
            pl.BlockSpec((2, WIN_H, GRID_W, WIN_H * GRID_W), lambda p, j: (p, 0, 0, 0)),
        ],
        out_specs=pl.BlockSpec((rb * GRID_W, LANES), lambda p, j: (j, p)),
        out_shape=jax.ShapeDtypeStruct((t, D_NA), BF16),
        scratch_shapes=[pltpu.VMEM((2, 2 * GRID_W, WIN_H * GRID_W), F32)],
        compiler_params=pltpu.CompilerParams(
            dimension_semantics=("arbitrary", "arbitrary"), vmem_limit_bytes=_vmem_limit(40 << 20)),
        name="attention",
    )(q, k, v, bt)


def _outproj_kernel(conv_ref, na_ref, x_ref, wo_ref, bo_ref, g_ref, wr_ref, br_ref,
                    h_ref, xn_ref, idx_ref, gate_ref):
    for s in range(x_ref.shape[0] // ROUTER_SUB):
        rows = slice(s * ROUTER_SUB, (s + 1) * ROUTER_SUB)
        mixed = jnp.concatenate([conv_ref[rows, :], na_ref[rows, :]], axis=-1)
        h = x_ref[rows, :] + jnp.dot(mixed, wo_ref[...], preferred_element_type=F32) + bo_ref[...]
        h_ref[rows, :] = h
        ms = jnp.mean(h * h, axis=-1, keepdims=True)
        xn = (h * lax.rsqrt(ms + EPS) * g_ref[...]).astype(BF16)
        xn_ref[rows, :] = xn
        logits = lax.dot_general(wr_ref[...], xn, (((1,), (1,)), ((), ())),
                                 preferred_element_type=F32) + br_ref[...]
        ids = lax.broadcasted_iota(jnp.int32, logits.shape, 0)
        vals, sels = [], []
        l = logits
        for _ in range(TOP_K):
            m = jnp.max(l, axis=0, keepdims=True)
            sel = jnp.min(jnp.where(l == m, ids, N_EXPERTS), axis=0, keepdims=True)
            vals.append(m)
            sels.append(sel)
            l = jnp.where(ids == sel, -jnp.inf, l)
        es = [jnp.exp(vk - vals[0]) for vk in vals]
        tot = es[0] + es[1] + es[2] + es[3]
        idx_ref[:, rows] = jnp.concatenate(sels, axis=0)
        gate_ref[:, rows] = jnp.concatenate([e / tot for e in es], axis=0)


def _outproj_router(conv_out, na_out, x2, w_out, b_out, g_ffn, w_router, b_router):
    t = x2.shape[0]
    tm = min(PROJ_TM, t)
    tok = lambda i: (i, 0)
    const = lambda i: (0, 0)
    return pl.pallas_call(
        _outproj_kernel,
        grid=(t // tm,),
        in_specs=[
            pl.BlockSpec((tm, D_CONV), tok),
            pl.BlockSpec((tm, D_NA), tok),
            pl.BlockSpec((tm, D_MODEL), tok),
            pl.BlockSpec((D_MODEL, D_MODEL), const),
            pl.BlockSpec((1, D_MODEL), const),
            pl.BlockSpec((1, D_MODEL), const),
            pl.BlockSpec((N_EXPERTS, D_MODEL), const),
            pl.BlockSpec((N_EXPERTS, 1), const),
        ],
        out_specs=[
            pl.BlockSpec((tm, D_MODEL), tok),
            pl.BlockSpec((tm, D_MODEL), tok),
            pl.BlockSpec((TOP_K, tm), lambda i: (0, i)),
            pl.BlockSpec((TOP_K, tm), lambda i: (0, i)),
        ],
        out_shape=[
            jax.ShapeDtypeStruct((t, D_MODEL), F32),
            jax.ShapeDtypeStruct((t, D_MODEL), BF16),
            jax.ShapeDtypeStruct((TOP_K, t), jnp.int32),
            jax.ShapeDtypeStruct((TOP_K, t), F32),
        ],
        compiler_params=pltpu.CompilerParams(
            dimension_semantics=("arbitrary",), vmem_limit_bytes=_vmem_limit(48 << 20)),
        name="outproj_router",
    )(conv_out, na_out, x2, w_out.astype(BF16), b_out.reshape(1, D_MODEL), g_ffn.reshape(1, D_MODEL),
      w_router.T.astype(BF16), b_router.reshape(N_EXPERTS, 1))


def _route_plan(idx_t, t):
    nt = t // ROUTE_TM
    experts = jnp.arange(N_EXPERTS, dtype=I32)
    onehot = idx_t.reshape(TOP_K, nt, ROUTE_TM, 1) == experts
    cnt = jnp.sum(onehot, axis=(0, 2), dtype=I32)
    sizes = jnp.sum(cnt, axis=0)
    padded = (sizes + ROUTE_RP + EXP_BM - 1) // EXP_BM * EXP_BM
    pad_ends = jnp.cumsum(padded)
    pad_off = pad_ends - padded
    tbase = pad_off[None, :] + jnp.cumsum(cnt, axis=0) - cnt
    cap = _sorted_rows(t)
    nb = cap // EXP_BM
    n_used = pad_ends[N_EXPERTS - 1] // EXP_BM
    blk_src = jnp.minimum(jnp.arange(nb, dtype=I32), n_used - 1)
    blk_exp = jnp.minimum(
        jnp.sum(blk_src[:, None] * EXP_BM >= pad_ends[None, :], axis=1, dtype=I32), N_EXPERTS - 1)
    row_end = jnp.sum(jnp.where(blk_exp[:, None] == experts[None, :], (pad_off + sizes)[None, :], 0), axis=1)
    blk_live = jnp.clip(row_end - blk_src * EXP_BM, 0, EXP_BM).astype(I32)
    multi = jnp.any(tbase % SUBLANES + cnt > ROUTE_RP, axis=1).astype(I32)
    return dict(cnt=cnt.reshape(-1), tbase=tbase.reshape(-1).astype(I32), multi=multi,
                last_blk=(pad_ends - EXP_BM).astype(I32), two_blk=(padded >= 2 * EXP_BM).astype(I32),
                blk_src=blk_src, blk_exp=blk_exp, blk_live=blk_live, n_used=n_used.reshape(1).astype(I32))


def _sorted_rows(t):
    cap = t * TOP_K + N_EXPERTS * (EXP_BM + ROUTE_RP)
    return (cap + EXP_BM - 1) // EXP_BM * EXP_BM


def _chunk_geometry(tb, n):
    head = tb % SUBLANES
    start = pl.multiple_of(tb - head, SUBLANES)
    nchunks = (head + n + ROUTE_RP - 1) // ROUTE_RP
    return start, head, nchunks


def _dispatch_kernel(tb_ref, cnt_ref, mt_ref, lb_ref, two_ref, nu_ref, idx_ref, tbv_ref, xn_ref, xs_hbm, loc_ref,
                     stage, ostage, zbuf, head_ref, cs_ref, ms_ref, sem, osem, zsem):
    i = pl.program_id(0)
    nt = pl.num_programs(0)
    slot = i % 2
    tm = idx_ref.shape[1]
    groups = ROUTE_RP // SUBLANES

    def geometry(e):
        return _chunk_geometry(tb_ref[i * N_EXPERTS + e], cnt_ref[i * N_EXPERTS + e])

    def chunk_copy(e, s):
        start, _, _ = geometry(e)
        return pltpu.make_async_copy(stage.at[s, pl.ds(e * ROUTE_RP, ROUTE_RP)],
                                     xs_hbm.at[pl.ds(start, ROUTE_RP)], sem.at[s])

    def zero_copy(start):
        return pltpu.make_async_copy(zbuf, xs_hbm.at[pl.ds(pl.multiple_of(start, EXP_BM), EXP_BM)], zsem)

    @pl.when(i == 0)
    def _():
        zbuf[...] = jnp.zeros(zbuf.shape, U32)
        head_ref[...] = jnp.zeros(head_ref.shape, U32)
        for phase in ("start", "wait"):
            for e in range(N_EXPERTS):
                getattr(zero_copy(lb_ref[e]), phase)()

                @pl.when(two_ref[e] > 0)
                def _():
                    getattr(zero_copy(lb_ref[e] - EXP_BM), phase)()

        def unused_block(bi, carry):
            cp = zero_copy(bi * EXP_BM)
            cp.start()
            cp.wait()
            return carry
        lax.fori_loop(nu_ref[0], xs_hbm.shape[0] // EXP_BM, unused_block, 0)

    idx = idx_ref[...]
    eio = lax.broadcasted_iota(I32, (N_EXPERTS, tm), 0)
    member = jnp.zeros((N_EXPERTS, tm), F32)
    for k in range(TOP_K):
        member = member + (idx[k:k + 1, :] == eio).astype(F32)
    tri = (lax.broadcasted_iota(I32, (tm, tm), 0) < lax.broadcasted_iota(I32, (tm, tm), 1)).astype(BF16)
    rank = jnp.dot(member.astype(BF16), tri, preferred_element_type=F32)
    pos = rank + (tbv_ref[0] % SUBLANES).astype(F32)
    loc_ref[...] = jnp.concatenate(
        [jnp.sum(jnp.where(idx[k:k + 1, :] == eio, pos, 0.0), axis=0, keepdims=True) for k in range(TOP_K)],
        axis=0).astype(I32)
    cs_ref[...] = pos
    ms_ref[...] = member

    jio = lax.broadcasted_iota(I32, (ROUTE_RP, tm), 0).astype(F32)
    sel = [jnp.where((jio == pos[e:e + 1, :]) & (member[e:e + 1, :] > 0.0), 1.0, 0.0).astype(BF16)
           for e in range(N_EXPERTS)]
    rows = jnp.dot(jnp.concatenate(sel, axis=0), xn_ref[...], preferred_element_type=F32)

    @pl.when(i > 0)
    def _():
        for e in range(N_EXPERTS):
            chunk_copy(e, 1 - slot).wait()

    stage[slot] = _pack_rows(rows)
    sub = lax.broadcasted_iota(I32, (SUBLANES, HALF), 0)
    for e in range(N_EXPERTS):
        _, head, _ = geometry(e)
        first = pl.ds(e * ROUTE_RP, SUBLANES)
        stage[slot, first, :] = jnp.where(sub < head, head_ref[e], stage[slot, first, :])
        chunk_copy(e, slot).start()
        g = jnp.minimum((head + cnt_ref[i * N_EXPERTS + e]) // SUBLANES, groups - 1)
        head_ref[e] = stage[slot, pl.ds(pl.multiple_of(e * ROUTE_RP + g * SUBLANES, SUBLANES), SUBLANES), :]

    def per_expert(e, carry):
        start, head, nchunks = geometry(e)
        end = head + cnt_ref[i * N_EXPERTS + e]

        def per_chunk(c, carry2):
            want = jio + (c * ROUTE_RP).astype(F32)
            pick = jnp.where((want == cs_ref[pl.ds(e, 1), :]) & (ms_ref[pl.ds(e, 1), :] > 0.0), 1.0, 0.0)
            ostage[...] = _pack_rows(jnp.dot(pick.astype(BF16), xn_ref[...], preferred_element_type=F32))
            cp = pltpu.make_async_copy(
                ostage, xs_hbm.at[pl.ds(pl.multiple_of(start + c * ROUTE_RP, SUBLANES), ROUTE_RP)], osem)
            cp.start()
            cp.wait()

            @pl.when(end // ROUTE_RP == c)
            def _():
                g = (end - c * ROUTE_RP) // SUBLANES
                head_ref[e] = ostage[pl.ds(pl.multiple_of(g * SUBLANES, SUBLANES), SUBLANES), :]
            return carry2
        return lax.fori_loop(1, nchunks, per_chunk, carry)

    @pl.when(mt_ref[i] > 0)
    def _():
        lax.fori_loop(0, N_EXPERTS, per_expert, 0)

    @pl.when(i == nt - 1)
    def _():
        for e in range(N_EXPERTS):
            chunk_copy(e, slot).wait()


def _dispatch(plan, idx_t, xn2):
    t = xn2.shape[0]
    tm = ROUTE_TM
    cap = _sorted_rows(t)
    grid_spec = pltpu.PrefetchScalarGridSpec(
        num_scalar_prefetch=6,
        grid=(t // tm,),
        in_specs=[
            pl.BlockSpec((TOP_K, tm), lambda i, *_: (0, i)),
            pl.BlockSpec((1, N_EXPERTS, 1), lambda i, *_: (i, 0, 0)),
            pl.BlockSpec((tm, D_MODEL), lambda i, *_: (i, 0)),
        ],
        out_specs=[
            pl.BlockSpec(memory_space=pl.ANY),
            pl.BlockSpec((TOP_K, tm), lambda i, *_: (0, i)),
        ],
        scratch_shapes=[
            pltpu.VMEM((2, N_EXPERTS * ROUTE_RP, HALF), U32),
            pltpu.VMEM((ROUTE_RP, HALF), U32),
            pltpu.VMEM((EXP_BM, HALF), U32),
            pltpu.VMEM((N_EXPERTS, SUBLANES, HALF), U32),
            pltpu.VMEM((N_EXPERTS, tm), F32),
            pltpu.VMEM((N_EXPERTS, tm), F32),
            pltpu.SemaphoreType.DMA((2,)),
            pltpu.SemaphoreType.DMA(()),
            pltpu.SemaphoreType.DMA(()),
        ],
    )
    return pl.pallas_call(
        _dispatch_kernel,
        grid_spec=grid_spec,
        out_shape=[
            jax.ShapeDtypeStruct((cap, HALF), U32),
            jax.ShapeDtypeStruct((TOP_K, t), I32),
        ],
        compiler_params=pltpu.CompilerParams(
            dimension_semantics=("arbitrary",), vmem_limit_bytes=_vmem_limit(32 << 20)),
        name="dispatch",
    )(plan["tbase"], plan["cnt"], plan["multi"], plan["last_blk"], plan["two_blk"], plan["n_used"], idx_t,
      plan["tbase"].reshape(t // tm, N_EXPERTS, 1), xn2)


def _expert_kernel(be_ref, src_ref, nu_ref, live_ref, xs_ref, wg_hbm, wu_hbm, wd_hbm, bg_ref, bu_ref, bd_ref,
                   ys_ref, wf32, wbf, wsem):
    b = pl.program_id(0)
    e = be_ref[b]

    def weight_copies(expert, par):
        return [pltpu.make_async_copy(w.at[expert], wf32.at[par, m], wsem.at[par])
                for m, w in enumerate((wg_hbm, wu_hbm, wd_hbm))]

    @pl.when(b == 0)
    def _():
        for cp in weight_copies(e, e % 2):
            cp.start()

    @pl.when(b < nu_ref[0])
    def _():
        @pl.when((b == 0) | (e != be_ref[jnp.maximum(b - 1, 0)]))
        def _():
            par = e % 2
            for cp in weight_copies(e, par):
                cp.wait()

            @pl.when(e + 1 < N_EXPERTS)
            def _():
                for cp in weight_copies(e + 1, 1 - par):
                    cp.start()

            for m in range(3):
                wbf[m] = wf32[par, m].astype(BF16)

        def ffn(rows):
            x = _unpack_rows(xs_ref[rows, :])
            gt = jnp.minimum(jnp.dot(x, wbf[0], preferred_element_type=F32) + bg_ref[0], SWIGLU_LIMIT)
            up = jnp.clip(jnp.dot(x, wbf[1], preferred_element_type=F32) + bu_ref[0],
                          -SWIGLU_LIMIT, SWIGLU_LIMIT)
            hdn = (up + 1.0) * (gt * jax.nn.sigmoid(SWIGLU_ALPHA * gt))
            y = jnp.dot(hdn.astype(BF16), wbf[2], preferred_element_type=F32) + bd_ref[0]
            ys_ref[rows, :] = _pack_rows(y.astype(BF16).astype(F32))

        live = live_ref[b]
        for p, part in enumerate(EXP_PARTS):
            smaller = EXP_PARTS[p + 1] if p + 1 < len(EXP_PARTS) else -1
            cond = live > smaller
            if p > 0:
                cond = cond & (live <= part)

            @pl.when(cond)
            def _(part=part):
                ffn(slice(0, part))
                if part < EXP_BM:
                    ys_ref[part:, :] = jnp.zeros((EXP_BM - part, HALF), U32)

    @pl.when(b >= nu_ref[0])
    def _():
        ys_ref[...] = jnp.zeros(ys_ref.shape, U32)


def _experts(plan, xs, w_gate, b_gate, w_up, b_up, w_down, b_down):
    cap = xs.shape[0]
    nb = cap // EXP_BM
    rows = lambda b, be, src, nu, live: (src[b], 0)
    wsel = lambda b, be, src, nu, live: (be[b], 0, 0)
    wspec = pl.BlockSpec(memory_space=pl.ANY)
    bspec = pl.BlockSpec((1, 1, D_FF), wsel)
    grid_spec = pltpu.PrefetchScalarGridSpec(
        num_scalar_prefetch=4,
        grid=(nb,),
        in_specs=[pl.BlockSpec((EXP_BM, HALF), rows), wspec, wspec, wspec, bspec, bspec, bspec],
        out_specs=pl.BlockSpec((EXP_BM, HALF), lambda b, be, src, nu, live: (b, 0)),
        scratch_shapes=[pltpu.VMEM((2, 3, D_MODEL, D_FF), F32),
                        pltpu.VMEM((3, D_MODEL, D_FF), BF16),
                        pltpu.SemaphoreType.DMA((2,))],
    )
    return pl.pallas_call(
        _expert_kernel,
        grid_spec=grid_spec,
        out_shape=jax.ShapeDtypeStruct((cap, HALF), U32),
        compiler_params=pltpu.CompilerParams(
            dimension_semantics=("arbitrary",), vmem_limit_bytes=_vmem_limit(56 << 20)),
        name="experts",
    )(plan["blk_exp"], plan["blk_src"], plan["n_used"], plan["blk_live"], xs, w_gate, w_up, w_down,
      b_gate.reshape(N_EXPERTS, 1, D_FF), b_up.reshape(N_EXPERTS, 1, D_FF),
      b_down.reshape(N_EXPERTS, 1, D_MODEL))


def _combine_kernel(tb_ref, cnt_ref, mt_ref, h_ref, idx_ref, loc_ref, gate_ref, ys_hbm, g_ref, o_ref,
                    ybuf, obuf, acc_ref, sem, osem):
    i = pl.program_id(0)
    nt = pl.num_programs(0)
    slot = i % 2
    tm = h_ref.shape[0]

    def geometry(step, e):
        return _chunk_geometry(tb_ref[step * N_EXPERTS + e], cnt_ref[step * N_EXPERTS + e])

    def chunk_copy(step, e, s):
        start, _, _ = geometry(step, e)
        return pltpu.make_async_copy(ys_hbm.at[pl.ds(start, ROUTE_RP)],
                                     ybuf.at[s, pl.ds(e * ROUTE_RP, ROUTE_RP)], sem.at[s])

    @pl.when(i == 0)
    def _():
        for e in range(N_EXPERTS):
            chunk_copy(0, e, 0).start()

    @pl.when(i + 1 < nt)
    def _():
        for e in range(N_EXPERTS):
            chunk_copy(i + 1, e, 1 - slot).start()

    def gate_matrix(rows, ncols, col_of):
        n = rows.stop - rows.start
        gate = gate_ref[rows, :]
        g_hi = gate.astype(BF16).astype(F32)
        g_lo = gate - g_hi
        colio = lax.broadcasted_iota(I32, (n, ncols), 1)
        hi = jnp.zeros((n, ncols), F32)
        lo = jnp.zeros((n, ncols), F32)
        for k in range(TOP_K):
            hit = colio == col_of[:, k:k + 1]
            hi = jnp.where(hit, g_hi[:, k:k + 1], hi)
            lo = jnp.where(hit, g_lo[:, k:k + 1], lo)
        return jnp.concatenate([hi, lo], axis=0).astype(BF16)

    for e in range(N_EXPERTS):
        chunk_copy(i, e, slot).wait()
    y = _unpack_rows(ybuf[slot])
    for s in range(tm // COMBINE_SUB):
        rows = slice(s * COMBINE_SUB, (s + 1) * COMBINE_SUB)
        idx = idx_ref[rows, :]
        loc = loc_ref[rows, :]
        col = jnp.where(loc < ROUTE_RP, idx * ROUTE_RP + loc, -1)
        z = jnp.dot(gate_matrix(rows, N_EXPERTS * ROUTE_RP, col), y, preferred_element_type=F32)
        acc_ref[rows, :] = z[:COMBINE_SUB] + z[COMBINE_SUB:]

    def per_expert(e, carry):
        start, _, nchunks = geometry(i, e)

        def per_chunk(c, carry2):
            cp = pltpu.make_async_copy(
                ys_hbm.at[pl.ds(pl.multiple_of(start + c * ROUTE_RP, SUBLANES), ROUTE_RP)], obuf, osem)
            cp.start()
            cp.wait()
            ccol = jnp.where(idx_ref[...] == e, loc_ref[...] - c * ROUTE_RP, -1)
            zc = jnp.dot(gate_matrix(slice(0, tm), ROUTE_RP, ccol), _unpack_rows(obuf[...]),
                         preferred_element_type=F32)
            acc_ref[...] += zc[:tm] + zc[tm:]
            return carry2
        return lax.fori_loop(1, nchunks, per_chunk, carry)

    @pl.when(mt_ref[i] > 0)
    def _():
        lax.fori_loop(0, N_EXPERTS, per_expert, 0)

    out = h_ref[...] + acc_ref[...]
    ms = jnp.mean(out * out, axis=-1, keepdims=True)
    o_ref[...] = out * lax.rsqrt(ms + EPS) * g_ref[...]


def _combine(plan, h, ys, idx_tm, loc_tm, gate_tm, g_final):
    t = h.shape[0]
    tm = ROUTE_TM
    tok = lambda i, *_: (i, 0)
    grid_spec = pltpu.PrefetchScalarGridSpec(
        num_scalar_prefetch=3,
        grid=(t // tm,),
        in_specs=[
            pl.BlockSpec((tm, D_MODEL), tok),
            pl.BlockSpec((tm, TOP_K), tok),
            pl.BlockSpec((tm, TOP_K), tok),
            pl.BlockSpec((tm, TOP_K), tok),
            pl.BlockSpec(memory_space=pl.ANY),
            pl.BlockSpec((1, D_MODEL), lambda i, *_: (0, 0)),
        ],
        out_specs=pl.BlockSpec((tm, D_MODEL), tok),
        scratch_shapes=[
            pltpu.VMEM((2, N_EXPERTS * ROUTE_RP, HALF), U32),
            pltpu.VMEM((ROUTE_RP, HALF), U32),
            pltpu.VMEM((tm, D_MODEL), F32),
            pltpu.SemaphoreType.DMA((2,)),
            pltpu.SemaphoreType.DMA(()),
        ],
    )
    return pl.pallas_call(
        _combine_kernel,
        grid_spec=grid_spec,
        out_shape=jax.ShapeDtypeStruct((t, D_MODEL), F32),
        compiler_params=pltpu.CompilerParams(
            dimension_semantics=("arbitrary",), vmem_limit_bytes=_vmem_limit(32 << 20)),
        name="combine",
    )(plan["tbase"], plan["cnt"], plan["multi"], h, idx_tm, loc_tm, gate_tm, ys,
      g_final.reshape(1, D_MODEL))


def _moe(h, xn2, idx_t, gate_t, w_gate, b_gate, w_up, b_up, w_down, b_down, g_final):
    t = h.shape[0]
    plan = _route_plan(idx_t, t)
    xs, loc_t = _dispatch(plan, idx_t, xn2)
    ys = _experts(plan, xs, w_gate, b_gate, w_up, b_up, w_down, b_down)
    return _combine(plan, h, ys, idx_t.T, loc_t.T, gate_t.T, g_final)


def kernel(x, g_mix, w_in, b_in, conv_w, conv_b, ln_g, ln_b, rpb, w_out, b_out, g_ffn, w_router,
           b_router, w_gate, b_gate, w_up, b_up, w_down, b_down, g_final):
    bsz, seq, d = x.shape
    assert bsz == 1 and d == D_MODEL and g_mix.shape[0] == 1
    assert seq % (GRID_W * WIN_H) == 0
    x2 = x.reshape(seq, d)
    u, q, k, v = _inproj(x2, g_mix[0], w_in[0], b_in[0])
    conv_out = _conv(u, conv_w[0], conv_b[0], ln_g[0], ln_b[0])
    na_out = _attention(q, k, v, rpb[0])
    h, xn2, idx_t, gate_t = _outproj_router(conv_out, na_out, x2, w_out[0], b_out[0], g_ffn[0],
                                            w_router[0], b_router[0])
    out = _moe(h, xn2, idx_t, gate_t, w_gate[0], b_gate[0], w_up[0], b_up[0], w_down[0], b_down[0],
               g_final)
    return out.reshape(bsz, seq, d)
```

```python
import functools

import numpy as np
import jax
import jax.numpy as jnp
from jax import lax
from jax.experimental import pallas as pl
from jax.experimental.pallas import tpu as pltpu

F32 = jnp.float32
BF16 = jnp.bfloat16
U32 = jnp.uint32
I32 = jnp.int32

D_MODEL = 1024
GRID_W = 64
D_CONV = 512
CONV_K = 31
NA_HEADS = 8
NA_HEAD_DIM = 64
D_NA = NA_HEADS * NA_HEAD_DIM
D_IN = 2 * D_CONV + 3 * D_NA
WIN_H = 8
WIN_W = 16
N_EXPERTS = 32
TOP_K = 4
D_FF = D_MODEL
SWIGLU_LIMIT = 7.0
SWIGLU_ALPHA = 1.702
EPS = 1e-5
MASK_VALUE = -1e30

V7X_VMEM_BYTES = 64 * 1024 * 1024
LANES = 128

PROJ_TM = 512
ROUTER_SUB = 256
CONV_TT = 256
CONV_HALO = 16
CONV_CH = 32
ATTN_RB = 32
ATTN_UNROLL = 4
ROUTE_TM = 256
COMBINE_SUB = 256
SUBLANES = 8
ROUTE_RP = 56
EXP_BM = 1024
EXP_PARTS = (1024, 512, 256, 128)
ZERO_ROWS = 256
HALF = D_MODEL // 2


def _vmem_limit(nbytes):
    return int(min(nbytes, V7X_VMEM_BYTES - 6 * 1024 * 1024))


def _pack_rows(x):
    lo = lax.bitcast_convert_type(x[:, :HALF], U32) >> 16
    hi = lax.bitcast_convert_type(x[:, HALF:], U32) & jnp.uint32(0xFFFF0000)
    return lo | hi


def _unpack_rows(w):
    lo = lax.bitcast_convert_type(w << 16, F32)
    hi = lax.bitcast_convert_type(w & jnp.uint32(0xFFFF0000), F32)
    return jnp.concatenate([lo, hi], axis=1).astype(BF16)


def _inproj_kernel(x_ref, g_ref, w_ref, b_ref, u_ref, q_ref, k_ref, v_ref):
    x = x_ref[...]
    ms = jnp.mean(x * x, axis=-1, keepdims=True)
    xn = (x * lax.rsqrt(ms + EPS) * g_ref[...]).astype(BF16)

    def proj(c):
        sl = slice(c * D_CONV, (c + 1) * D_CONV)
        return jnp.dot(xn, w_ref[:, sl], preferred_element_type=F32) + b_ref[:, sl]

    u_ref[...] = proj(0) * jax.nn.sigmoid(proj(1))
    q_ref[...] = (proj(2) * (NA_HEAD_DIM ** -0.5)).astype(BF16)
    k_ref[...] = proj(3).astype(BF16)
    v_ref[...] = proj(4).astype(BF16)


def _inproj(x2, g_mix, w_in, b_in):
    t = x2.shape[0]
    tm = min(PROJ_TM, t)
    tok = lambda i: (i, 0)
    const = lambda i: (0, 0)
    return pl.pallas_call(
        _inproj_kernel,
        grid=(t // tm,),
        in_specs=[
            pl.BlockSpec((tm, D_MODEL), tok),
            pl.BlockSpec((1, D_MODEL), const),
            pl.BlockSpec((D_MODEL, D_IN), const),
            pl.BlockSpec((1, D_IN), const),
        ],
        out_specs=[
            pl.BlockSpec((tm, D_CONV), tok),
            pl.BlockSpec((tm, D_NA), tok),
            pl.BlockSpec((tm, D_NA), tok),
            pl.BlockSpec((tm, D_NA), tok),
        ],
        out_shape=[
            jax.ShapeDtypeStruct((t, D_CONV), F32),
            jax.ShapeDtypeStruct((t, D_NA), BF16),
            jax.ShapeDtypeStruct((t, D_NA), BF16),
            jax.ShapeDtypeStruct((t, D_NA), BF16),
        ],
        compiler_params=pltpu.CompilerParams(
            dimension_semantics=("arbitrary",), vmem_limit_bytes=_vmem_limit(48 << 20)),
        name="inproj",
    )(x2, g_mix.reshape(1, D_MODEL), w_in.astype(BF16), b_in.reshape(1, D_IN))


def _conv_kernel(prev_ref, cur_ref, next_ref, w_ref, cb_ref, lg_ref, lb_ref, o_ref, ext_ref, sh_ref):
    i = pl.program_id(0)
    n = pl.num_programs(0)
    tt = cur_ref.shape[0]
    ext_ref[0:CONV_HALO, :] = jnp.where(i > 0, prev_ref[...], 0.0)
    ext_ref[CONV_HALO:CONV_HALO + tt, :] = cur_ref[...]
    ext_ref[CONV_HALO + tt:2 * CONV_HALO + tt, :] = jnp.where(i < n - 1, next_ref[...], 0.0)
    base = CONV_HALO - CONV_K // 2
    span = (CONV_K - 1) // SUBLANES * SUBLANES
    for b in range(SUBLANES):
        sh_ref[b] = ext_ref[base + b:base + b + tt + span, :]
    for c in range(tt // CONV_CH):
        acc = jnp.zeros((CONV_CH, D_CONV), F32)
        for b in range(SUBLANES):
            for a in range((CONV_K - 1 - b) // SUBLANES + 1):
                r = c * CONV_CH + SUBLANES * a
                acc = acc + sh_ref[b, r:r + CONV_CH, :] * w_ref[SUBLANES * a + b:SUBLANES * a + b + 1, :]
        acc = acc + cb_ref[...]
        mu = jnp.mean(acc, axis=-1, keepdims=True)
        d = acc - mu
        var = jnp.mean(d * d, axis=-1, keepdims=True)
        un = d * lax.rsqrt(var + EPS) * lg_ref[...] + lb_ref[...]
        o_ref[c * CONV_CH:(c + 1) * CONV_CH, :] = (un * jax.nn.sigmoid(un)).astype(o_ref.dtype)


def _conv(u, conv_w, conv_b, ln_g, ln_b):
    t = u.shape[0]
    tt = min(CONV_TT, t)
    hb = tt // CONV_HALO
    nhb = t // CONV_HALO
    const = lambda i: (0, 0)
    w = jnp.zeros((CONV_K + 1, D_CONV), F32).at[:CONV_K].set(conv_w)
    return pl.pallas_call(
        _conv_kernel,
        grid=(t // tt,),
        in_specs=[
            pl.BlockSpec((CONV_HALO, D_CONV), lambda i: (jnp.maximum(i * hb - 1, 0), 0)),
            pl.BlockSpec((tt, D_CONV), lambda i: (i, 0)),
            pl.BlockSpec((CONV_HALO, D_CONV), lambda i: (jnp.minimum((i + 1) * hb, nhb - 1), 0)),
            pl.BlockSpec((CONV_K + 1, D_CONV), const),
            pl.BlockSpec((1, D_CONV), const),
            pl.BlockSpec((1, D_CONV), const),
            pl.BlockSpec((1, D_CONV), const),
        ],
        out_specs=pl.BlockSpec((tt, D_CONV), lambda i: (i, 0)),
        out_shape=jax.ShapeDtypeStruct((t, D_CONV), BF16),
        scratch_shapes=[pltpu.VMEM((tt + 2 * CONV_HALO, D_CONV), F32),
                        pltpu.VMEM((SUBLANES, tt + (CONV_K - 1) // SUBLANES * SUBLANES, D_CONV), F32)],
        compiler_params=pltpu.CompilerParams(dimension_semantics=("arbitrary",)),
        name="conv",
    )(u, u, u, w, conv_b.reshape(1, D_CONV), ln_g.reshape(1, D_CONV), ln_b.reshape(1, D_CONV))


def _bias_table(rpb):
    col = np.arange(GRID_W)
    c0 = np.clip(col - WIN_W // 2, 0, GRID_W - WIN_W)
    valid = (col[None, :] >= c0[:, None]) & (col[None, :] < c0[:, None] + WIN_W)
    dcol = np.clip(col[None, :] - col[:, None], -(WIN_W - 1), WIN_W - 1) + (WIN_W - 1)
    onehot = ((dcol[:, :, None] == np.arange(2 * WIN_W - 1)) & valid[:, :, None]).astype(np.float32)
    tcol = jnp.einsum("hdj,ckj->hdck", rpb.astype(F32), onehot, precision=lax.Precision.HIGHEST)
    tcol = jnp.where(valid[None, None], tcol, MASK_VALUE)
    b = jnp.stack([tcol[:, WIN_H - 1 - c:2 * WIN_H - 1 - c] for c in range(WIN_H)], axis=1)
    b = jnp.transpose(b, (0, 1, 3, 2, 4))
    return b.reshape(NA_HEADS, WIN_H, GRID_W, WIN_H * GRID_W)


def _attn_kernel(q_ref, k_ref, v_ref, bt_ref, o_ref, s_ref, *, rb, rows):
    jb = pl.program_id(1)
    lane = lax.broadcasted_iota(jnp.int32, (GRID_W, LANES), 1)
    first_head = lane < NA_HEAD_DIM
    nkeys = WIN_H * GRID_W

    def window(r):
        i = jb * rb + r
        r0 = jnp.clip(i - WIN_H // 2, 0, rows - WIN_H)
        return i - r0, pl.multiple_of(r0 * GRID_W, GRID_W)

    def scores(r):
        _, start = window(r)
        q2 = q_ref[pl.ds(pl.multiple_of(r * GRID_W, GRID_W), GRID_W), :]
        zero = jnp.zeros_like(q2)
        qst = jnp.concatenate([jnp.where(first_head, q2, zero), jnp.where(first_head, zero, q2)], axis=0)
        ks = k_ref[pl.ds(start, nkeys), :]
        return lax.dot_general(qst, ks, (((1,), (1,)), ((), ())), preferred_element_type=F32)

    def finish(r, s):
        c, start = window(r)
        vs = v_ref[pl.ds(start, nkeys), :]
        bias = jnp.concatenate([bt_ref[0, c], bt_ref[1, c]], axis=0)
        s = jnp.where(bias > 0.5 * MASK_VALUE, s + bias, MASK_VALUE)
        m = jnp.max(s, axis=-1, keepdims=True)
        p = jnp.exp(s - m)
        l = jnp.sum(p, axis=-1, keepdims=True)
        o = jnp.dot(p.astype(BF16), vs, preferred_element_type=F32) / l
        o2 = jnp.where(first_head, o[:GRID_W], o[GRID_W:])
        o_ref[pl.ds(pl.multiple_of(r * GRID_W, GRID_W), GRID_W), :] = o2.astype(o_ref.dtype)

    s_ref[0] = scores(0)

    def pair(p, carry):
        r = 2 * p
        s_ref[1] = scores(r + 1)
        finish(r, s_ref[0])
        s_ref[0] = scores(jnp.minimum(r + 2, rb - 1))
        finish(r + 1, s_ref[1])
        return carry

    lax.fori_loop(0, rb // 2, pair, 0, unroll=ATTN_UNROLL)


def _attention(q, k, v, rpb):
    t = q.shape[0]
    rows = t // GRID_W
    rb = min(ATTN_RB, rows)
    npairs = D_NA // LANES
    bt = _bias_table(rpb)
    kern = functools.partial(_attn_kernel, rb=rb, rows=rows)
    return pl.pallas_call(
        kern,
        grid=(npairs, rows // rb),
        in_specs=[
            pl.BlockSpec((rb * GRID_W, LANES), lambda p, j: (j, p)),
            pl.BlockSpec((t, LANES), lambda p, j: (0, p)),
            pl.BlockSpec((t, LANES), lambda p, j: (0, p)),
            pl.BlockSpec((2, WIN_H, GRID_W, WIN_H * GRID_W), lambda p, j: (p, 0, 0, 0)),
        ],
        out_specs=pl.BlockSpec((rb * GRID_W, LANES), lambda p, j: (j, p)),
        out_shape=jax.ShapeDtypeStruct((t, D_NA), BF16),
        scratch_shapes=[pltpu.VMEM((2, 2 * GRID_W, WIN_H * GRID_W), F32)],
        compiler_params=pltpu.CompilerParams(
            dimension_semantics=("arbitrary", "arbitrary"), vmem_limit_bytes=_vmem_limit(40 << 20)),
        name="attention",
    )(q, k, v, bt)


def _outproj_kernel(conv_ref, na_ref, x_ref, wo_ref, bo_ref, g_ref, wr_ref, br_ref,
                    h_ref, xn_ref, idx_ref, gate_ref):
    for s in range(x_ref.shape[0] // ROUTER_SUB):
        rows = slice(s * ROUTER_SUB, (s + 1) * ROUTER_SUB)
        mixed = jnp.concatenate([conv_ref[rows, :], na_ref[rows, :]], axis=-1)
        h = x_ref[rows, :] + jnp.dot(mixed, wo_ref[...], preferred_element_type=F32) + bo_ref[...]
        h_ref[rows, :] = h
        ms = jnp.mean(h * h, axis=-1, keepdims=True)
        xn = (h * lax.rsqrt(ms + EPS) * g_ref[...]).astype(BF16)
        xn_ref[rows, :] = xn
        logits = lax.dot_general(wr_ref[...], xn, (((1,), (1,)), ((), ())),
                                 preferred_element_type=F32) + br_ref[...]
        ids = lax.broadcasted_iota(jnp.int32, logits.shape, 0)
        vals, sels = [], []
        l = logits
        for _ in range(TOP_K):
            m = jnp.max(l, axis=0, keepdims=True)
            sel = jnp.min(jnp.where(l == m, ids, N_EXPERTS), axis=0, keepdims=True)
            vals.append(m)
            sels.append(sel)
            l = jnp.where(ids == sel, -jnp.inf, l)
        es = [jnp.exp(vk - vals[0]) for vk in vals]
        tot = es[0] + es[1] + es[2] + es[3]
        idx_ref[:, rows] = jnp.concatenate(sels, axis=0)
        gate_ref[:, rows] = jnp.concatenate([e / tot for e in es], axis=0)


def _outproj_router(conv_out, na_out, x2, w_out, b_out, g_ffn, w_router, b_router):
    t = x2.shape[0]
    tm = min(PROJ_TM, t)
    tok = lambda i: (i, 0)
    const = lambda i: (0, 0)
    return pl.pallas_call(
        _outproj_kernel,
        grid=(t // tm,),
        in_specs=[
            pl.BlockSpec((tm, D_CONV), tok),
            pl.BlockSpec((tm, D_NA), tok),
            pl.BlockSpec((tm, D_MODEL), tok),
            pl.BlockSpec((D_MODEL, D_MODEL), const),
            pl.BlockSpec((1, D_MODEL), const),
            pl.BlockSpec((1, D_MODEL), const),
            pl.BlockSpec((N_EXPERTS, D_MODEL), const),
            pl.BlockSpec((N_EXPERTS, 1), const),
        ],
        out_specs=[
            pl.BlockSpec((tm, D_MODEL), tok),
            pl.BlockSpec((tm, D_MODEL), tok),
            pl.BlockSpec((TOP_K, tm), lambda i: (0, i)),
            pl.BlockSpec((TOP_K, tm), lambda i: (0, i)),
        ],
        out_shape=[
            jax.ShapeDtypeStruct((t, D_MODEL), F32),
            jax.ShapeDtypeStruct((t, D_MODEL), BF16),
            jax.ShapeDtypeStruct((TOP_K, t), jnp.int32),
            jax.ShapeDtypeStruct((TOP_K, t), F32),
        ],
        compiler_params=pltpu.CompilerParams(
            dimension_semantics=("arbitrary",), vmem_limit_bytes=_vmem_limit(48 << 20)),
        name="outproj_router",
    )(conv_out, na_out, x2, w_out.astype(BF16), b_out.reshape(1, D_MODEL), g_ffn.reshape(1, D_MODEL),
      w_router.T.astype(BF16), b_router.reshape(N_EXPERTS, 1))


def _route_plan(idx_t, t):
    nt = t // ROUTE_TM
    experts = jnp.arange(N_EXPERTS, dtype=I32)
    onehot = idx_t.reshape(TOP_K, nt, ROUTE_TM, 1) == experts
    cnt = jnp.sum(onehot, axis=(0, 2), dtype=I32)
    sizes = jnp.sum(cnt, axis=0)
    padded = (sizes + ROUTE_RP + EXP_BM - 1) // EXP_BM * EXP_BM
    pad_ends = jnp.cumsum(padded)
    pad_off = pad_ends - padded
    tbase = pad_off[None, :] + jnp.cumsum(cnt, axis=0) - cnt
    cap = _sorted_rows(t)
    nb = cap // EXP_BM
    n_used = pad_ends[N_EXPERTS - 1] // EXP_BM
    blk_src = jnp.minimum(jnp.arange(nb, dtype=I32), n_used - 1)
    blk_exp = jnp.minimum(
        jnp.sum(blk_src[:, None] * EXP_BM >= pad_ends[None, :], axis=1, dtype=I32), N_EXPERTS - 1)
    row_end = jnp.sum(jnp.where(blk_exp[:, None] == experts[None, :], (pad_off + sizes)[None, :], 0), axis=1)
    blk_live = jnp.clip(row_end - blk_src * EXP_BM, 0, EXP_BM).astype(I32)
    multi = jnp.any(tbase % SUBLANES + cnt > ROUTE_RP, axis=1).astype(I32)
    zero_from = jnp.concatenate([(pad_off + sizes) // ZERO_ROWS * ZERO_ROWS, pad_ends[N_EXPERTS - 1:]])
    zero_to = jnp.concatenate([pad_ends, jnp.full((1,), cap, I32)])
    return dict(cnt=cnt.reshape(-1), tbase=tbase.reshape(-1).astype(I32), multi=multi,
                zero_start=zero_from.astype(I32), zero_cnt=((zero_to - zero_from) // ZERO_ROWS).astype(I32),
                blk_src=blk_src, blk_exp=blk_exp, blk_live=blk_live, n_used=n_used.reshape(1).astype(I32))


def _sorted_rows(t):
    cap = t * TOP_K + N_EXPERTS * (EXP_BM + ROUTE_RP)
    return (cap + EXP_BM - 1) // EXP_BM * EXP_BM


def _chunk_geometry(tb, n):
    head = tb % SUBLANES
    start = pl.multiple_of(tb - head, SUBLANES)
    nchunks = (head + n + ROUTE_RP - 1) // ROUTE_RP
    return start, head, nchunks


def _dispatch_kernel(tb_ref, cnt_ref, mt_ref, zs_ref, zc_ref, idx_ref, tbv_ref, xn_ref, xs_hbm, loc_ref,
                     stage, ostage, zbuf, head_ref, cs_ref, ms_ref, sem, osem, zsem):
    i = pl.program_id(0)
    nt = pl.num_programs(0)
    slot = i % 2
    tm = idx_ref.shape[1]
    groups = ROUTE_RP // SUBLANES

    def geometry(e):
        return _chunk_geometry(tb_ref[i * N_EXPERTS + e], cnt_ref[i * N_EXPERTS + e])

    def chunk_copy(e, s):
        start, _, _ = geometry(e)
        return pltpu.make_async_copy(stage.at[s, pl.ds(e * ROUTE_RP, ROUTE_RP)],
                                     xs_hbm.at[pl.ds(start, ROUTE_RP)], sem.at[s])

    def zero_copy(start):
        return pltpu.make_async_copy(zbuf, xs_hbm.at[pl.ds(pl.multiple_of(start, ZERO_ROWS), ZERO_ROWS)], zsem)

    @pl.when(i == 0)
    def _():
        zbuf[...] = jnp.zeros(zbuf.shape, U32)
        head_ref[...] = jnp.zeros(head_ref.shape, U32)
        for phase in ("start", "wait"):
            for j in range(N_EXPERTS + 1):
                def piece(q, carry, j=j, phase=phase):
                    getattr(zero_copy(zs_ref[j] + q * ZERO_ROWS), phase)()
                    return carry
                lax.fori_loop(0, zc_ref[j], piece, 0)

    idx = idx_ref[...]
    eio = lax.broadcasted_iota(I32, (N_EXPERTS, tm), 0)
    member = jnp.zeros((N_EXPERTS, tm), F32)
    for k in range(TOP_K):
        member = member + (idx[k:k + 1, :] == eio).astype(F32)
    tri = (lax.broadcasted_iota(I32, (tm, tm), 0) < lax.broadcasted_iota(I32, (tm, tm), 1)).astype(BF16)
    rank = jnp.dot(member.astype(BF16), tri, preferred_element_type=F32)
    pos = rank + (tbv_ref[0] % SUBLANES).astype(F32)
    loc_ref[...] = jnp.concatenate(
        [jnp.sum(jnp.where(idx[k:k + 1, :] == eio, pos, 0.0), axis=0, keepdims=True) for k in range(TOP_K)],
        axis=0).astype(I32)
    cs_ref[...] = pos
    ms_ref[...] = member

    jio = lax.broadcasted_iota(I32, (ROUTE_RP, tm), 0).astype(F32)
    sel = [jnp.where((jio == pos[e:e + 1, :]) & (member[e:e + 1, :] > 0.0), 1.0, 0.0).astype(BF16)
           for e in range(N_EXPERTS)]
    rows = jnp.dot(jnp.concatenate(sel, axis=0), xn_ref[...], preferred_element_type=F32)

    @pl.when(i > 0)
    def _():
        for e in range(N_EXPERTS):
            chunk_copy(e, 1 - slot).wait()

    stage[slot] = _pack_rows(rows)
    sub = lax.broadcasted_iota(I32, (SUBLANES, HALF), 0)
    for e in range(N_EXPERTS):
        _, head, _ = geometry(e)
        first = pl.ds(e * ROUTE_RP, SUBLANES)
        stage[slot, first, :] = jnp.where(sub < head, head_ref[e], stage[slot, first, :])
        chunk_copy(e, slot).start()
        g = jnp.minimum((head + cnt_ref[i * N_EXPERTS + e]) // SUBLANES, groups - 1)
        head_ref[e] = stage[slot, pl.ds(pl.multiple_of(e * ROUTE_RP + g * SUBLANES, SUBLANES), SUBLANES), :]

    def per_expert(e, carry):
        start, head, nchunks = geometry(e)
        end = head + cnt_ref[i * N_EXPERTS + e]

        def per_chunk(c, carry2):
            want = jio + (c * ROUTE_RP).astype(F32)
            pick = jnp.where((want == cs_ref[pl.ds(e, 1), :]) & (ms_ref[pl.ds(e, 1), :] > 0.0), 1.0, 0.0)
            ostage[...] = _pack_rows(jnp.dot(pick.astype(BF16), xn_ref[...], preferred_element_type=F32))
            cp = pltpu.make_async_copy(
                ostage, xs_hbm.at[pl.ds(pl.multiple_of(start + c * ROUTE_RP, SUBLANES), ROUTE_RP)], osem)
            cp.start()
            cp.wait()

            @pl.when(end // ROUTE_RP == c)
            def _():
                g = (end - c * ROUTE_RP) // SUBLANES
                head_ref[e] = ostage[pl.ds(pl.multiple_of(g * SUBLANES, SUBLANES), SUBLANES), :]
            return carry2
        return lax.fori_loop(1, nchunks, per_chunk, carry)

    @pl.when(mt_ref[i] > 0)
    def _():
        lax.fori_loop(0, N_EXPERTS, per_expert, 0)

    @pl.when(i == nt - 1)
    def _():
        for e in range(N_EXPERTS):
            chunk_copy(e, slot).wait()


def _dispatch(plan, idx_t, xn2):
    t = xn2.shape[0]
    tm = ROUTE_TM
    cap = _sorted_rows(t)
    grid_spec = pltpu.PrefetchScalarGridSpec(
        num_scalar_prefetch=5,
        grid=(t // tm,),
        in_specs=[
            pl.BlockSpec((TOP_K, tm), lambda i, *_: (0, i)),
            pl.BlockSpec((1, N_EXPERTS, 1), lambda i, *_: (i, 0, 0)),
            pl.BlockSpec((tm, D_MODEL), lambda i, *_: (i, 0)),
        ],
        out_specs=[
            pl.BlockSpec(memory_space=pl.ANY),
            pl.BlockSpec((TOP_K, tm), lambda i, *_: (0, i)),
        ],
        scratch_shapes=[
            pltpu.VMEM((2, N_EXPERTS * ROUTE_RP, HALF), U32),
            pltpu.VMEM((ROUTE_RP, HALF), U32),
            pltpu.VMEM((ZERO_ROWS, HALF), U32),
            pltpu.VMEM((N_EXPERTS, SUBLANES, HALF), U32),
            pltpu.VMEM((N_EXPERTS, tm), F32),
            pltpu.VMEM((N_EXPERTS, tm), F32),
            pltpu.SemaphoreType.DMA((2,)),
            pltpu.SemaphoreType.DMA(()),
            pltpu.SemaphoreType.DMA(()),
        ],
    )
    return pl.pallas_call(
        _dispatch_kernel,
        grid_spec=grid_spec,
        out_shape=[
            jax.ShapeDtypeStruct((cap, HALF), U32),
            jax.ShapeDtypeStruct((TOP_K, t), I32),
        ],
        compiler_params=pltpu.CompilerParams(
            dimension_semantics=("arbitrary",), vmem_limit_bytes=_vmem_limit(32 << 20)),
        name="dispatch",
    )(plan["tbase"], plan["cnt"], plan["multi"], plan["zero_start"], plan["zero_cnt"], idx_t,
      plan["tbase"].reshape(t // tm, N_EXPERTS, 1), xn2)


def _expert_kernel(be_ref, src_ref, nu_ref, live_ref, xs_ref, wg_hbm, wu_hbm, wd_hbm, bg_ref, bu_ref, bd_ref,
                   ys_ref, wf32, wbf, wsem):
    b = pl.program_id(0)
    e = be_ref[b]

    def weight_copies(expert):
        return [pltpu.make_async_copy(w.at[expert], wf32.at[m], wsem)
                for m, w in enumerate((wg_hbm, wu_hbm, wd_hbm))]

    @pl.when(b == 0)
    def _():
        for cp in weight_copies(e):
            cp.start()

    @pl.when(b < nu_ref[0])
    def _():
        @pl.when((b == 0) | (e != be_ref[jnp.maximum(b - 1, 0)]))
        def _():
            for cp in weight_copies(e):
                cp.wait()
            for m in range(3):
                wbf[m] = wf32[m].astype(BF16)

            @pl.when(e + 1 < N_EXPERTS)
            def _():
                for cp in weight_copies(e + 1):
                    cp.start()

        def ffn(rows):
            x = _unpack_rows(xs_ref[rows, :])
            gt = jnp.minimum(jnp.dot(x, wbf[0], preferred_element_type=F32) + bg_ref[0], SWIGLU_LIMIT)
            up = jnp.clip(jnp.dot(x, wbf[1], preferred_element_type=F32) + bu_ref[0],
                          -SWIGLU_LIMIT, SWIGLU_LIMIT)
            hdn = (up + 1.0) * (gt * jax.nn.sigmoid(SWIGLU_ALPHA * gt))
            y = jnp.dot(hdn.astype(BF16), wbf[2], preferred_element_type=F32) + bd_ref[0]
            ys_ref[rows, :] = _pack_rows(y.astype(BF16).astype(F32))

        live = live_ref[b]
        for p, part in enumerate(EXP_PARTS):
            smaller = EXP_PARTS[p + 1] if p + 1 < len(EXP_PARTS) else -1
            cond = live > smaller
            if p > 0:
                cond = cond & (live <= part)

            @pl.when(cond)
            def _(part=part):
                ffn(slice(0, part))
                if part < EXP_BM:
                    ys_ref[part:, :] = jnp.zeros((EXP_BM - part, HALF), U32)

    @pl.when(b >= nu_ref[0])
    def _():
        ys_ref[...] = jnp.zeros(ys_ref.shape, U32)


def _experts(plan, xs, w_gate, b_gate, w_up, b_up, w_down, b_down):
    cap = xs.shape[0]
    nb = cap // EXP_BM
    rows = lambda b, be, src, nu, live: (src[b], 0)
    wsel = lambda b, be, src, nu, live: (be[b], 0, 0)
    wspec = pl.BlockSpec(memory_space=pl.ANY)
    bspec = pl.BlockSpec((1, 1, D_FF), wsel)
    grid_spec = pltpu.PrefetchScalarGridSpec(
        num_scalar_prefetch=4,
        grid=(nb,),
        in_specs=[pl.BlockSpec((EXP_BM, HALF), rows), wspec, wspec, wspec, bspec, bspec, bspec],
        out_specs=pl.BlockSpec((EXP_BM, HALF), lambda b, be, src, nu, live: (b, 0)),
        scratch_shapes=[pltpu.VMEM((3, D_MODEL, D_FF), F32),
                        pltpu.VMEM((3, D_MODEL, D_FF), BF16),
                        pltpu.SemaphoreType.DMA(())],
    )
    return pl.pallas_call(
        _expert_kernel,
        grid_spec=grid_spec,
        out_shape=jax.ShapeDtypeStruct((cap, HALF), U32),
        compiler_params=pltpu.CompilerParams(
            dimension_semantics=("arbitrary",), vmem_limit_bytes=_vmem_limit(56 << 20)),
        name="experts",
    )(plan["blk_exp"], plan["blk_src"], plan["n_used"], plan["blk_live"], xs, w_gate, w_up, w_down,
      b_gate.reshape(N_EXPERTS, 1, D_FF), b_up.reshape(N_EXPERTS, 1, D_FF),
      b_down.reshape(N_EXPERTS, 1, D_MODEL))


def _combine_kernel(tb_ref, cnt_ref, mt_ref, h_ref, idx_ref, loc_ref, gate_ref, ys_hbm, g_ref, o_ref,
                    ybuf, obuf, acc_ref, sem, osem):
    i = pl.program_id(0)
    nt = pl.num_programs(0)
    slot = i % 2
    tm = h_ref.shape[0]

    def geometry(step, e):
        return _chunk_geometry(tb_ref[step * N_EXPERTS + e], cnt_ref[step * N_EXPERTS + e])

    def chunk_copy(step, e, s):
        start, _, _ = geometry(step, e)
        return pltpu.make_async_copy(ys_hbm.at[pl.ds(start, ROUTE_RP)],
                                     ybuf.at[s, pl.ds(e * ROUTE_RP, ROUTE_RP)], sem.at[s])

    @pl.when(i == 0)
    def _():
        for e in range(N_EXPERTS):
            chunk_copy(0, e, 0).start()

    @pl.when(i + 1 < nt)
    def _():
        for e in range(N_EXPERTS):
            chunk_copy(i + 1, e, 1 - slot).start()

    def gate_matrix(rows, ncols, col_of):
        n = rows.stop - rows.start
        gate = gate_ref[rows, :]
        g_hi = gate.astype(BF16).astype(F32)
        g_lo = gate - g_hi
        colio = lax.broadcasted_iota(I32, (n, ncols), 1)
        hi = jnp.zeros((n, ncols), F32)
        lo = jnp.zeros((n, ncols), F32)
        for k in range(TOP_K):
            hit = colio == col_of[:, k:k + 1]
            hi = jnp.where(hit, g_hi[:, k:k + 1], hi)
            lo = jnp.where(hit, g_lo[:, k:k + 1], lo)
        return jnp.concatenate([hi, lo], axis=0).astype(BF16)

    for e in range(N_EXPERTS):
        chunk_copy(i, e, slot).wait()
    y = _unpack_rows(ybuf[slot])
    for s in range(tm // COMBINE_SUB):
        rows = slice(s * COMBINE_SUB, (s + 1) * COMBINE_SUB)
        idx = idx_ref[rows, :]
        loc = loc_ref[rows, :]
        col = jnp.where(loc < ROUTE_RP, idx * ROUTE_RP + loc, -1)
        z = jnp.dot(gate_matrix(rows, N_EXPERTS * ROUTE_RP, col), y, preferred_element_type=F32)
        acc_ref[rows, :] = z[:COMBINE_SUB] + z[COMBINE_SUB:]

    def per_expert(e, carry):
        start, _, nchunks = geometry(i, e)

        def per_chunk(c, carry2):
            cp = pltpu.make_async_copy(
                ys_hbm.at[pl.ds(pl.multiple_of(start + c * ROUTE_RP, SUBLANES), ROUTE_RP)], obuf, osem)
            cp.start()
            cp.wait()
            ccol = jnp.where(idx_ref[...] == e, loc_ref[...] - c * ROUTE_RP, -1)
            zc = jnp.dot(gate_matrix(slice(0, tm), ROUTE_RP, ccol), _unpack_rows(obuf[...]),
                         preferred_element_type=F32)
            acc_ref[...] += zc[:tm] + zc[tm:]
            return carry2
        return lax.fori_loop(1, nchunks, per_chunk, carry)

    @pl.when(mt_ref[i] > 0)
    def _():
        lax.fori_loop(0, N_EXPERTS, per_expert, 0)

    out = h_ref[...] + acc_ref[...]
    ms = jnp.mean(out * out, axis=-1, keepdims=True)
    o_ref[...] = out * lax.rsqrt(ms + EPS) * g_ref[...]


def _combine(plan, h, ys, idx_tm, loc_tm, gate_tm, g_final):
    t = h.shape[0]
    tm = ROUTE_TM
    tok = lambda i, *_: (i, 0)
    grid_spec = pltpu.PrefetchScalarGridSpec(
        num_scalar_prefetch=3,
        grid=(t // tm,),
        in_specs=[
            pl.BlockSpec((tm, D_MODEL), tok),
            pl.BlockSpec((tm, TOP_K), tok),
            pl.BlockSpec((tm, TOP_K), tok),
            pl.BlockSpec((tm, TOP_K), tok),
            pl.BlockSpec(memory_space=pl.ANY),
            pl.BlockSpec((1, D_MODEL), lambda i, *_: (0, 0)),
        ],
        out_specs=pl.BlockSpec((tm, D_MODEL), tok),
        scratch_shapes=[
            pltpu.VMEM((2, N_EXPERTS * ROUTE_RP, HALF), U32),
            pltpu.VMEM((ROUTE_RP, HALF), U32),
            pltpu.VMEM((tm, D_MODEL), F32),
            pltpu.SemaphoreType.DMA((2,)),
            pltpu.SemaphoreType.DMA(()),
        ],
    )
    return pl.pallas_call(
        _combine_kernel,
        grid_spec=grid_spec,
        out_shape=jax.ShapeDtypeStruct((t, D_MODEL), F32),
        compiler_params=pltpu.CompilerParams(
            dimension_semantics=("arbitrary",), vmem_limit_bytes=_vmem_limit(32 << 20)),
        name="combine",
    )(plan["tbase"], plan["cnt"], plan["multi"], h, idx_tm, loc_tm, gate_tm, ys,
      g_final.reshape(1, D_MODEL))


def _moe(h, xn2, idx_t, gate_t, w_gate, b_gate, w_up, b_up, w_down, b_down, g_final):
    t = h.shape[0]
    plan = _route_plan(idx_t, t)
    xs, loc_t = _dispatch(plan, idx_t, xn2)
    ys = _experts(plan, xs, w_gate, b_gate, w_up, b_up, w_down, b_down)
    return _combine(plan, h, ys, idx_t.T, loc_t.T, gate_t.T, g_final)


def kernel(x, g_mix, w_in, b_in, conv_w, conv_b, ln_g, ln_b, rpb, w_out, b_out, g_ffn, w_router,
           b_router, w_gate, b_gate, w_up, b_up, w_down, b_down, g_final):
    bsz, seq, d = x.shape
    assert bsz == 1 and d == D_MODEL and g_mix.shape[0] == 1
    assert seq % (GRID_W * WIN_H) == 0
    x2 = x.reshape(seq, d)
    u, q, k, v = _inproj(x2, g_mix[0], w_in[0], b_in[0])
    conv_out = _conv(u, conv_w[0], conv_b[0], ln_g[0], ln_b[0])
    na_out = _attention(q, k, v, rpb[0])
    h, xn2, idx_t, gate_t = _outproj_router(conv_out, na_out, x2, w_out[0], b_out[0], g_ffn[0],
                                            w_router[0], b_router[0])
    out = _moe(h, xn2, idx_t, gate_t, w_gate[0], b_gate[0], w_up[0], b_up[0], w_down[0], b_down[0],
               g_final)
    return out.reshape(bsz, seq, d)
```

```python
import functools

import numpy as np
import jax
import jax.numpy as jnp
from jax import lax
from jax.experimental import pallas as pl
from jax.experimental.pallas import tpu as pltpu

F32 = jnp.float32
BF16 = jnp.bfloat16
U32 = jnp.uint32
I32 = jnp.int32

D_MODEL = 1024
GRID_W = 64
D_CONV = 512
CONV_K = 31
NA_HEADS = 8
NA_HEAD_DIM = 64
D_NA = NA_HEADS * NA_HEAD_DIM
D_IN = 2 * D_CONV + 3 * D_NA
WIN_H = 8
WIN_W = 16
N_EXPERTS = 32
TOP_K = 4
D_FF = D_MODEL
SWIGLU_LIMIT = 7.0
SWIGLU_ALPHA = 1.702
EPS = 1e-5
MASK_VALUE = -1e30

V7X_VMEM_BYTES = 64 * 1024 * 1024
LANES = 128

PROJ_TM = 512
ROUTER_SUB = 256
CONV_HALO = 16
CONV_CH = 32
ATTN_RB = 32
ATTN_UNROLL = 8
ROUTE_TM = 256
SUBLANES = 8
ROUTE_RP = 56
EXP_BM = 512
EXP_PARTS = (512, 256, 128)
ZERO_ROWS = 256
HALF = D_MODEL // 2


def _vmem_limit(nbytes):
    return int(min(nbytes, V7X_VMEM_BYTES - 6 * 1024 * 1024))


def _pack_rows(x):
    lo = lax.bitcast_convert_type(x[:, :HALF], U32) >> 16
    hi = lax.bitcast_convert_type(x[:, HALF:], U32) & jnp.uint32(0xFFFF0000)
    return lo | hi


def _unpack_rows(w):
    lo = lax.bitcast_convert_type(w << 16, F32)
    hi = lax.bitcast_convert_type(w & jnp.uint32(0xFFFF0000), F32)
    return jnp.concatenate([lo, hi], axis=1).astype(BF16)


def _inproj_conv_kernel(x_ref, g_ref, w_ref, b_ref, cw_ref, cb_ref, lg_ref, lb_ref,
                        q_ref, k_ref, v_ref, o_ref, ext_ref, u_ref, sh_ref):
    i = pl.program_id(0)
    nt = pl.num_programs(0) - 1
    tm = x_ref.shape[0]

    @pl.when(i == 0)
    def _():
        ext_ref[...] = jnp.zeros(ext_ref.shape, F32)

    x = x_ref[...]
    ms = jnp.mean(x * x, axis=-1, keepdims=True)
    xn = (x * lax.rsqrt(ms + EPS) * g_ref[...]).astype(BF16)

    def proj(c):
        sl = slice(c * D_CONV, (c + 1) * D_CONV)
        return jnp.dot(xn, w_ref[:, sl], preferred_element_type=F32) + b_ref[:, sl]

    u_ref[...] = proj(0) * jax.nn.sigmoid(proj(1))
    q_ref[...] = (proj(2) * (NA_HEAD_DIM ** -0.5)).astype(BF16)
    k_ref[...] = proj(3).astype(BF16)
    v_ref[...] = proj(4).astype(BF16)

    ext_ref[CONV_HALO + tm:, :] = jnp.where(i < nt, u_ref[0:CONV_HALO, :], 0.0)
    base = CONV_HALO - CONV_K // 2
    span = (CONV_K - 1) // SUBLANES * SUBLANES
    for b in range(SUBLANES):
        sh_ref[b] = ext_ref[base + b:base + b + tm + span, :]
    for c in range(tm // CONV_CH):
        acc = jnp.zeros((CONV_CH, D_CONV), F32)
        for b in range(SUBLANES):
            for a in range((CONV_K - 1 - b) // SUBLANES + 1):
                r = c * CONV_CH + SUBLANES * a
                acc = acc + sh_ref[b, r:r + CONV_CH, :] * cw_ref[SUBLANES * a + b:SUBLANES * a + b + 1, :]
        acc = acc + cb_ref[...]
        mu = jnp.mean(acc, axis=-1, keepdims=True)
        d = acc - mu
        var = jnp.mean(d * d, axis=-1, keepdims=True)
        un = d * lax.rsqrt(var + EPS) * lg_ref[...] + lb_ref[...]
        o_ref[c * CONV_CH:(c + 1) * CONV_CH, :] = (un * jax.nn.sigmoid(un)).astype(o_ref.dtype)

    ext_ref[0:CONV_HALO, :] = ext_ref[tm:tm + CONV_HALO, :]
    ext_ref[CONV_HALO:CONV_HALO + tm, :] = u_ref[...]


def _inproj_conv(x2, g_mix, w_in, b_in, conv_w, conv_b, ln_g, ln_b):
    t = x2.shape[0]
    tm = min(PROJ_TM, t)
    nt = t // tm
    tok = lambda i: (jnp.minimum(i, nt - 1), 0)
    prev = lambda i: (jnp.maximum(i - 1, 0), 0)
    const = lambda i: (0, 0)
    w = jnp.zeros((CONV_K + 1, D_CONV), F32).at[:CONV_K].set(conv_w)
    span = (CONV_K - 1) // SUBLANES * SUBLANES
    return pl.pallas_call(
        _inproj_conv_kernel,
        grid=(nt + 1,),
        in_specs=[
            pl.BlockSpec((tm, D_MODEL), tok),
            pl.BlockSpec((1, D_MODEL), const),
            pl.BlockSpec((D_MODEL, D_IN), const),
            pl.BlockSpec((1, D_IN), const),
            pl.BlockSpec((CONV_K + 1, D_CONV), const),
            pl.BlockSpec((1, D_CONV), const),
            pl.BlockSpec((1, D_CONV), const),
            pl.BlockSpec((1, D_CONV), const),
        ],
        out_specs=[
            pl.BlockSpec((tm, D_NA), tok),
            pl.BlockSpec((tm, D_NA), tok),
            pl.BlockSpec((tm, D_NA), tok),
            pl.BlockSpec((tm, D_CONV), prev),
        ],
        out_shape=[
            jax.ShapeDtypeStruct((t, D_NA), BF16),
            jax.ShapeDtypeStruct((t, D_NA), BF16),
            jax.ShapeDtypeStruct((t, D_NA), BF16),
            jax.ShapeDtypeStruct((t, D_CONV), BF16),
        ],
        scratch_shapes=[pltpu.VMEM((tm + 2 * CONV_HALO, D_CONV), F32),
                        pltpu.VMEM((tm, D_CONV), F32),
                        pltpu.VMEM((SUBLANES, tm + span, D_CONV), F32)],
        compiler_params=pltpu.CompilerParams(
            dimension_semantics=("arbitrary",), vmem_limit_bytes=_vmem_limit(56 << 20)),
        name="inproj_conv",
    )(x2, g_mix.reshape(1, D_MODEL), w_in.astype(BF16), b_in.reshape(1, D_IN), w,
      conv_b.reshape(1, D_CONV), ln_g.reshape(1, D_CONV), ln_b.reshape(1, D_CONV))


def _bias_table(rpb):
    col = np.arange(GRID_W)
    c0 = np.clip(col - WIN_W // 2, 0, GRID_W - WIN_W)
    valid = (col[None, :] >= c0[:, None]) & (col[None, :] < c0[:, None] + WIN_W)
    dcol = np.clip(col[None, :] - col[:, None], -(WIN_W - 1), WIN_W - 1) + (WIN_W - 1)
    col_sel = ((dcol[:, :, None] == np.arange(2 * WIN_W - 1)) & valid[:, :, None]).astype(np.float32)
    cls = np.arange(WIN_H)
    drow = cls[None, :] - cls[:, None] + (WIN_H - 1)
    row_sel = (drow[:, :, None] == np.arange(2 * WIN_H - 1)).astype(np.float32)
    b = jnp.einsum("hdj,crd,qkj->hcqrk", rpb.astype(F32), row_sel, col_sel,
                   precision=lax.Precision.HIGHEST)
    b = jnp.where(valid[None, None, :, None, :], b, MASK_VALUE)
    return b.reshape(NA_HEADS, WIN_H, GRID_W, WIN_H * GRID_W)


def _attn_kernel(q_ref, k_ref, v_ref, bt_ref, o_ref, s_ref, *, rb, rows):
    jb = pl.program_id(1)
    lane = lax.broadcasted_iota(jnp.int32, (GRID_W, LANES), 1)
    first_head = lane < NA_HEAD_DIM
    nkeys = WIN_H * GRID_W

    def window(r):
        i = jb * rb + r
        r0 = jnp.clip(i - WIN_H // 2, 0, rows - WIN_H)
        return i - r0, pl.multiple_of(r0 * GRID_W, GRID_W)

    def scores(r):
        _, start = window(r)
        q2 = q_ref[pl.ds(pl.multiple_of(r * GRID_W, GRID_W), GRID_W), :]
        zero = jnp.zeros_like(q2)
        qst = jnp.concatenate([jnp.where(first_head, q2, zero), jnp.where(first_head, zero, q2)], axis=0)
        ks = k_ref[pl.ds(start, nkeys), :]
        return lax.dot_general(qst, ks, (((1,), (1,)), ((), ())), preferred_element_type=F32)

    def finish(r, s):
        c, start = window(r)
        vs = v_ref[pl.ds(start, nkeys), :]
        bias = jnp.concatenate([bt_ref[0, c], bt_ref[1, c]], axis=0)
        s = jnp.where(bias > 0.5 * MASK_VALUE, s + bias, MASK_VALUE)
        m = jnp.max(s, axis=-1, keepdims=True)
        p = jnp.exp(s - m)
        l = jnp.sum(p, axis=-1, keepdims=True)
        o = jnp.dot(p.astype(BF16), vs, preferred_element_type=F32) / l
        o2 = jnp.where(first_head, o[:GRID_W], o[GRID_W:])
        o_ref[pl.ds(pl.multiple_of(r * GRID_W, GRID_W), GRID_W), :] = o2.astype(o_ref.dtype)

    s_ref[0] = scores(0)

    def pair(p, carry):
        r = 2 * p
        s_ref[1] = scores(r + 1)
        finish(r, s_ref[0])
        s_ref[0] = scores(jnp.minimum(r + 2, rb - 1))
        finish(r + 1, s_ref[1])
        return carry

    lax.fori_loop(0, rb // 2, pair, 0, unroll=ATTN_UNROLL)


def _attention(q, k, v, rpb):
    t = q.shape[0]
    rows = t // GRID_W
    rb = min(ATTN_RB, rows)
    npairs = D_NA // LANES
    bt = _bias_table(rpb)
    kern = functools.partial(_attn_kernel, rb=rb, rows=rows)
    return pl.pallas_call(
        kern,
        grid=(npairs, rows // rb),
        in_specs=[
            pl.BlockSpec((rb * GRID_W, LANES), lambda p, j: (j, p)),
            pl.BlockSpec((t, LANES), lambda p, j: (0, p)),
            pl.BlockSpec((t, LANES), lambda p, j: (0, p)),
            pl.BlockSpec((2, WIN_H, GRID_W, WIN_H * GRID_W), lambda p, j: (p, 0, 0, 0)),
        ],
        out_specs=pl.BlockSpec((rb * GRID_W, LANES), lambda p, j: (j, p)),
        out_shape=jax.ShapeDtypeStruct((t, D_NA), BF16),
        scratch_shapes=[pltpu.VMEM((2, 2 * GRID_W, WIN_H * GRID_W), F32)],
        compiler_params=pltpu.CompilerParams(
            dimension_semantics=("arbitrary", "arbitrary"), vmem_limit_bytes=_vmem_limit(40 << 20)),
        name="attention",
    )(q, k, v, bt)


def _outproj_kernel(conv_ref, na_ref, x_ref, wo_ref, bo_ref, g_ref, wr_ref, br_ref,
                    h_ref, xn_ref, idx_ref, gate_ref):
    for s in range(x_ref.shape[0] // ROUTER_SUB):
        rows = slice(s * ROUTER_SUB, (s + 1) * ROUTER_SUB)
        mixed = jnp.concatenate([conv_ref[rows, :], na_ref[rows, :]], axis=-1)
        h = x_ref[rows, :] + jnp.dot(mixed, wo_ref[...], preferred_element_type=F32) + bo_ref[...]
        h_ref[rows, :] = h
        ms = jnp.mean(h * h, axis=-1, keepdims=True)
        xn = (h * lax.rsqrt(ms + EPS) * g_ref[...]).astype(BF16)
        xn_ref[rows, :] = xn
        logits = lax.dot_general(wr_ref[...], xn, (((1,), (1,)), ((), ())),
                                 preferred_element_type=F32) + br_ref[...]
        ids = lax.broadcasted_iota(jnp.int32, logits.shape, 0)
        vals, sels = [], []
        l = logits
        for _ in range(TOP_K):
            m = jnp.max(l, axis=0, keepdims=True)
            sel = jnp.min(jnp.where(l == m, ids, N_EXPERTS), axis=0, keepdims=True)
            vals.append(m)
            sels.append(sel)
            l = jnp.where(ids == sel, -jnp.inf, l)
        es = [jnp.exp(vk - vals[0]) for vk in vals]
        tot = es[0] + es[1] + es[2] + es[3]
        idx_ref[:, rows] = jnp.concatenate(sels, axis=0)
        gate_ref[:, rows] = jnp.concatenate([e / tot for e in es], axis=0)


def _outproj_router(conv_out, na_out, x2, w_out, b_out, g_ffn, w_router, b_router):
    t = x2.shape[0]
    tm = min(PROJ_TM, t)
    tok = lambda i: (i, 0)
    const = lambda i: (0, 0)
    return pl.pallas_call(
        _outproj_kernel,
        grid=(t // tm,),
        in_specs=[
            pl.BlockSpec((tm, D_CONV), tok),
            pl.BlockSpec((tm, D_NA), tok),
            pl.BlockSpec((tm, D_MODEL), tok),
            pl.BlockSpec((D_MODEL, D_MODEL), const),
            pl.BlockSpec((1, D_MODEL), const),
            pl.BlockSpec((1, D_MODEL), const),
            pl.BlockSpec((N_EXPERTS, D_MODEL), const),
            pl.BlockSpec((N_EXPERTS, 1), const),
        ],
        out_specs=[
            pl.BlockSpec((tm, D_MODEL), tok),
            pl.BlockSpec((tm, D_MODEL), tok),
            pl.BlockSpec((TOP_K, tm), lambda i: (0, i)),
            pl.BlockSpec((TOP_K, tm), lambda i: (0, i)),
        ],
        out_shape=[
            jax.ShapeDtypeStruct((t, D_MODEL), F32),
            jax.ShapeDtypeStruct((t, D_MODEL), BF16),
            jax.ShapeDtypeStruct((TOP_K, t), jnp.int32),
            jax.ShapeDtypeStruct((TOP_K, t), F32),
        ],
        compiler_params=pltpu.CompilerParams(
            dimension_semantics=("arbitrary",), vmem_limit_bytes=_vmem_limit(48 << 20)),
        name="outproj_router",
    )(conv_out, na_out, x2, w_out.astype(BF16), b_out.reshape(1, D_MODEL), g_ffn.reshape(1, D_MODEL),
      w_router.T.astype(BF16), b_router.reshape(N_EXPERTS, 1))


def _route_plan(idx_t, t):
    nt = t // ROUTE_TM
    experts = jnp.arange(N_EXPERTS, dtype=I32)
    onehot = idx_t.reshape(TOP_K, nt, ROUTE_TM, 1) == experts
    cnt = jnp.sum(onehot, axis=(0, 2), dtype=I32)
    sizes = jnp.sum(cnt, axis=0)
    padded = (sizes + ROUTE_RP + EXP_BM - 1) // EXP_BM * EXP_BM
    pad_ends = jnp.cumsum(padded)
    pad_off = pad_ends - padded
    tbase = pad_off[None, :] + jnp.cumsum(cnt, axis=0) - cnt
    cap = _sorted_rows(t)
    nb = cap // EXP_BM
    n_used = pad_ends[N_EXPERTS - 1] // EXP_BM
    blk_src = jnp.minimum(jnp.arange(nb, dtype=I32), n_used - 1)
    blk_exp = jnp.minimum(
        jnp.sum(blk_src[:, None] * EXP_BM >= pad_ends[None, :], axis=1, dtype=I32), N_EXPERTS - 1)
    row_end = jnp.sum(jnp.where(blk_exp[:, None] == experts[None, :], (pad_off + sizes)[None, :], 0), axis=1)
    blk_live = jnp.clip(row_end - blk_src * EXP_BM, 0, EXP_BM).astype(I32)
    multi = jnp.any(tbase % SUBLANES + cnt > ROUTE_RP, axis=1).astype(I32)
    zero_from = jnp.concatenate([(pad_off + sizes) // ZERO_ROWS * ZERO_ROWS, pad_ends[N_EXPERTS - 1:]])
    zero_to = jnp.concatenate([pad_ends, jnp.full((1,), cap, I32)])
    return dict(cnt=cnt.reshape(-1), tbase=tbase.reshape(-1).astype(I32), multi=multi,
                zero_start=zero_from.astype(I32), zero_cnt=((zero_to - zero_from) // ZERO_ROWS).astype(I32),
                blk_src=blk_src, blk_exp=blk_exp, blk_live=blk_live, n_used=n_used.reshape(1).astype(I32))


def _sorted_rows(t):
    cap = t * TOP_K + N_EXPERTS * (EXP_BM + ROUTE_RP)
    return (cap + EXP_BM - 1) // EXP_BM * EXP_BM


def _chunk_geometry(tb, n):
    head = tb % SUBLANES
    start = pl.multiple_of(tb - head, SUBLANES)
    nchunks = (head + n + ROUTE_RP - 1) // ROUTE_RP
    return start, head, nchunks


def _dispatch_kernel(tb_ref, cnt_ref, mt_ref, zs_ref, zc_ref, idx_ref, tbv_ref, xn_ref, xs_hbm, loc_ref,
                     stage, ostage, zbuf, head_ref, cs_ref, ms_ref, sem, osem, zsem):
    i = pl.program_id(0)
    nt = pl.num_programs(0)
    slot = i % 2
    tm = idx_ref.shape[1]
    groups = ROUTE_RP // SUBLANES

    def geometry(e):
        return _chunk_geometry(tb_ref[i * N_EXPERTS + e], cnt_ref[i * N_EXPERTS + e])

    def chunk_copy(e, s):
        start, _, _ = geometry(e)
        return pltpu.make_async_copy(stage.at[s, pl.ds(e * ROUTE_RP, ROUTE_RP)],
                                     xs_hbm.at[pl.ds(start, ROUTE_RP)], sem.at[s])

    def zero_copy(start):
        return pltpu.make_async_copy(zbuf, xs_hbm.at[pl.ds(pl.multiple_of(start, ZERO_ROWS), ZERO_ROWS)], zsem)

    @pl.when(i == 0)
    def _():
        zbuf[...] = jnp.zeros(zbuf.shape, U32)
        head_ref[...] = jnp.zeros(head_ref.shape, U32)
        for phase in ("start", "wait"):
            for j in range(N_EXPERTS + 1):
                def piece(q, carry, j=j, phase=phase):
                    getattr(zero_copy(zs_ref[j] + q * ZERO_ROWS), phase)()
                    return carry
                lax.fori_loop(0, zc_ref[j], piece, 0)

    idx = idx_ref[...]
    eio = lax.broadcasted_iota(I32, (N_EXPERTS, tm), 0)
    member = jnp.zeros((N_EXPERTS, tm), F32)
    for k in range(TOP_K):
        member = member + (idx[k:k + 1, :] == eio).astype(F32)
    tri = (lax.broadcasted_iota(I32, (tm, tm), 0) < lax.broadcasted_iota(I32, (tm, tm), 1)).astype(BF16)
    rank = jnp.dot(member.astype(BF16), tri, preferred_element_type=F32)
    pos = rank + (tbv_ref[0] % SUBLANES).astype(F32)
    loc_ref[...] = jnp.concatenate(
        [jnp.sum(jnp.where(idx[k:k + 1, :] == eio, pos, 0.0), axis=0, keepdims=True) for k in range(TOP_K)],
        axis=0).astype(I32)
    cs_ref[...] = pos
    ms_ref[...] = member

    jio = lax.broadcasted_iota(I32, (ROUTE_RP, tm), 0).astype(F32)
    sel = [jnp.where((jio == pos[e:e + 1, :]) & (member[e:e + 1, :] > 0.0), 1.0, 0.0).astype(BF16)
           for e in range(N_EXPERTS)]
    rows = jnp.dot(jnp.concatenate(sel, axis=0), xn_ref[...], preferred_element_type=F32)

    @pl.when(i > 0)
    def _():
        for e in range(N_EXPERTS):
            chunk_copy(e, 1 - slot).wait()

    stage[slot] = _pack_rows(rows)
    sub = lax.broadcasted_iota(I32, (SUBLANES, HALF), 0)
    for e in range(N_EXPERTS):
        _, head, _ = geometry(e)
        first = pl.ds(e * ROUTE_RP, SUBLANES)
        stage[slot, first, :] = jnp.where(sub < head, head_ref[e], stage[slot, first, :])
        chunk_copy(e, slot).start()
        g = jnp.minimum((head + cnt_ref[i * N_EXPERTS + e]) // SUBLANES, groups - 1)
        head_ref[e] = stage[slot, pl.ds(pl.multiple_of(e * ROUTE_RP + g * SUBLANES, SUBLANES), SUBLANES), :]

    def per_expert(e, carry):
        start, head, nchunks = geometry(e)
        end = head + cnt_ref[i * N_EXPERTS + e]

        def per_chunk(c, carry2):
            want = jio + (c * ROUTE_RP).astype(F32)
            pick = jnp.where((want == cs_ref[pl.ds(e, 1), :]) & (ms_ref[pl.ds(e, 1), :] > 0.0), 1.0, 0.0)
            ostage[...] = _pack_rows(jnp.dot(pick.astype(BF16), xn_ref[...], preferred_element_type=F32))
            cp = pltpu.make_async_copy(
                ostage, xs_hbm.at[pl.ds(pl.multiple_of(start + c * ROUTE_RP, SUBLANES), ROUTE_RP)], osem)
            cp.start()
            cp.wait()

            @pl.when(end // ROUTE_RP == c)
            def _():
                g = (end - c * ROUTE_RP) // SUBLANES
                head_ref[e] = ostage[pl.ds(pl.multiple_of(g * SUBLANES, SUBLANES), SUBLANES), :]
            return carry2
        return lax.fori_loop(1, nchunks, per_chunk, carry)

    @pl.when(mt_ref[i] > 0)
    def _():
        lax.fori_loop(0, N_EXPERTS, per_expert, 0)

    @pl.when(i == nt - 1)
    def _():
        for e in range(N_EXPERTS):
            chunk_copy(e, slot).wait()


def _dispatch(plan, idx_t, xn2):
    t = xn2.shape[0]
    tm = ROUTE_TM
    cap = _sorted_rows(t)
    grid_spec = pltpu.PrefetchScalarGridSpec(
        num_scalar_prefetch=5,
        grid=(t // tm,),
        in_specs=[
            pl.BlockSpec((TOP_K, tm), lambda i, *_: (0, i)),
            pl.BlockSpec((1, N_EXPERTS, 1), lambda i, *_: (i, 0, 0)),
            pl.BlockSpec((tm, D_MODEL), lambda i, *_: (i, 0)),
        ],
        out_specs=[
            pl.BlockSpec(memory_space=pl.ANY),
            pl.BlockSpec((TOP_K, tm), lambda i, *_: (0, i)),
        ],
        scratch_shapes=[
            pltpu.VMEM((2, N_EXPERTS * ROUTE_RP, HALF), U32),
            pltpu.VMEM((ROUTE_RP, HALF), U32),
            pltpu.VMEM((ZERO_ROWS, HALF), U32),
            pltpu.VMEM((N_EXPERTS, SUBLANES, HALF), U32),
            pltpu.VMEM((N_EXPERTS, tm), F32),
            pltpu.VMEM((N_EXPERTS, tm), F32),
            pltpu.SemaphoreType.DMA((2,)),
            pltpu.SemaphoreType.DMA(()),
            pltpu.SemaphoreType.DMA(()),
        ],
    )
    return pl.pallas_call(
        _dispatch_kernel,
        grid_spec=grid_spec,
        out_shape=[
            jax.ShapeDtypeStruct((cap, HALF), U32),
            jax.ShapeDtypeStruct((TOP_K, t), I32),
        ],
        compiler_params=pltpu.CompilerParams(
            dimension_semantics=("arbitrary",), vmem_limit_bytes=_vmem_limit(32 << 20)),
        name="dispatch",
    )(plan["tbase"], plan["cnt"], plan["multi"], plan["zero_start"], plan["zero_cnt"], idx_t,
      plan["tbase"].reshape(t // tm, N_EXPERTS, 1), xn2)


def _expert_kernel(be_ref, src_ref, nu_ref, live_ref, xs_ref, wg_hbm, wu_hbm, wd_hbm, bg_ref, bu_ref, bd_ref,
                   ys_ref, wf32, wbf, wsem):
    b = pl.program_id(0)
    e = be_ref[b]

    def weight_copies(expert, par):
        return [pltpu.make_async_copy(w.at[expert], wf32.at[par, m], wsem.at[par])
                for m, w in enumerate((wg_hbm, wu_hbm, wd_hbm))]

    @pl.when(b == 0)
    def _():
        for cp in weight_copies(e, e % 2):
            cp.start()

    @pl.when(b < nu_ref[0])
    def _():
        @pl.when((b == 0) | (e != be_ref[jnp.maximum(b - 1, 0)]))
        def _():
            par = e % 2
            for cp in weight_copies(e, par):
                cp.wait()

            @pl.when(e + 1 < N_EXPERTS)
            def _():
                for cp in weight_copies(e + 1, 1 - par):
                    cp.start()

            for m in range(3):
                wbf[m] = wf32[par, m].astype(BF16)

        def ffn(rows):
            x = _unpack_rows(xs_ref[rows, :])
            gt = jnp.minimum(jnp.dot(x, wbf[0], preferred_element_type=F32) + bg_ref[0], SWIGLU_LIMIT)
            up = jnp.clip(jnp.dot(x, wbf[1], preferred_element_type=F32) + bu_ref[0],
                          -SWIGLU_LIMIT, SWIGLU_LIMIT)
            hdn = (up + 1.0) * (gt * jax.nn.sigmoid(SWIGLU_ALPHA * gt))
            y = jnp.dot(hdn.astype(BF16), wbf[2], preferred_element_type=F32) + bd_ref[0]
            ys_ref[rows, :] = _pack_rows(y.astype(BF16).astype(F32))

        live = live_ref[b]
        for p, part in enumerate(EXP_PARTS):
            smaller = EXP_PARTS[p + 1] if p + 1 < len(EXP_PARTS) else -1
            cond = live > smaller
            if p > 0:
                cond = cond & (live <= part)

            @pl.when(cond)
            def _(part=part):
                ffn(slice(0, part))
                if part < EXP_BM:
                    ys_ref[part:, :] = jnp.zeros((EXP_BM - part, HALF), U32)

    @pl.when(b >= nu_ref[0])
    def _():
        ys_ref[...] = jnp.zeros(ys_ref.shape, U32)


def _experts(plan, xs, w_gate, b_gate, w_up, b_up, w_down, b_down):
    cap = xs.shape[0]
    nb = cap // EXP_BM
    rows = lambda b, be, src, nu, live: (src[b], 0)
    wsel = lambda b, be, src, nu, live: (be[b], 0, 0)
    wspec = pl.BlockSpec(memory_space=pl.ANY)
    bspec = pl.BlockSpec((1, 1, D_FF), wsel)
    grid_spec = pltpu.PrefetchScalarGridSpec(
        num_scalar_prefetch=4,
        grid=(nb,),
        in_specs=[pl.BlockSpec((EXP_BM, HALF), rows), wspec, wspec, wspec, bspec, bspec, bspec],
        out_specs=pl.BlockSpec((EXP_BM, HALF), lambda b, be, src, nu, live: (b, 0)),
        scratch_shapes=[pltpu.VMEM((2, 3, D_MODEL, D_FF), F32),
                        pltpu.VMEM((3, D_MODEL, D_FF), BF16),
                        pltpu.SemaphoreType.DMA((2,))],
    )
    return pl.pallas_call(
        _expert_kernel,
        grid_spec=grid_spec,
        out_shape=jax.ShapeDtypeStruct((cap, HALF), U32),
        compiler_params=pltpu.CompilerParams(
            dimension_semantics=("arbitrary",), vmem_limit_bytes=_vmem_limit(56 << 20)),
        name="experts",
    )(plan["blk_exp"], plan["blk_src"], plan["n_used"], plan["blk_live"], xs, w_gate, w_up, w_down,
      b_gate.reshape(N_EXPERTS, 1, D_FF), b_up.reshape(N_EXPERTS, 1, D_FF),
      b_down.reshape(N_EXPERTS, 1, D_MODEL))


def _combine_kernel(tb_ref, cnt_ref, mt_ref, h_ref, idx_ref, loc_ref, gate_ref, ys_hbm, g_ref, o_ref,
                    ybuf, obuf, acc_ref, sem, osem):
    i = pl.program_id(0)
    nt = pl.num_programs(0)
    slot = i % 2
    tm = h_ref.shape[0]

    def geometry(step, e):
        return _chunk_geometry(tb_ref[step * N_EXPERTS + e], cnt_ref[step * N_EXPERTS + e])

    def chunk_copy(step, e, s):
        start, _, _ = geometry(step, e)
        return pltpu.make_async_copy(ys_hbm.at[pl.ds(start, ROUTE_RP)],
                                     ybuf.at[s, pl.ds(e * ROUTE_RP, ROUTE_RP)], sem.at[s])

    @pl.when(i == 0)
    def _():
        for e in range(N_EXPERTS):
            chunk_copy(0, e, 0).start()

    @pl.when(i + 1 < nt)
    def _():
        for e in range(N_EXPERTS):
            chunk_copy(i + 1, e, 1 - slot).start()

    def gate_matrix(ncols, col_of):
        gate = gate_ref[...]
        colio = lax.broadcasted_iota(I32, (tm, ncols), 1)
        g = jnp.zeros((tm, ncols), F32)
        for k in range(TOP_K):
            g = jnp.where(colio == col_of[:, k:k + 1], gate[:, k:k + 1], g)
        return g.astype(BF16)

    for e in range(N_EXPERTS):
        chunk_copy(i, e, slot).wait()
    idx = idx_ref[...]
    loc = loc_ref[...]
    col = jnp.where(loc < ROUTE_RP, idx * ROUTE_RP + loc, -1)
    acc_ref[...] = jnp.dot(gate_matrix(N_EXPERTS * ROUTE_RP, col), _unpack_rows(ybuf[slot]),
                           preferred_element_type=F32)

    def per_expert(e, carry):
        start, _, nchunks = geometry(i, e)

        def per_chunk(c, carry2):
            cp = pltpu.make_async_copy(
                ys_hbm.at[pl.ds(pl.multiple_of(start + c * ROUTE_RP, SUBLANES), ROUTE_RP)], obuf, osem)
            cp.start()
            cp.wait()
            ccol = jnp.where(idx == e, loc - c * ROUTE_RP, -1)
            acc_ref[...] += jnp.dot(gate_matrix(ROUTE_RP, ccol), _unpack_rows(obuf[...]),
                                    preferred_element_type=F32)
            return carry2
        return lax.fori_loop(1, nchunks, per_chunk, carry)

    @pl.when(mt_ref[i] > 0)
    def _():
        lax.fori_loop(0, N_EXPERTS, per_expert, 0)

    out = h_ref[...] + acc_ref[...]
    ms = jnp.mean(out * out, axis=-1, keepdims=True)
    o_ref[...] = out * lax.rsqrt(ms + EPS) * g_ref[...]


def _combine(plan, h, ys, idx_tm, loc_tm, gate_tm, g_final):
    t = h.shape[0]
    tm = ROUTE_TM
    tok = lambda i, *_: (i, 0)
    grid_spec = pltpu.PrefetchScalarGridSpec(
        num_scalar_prefetch=3,
        grid=(t // tm,),
        in_specs=[
            pl.BlockSpec((tm, D_MODEL), tok),
            pl.BlockSpec((tm, TOP_K), tok),
            pl.BlockSpec((tm, TOP_K), tok),
            pl.BlockSpec((tm, TOP_K), tok),
            pl.BlockSpec(memory_space=pl.ANY),
            pl.BlockSpec((1, D_MODEL), lambda i, *_: (0, 0)),
        ],
        out_specs=pl.BlockSpec((tm, D_MODEL), tok),
        scratch_shapes=[
            pltpu.VMEM((2, N_EXPERTS * ROUTE_RP, HALF), U32),
            pltpu.VMEM((ROUTE_RP, HALF), U32),
            pltpu.VMEM((tm, D_MODEL), F32),
            pltpu.SemaphoreType.DMA((2,)),
            pltpu.SemaphoreType.DMA(()),
        ],
    )
    return pl.pallas_call(
        _combine_kernel,
        grid_spec=grid_spec,
        out_shape=jax.ShapeDtypeStruct((t, D_MODEL), F32),
        compiler_params=pltpu.CompilerParams(
            dimension_semantics=("arbitrary",), vmem_limit_bytes=_vmem_limit(32 << 20)),
        name="combine",
    )(plan["tbase"], plan["cnt"], plan["multi"], h, idx_tm, loc_tm, gate_tm, ys,
      g_final.reshape(1, D_MODEL))


def _moe(h, xn2, idx_t, gate_t, w_gate, b_gate, w_up, b_up, w_down, b_down, g_final):
    t = h.shape[0]
    plan = _route_plan(idx_t, t)
    xs, loc_t = _dispatch(plan, idx_t, xn2)
    ys = _experts(plan, xs, w_gate, b_gate, w_up, b_up, w_down, b_down)
    return _combine(plan, h, ys, idx_t.T, loc_t.T, gate_t.T, g_final)


def kernel(x, g_mix, w_in, b_in, conv_w, conv_b, ln_g, ln_b, rpb, w_out, b_out, g_ffn, w_router,
           b_router, w_gate, b_gate, w_up, b_up, w_down, b_down, g_final):
    bsz, seq, d = x.shape
    assert bsz == 1 and d == D_MODEL and g_mix.shape[0] == 1
    assert seq % (GRID_W * WIN_H) == 0
    x2 = x.reshape(seq, d)
    q, k, v, conv_out = _inproj_conv(x2, g_mix[0], w_in[0], b_in[0], conv_w[0], conv_b[0], ln_g[0], ln_b[0])
    na_out = _attention(q, k, v, rpb[0])
    h, xn2, idx_t, gate_t = _outproj_router(conv_out, na_out, x2, w_out[0], b_out[0], g_ffn[0],
                                            w_router[0], b_router[0])
    out = _moe(h, xn2, idx_t, gate_t, w_gate[0], b_gate[0], w_up[0], b_up[0], w_down[0], b_down[0],
               g_final)
    return out.reshape(bsz, seq, d)
```

```python
import functools

import numpy as np
import jax
import jax.numpy as jnp
from jax import lax
from jax.experimental import pallas as pl
from jax.experimental.pallas import tpu as pltpu

F32 = jnp.float32
BF16 = jnp.bfloat16
U32 = jnp.uint32
I32 = jnp.int32

D_MODEL = 1024
GRID_W = 64
D_CONV = 512
CONV_K = 31
NA_HEADS = 8
NA_HEAD_DIM = 64
D_NA = NA_HEADS * NA_HEAD_DIM
D_IN = 2 * D_CONV + 3 * D_NA
WIN_H = 8
WIN_W = 16
N_EXPERTS = 32
TOP_K = 4
D_FF = D_MODEL
SWIGLU_LIMIT = 7.0
SWIGLU_ALPHA = 1.702
EPS = 1e-5
MASK_VALUE = -1e30

V7X_VMEM_BYTES = 64 * 1024 * 1024
LANES = 128

PROJ_TM = 512
ROUTER_SUB = 256
CONV_HALO = 16
CONV_CH = 32
ATTN_RB = 32
ATTN_UNROLL = 8
ROUTE_TM = 512
SUBLANES = 8
ROUTE_RP = 96
EXP_BM = 512
EXP_PARTS = (512, 256, 128)
ZERO_ROWS = 256
HALF = D_MODEL // 2
ROW_BLOCK = 16


def _vmem_limit(nbytes):
    return int(min(nbytes, V7X_VMEM_BYTES - 6 * 1024 * 1024))


def _row_blocks(n):
    return [slice(r, min(r + ROW_BLOCK, n)) for r in range(0, n, ROW_BLOCK)]


def _pack_rows(x):
    out = []
    for rows in _row_blocks(x.shape[0]):
        lo = lax.bitcast_convert_type(x[rows, :HALF], U32) >> 16
        hi = lax.bitcast_convert_type(x[rows, HALF:], U32) & jnp.uint32(0xFFFF0000)
        out.append(lo | hi)
    return jnp.concatenate(out, axis=0)


def _unpack_rows(w):
    out = []
    for rows in _row_blocks(w.shape[0]):
        lo = lax.bitcast_convert_type(w[rows] << 16, F32)
        hi = lax.bitcast_convert_type(w[rows] & jnp.uint32(0xFFFF0000), F32)
        out.append(jnp.concatenate([lo, hi], axis=1).astype(BF16))
    return jnp.concatenate(out, axis=0)


def _inproj_conv_kernel(x_ref, g_ref, w_ref, b_ref, cw_ref, cb_ref, lg_ref, lb_ref,
                        q_ref, k_ref, v_ref, o_ref, ext_ref, u_ref, sh_ref):
    i = pl.program_id(0)
    nt = pl.num_programs(0) - 1
    tm = x_ref.shape[0]

    @pl.when(i == 0)
    def _():
        ext_ref[...] = jnp.zeros(ext_ref.shape, F32)

    x = x_ref[...]
    ms = jnp.mean(x * x, axis=-1, keepdims=True)
    xn = (x * lax.rsqrt(ms + EPS) * g_ref[...]).astype(BF16)

    def proj(c):
        sl = slice(c * D_CONV, (c + 1) * D_CONV)
        return jnp.dot(xn, w_ref[:, sl], preferred_element_type=F32) + b_ref[:, sl]

    u_ref[...] = proj(0) * jax.nn.sigmoid(proj(1))
    q_ref[...] = (proj(2) * (NA_HEAD_DIM ** -0.5)).astype(BF16)
    k_ref[...] = proj(3).astype(BF16)
    v_ref[...] = proj(4).astype(BF16)

    ext_ref[CONV_HALO + tm:, :] = jnp.where(i < nt, u_ref[0:CONV_HALO, :], 0.0)
    base = CONV_HALO - CONV_K // 2
    span = (CONV_K - 1) // SUBLANES * SUBLANES
    for b in range(SUBLANES):
        sh_ref[b] = ext_ref[base + b:base + b + tm + span, :]
    for c in range(tm // CONV_CH):
        acc = jnp.zeros((CONV_CH, D_CONV), F32)
        for b in range(SUBLANES):
            for a in range((CONV_K - 1 - b) // SUBLANES + 1):
                r = c * CONV_CH + SUBLANES * a
                acc = acc + sh_ref[b, r:r + CONV_CH, :] * cw_ref[SUBLANES * a + b:SUBLANES * a + b + 1, :]
        acc = acc + cb_ref[...]
        mu = jnp.mean(acc, axis=-1, keepdims=True)
        d = acc - mu
        var = jnp.mean(d * d, axis=-1, keepdims=True)
        un = d * lax.rsqrt(var + EPS) * lg_ref[...] + lb_ref[...]
        o_ref[c * CONV_CH:(c + 1) * CONV_CH, :] = (un * jax.nn.sigmoid(un)).astype(o_ref.dtype)

    ext_ref[0:CONV_HALO, :] = ext_ref[tm:tm + CONV_HALO, :]
    ext_ref[CONV_HALO:CONV_HALO + tm, :] = u_ref[...]


def _inproj_conv(x2, g_mix, w_in, b_in, conv_w, conv_b, ln_g, ln_b):
    t = x2.shape[0]
    tm = min(PROJ_TM, t)
    nt = t // tm
    tok = lambda i: (jnp.minimum(i, nt - 1), 0)
    prev = lambda i: (jnp.maximum(i - 1, 0), 0)
    const = lambda i: (0, 0)
    w = jnp.zeros((CONV_K + 1, D_CONV), F32).at[:CONV_K].set(conv_w)
    span = (CONV_K - 1) // SUBLANES * SUBLANES
    return pl.pallas_call(
        _inproj_conv_kernel,
        grid=(nt + 1,),
        in_specs=[
            pl.BlockSpec((tm, D_MODEL), tok),
            pl.BlockSpec((1, D_MODEL), const),
            pl.BlockSpec((D_MODEL, D_IN), const),
            pl.BlockSpec((1, D_IN), const),
            pl.BlockSpec((CONV_K + 1, D_CONV), const),
            pl.BlockSpec((1, D_CONV), const),
            pl.BlockSpec((1, D_CONV), const),
            pl.BlockSpec((1, D_CONV), const),
        ],
        out_specs=[
            pl.BlockSpec((tm, D_NA), tok),
            pl.BlockSpec((tm, D_NA), tok),
            pl.BlockSpec((tm, D_NA), tok),
            pl.BlockSpec((tm, D_CONV), prev),
        ],
        out_shape=[
            jax.ShapeDtypeStruct((t, D_NA), BF16),
            jax.ShapeDtypeStruct((t, D_NA), BF16),
            jax.ShapeDtypeStruct((t, D_NA), BF16),
            jax.ShapeDtypeStruct((t, D_CONV), BF16),
        ],
        scratch_shapes=[pltpu.VMEM((tm + 2 * CONV_HALO, D_CONV), F32),
                        pltpu.VMEM((tm, D_CONV), F32),
                        pltpu.VMEM((SUBLANES, tm + span, D_CONV), F32)],
        compiler_params=pltpu.CompilerParams(
            dimension_semantics=("arbitrary",), vmem_limit_bytes=_vmem_limit(56 << 20)),
        name="inproj_conv",
    )(x2, g_mix.reshape(1, D_MODEL), w_in.astype(BF16), b_in.reshape(1, D_IN), w,
      conv_b.reshape(1, D_CONV), ln_g.reshape(1, D_CONV), ln_b.reshape(1, D_CONV))


def _bias_table(rpb):
    col = np.arange(GRID_W)
    c0 = np.clip(col - WIN_W // 2, 0, GRID_W - WIN_W)
    valid = (col[None, :] >= c0[:, None]) & (col[None, :] < c0[:, None] + WIN_W)
    dcol = np.clip(col[None, :] - col[:, None], -(WIN_W - 1), WIN_W - 1) + (WIN_W - 1)
    col_sel = ((dcol[:, :, None] == np.arange(2 * WIN_W - 1)) & valid[:, :, None]).astype(np.float32)
    cls = np.arange(WIN_H)
    drow = cls[None, :] - cls[:, None] + (WIN_H - 1)
    row_sel = (drow[:, :, None] == np.arange(2 * WIN_H - 1)).astype(np.float32)
    b = jnp.einsum("hdj,crd,qkj->hcqrk", rpb.astype(F32), row_sel, col_sel,
                   precision=lax.Precision.HIGHEST)
    b = jnp.where(valid[None, None, :, None, :], b, MASK_VALUE)
    return b.reshape(NA_HEADS, WIN_H, GRID_W, WIN_H * GRID_W)


def _attn_kernel(q_ref, k_ref, v_ref, bt_ref, o_ref, s_ref, *, rb, rows):
    jb = pl.program_id(1)
    lane = lax.broadcasted_iota(jnp.int32, (GRID_W, LANES), 1)
    first_head = lane < NA_HEAD_DIM
    nkeys = WIN_H * GRID_W

    def window(r):
        i = jb * rb + r
        r0 = jnp.clip(i - WIN_H // 2, 0, rows - WIN_H)
        return i - r0, pl.multiple_of(r0 * GRID_W, GRID_W)

    def scores(r):
        _, start = window(r)
        q2 = q_ref[pl.ds(pl.multiple_of(r * GRID_W, GRID_W), GRID_W), :]
        zero = jnp.zeros_like(q2)
        qst = jnp.concatenate([jnp.where(first_head, q2, zero), jnp.where(first_head, zero, q2)], axis=0)
        ks = k_ref[pl.ds(start, nkeys), :]
        return lax.dot_general(qst, ks, (((1,), (1,)), ((), ())), preferred_element_type=F32)

    def finish(r, s):
        c, start = window(r)
        vs = v_ref[pl.ds(start, nkeys), :]
        bias = jnp.concatenate([bt_ref[0, c], bt_ref[1, c]], axis=0)
        s = jnp.where(bias > 0.5 * MASK_VALUE, s + bias, MASK_VALUE)
        m = jnp.max(s, axis=-1, keepdims=True)
        p = jnp.exp(s - m)
        l = jnp.sum(p, axis=-1, keepdims=True)
        o = jnp.dot(p.astype(BF16), vs, preferred_element_type=F32) / l
        o2 = jnp.where(first_head, o[:GRID_W], o[GRID_W:])
        o_ref[pl.ds(pl.multiple_of(r * GRID_W, GRID_W), GRID_W), :] = o2.astype(o_ref.dtype)

    s_ref[0] = scores(0)

    def pair(p, carry):
        r = 2 * p
        s_ref[1] = scores(r + 1)
        finish(r, s_ref[0])
        s_ref[0] = scores(jnp.minimum(r + 2, rb - 1))
        finish(r + 1, s_ref[1])
        return carry

    lax.fori_loop(0, rb // 2, pair, 0, unroll=ATTN_UNROLL)


def _attention(q, k, v, rpb):
    t = q.shape[0]
    rows = t // GRID_W
    rb = min(ATTN_RB, rows)
    npairs = D_NA // LANES
    bt = _bias_table(rpb)
    kern = functools.partial(_attn_kernel, rb=rb, rows=rows)
    return pl.pallas_call(
        kern,
        grid=(npairs, rows // rb),
        in_specs=[
            pl.BlockSpec((rb * GRID_W, LANES), lambda p, j: (j, p)),
            pl.BlockSpec((t, LANES), lambda p, j: (0, p)),
            pl.BlockSpec((t, LANES), lambda p, j: (0, p)),
            pl.BlockSpec((2, WIN_H, GRID_W, WIN_H * GRID_W), lambda p, j: (p, 0, 0, 0)),
        ],
        out_specs=pl.BlockSpec((rb * GRID_W, LANES), lambda p, j: (j, p)),
        out_shape=jax.ShapeDtypeStruct((t, D_NA), BF16),
        scratch_shapes=[pltpu.VMEM((2, 2 * GRID_W, WIN_H * GRID_W), F32)],
        compiler_params=pltpu.CompilerParams(
            dimension_semantics=("arbitrary", "arbitrary"), vmem_limit_bytes=_vmem_limit(40 << 20)),
        name="attention",
    )(q, k, v, bt)


def _outproj_kernel(conv_ref, na_ref, x_ref, wo_ref, bo_ref, g_ref, wr_ref, br_ref,
                    h_ref, xn_ref, idx_ref, gate_ref):
    for s in range(x_ref.shape[0] // ROUTER_SUB):
        rows = slice(s * ROUTER_SUB, (s + 1) * ROUTER_SUB)
        mixed = jnp.concatenate([conv_ref[rows, :], na_ref[rows, :]], axis=-1)
        h = x_ref[rows, :] + jnp.dot(mixed, wo_ref[...], preferred_element_type=F32) + bo_ref[...]
        h_ref[rows, :] = h
        ms = jnp.mean(h * h, axis=-1, keepdims=True)
        xn = (h * lax.rsqrt(ms + EPS) * g_ref[...]).astype(BF16)
        xn_ref[rows, :] = xn
        logits = lax.dot_general(wr_ref[...], xn, (((1,), (1,)), ((), ())),
                                 preferred_element_type=F32) + br_ref[...]
        ids = lax.broadcasted_iota(jnp.int32, logits.shape, 0)
        vals, sels = [], []
        l = logits
        for _ in range(TOP_K):
            m = jnp.max(l, axis=0, keepdims=True)
            sel = jnp.min(jnp.where(l == m, ids, N_EXPERTS), axis=0, keepdims=True)
            vals.append(m)
            sels.append(sel)
            l = jnp.where(ids == sel, -jnp.inf, l)
        es = [jnp.exp(vk - vals[0]) for vk in vals]
        tot = es[0] + es[1] + es[2] + es[3]
        idx_ref[:, rows] = jnp.concatenate(sels, axis=0)
        gate_ref[:, rows] = jnp.concatenate([e / tot for e in es], axis=0)


def _outproj_router(conv_out, na_out, x2, w_out, b_out, g_ffn, w_router, b_router):
    t = x2.shape[0]
    tm = min(PROJ_TM, t)
    tok = lambda i: (i, 0)
    const = lambda i: (0, 0)
    return pl.pallas_call(
        _outproj_kernel,
        grid=(t // tm,),
        in_specs=[
            pl.BlockSpec((tm, D_CONV), tok),
            pl.BlockSpec((tm, D_NA), tok),
            pl.BlockSpec((tm, D_MODEL), tok),
            pl.BlockSpec((D_MODEL, D_MODEL), const),
            pl.BlockSpec((1, D_MODEL), const),
            pl.BlockSpec((1, D_MODEL), const),
            pl.BlockSpec((N_EXPERTS, D_MODEL), const),
            pl.BlockSpec((N_EXPERTS, 1), const),
        ],
        out_specs=[
            pl.BlockSpec((tm, D_MODEL), tok),
            pl.BlockSpec((tm, D_MODEL), tok),
            pl.BlockSpec((TOP_K, tm), lambda i: (0, i)),
            pl.BlockSpec((TOP_K, tm), lambda i: (0, i)),
        ],
        out_shape=[
            jax.ShapeDtypeStruct((t, D_MODEL), F32),
            jax.ShapeDtypeStruct((t, D_MODEL), BF16),
            jax.ShapeDtypeStruct((TOP_K, t), jnp.int32),
            jax.ShapeDtypeStruct((TOP_K, t), F32),
        ],
        compiler_params=pltpu.CompilerParams(
            dimension_semantics=("arbitrary",), vmem_limit_bytes=_vmem_limit(48 << 20)),
        name="outproj_router",
    )(conv_out, na_out, x2, w_out.astype(BF16), b_out.reshape(1, D_MODEL), g_ffn.reshape(1, D_MODEL),
      w_router.T.astype(BF16), b_router.reshape(N_EXPERTS, 1))


def _route_plan(idx_t, t):
    nt = t // ROUTE_TM
    experts = jnp.arange(N_EXPERTS, dtype=I32)
    onehot = idx_t.reshape(TOP_K, nt, ROUTE_TM, 1) == experts
    cnt = jnp.sum(onehot, axis=(0, 2), dtype=I32)
    sizes = jnp.sum(cnt, axis=0)
    padded = (sizes + ROUTE_RP + EXP_BM - 1) // EXP_BM * EXP_BM
    pad_ends = jnp.cumsum(padded)
    pad_off = pad_ends - padded
    tbase = pad_off[None, :] + jnp.cumsum(cnt, axis=0) - cnt
    cap = _sorted_rows(t)
    nb = cap // EXP_BM
    n_used = pad_ends[N_EXPERTS - 1] // EXP_BM
    blk_src = jnp.minimum(jnp.arange(nb, dtype=I32), n_used - 1)
    blk_exp = jnp.minimum(
        jnp.sum(blk_src[:, None] * EXP_BM >= pad_ends[None, :], axis=1, dtype=I32), N_EXPERTS - 1)
    row_end = jnp.sum(jnp.where(blk_exp[:, None] == experts[None, :], (pad_off + sizes)[None, :], 0), axis=1)
    blk_live = jnp.clip(row_end - blk_src * EXP_BM, 0, EXP_BM).astype(I32)
    multi = jnp.any(tbase % SUBLANES + cnt > ROUTE_RP, axis=1).astype(I32)
    zero_from = jnp.concatenate([(pad_off + sizes) // ZERO_ROWS * ZERO_ROWS, pad_ends[N_EXPERTS - 1:]])
    zero_to = jnp.concatenate([pad_ends, jnp.full((1,), cap, I32)])
    return dict(cnt=cnt.reshape(-1), tbase=tbase.reshape(-1).astype(I32), multi=multi,
                zero_start=zero_from.astype(I32), zero_cnt=((zero_to - zero_from) // ZERO_ROWS).astype(I32),
                blk_src=blk_src, blk_exp=blk_exp, blk_live=blk_live, n_used=n_used.reshape(1).astype(I32))


def _sorted_rows(t):
    cap = t * TOP_K + N_EXPERTS * (EXP_BM + ROUTE_RP)
    return (cap + EXP_BM - 1) // EXP_BM * EXP_BM


def _chunk_geometry(tb, n):
    head = tb % SUBLANES
    start = pl.multiple_of(tb - head, SUBLANES)
    nchunks = (head + n + ROUTE_RP - 1) // ROUTE_RP
    return start, head, nchunks


def _dispatch_kernel(tb_ref, cnt_ref, mt_ref, zs_ref, zc_ref, idx_ref, tbv_ref, xn_ref, xs_hbm, loc_ref,
                     stage, ostage, zbuf, head_ref, cs_ref, ms_ref, sem, osem, zsem):
    i = pl.program_id(0)
    nt = pl.num_programs(0)
    slot = i % 2
    tm = idx_ref.shape[1]
    groups = ROUTE_RP // SUBLANES

    def geometry(e, step=None):
        step = i if step is None else step
        return _chunk_geometry(tb_ref[step * N_EXPERTS + e], cnt_ref[step * N_EXPERTS + e])

    def chunk_copy(step, e):
        start, _, _ = geometry(e, step)
        return pltpu.make_async_copy(stage.at[step % 2, pl.ds(e * ROUTE_RP, ROUTE_RP)],
                                     xs_hbm.at[pl.ds(start, ROUTE_RP)], sem.at[step % 2])

    def all_chunks(step, phase):
        for e in range(N_EXPERTS):
            getattr(chunk_copy(step, e), phase)()

    @pl.when(i >= 2)
    def _():
        all_chunks(i - 2, "wait")

    @pl.when(i >= 1)
    def _():
        all_chunks(i - 1, "start")

    def zero_copy(start):
        return pltpu.make_async_copy(zbuf, xs_hbm.at[pl.ds(pl.multiple_of(start, ZERO_ROWS), ZERO_ROWS)], zsem)

    @pl.when(i == 0)
    def _():
        zbuf[...] = jnp.zeros(zbuf.shape, U32)
        head_ref[...] = jnp.zeros(head_ref.shape, U32)
        for phase in ("start", "wait"):
            for j in range(N_EXPERTS + 1):
                def piece(q, carry, j=j, phase=phase):
                    getattr(zero_copy(zs_ref[j] + q * ZERO_ROWS), phase)()
                    return carry
                lax.fori_loop(0, zc_ref[j], piece, 0)

    idx = idx_ref[...]
    eio = lax.broadcasted_iota(I32, (N_EXPERTS, tm), 0)
    member = jnp.zeros((N_EXPERTS, tm), F32)
    for k in range(TOP_K):
        member = member + (idx[k:k + 1, :] == eio).astype(F32)
    tri = (lax.broadcasted_iota(I32, (tm, tm), 0) < lax.broadcasted_iota(I32, (tm, tm), 1)).astype(BF16)
    rank = jnp.dot(member.astype(BF16), tri, preferred_element_type=F32)
    pos = rank + tbv_ref[0][:, 0:1].astype(F32)
    loc_ref[...] = jnp.concatenate(
        [jnp.sum(jnp.where(idx[k:k + 1, :] == eio, pos, 0.0), axis=0, keepdims=True) for k in range(TOP_K)],
        axis=0).astype(I32)
    cs_ref[...] = pos
    ms_ref[...] = member

    jio = lax.broadcasted_iota(I32, (ROUTE_RP, tm), 0).astype(F32)
    sel = [jnp.where((jio == pos[e:e + 1, :]) & (member[e:e + 1, :] > 0.0), 1.0, 0.0).astype(BF16)
           for e in range(N_EXPERTS)]
    rows = jnp.dot(jnp.concatenate(sel, axis=0), xn_ref[...], preferred_element_type=F32)

    stage[slot] = _pack_rows(rows)
    sub = lax.broadcasted_iota(I32, (SUBLANES, HALF), 0)
    for e in range(N_EXPERTS):
        _, head, _ = geometry(e)
        first = pl.ds(e * ROUTE_RP, SUBLANES)
        stage[slot, first, :] = jnp.where(sub < head, head_ref[e], stage[slot, first, :])
        g = jnp.minimum((head + cnt_ref[i * N_EXPERTS + e]) // SUBLANES, groups - 1)
        head_ref[e] = stage[slot, pl.ds(pl.multiple_of(e * ROUTE_RP + g * SUBLANES, SUBLANES), SUBLANES), :]

    def per_expert(e, carry):
        start, head, nchunks = geometry(e)
        end = head + cnt_ref[i * N_EXPERTS + e]

        def per_chunk(c, carry2):
            want = jio + (c * ROUTE_RP).astype(F32)
            pick = jnp.where((want == cs_ref[pl.ds(e, 1), :]) & (ms_ref[pl.ds(e, 1), :] > 0.0), 1.0, 0.0)
            ostage[...] = _pack_rows(jnp.dot(pick.astype(BF16), xn_ref[...], preferred_element_type=F32))
            cp = pltpu.make_async_copy(
                ostage, xs_hbm.at[pl.ds(pl.multiple_of(start + c * ROUTE_RP, SUBLANES), ROUTE_RP)], osem)
            cp.start()
            cp.wait()

            @pl.when(end // ROUTE_RP == c)
            def _():
                g = (end - c * ROUTE_RP) // SUBLANES
                head_ref[e] = ostage[pl.ds(pl.multiple_of(g * SUBLANES, SUBLANES), SUBLANES), :]
            return carry2
        return lax.fori_loop(1, nchunks, per_chunk, carry)

    @pl.when(mt_ref[i] > 0)
    def _():
        lax.fori_loop(0, N_EXPERTS, per_expert, 0)

    @pl.when(i == nt - 1)
    def _():
        @pl.when(i >= 1)
        def _():
            all_chunks(i - 1, "wait")
        all_chunks(i, "start")
        all_chunks(i, "wait")


def _dispatch(plan, idx_t, xn2):
    t = xn2.shape[0]
    tm = ROUTE_TM
    cap = _sorted_rows(t)
    grid_spec = pltpu.PrefetchScalarGridSpec(
        num_scalar_prefetch=5,
        grid=(t // tm,),
        in_specs=[
            pl.BlockSpec((TOP_K, tm), lambda i, *_: (0, i)),
            pl.BlockSpec((1, N_EXPERTS, LANES), lambda i, *_: (i, 0, 0)),
            pl.BlockSpec((tm, D_MODEL), lambda i, *_: (i, 0)),
        ],
        out_specs=[
            pl.BlockSpec(memory_space=pl.ANY),
            pl.BlockSpec((TOP_K, tm), lambda i, *_: (0, i)),
        ],
        scratch_shapes=[
            pltpu.VMEM((2, N_EXPERTS * ROUTE_RP, HALF), U32),
            pltpu.VMEM((ROUTE_RP, HALF), U32),
            pltpu.VMEM((ZERO_ROWS, HALF), U32),
            pltpu.VMEM((N_EXPERTS, SUBLANES, HALF), U32),
            pltpu.VMEM((N_EXPERTS, tm), F32),
            pltpu.VMEM((N_EXPERTS, tm), F32),
            pltpu.SemaphoreType.DMA((2,)),
            pltpu.SemaphoreType.DMA(()),
            pltpu.SemaphoreType.DMA(()),
        ],
    )
    return pl.pallas_call(
        _dispatch_kernel,
        grid_spec=grid_spec,
        out_shape=[
            jax.ShapeDtypeStruct((cap, HALF), U32),
            jax.ShapeDtypeStruct((TOP_K, t), I32),
        ],
        compiler_params=pltpu.CompilerParams(
            dimension_semantics=("arbitrary",), vmem_limit_bytes=_vmem_limit(56 << 20)),
        name="dispatch",
    )(plan["tbase"], plan["cnt"], plan["multi"], plan["zero_start"], plan["zero_cnt"], idx_t,
      jnp.broadcast_to((plan["tbase"] % SUBLANES).reshape(t // tm, N_EXPERTS, 1), (t // tm, N_EXPERTS, LANES)),
      xn2)


def _expert_kernel(be_ref, src_ref, nu_ref, live_ref, xs_ref, wg_hbm, wu_hbm, wd_hbm, bg_ref, bu_ref, bd_ref,
                   ys_ref, wf32, wbf, wsem):
    b = pl.program_id(0)
    e = be_ref[b]

    def weight_copies(expert, par):
        return [pltpu.make_async_copy(w.at[expert], wf32.at[par, m], wsem.at[par])
                for m, w in enumerate((wg_hbm, wu_hbm, wd_hbm))]

    @pl.when(b == 0)
    def _():
        for cp in weight_copies(e, e % 2):
            cp.start()

    @pl.when(b < nu_ref[0])
    def _():
        @pl.when((b == 0) | (e != be_ref[jnp.maximum(b - 1, 0)]))
        def _():
            par = e % 2
            for cp in weight_copies(e, par):
                cp.wait()

            @pl.when(e + 1 < N_EXPERTS)
            def _():
                for cp in weight_copies(e + 1, 1 - par):
                    cp.start()

            for m in range(3):
                wbf[m] = wf32[par, m].astype(BF16)

        def ffn(rows):
            x = _unpack_rows(xs_ref[rows, :])
            gt = jnp.minimum(jnp.dot(x, wbf[0], preferred_element_type=F32) + bg_ref[0], SWIGLU_LIMIT)
            up = jnp.clip(jnp.dot(x, wbf[1], preferred_element_type=F32) + bu_ref[0],
                          -SWIGLU_LIMIT, SWIGLU_LIMIT)
            hdn = (up + 1.0) * (gt * jax.nn.sigmoid(SWIGLU_ALPHA * gt))
            y = jnp.dot(hdn.astype(BF16), wbf[2], preferred_element_type=F32) + bd_ref[0]
            ys_ref[rows, :] = _pack_rows(y.astype(BF16).astype(F32))

        live = live_ref[b]
        for p, part in enumerate(EXP_PARTS):
            smaller = EXP_PARTS[p + 1] if p + 1 < len(EXP_PARTS) else -1
            cond = live > smaller
            if p > 0:
                cond = cond & (live <= part)

            @pl.when(cond)
            def _(part=part):
                ffn(slice(0, part))
                if part < EXP_BM:
                    ys_ref[part:, :] = jnp.zeros((EXP_BM - part, HALF), U32)

    @pl.when(b >= nu_ref[0])
    def _():
        ys_ref[...] = jnp.zeros(ys_ref.shape, U32)


def _experts(plan, xs, w_gate, b_gate, w_up, b_up, w_down, b_down):
    cap = xs.shape[0]
    nb = cap // EXP_BM
    rows = lambda b, be, src, nu, live: (src[b], 0)
    wsel = lambda b, be, src, nu, live: (be[b], 0, 0)
    wspec = pl.BlockSpec(memory_space=pl.ANY)
    bspec = pl.BlockSpec((1, 1, D_FF), wsel)
    grid_spec = pltpu.PrefetchScalarGridSpec(
        num_scalar_prefetch=4,
        grid=(nb,),
        in_specs=[pl.BlockSpec((EXP_BM, HALF), rows), wspec, wspec, wspec, bspec, bspec, bspec],
        out_specs=pl.BlockSpec((EXP_BM, HALF), lambda b, be, src, nu, live: (b, 0)),
        scratch_shapes=[pltpu.VMEM((2, 3, D_MODEL, D_FF), F32),
                        pltpu.VMEM((3, D_MODEL, D_FF), BF16),
                        pltpu.SemaphoreType.DMA((2,))],
    )
    return pl.pallas_call(
        _expert_kernel,
        grid_spec=grid_spec,
        out_shape=jax.ShapeDtypeStruct((cap, HALF), U32),
        compiler_params=pltpu.CompilerParams(
            dimension_semantics=("arbitrary",), vmem_limit_bytes=_vmem_limit(56 << 20)),
        name="experts",
    )(plan["blk_exp"], plan["blk_src"], plan["n_used"], plan["blk_live"], xs, w_gate, w_up, w_down,
      b_gate.reshape(N_EXPERTS, 1, D_FF), b_up.reshape(N_EXPERTS, 1, D_FF),
      b_down.reshape(N_EXPERTS, 1, D_MODEL))


def _combine_kernel(tb_ref, cnt_ref, mt_ref, h_ref, idx_ref, loc_ref, gate_ref, ys_hbm, g_ref, o_ref,
                    ybuf, obuf, acc_ref, sem, osem):
    i = pl.program_id(0)
    nt = pl.num_programs(0)
    slot = i % 2
    tm = h_ref.shape[0]

    def geometry(step, e):
        return _chunk_geometry(tb_ref[step * N_EXPERTS + e], cnt_ref[step * N_EXPERTS + e])

    def chunk_copy(step, e, s):
        start, _, _ = geometry(step, e)
        return pltpu.make_async_copy(ys_hbm.at[pl.ds(start, ROUTE_RP)],
                                     ybuf.at[s, pl.ds(e * ROUTE_RP, ROUTE_RP)], sem.at[s])

    @pl.when(i == 0)
    def _():
        for e in range(N_EXPERTS):
            chunk_copy(0, e, 0).start()

    @pl.when(i + 1 < nt)
    def _():
        for e in range(N_EXPERTS):
            chunk_copy(i + 1, e, 1 - slot).start()

    idx_loc = jnp.transpose(jnp.concatenate([idx_ref[...], loc_ref[...]], axis=0))
    idx, loc = idx_loc[:, :TOP_K], idx_loc[:, TOP_K:]
    gate = jnp.transpose(jnp.concatenate([gate_ref[...], gate_ref[...]], axis=0))[:, :TOP_K]

    def gate_matrix(ncols, col_of):
        out = []
        for rows in _row_blocks(tm):
            colio = lax.broadcasted_iota(I32, (rows.stop - rows.start, ncols), 1)
            g = jnp.zeros(colio.shape, F32)
            for k in range(TOP_K):
                g = jnp.where(colio == col_of[rows, k:k + 1], gate[rows, k:k + 1], g)
            out.append(g.astype(BF16))
        return jnp.concatenate(out, axis=0)

    for e in range(N_EXPERTS):
        chunk_copy(i, e, slot).wait()
    col = jnp.where(loc < ROUTE_RP, idx * ROUTE_RP + loc, -1)
    acc_ref[...] = jnp.dot(gate_matrix(N_EXPERTS * ROUTE_RP, col), _unpack_rows(ybuf[slot]),
                           preferred_element_type=F32)

    def per_expert(e, carry):
        start, _, nchunks = geometry(i, e)

        def per_chunk(c, carry2):
            cp = pltpu.make_async_copy(
                ys_hbm.at[pl.ds(pl.multiple_of(start + c * ROUTE_RP, SUBLANES), ROUTE_RP)], obuf, osem)
            cp.start()
            cp.wait()
            ccol = jnp.where(idx == e, loc - c * ROUTE_RP, -1)
            acc_ref[...] += jnp.dot(gate_matrix(ROUTE_RP, ccol), _unpack_rows(obuf[...]),
                                    preferred_element_type=F32)
            return carry2
        return lax.fori_loop(1, nchunks, per_chunk, carry)

    @pl.when(mt_ref[i] > 0)
    def _():
        lax.fori_loop(0, N_EXPERTS, per_expert, 0)

    out = h_ref[...] + acc_ref[...]
    ms = jnp.mean(out * out, axis=-1, keepdims=True)
    o_ref[...] = out * lax.rsqrt(ms + EPS) * g_ref[...]


def _combine(plan, h, ys, idx_t, loc_t, gate_t, g_final):
    t = h.shape[0]
    tm = ROUTE_TM
    tok = lambda i, *_: (i, 0)
    lane = lambda i, *_: (0, i)
    grid_spec = pltpu.PrefetchScalarGridSpec(
        num_scalar_prefetch=3,
        grid=(t // tm,),
        in_specs=[
            pl.BlockSpec((tm, D_MODEL), tok),
            pl.BlockSpec((TOP_K, tm), lane),
            pl.BlockSpec((TOP_K, tm), lane),
            pl.BlockSpec((TOP_K, tm), lane),
            pl.BlockSpec(memory_space=pl.ANY),
            pl.BlockSpec((1, D_MODEL), lambda i, *_: (0, 0)),
        ],
        out_specs=pl.BlockSpec((tm, D_MODEL), tok),
        scratch_shapes=[
            pltpu.VMEM((2, N_EXPERTS * ROUTE_RP, HALF), U32),
            pltpu.VMEM((ROUTE_RP, HALF), U32),
            pltpu.VMEM((tm, D_MODEL), F32),
            pltpu.SemaphoreType.DMA((2,)),
            pltpu.SemaphoreType.DMA(()),
        ],
    )
    return pl.pallas_call(
        _combine_kernel,
        grid_spec=grid_spec,
        out_shape=jax.ShapeDtypeStruct((t, D_MODEL), F32),
        compiler_params=pltpu.CompilerParams(
            dimension_semantics=("arbitrary",), vmem_limit_bytes=_vmem_limit(56 << 20)),
        name="combine",
    )(plan["tbase"], plan["cnt"], plan["multi"], h, idx_t, loc_t, gate_t, ys,
      g_final.reshape(1, D_MODEL))


def _moe(h, xn2, idx_t, gate_t, w_gate, b_gate, w_up, b_up, w_down, b_down, g_final):
    t = h.shape[0]
    plan = _route_plan(idx_t, t)
    xs, loc_t = _dispatch(plan, idx_t, xn2)
    ys = _experts(plan, xs, w_gate, b_gate, w_up, b_up, w_down, b_down)
    return _combine(plan, h, ys, idx_t, loc_t, gate_t, g_final)


def kernel(x, g_mix, w_in, b_in, conv_w, conv_b, ln_g, ln_b, rpb, w_out, b_out, g_ffn, w_router,
           b_router, w_gate, b_gate, w_up, b_up, w_down, b_down, g_final):
    bsz, seq, d = x.shape
    assert bsz == 1 and d == D_MODEL and g_mix.shape[0] == 1
    assert seq % (GRID_W * WIN_H) == 0
    x2 = x.reshape(seq, d)
    q, k, v, conv_out = _inproj_conv(x2, g_mix[0], w_in[0], b_in[0], conv_w[0], conv_b[0], ln_g[0], ln_b[0])
    na_out = _attention(q, k, v, rpb[0])
    h, xn2, idx_t, gate_t = _outproj_router(conv_out, na_out, x2, w_out[0], b_out[0], g_ffn[0],
                                            w_router[0], b_router[0])
    out = _moe(h, xn2, idx_t, gate_t, w_gate[0], b_gate[0], w_up[0], b_up[0], w_down[0], b_down[0],
               g_final)
    return out.reshape(bsz, seq, d)
```

```python
import functools

import numpy as np
import jax
import jax.numpy as jnp
from jax import lax
from jax.experimental import pallas as pl
from jax.experimental.pallas import tpu as pltpu

F32 = jnp.float32
BF16 = jnp.bfloat16
U32 = jnp.uint32
I32 = jnp.int32

D_MODEL = 1024
GRID_W = 64
D_CONV = 512
CONV_K = 31
NA_HEADS = 8
NA_HEAD_DIM = 64
D_NA = NA_HEADS * NA_HEAD_DIM
D_IN = 2 * D_CONV + 3 * D_NA
WIN_H = 8
WIN_W = 16
N_EXPERTS = 32
TOP_K = 4
D_FF = D_MODEL
SWIGLU_LIMIT = 7.0
SWIGLU_ALPHA = 1.702
EPS = 1e-5
MASK_VALUE = -1e30

V7X_VMEM_BYTES = 64 * 1024 * 1024
LANES = 128

PROJ_TM = 512
ROUTER_SUB = 256
CONV_HALO = 16
CONV_CH = 32
ATTN_RB = 32
ATTN_UNROLL = 8
ROUTE_TM = 256
SUBLANES = 8
ROUTE_RP = 56
EXP_BM = 512
EXP_PARTS = (512, 256, 128)
ZERO_ROWS = 256
HALF = D_MODEL // 2
ROW_BLOCK = 16


def _vmem_limit(nbytes):
    return int(min(nbytes, V7X_VMEM_BYTES - 6 * 1024 * 1024))


def _row_blocks(n):
    return [slice(r, min(r + ROW_BLOCK, n)) for r in range(0, n, ROW_BLOCK)]


def _pack_rows(x):
    out = []
    for rows in _row_blocks(x.shape[0]):
        lo = lax.bitcast_convert_type(x[rows, :HALF], U32) >> 16
        hi = lax.bitcast_convert_type(x[rows, HALF:], U32) & jnp.uint32(0xFFFF0000)
        out.append(lo | hi)
    return jnp.concatenate(out, axis=0)


def _unpack_rows(w):
    out = []
    for rows in _row_blocks(w.shape[0]):
        lo = lax.bitcast_convert_type(w[rows] << 16, F32)
        hi = lax.bitcast_convert_type(w[rows] & jnp.uint32(0xFFFF0000), F32)
        out.append(jnp.concatenate([lo, hi], axis=1).astype(BF16))
    return jnp.concatenate(out, axis=0)


def _inproj_conv_kernel(x_ref, g_ref, w_ref, b_ref, cw_ref, cb_ref, lg_ref, lb_ref,
                        q_ref, k_ref, v_ref, o_ref, ext_ref, u_ref, sh_ref):
    i = pl.program_id(0)
    nt = pl.num_programs(0) - 1
    tm = x_ref.shape[0]

    @pl.when(i == 0)
    def _():
        ext_ref[...] = jnp.zeros(ext_ref.shape, F32)

    x = x_ref[...]
    ms = jnp.mean(x * x, axis=-1, keepdims=True)
    xn = (x * lax.rsqrt(ms + EPS) * g_ref[...]).astype(BF16)

    def proj(c):
        sl = slice(c * D_CONV, (c + 1) * D_CONV)
        return jnp.dot(xn, w_ref[:, sl], preferred_element_type=F32) + b_ref[:, sl]

    u_ref[...] = proj(0) * jax.nn.sigmoid(proj(1))
    q_ref[...] = (proj(2) * (NA_HEAD_DIM ** -0.5)).astype(BF16)
    k_ref[...] = proj(3).astype(BF16)
    v_ref[...] = proj(4).astype(BF16)

    ext_ref[CONV_HALO + tm:, :] = jnp.where(i < nt, u_ref[0:CONV_HALO, :], 0.0)
    base = CONV_HALO - CONV_K // 2
    span = (CONV_K - 1) // SUBLANES * SUBLANES
    for b in range(SUBLANES):
        sh_ref[b] = ext_ref[base + b:base + b + tm + span, :]
    for c in range(tm // CONV_CH):
        acc = jnp.zeros((CONV_CH, D_CONV), F32)
        for b in range(SUBLANES):
            for a in range((CONV_K - 1 - b) // SUBLANES + 1):
                r = c * CONV_CH + SUBLANES * a
                acc = acc + sh_ref[b, r:r + CONV_CH, :] * cw_ref[SUBLANES * a + b:SUBLANES * a + b + 1, :]
        acc = acc + cb_ref[...]
        mu = jnp.mean(acc, axis=-1, keepdims=True)
        d = acc - mu
        var = jnp.mean(d * d, axis=-1, keepdims=True)
        un = d * lax.rsqrt(var + EPS) * lg_ref[...] + lb_ref[...]
        o_ref[c * CONV_CH:(c + 1) * CONV_CH, :] = (un * jax.nn.sigmoid(un)).astype(o_ref.dtype)

    ext_ref[0:CONV_HALO, :] = ext_ref[tm:tm + CONV_HALO, :]
    ext_ref[CONV_HALO:CONV_HALO + tm, :] = u_ref[...]


def _inproj_conv(x2, g_mix, w_in, b_in, conv_w, conv_b, ln_g, ln_b):
    t = x2.shape[0]
    tm = min(PROJ_TM, t)
    nt = t // tm
    tok = lambda i: (jnp.minimum(i, nt - 1), 0)
    prev = lambda i: (jnp.maximum(i - 1, 0), 0)
    const = lambda i: (0, 0)
    w = jnp.zeros((CONV_K + 1, D_CONV), F32).at[:CONV_K].set(conv_w)
    span = (CONV_K - 1) // SUBLANES * SUBLANES
    return pl.pallas_call(
        _inproj_conv_kernel,
        grid=(nt + 1,),
        in_specs=[
            pl.BlockSpec((tm, D_MODEL), tok),
            pl.BlockSpec((1, D_MODEL), const),
            pl.BlockSpec((D_MODEL, D_IN), const),
            pl.BlockSpec((1, D_IN), const),
            pl.BlockSpec((CONV_K + 1, D_CONV), const),
            pl.BlockSpec((1, D_CONV), const),
            pl.BlockSpec((1, D_CONV), const),
            pl.BlockSpec((1, D_CONV), const),
        ],
        out_specs=[
            pl.BlockSpec((tm, D_NA), tok),
            pl.BlockSpec((tm, D_NA), tok),
            pl.BlockSpec((tm, D_NA), tok),
            pl.BlockSpec((tm, D_CONV), prev),
        ],
        out_shape=[
            jax.ShapeDtypeStruct((t, D_NA), BF16),
            jax.ShapeDtypeStruct((t, D_NA), BF16),
            jax.ShapeDtypeStruct((t, D_NA), BF16),
            jax.ShapeDtypeStruct((t, D_CONV), BF16),
        ],
        scratch_shapes=[pltpu.VMEM((tm + 2 * CONV_HALO, D_CONV), F32),
                        pltpu.VMEM((tm, D_CONV), F32),
                        pltpu.VMEM((SUBLANES, tm + span, D_CONV), F32)],
        compiler_params=pltpu.CompilerParams(
            dimension_semantics=("arbitrary",), vmem_limit_bytes=_vmem_limit(56 << 20)),
        name="inproj_conv",
    )(x2, g_mix.reshape(1, D_MODEL), w_in.astype(BF16), b_in.reshape(1, D_IN), w,
      conv_b.reshape(1, D_CONV), ln_g.reshape(1, D_CONV), ln_b.reshape(1, D_CONV))


def _column_bias(rpb):
    col = np.arange(GRID_W)
    c0 = np.clip(col - WIN_W // 2, 0, GRID_W - WIN_W)
    valid = (col[None, :] >= c0[:, None]) & (col[None, :] < c0[:, None] + WIN_W)
    dcol = np.clip(col[None, :] - col[:, None], -(WIN_W - 1), WIN_W - 1) + (WIN_W - 1)
    col_sel = ((dcol[:, :, None] == np.arange(2 * WIN_W - 1)) & valid[:, :, None]).astype(np.float32)
    b = jnp.einsum("hdj,qkj->hdqk", rpb.astype(F32), col_sel, precision=lax.Precision.HIGHEST)
    b = jnp.where(valid[None, None], b, MASK_VALUE)
    return jnp.concatenate([b, b], axis=-1)


def _attn_kernel(q_ref, k_ref, v_ref, cb_ref, o_ref, s_ref, bt_ref, *, rb, rows):
    jb = pl.program_id(1)
    lane = lax.broadcasted_iota(jnp.int32, (GRID_W, LANES), 1)
    first_head = lane < NA_HEAD_DIM
    nkeys = WIN_H * GRID_W

    @pl.when(jb == 0)
    def _():
        for h in range(2):
            for c in range(WIN_H):
                for rr in range(0, WIN_H, 2):
                    even = cb_ref[h, rr - c + WIN_H - 1]
                    odd = cb_ref[h, rr + 1 - c + WIN_H - 1]
                    bt_ref[h, c, :, rr * GRID_W:(rr + 2) * GRID_W] = jnp.where(first_head, even, odd)

    def window(r):
        i = jb * rb + r
        r0 = jnp.clip(i - WIN_H // 2, 0, rows - WIN_H)
        return i - r0, pl.multiple_of(r0 * GRID_W, GRID_W)

    def scores(r):
        _, start = window(r)
        q2 = q_ref[pl.ds(pl.multiple_of(r * GRID_W, GRID_W), GRID_W), :]
        zero = jnp.zeros_like(q2)
        qst = jnp.concatenate([jnp.where(first_head, q2, zero), jnp.where(first_head, zero, q2)], axis=0)
        ks = k_ref[pl.ds(start, nkeys), :]
        return lax.dot_general(qst, ks, (((1,), (1,)), ((), ())), preferred_element_type=F32)

    def finish(r, s):
        c, start = window(r)
        vs = v_ref[pl.ds(start, nkeys), :]
        bias = jnp.concatenate([bt_ref[0, c], bt_ref[1, c]], axis=0)
        s = jnp.where(bias > 0.5 * MASK_VALUE, s + bias, MASK_VALUE)
        m = jnp.max(s, axis=-1, keepdims=True)
        p = jnp.exp(s - m)
        l = jnp.sum(p, axis=-1, keepdims=True)
        o = jnp.dot(p.astype(BF16), vs, preferred_element_type=F32) / l
        o2 = jnp.where(first_head, o[:GRID_W], o[GRID_W:])
        o_ref[pl.ds(pl.multiple_of(r * GRID_W, GRID_W), GRID_W), :] = o2.astype(o_ref.dtype)

    s_ref[0] = scores(0)

    def pair(p, carry):
        r = 2 * p
        s_ref[1] = scores(r + 1)
        finish(r, s_ref[0])
        s_ref[0] = scores(jnp.minimum(r + 2, rb - 1))
        finish(r + 1, s_ref[1])
        return carry

    lax.fori_loop(0, rb // 2, pair, 0, unroll=ATTN_UNROLL)


def _attention(q, k, v, rpb):
    t = q.shape[0]
    rows = t // GRID_W
    rb = min(ATTN_RB, rows)
    npairs = D_NA // LANES
    cb = _column_bias(rpb)
    kern = functools.partial(_attn_kernel, rb=rb, rows=rows)
    return pl.pallas_call(
        kern,
        grid=(npairs, rows // rb),
        in_specs=[
            pl.BlockSpec((rb * GRID_W, LANES), lambda p, j: (j, p)),
            pl.BlockSpec((t, LANES), lambda p, j: (0, p)),
            pl.BlockSpec((t, LANES), lambda p, j: (0, p)),
            pl.BlockSpec((2, 2 * WIN_H - 1, GRID_W, 2 * GRID_W), lambda p, j: (p, 0, 0, 0)),
        ],
        out_specs=pl.BlockSpec((rb * GRID_W, LANES), lambda p, j: (j, p)),
        out_shape=jax.ShapeDtypeStruct((t, D_NA), BF16),
        scratch_shapes=[pltpu.VMEM((2, 2 * GRID_W, WIN_H * GRID_W), F32),
                        pltpu.VMEM((2, WIN_H, GRID_W, WIN_H * GRID_W), F32)],
        compiler_params=pltpu.CompilerParams(
            dimension_semantics=("arbitrary", "arbitrary"), vmem_limit_bytes=_vmem_limit(40 << 20)),
        name="attention",
    )(q, k, v, cb)


def _outproj_kernel(conv_ref, na_ref, x_ref, wo_ref, bo_ref, g_ref, wr_ref, br_ref,
                    h_ref, xn_ref, idx_ref, gate_ref):
    for s in range(x_ref.shape[0] // ROUTER_SUB):
        rows = slice(s * ROUTER_SUB, (s + 1) * ROUTER_SUB)
        mixed = jnp.concatenate([conv_ref[rows, :], na_ref[rows, :]], axis=-1)
        h = x_ref[rows, :] + jnp.dot(mixed, wo_ref[...], preferred_element_type=F32) + bo_ref[...]
        h_ref[rows, :] = h
        ms = jnp.mean(h * h, axis=-1, keepdims=True)
        xn = (h * lax.rsqrt(ms + EPS) * g_ref[...]).astype(BF16)
        xn_ref[rows, :] = xn
        logits = jnp.transpose(jnp.dot(xn, wr_ref[...], preferred_element_type=F32) + br_ref[...])[:N_EXPERTS]
        ids = lax.broadcasted_iota(jnp.int32, logits.shape, 0)
        vals, sels = [], []
        l = logits
        for _ in range(TOP_K):
            m = jnp.max(l, axis=0, keepdims=True)
            sel = jnp.min(jnp.where(l == m, ids, N_EXPERTS), axis=0, keepdims=True)
            vals.append(m)
            sels.append(sel)
            l = jnp.where(ids == sel, -jnp.inf, l)
        es = [jnp.exp(vk - vals[0]) for vk in vals]
        tot = es[0] + es[1] + es[2] + es[3]
        idx_ref[:, rows] = jnp.concatenate(sels, axis=0)
        gate_ref[:, rows] = jnp.concatenate([e / tot for e in es], axis=0)


def _outproj_router(conv_out, na_out, x2, w_out, b_out, g_ffn, w_router, b_router):
    t = x2.shape[0]
    tm = min(PROJ_TM, t)
    tok = lambda i: (i, 0)
    const = lambda i: (0, 0)
    return pl.pallas_call(
        _outproj_kernel,
        grid=(t // tm,),
        in_specs=[
            pl.BlockSpec((tm, D_CONV), tok),
            pl.BlockSpec((tm, D_NA), tok),
            pl.BlockSpec((tm, D_MODEL), tok),
            pl.BlockSpec((D_MODEL, D_MODEL), const),
            pl.BlockSpec((1, D_MODEL), const),
            pl.BlockSpec((1, D_MODEL), const),
            pl.BlockSpec((D_MODEL, LANES), const),
            pl.BlockSpec((1, LANES), const),
        ],
        out_specs=[
            pl.BlockSpec((tm, D_MODEL), tok),
            pl.BlockSpec((tm, D_MODEL), tok),
            pl.BlockSpec((TOP_K, tm), lambda i: (0, i)),
            pl.BlockSpec((TOP_K, tm), lambda i: (0, i)),
        ],
        out_shape=[
            jax.ShapeDtypeStruct((t, D_MODEL), F32),
            jax.ShapeDtypeStruct((t, D_MODEL), BF16),
            jax.ShapeDtypeStruct((TOP_K, t), jnp.int32),
            jax.ShapeDtypeStruct((TOP_K, t), F32),
        ],
        compiler_params=pltpu.CompilerParams(
            dimension_semantics=("arbitrary",), vmem_limit_bytes=_vmem_limit(48 << 20)),
        name="outproj_router",
    )(conv_out, na_out, x2, w_out.astype(BF16), b_out.reshape(1, D_MODEL), g_ffn.reshape(1, D_MODEL),
      jnp.pad(w_router.astype(BF16), ((0, 0), (0, LANES - N_EXPERTS))),
      jnp.pad(b_router.reshape(1, N_EXPERTS), ((0, 0), (0, LANES - N_EXPERTS))))


def _route_plan(idx_t, t):
    nt = t // ROUTE_TM
    experts = jnp.arange(N_EXPERTS, dtype=I32)
    onehot = idx_t.reshape(TOP_K, nt, ROUTE_TM, 1) == experts
    cnt = jnp.sum(onehot, axis=(0, 2), dtype=I32)
    sizes = jnp.sum(cnt, axis=0)
    padded = (sizes + ROUTE_RP + EXP_BM - 1) // EXP_BM * EXP_BM
    pad_ends = jnp.cumsum(padded)
    pad_off = pad_ends - padded
    tbase = pad_off[None, :] + jnp.cumsum(cnt, axis=0) - cnt
    cap = _sorted_rows(t)
    nb = cap // EXP_BM
    n_used = pad_ends[N_EXPERTS - 1] // EXP_BM
    blk_src = jnp.minimum(jnp.arange(nb, dtype=I32), n_used - 1)
    blk_exp = jnp.minimum(
        jnp.sum(blk_src[:, None] * EXP_BM >= pad_ends[None, :], axis=1, dtype=I32), N_EXPERTS - 1)
    row_end = jnp.sum(jnp.where(blk_exp[:, None] == experts[None, :], (pad_off + sizes)[None, :], 0), axis=1)
    blk_live = jnp.clip(row_end - blk_src * EXP_BM, 0, EXP_BM).astype(I32)
    multi = jnp.any(tbase % SUBLANES + cnt > ROUTE_RP, axis=1).astype(I32)
    zero_from = jnp.concatenate([(pad_off + sizes) // ZERO_ROWS * ZERO_ROWS, pad_ends[N_EXPERTS - 1:]])
    zero_to = jnp.concatenate([pad_ends, jnp.full((1,), cap, I32)])
    return dict(cnt=cnt.reshape(-1), tbase=tbase.reshape(-1).astype(I32), multi=multi,
                zero_start=zero_from.astype(I32), zero_cnt=((zero_to - zero_from) // ZERO_ROWS).astype(I32),
                blk_src=blk_src, blk_exp=blk_exp, blk_live=blk_live, n_used=n_used.reshape(1).astype(I32))


def _sorted_rows(t):
    cap = t * TOP_K + N_EXPERTS * (EXP_BM + ROUTE_RP)
    return (cap + EXP_BM - 1) // EXP_BM * EXP_BM


def _chunk_geometry(tb, n):
    head = tb % SUBLANES
    start = pl.multiple_of(tb - head, SUBLANES)
    nchunks = (head + n + ROUTE_RP - 1) // ROUTE_RP
    return start, head, nchunks


def _dispatch_kernel(tb_ref, cnt_ref, mt_ref, zs_ref, zc_ref, idx_ref, tbv_ref, xn_ref, xs_hbm, loc_ref,
                     stage, ostage, zbuf, head_ref, cs_ref, ms_ref, sem, osem, zsem):
    i = pl.program_id(0)
    nt = pl.num_programs(0)
    slot = i % 2
    tm = idx_ref.shape[1]
    groups = ROUTE_RP // SUBLANES

    def geometry(e, step=None):
        step = i if step is None else step
        return _chunk_geometry(tb_ref[step * N_EXPERTS + e], cnt_ref[step * N_EXPERTS + e])

    def chunk_copy(step, e):
        start, _, _ = geometry(e, step)
        return pltpu.make_async_copy(stage.at[step % 2, pl.ds(e * ROUTE_RP, ROUTE_RP)],
                                     xs_hbm.at[pl.ds(start, ROUTE_RP)], sem.at[step % 2])

    def all_chunks(step, phase):
        for e in range(N_EXPERTS):
            getattr(chunk_copy(step, e), phase)()

    @pl.when(i >= 2)
    def _():
        all_chunks(i - 2, "wait")

    @pl.when(i >= 1)
    def _():
        all_chunks(i - 1, "start")

    def zero_copy(start):
        return pltpu.make_async_copy(zbuf, xs_hbm.at[pl.ds(pl.multiple_of(start, ZERO_ROWS), ZERO_ROWS)], zsem)

    @pl.when(i == 0)
    def _():
        zbuf[...] = jnp.zeros(zbuf.shape, U32)
        head_ref[...] = jnp.zeros(head_ref.shape, U32)
        for phase in ("start", "wait"):
            for j in range(N_EXPERTS + 1):
                def piece(q, carry, j=j, phase=phase):
                    getattr(zero_copy(zs_ref[j] + q * ZERO_ROWS), phase)()
                    return carry
                lax.fori_loop(0, zc_ref[j], piece, 0)

    idx = idx_ref[...]
    eio = lax.broadcasted_iota(I32, (N_EXPERTS, tm), 0)
    member = jnp.zeros((N_EXPERTS, tm), F32)
    for k in range(TOP_K):
        member = member + (idx[k:k + 1, :] == eio).astype(F32)
    tri = (lax.broadcasted_iota(I32, (tm, tm), 0) < lax.broadcasted_iota(I32, (tm, tm), 1)).astype(BF16)
    rank = jnp.dot(member.astype(BF16), tri, preferred_element_type=F32)
    pos = rank + tbv_ref[0][:, 0:1].astype(F32)
    loc_ref[...] = jnp.concatenate(
        [jnp.sum(jnp.where(idx[k:k + 1, :] == eio, pos, 0.0), axis=0, keepdims=True) for k in range(TOP_K)],
        axis=0).astype(I32)
    cs_ref[...] = pos
    ms_ref[...] = member

    jio = lax.broadcasted_iota(I32, (ROUTE_RP, tm), 0).astype(F32)
    sel = [jnp.where((jio == pos[e:e + 1, :]) & (member[e:e + 1, :] > 0.0), 1.0, 0.0).astype(BF16)
           for e in range(N_EXPERTS)]
    rows = jnp.dot(jnp.concatenate(sel, axis=0), xn_ref[...], preferred_element_type=F32)

    stage[slot] = _pack_rows(rows)
    sub = lax.broadcasted_iota(I32, (SUBLANES, HALF), 0)
    for e in range(N_EXPERTS):
        _, head, _ = geometry(e)
        first = pl.ds(e * ROUTE_RP, SUBLANES)
        stage[slot, first, :] = jnp.where(sub < head, head_ref[e], stage[slot, first, :])
        g = jnp.minimum((head + cnt_ref[i * N_EXPERTS + e]) // SUBLANES, groups - 1)
        head_ref[e] = stage[slot, pl.ds(pl.multiple_of(e * ROUTE_RP + g * SUBLANES, SUBLANES), SUBLANES), :]

    def per_expert(e, carry):
        start, head, nchunks = geometry(e)
        end = head + cnt_ref[i * N_EXPERTS + e]

        def per_chunk(c, carry2):
            want = jio + (c * ROUTE_RP).astype(F32)
            pick = jnp.where((want == cs_ref[pl.ds(e, 1), :]) & (ms_ref[pl.ds(e, 1), :] > 0.0), 1.0, 0.0)
            ostage[...] = _pack_rows(jnp.dot(pick.astype(BF16), xn_ref[...], preferred_element_type=F32))
            cp = pltpu.make_async_copy(
                ostage, xs_hbm.at[pl.ds(pl.multiple_of(start + c * ROUTE_RP, SUBLANES), ROUTE_RP)], osem)
            cp.start()
            cp.wait()

            @pl.when(end // ROUTE_RP == c)
            def _():
                g = (end - c * ROUTE_RP) // SUBLANES
                head_ref[e] = ostage[pl.ds(pl.multiple_of(g * SUBLANES, SUBLANES), SUBLANES), :]
            return carry2
        return lax.fori_loop(1, nchunks, per_chunk, carry)

    @pl.when(mt_ref[i] > 0)
    def _():
        lax.fori_loop(0, N_EXPERTS, per_expert, 0)

    @pl.when(i == nt - 1)
    def _():
        @pl.when(i >= 1)
        def _():
            all_chunks(i - 1, "wait")
        all_chunks(i, "start")
        all_chunks(i, "wait")


def _dispatch(plan, idx_t, xn2):
    t = xn2.shape[0]
    tm = ROUTE_TM
    cap = _sorted_rows(t)
    grid_spec = pltpu.PrefetchScalarGridSpec(
        num_scalar_prefetch=5,
        grid=(t // tm,),
        in_specs=[
            pl.BlockSpec((TOP_K, tm), lambda i, *_: (0, i)),
            pl.BlockSpec((1, N_EXPERTS, LANES), lambda i, *_: (i, 0, 0)),
            pl.BlockSpec((tm, D_MODEL), lambda i, *_: (i, 0)),
        ],
        out_specs=[
            pl.BlockSpec(memory_space=pl.ANY),
            pl.BlockSpec((TOP_K, tm), lambda i, *_: (0, i)),
        ],
        scratch_shapes=[
            pltpu.VMEM((2, N_EXPERTS * ROUTE_RP, HALF), U32),
            pltpu.VMEM((ROUTE_RP, HALF), U32),
            pltpu.VMEM((ZERO_ROWS, HALF), U32),
            pltpu.VMEM((N_EXPERTS, SUBLANES, HALF), U32),
            pltpu.VMEM((N_EXPERTS, tm), F32),
            pltpu.VMEM((N_EXPERTS, tm), F32),
            pltpu.SemaphoreType.DMA((2,)),
            pltpu.SemaphoreType.DMA(()),
            pltpu.SemaphoreType.DMA(()),
        ],
    )
    return pl.pallas_call(
        _dispatch_kernel,
        grid_spec=grid_spec,
        out_shape=[
            jax.ShapeDtypeStruct((cap, HALF), U32),
            jax.ShapeDtypeStruct((TOP_K, t), I32),
        ],
        compiler_params=pltpu.CompilerParams(
            dimension_semantics=("arbitrary",), vmem_limit_bytes=_vmem_limit(32 << 20)),
        name="dispatch",
    )(plan["tbase"], plan["cnt"], plan["multi"], plan["zero_start"], plan["zero_cnt"], idx_t,
      jnp.broadcast_to((plan["tbase"] % SUBLANES).reshape(t // tm, N_EXPERTS, 1), (t // tm, N_EXPERTS, LANES)),
      xn2)


def _expert_kernel(be_ref, src_ref, nu_ref, live_ref, xs_ref, wg_hbm, wu_hbm, wd_hbm, bg_ref, bu_ref, bd_ref,
                   ys_ref, wf32, wbf, wsem):
    b = pl.program_id(0)
    e = be_ref[b]

    def weight_copies(expert, par):
        return [pltpu.make_async_copy(w.at[expert], wf32.at[par, m], wsem.at[par])
                for m, w in enumerate((wg_hbm, wu_hbm, wd_hbm))]

    @pl.when(b == 0)
    def _():
        for cp in weight_copies(e, e % 2):
            cp.start()

    @pl.when(b < nu_ref[0])
    def _():
        @pl.when((b == 0) | (e != be_ref[jnp.maximum(b - 1, 0)]))
        def _():
            par = e % 2
            for cp in weight_copies(e, par):
                cp.wait()

            @pl.when(e + 1 < N_EXPERTS)
            def _():
                for cp in weight_copies(e + 1, 1 - par):
                    cp.start()

            for m in range(3):
                wbf[m] = wf32[par, m].astype(BF16)

        def ffn(rows):
            x = _unpack_rows(xs_ref[rows, :])
            gt = jnp.minimum(jnp.dot(x, wbf[0], preferred_element_type=F32) + bg_ref[0], SWIGLU_LIMIT)
            up = jnp.clip(jnp.dot(x, wbf[1], preferred_element_type=F32) + bu_ref[0],
                          -SWIGLU_LIMIT, SWIGLU_LIMIT)
            hdn = (up + 1.0) * (gt * jax.nn.sigmoid(SWIGLU_ALPHA * gt))
            y = jnp.dot(hdn.astype(BF16), wbf[2], preferred_element_type=F32) + bd_ref[0]
            ys_ref[rows, :] = _pack_rows(y.astype(BF16).astype(F32))

        live = live_ref[b]
        for p, part in enumerate(EXP_PARTS):
            smaller = EXP_PARTS[p + 1] if p + 1 < len(EXP_PARTS) else -1
            cond = live > smaller
            if p > 0:
                cond = cond & (live <= part)

            @pl.when(cond)
            def _(part=part):
                ffn(slice(0, part))
                if part < EXP_BM:
                    ys_ref[part:, :] = jnp.zeros((EXP_BM - part, HALF), U32)

    @pl.when(b >= nu_ref[0])
    def _():
        ys_ref[...] = jnp.zeros(ys_ref.shape, U32)


def _experts(plan, xs, w_gate, b_gate, w_up, b_up, w_down, b_down):
    cap = xs.shape[0]
    nb = cap // EXP_BM
    rows = lambda b, be, src, nu, live: (src[b], 0)
    wsel = lambda b, be, src, nu, live: (be[b], 0, 0)
    wspec = pl.BlockSpec(memory_space=pl.ANY)
    bspec = pl.BlockSpec((1, 1, D_FF), wsel)
    grid_spec = pltpu.PrefetchScalarGridSpec(
        num_scalar_prefetch=4,
        grid=(nb,),
        in_specs=[pl.BlockSpec((EXP_BM, HALF), rows), wspec, wspec, wspec, bspec, bspec, bspec],
        out_specs=pl.BlockSpec((EXP_BM, HALF), lambda b, be, src, nu, live: (b, 0)),
        scratch_shapes=[pltpu.VMEM((2, 3, D_MODEL, D_FF), F32),
                        pltpu.VMEM((3, D_MODEL, D_FF), BF16),
                        pltpu.SemaphoreType.DMA((2,))],
    )
    return pl.pallas_call(
        _expert_kernel,
        grid_spec=grid_spec,
        out_shape=jax.ShapeDtypeStruct((cap, HALF), U32),
        compiler_params=pltpu.CompilerParams(
            dimension_semantics=("arbitrary",), vmem_limit_bytes=_vmem_limit(56 << 20)),
        name="experts",
    )(plan["blk_exp"], plan["blk_src"], plan["n_used"], plan["blk_live"], xs, w_gate, w_up, w_down,
      b_gate.reshape(N_EXPERTS, 1, D_FF), b_up.reshape(N_EXPERTS, 1, D_FF),
      b_down.reshape(N_EXPERTS, 1, D_MODEL))


def _combine_kernel(tb_ref, cnt_ref, mt_ref, h_ref, idx_ref, loc_ref, gate_ref, ys_hbm, g_ref, o_ref,
                    ybuf, obuf, acc_ref, sem, osem):
    i = pl.program_id(0)
    nt = pl.num_programs(0)
    slot = i % 2
    tm = h_ref.shape[0]

    def geometry(step, e):
        return _chunk_geometry(tb_ref[step * N_EXPERTS + e], cnt_ref[step * N_EXPERTS + e])

    def chunk_copy(step, e, s):
        start, _, _ = geometry(step, e)
        return pltpu.make_async_copy(ys_hbm.at[pl.ds(start, ROUTE_RP)],
                                     ybuf.at[s, pl.ds(e * ROUTE_RP, ROUTE_RP)], sem.at[s])

    @pl.when(i == 0)
    def _():
        for e in range(N_EXPERTS):
            chunk_copy(0, e, 0).start()

    @pl.when(i + 1 < nt)
    def _():
        for e in range(N_EXPERTS):
            chunk_copy(i + 1, e, 1 - slot).start()

    idx_loc = jnp.transpose(jnp.concatenate([idx_ref[...], loc_ref[...]], axis=0))
    idx, loc = idx_loc[:, :TOP_K], idx_loc[:, TOP_K:]
    gate = jnp.transpose(jnp.concatenate([gate_ref[...], gate_ref[...]], axis=0))[:, :TOP_K]

    def gate_matrix(ncols, col_of):
        out = []
        for rows in _row_blocks(tm):
            colio = lax.broadcasted_iota(I32, (rows.stop - rows.start, ncols), 1)
            g = jnp.zeros(colio.shape, F32)
            for k in range(TOP_K):
                g = jnp.where(colio == col_of[rows, k:k + 1], gate[rows, k:k + 1], g)
            out.append(g.astype(BF16))
        return jnp.concatenate(out, axis=0)

    for e in range(N_EXPERTS):
        chunk_copy(i, e, slot).wait()
    col = jnp.where(loc < ROUTE_RP, idx * ROUTE_RP + loc, -1)
    acc_ref[...] = jnp.dot(gate_matrix(N_EXPERTS * ROUTE_RP, col), _unpack_rows(ybuf[slot]),
                           preferred_element_type=F32)

    def per_expert(e, carry):
        start, _, nchunks = geometry(i, e)

        def per_chunk(c, carry2):
            cp = pltpu.make_async_copy(
                ys_hbm.at[pl.ds(pl.multiple_of(start + c * ROUTE_RP, SUBLANES), ROUTE_RP)], obuf, osem)
            cp.start()
            cp.wait()
            ccol = jnp.where(idx == e, loc - c * ROUTE_RP, -1)
            acc_ref[...] += jnp.dot(gate_matrix(ROUTE_RP, ccol), _unpack_rows(obuf[...]),
                                    preferred_element_type=F32)
            return carry2
        return lax.fori_loop(1, nchunks, per_chunk, carry)

    @pl.when(mt_ref[i] > 0)
    def _():
        lax.fori_loop(0, N_EXPERTS, per_expert, 0)

    out = h_ref[...] + acc_ref[...]
    ms = jnp.mean(out * out, axis=-1, keepdims=True)
    o_ref[...] = out * lax.rsqrt(ms + EPS) * g_ref[...]


def _combine(plan, h, ys, idx_t, loc_t, gate_t, g_final):
    t = h.shape[0]
    tm = ROUTE_TM
    tok = lambda i, *_: (i, 0)
    lane = lambda i, *_: (0, i)
    grid_spec = pltpu.PrefetchScalarGridSpec(
        num_scalar_prefetch=3,
        grid=(t // tm,),
        in_specs=[
            pl.BlockSpec((tm, D_MODEL), tok),
            pl.BlockSpec((TOP_K, tm), lane),
            pl.BlockSpec((TOP_K, tm), lane),
            pl.BlockSpec((TOP_K, tm), lane),
            pl.BlockSpec(memory_space=pl.ANY),
            pl.BlockSpec((1, D_MODEL), lambda i, *_: (0, 0)),
        ],
        out_specs=pl.BlockSpec((tm, D_MODEL), tok),
        scratch_shapes=[
            pltpu.VMEM((2, N_EXPERTS * ROUTE_RP, HALF), U32),
            pltpu.VMEM((ROUTE_RP, HALF), U32),
            pltpu.VMEM((tm, D_MODEL), F32),
            pltpu.SemaphoreType.DMA((2,)),
            pltpu.SemaphoreType.DMA(()),
        ],
    )
    return pl.pallas_call(
        _combine_kernel,
        grid_spec=grid_spec,
        out_shape=jax.ShapeDtypeStruct((t, D_MODEL), F32),
        compiler_params=pltpu.CompilerParams(
            dimension_semantics=("arbitrary",), vmem_limit_bytes=_vmem_limit(32 << 20)),
        name="combine",
    )(plan["tbase"], plan["cnt"], plan["multi"], h, idx_t, loc_t, gate_t, ys,
      g_final.reshape(1, D_MODEL))


def _moe(h, xn2, idx_t, gate_t, w_gate, b_gate, w_up, b_up, w_down, b_down, g_final):
    t = h.shape[0]
    plan = _route_plan(idx_t, t)
    xs, loc_t = _dispatch(plan, idx_t, xn2)
    ys = _experts(plan, xs, w_gate, b_gate, w_up, b_up, w_down, b_down)
    return _combine(plan, h, ys, idx_t, loc_t, gate_t, g_final)


def kernel(x, g_mix, w_in, b_in, conv_w, conv_b, ln_g, ln_b, rpb, w_out, b_out, g_ffn, w_router,
           b_router, w_gate, b_gate, w_up, b_up, w_down, b_down, g_final):
    bsz, seq, d = x.shape
    assert bsz == 1 and d == D_MODEL and g_mix.shape[0] == 1
    assert seq % (GRID_W * WIN_H) == 0
    x2 = x.reshape(seq, d)
    q, k, v, conv_out = _inproj_conv(x2, g_mix[0], w_in[0], b_in[0], conv_w[0], conv_b[0], ln_g[0], ln_b[0])
    na_out = _attention(q, k, v, rpb[0])
    h, xn2, idx_t, gate_t = _outproj_router(conv_out, na_out, x2, w_out[0], b_out[0], g_ffn[0],
                                            w_router[0], b_router[0])
    out = _moe(h, xn2, idx_t, gate_t, w_gate[0], b_gate[0], w_up[0], b_up[0], w_down[0], b_down[0],
               g_final)
    return out.reshape(bsz, seq, d)
```

```python
import functools

import numpy as np
import jax
import jax.numpy as jnp
from jax import lax
from jax.experimental import pallas as pl
from jax.experimental.pallas import tpu as pltpu

F32 = jnp.float32
BF16 = jnp.bfloat16
U32 = jnp.uint32
I32 = jnp.int32

D_MODEL = 1024
GRID_W = 64
D_CONV = 512
CONV_K = 31
NA_HEADS = 8
NA_HEAD_DIM = 64
D_NA = NA_HEADS * NA_HEAD_DIM
D_IN = 2 * D_CONV + 3 * D_NA
WIN_H = 8
WIN_W = 16
N_EXPERTS = 32
TOP_K = 4
D_FF = D_MODEL
SWIGLU_LIMIT = 7.0
SWIGLU_ALPHA = 1.702
EPS = 1e-5
MASK_VALUE = -1e30

V7X_VMEM_BYTES = 64 * 1024 * 1024
LANES = 128

PROJ_TM = 512
ROUTER_SUB = 256
CONV_HALO = 16
CONV_CH = 32
ATTN_RB = 32
ATTN_UNROLL = 8
ROUTE_TM = 256
SUBLANES = 8
ROUTE_RP = 56
EXP_BM = 512
EXP_PARTS = (512, 256, 128, 0)
ZERO_ROWS = 256
HALF = D_MODEL // 2
ROW_BLOCK = 16


def _vmem_limit(nbytes):
    return int(min(nbytes, V7X_VMEM_BYTES - 6 * 1024 * 1024))


def _row_blocks(n):
    return [slice(r, min(r + ROW_BLOCK, n)) for r in range(0, n, ROW_BLOCK)]


def _pack_rows(x):
    out = []
    for rows in _row_blocks(x.shape[0]):
        lo = lax.bitcast_convert_type(x[rows, :HALF], U32) >> 16
        hi = lax.bitcast_convert_type(x[rows, HALF:], U32) & jnp.uint32(0xFFFF0000)
        out.append(lo | hi)
    return jnp.concatenate(out, axis=0)


def _unpack_rows(w):
    out = []
    for rows in _row_blocks(w.shape[0]):
        lo = lax.bitcast_convert_type(w[rows] << 16, F32)
        hi = lax.bitcast_convert_type(w[rows] & jnp.uint32(0xFFFF0000), F32)
        out.append(jnp.concatenate([lo, hi], axis=1).astype(BF16))
    return jnp.concatenate(out, axis=0)


def _inproj_conv_kernel(x_ref, g_ref, w_ref, b_ref, cw_ref, cb_ref, lg_ref, lb_ref,
                        q_ref, k_ref, v_ref, o_ref, ext_ref, u_ref, sh_ref):
    i = pl.program_id(0)
    nt = pl.num_programs(0) - 1
    tm = x_ref.shape[0]

    @pl.when(i == 0)
    def _():
        ext_ref[...] = jnp.zeros(ext_ref.shape, F32)

    x = x_ref[...]
    ms = jnp.mean(x * x, axis=-1, keepdims=True)
    xn = (x * lax.rsqrt(ms + EPS) * g_ref[...]).astype(BF16)

    def proj(c):
        sl = slice(c * D_CONV, (c + 1) * D_CONV)
        return jnp.dot(xn, w_ref[:, sl], preferred_element_type=F32) + b_ref[:, sl]

    u_ref[...] = proj(0) * jax.nn.sigmoid(proj(1))
    q_ref[...] = (proj(2) * (NA_HEAD_DIM ** -0.5)).astype(BF16)
    k_ref[...] = proj(3).astype(BF16)
    v_ref[...] = proj(4).astype(BF16)

    ext_ref[CONV_HALO + tm:, :] = jnp.where(i < nt, u_ref[0:CONV_HALO, :], 0.0)
    base = CONV_HALO - CONV_K // 2
    span = (CONV_K - 1) // SUBLANES * SUBLANES
    for b in range(SUBLANES):
        sh_ref[b] = ext_ref[base + b:base + b + tm + span, :]
    for c in range(tm // CONV_CH):
        acc = jnp.zeros((CONV_CH, D_CONV), F32)
        for b in range(SUBLANES):
            for a in range((CONV_K - 1 - b) // SUBLANES + 1):
                r = c * CONV_CH + SUBLANES * a
                acc = acc + sh_ref[b, r:r + CONV_CH, :] * cw_ref[SUBLANES * a + b:SUBLANES * a + b + 1, :]
        acc = acc + cb_ref[...]
        mu = jnp.mean(acc, axis=-1, keepdims=True)
        d = acc - mu
        var = jnp.mean(d * d, axis=-1, keepdims=True)
        un = d * lax.rsqrt(var + EPS) * lg_ref[...] + lb_ref[...]
        o_ref[c * CONV_CH:(c + 1) * CONV_CH, :] = (un * jax.nn.sigmoid(un)).astype(o_ref.dtype)

    ext_ref[0:CONV_HALO, :] = ext_ref[tm:tm + CONV_HALO, :]
    ext_ref[CONV_HALO:CONV_HALO + tm, :] = u_ref[...]


def _inproj_conv(x2, g_mix, w_in, b_in, conv_w, conv_b, ln_g, ln_b):
    t = x2.shape[0]
    tm = min(PROJ_TM, t)
    nt = t // tm
    tok = lambda i: (jnp.minimum(i, nt - 1), 0)
    prev = lambda i: (jnp.maximum(i - 1, 0), 0)
    const = lambda i: (0, 0)
    w = jnp.zeros((CONV_K + 1, D_CONV), F32).at[:CONV_K].set(conv_w)
    span = (CONV_K - 1) // SUBLANES * SUBLANES
    return pl.pallas_call(
        _inproj_conv_kernel,
        grid=(nt + 1,),
        in_specs=[
            pl.BlockSpec((tm, D_MODEL), tok),
            pl.BlockSpec((1, D_MODEL), const),
            pl.BlockSpec((D_MODEL, D_IN), const),
            pl.BlockSpec((1, D_IN), const),
            pl.BlockSpec((CONV_K + 1, D_CONV), const),
            pl.BlockSpec((1, D_CONV), const),
            pl.BlockSpec((1, D_CONV), const),
            pl.BlockSpec((1, D_CONV), const),
        ],
        out_specs=[
            pl.BlockSpec((tm, D_NA), tok),
            pl.BlockSpec((tm, D_NA), tok),
            pl.BlockSpec((tm, D_NA), tok),
            pl.BlockSpec((tm, D_CONV), prev),
        ],
        out_shape=[
            jax.ShapeDtypeStruct((t, D_NA), BF16),
            jax.ShapeDtypeStruct((t, D_NA), BF16),
            jax.ShapeDtypeStruct((t, D_NA), BF16),
            jax.ShapeDtypeStruct((t, D_CONV), BF16),
        ],
        scratch_shapes=[pltpu.VMEM((tm + 2 * CONV_HALO, D_CONV), F32),
                        pltpu.VMEM((tm, D_CONV), F32),
                        pltpu.VMEM((SUBLANES, tm + span, D_CONV), F32)],
        compiler_params=pltpu.CompilerParams(
            dimension_semantics=("arbitrary",), vmem_limit_bytes=_vmem_limit(56 << 20)),
        name="inproj_conv",
    )(x2, g_mix.reshape(1, D_MODEL), w_in.astype(BF16), b_in.reshape(1, D_IN), w,
      conv_b.reshape(1, D_CONV), ln_g.reshape(1, D_CONV), ln_b.reshape(1, D_CONV))


def _column_bias(rpb):
    col = np.arange(GRID_W)
    c0 = np.clip(col - WIN_W // 2, 0, GRID_W - WIN_W)
    valid = (col[None, :] >= c0[:, None]) & (col[None, :] < c0[:, None] + WIN_W)
    dcol = np.clip(col[None, :] - col[:, None], -(WIN_W - 1), WIN_W - 1) + (WIN_W - 1)
    col_sel = ((dcol[:, :, None] == np.arange(2 * WIN_W - 1)) & valid[:, :, None]).astype(np.float32)
    b = jnp.einsum("hdj,qkj->hdqk", rpb.astype(F32), col_sel, precision=lax.Precision.HIGHEST)
    b = jnp.where(valid[None, None], b, MASK_VALUE)
    return jnp.concatenate([b, b], axis=-1)


def _attn_kernel(q_ref, k_ref, v_ref, cb_ref, o_ref, s_ref, bt_ref, *, rb, rows):
    jb = pl.program_id(1)
    lane = lax.broadcasted_iota(jnp.int32, (GRID_W, LANES), 1)
    first_head = lane < NA_HEAD_DIM
    nkeys = WIN_H * GRID_W

    @pl.when(jb == 0)
    def _():
        for h in range(2):
            for c in range(WIN_H):
                for rr in range(0, WIN_H, 2):
                    even = cb_ref[h, rr - c + WIN_H - 1]
                    odd = cb_ref[h, rr + 1 - c + WIN_H - 1]
                    bt_ref[h, c, :, rr * GRID_W:(rr + 2) * GRID_W] = jnp.where(first_head, even, odd)

    def window(r):
        i = jb * rb + r
        r0 = jnp.clip(i - WIN_H // 2, 0, rows - WIN_H)
        return i - r0, pl.multiple_of(r0 * GRID_W, GRID_W)

    def scores(r):
        _, start = window(r)
        q2 = q_ref[pl.ds(pl.multiple_of(r * GRID_W, GRID_W), GRID_W), :]
        zero = jnp.zeros_like(q2)
        qst = jnp.concatenate([jnp.where(first_head, q2, zero), jnp.where(first_head, zero, q2)], axis=0)
        ks = k_ref[pl.ds(start, nkeys), :]
        return lax.dot_general(qst, ks, (((1,), (1,)), ((), ())), preferred_element_type=F32)

    def finish(r, s):
        c, start = window(r)
        vs = v_ref[pl.ds(start, nkeys), :]
        bias = jnp.concatenate([bt_ref[0, c], bt_ref[1, c]], axis=0)
        s = jnp.where(bias > 0.5 * MASK_VALUE, s + bias, MASK_VALUE)
        m = jnp.max(s, axis=-1, keepdims=True)
        p = jnp.exp(s - m)
        l = jnp.sum(p, axis=-1, keepdims=True)
        o = jnp.dot(p.astype(BF16), vs, preferred_element_type=F32) / l
        o2 = jnp.where(first_head, o[:GRID_W], o[GRID_W:])
        o_ref[pl.ds(pl.multiple_of(r * GRID_W, GRID_W), GRID_W), :] = o2.astype(o_ref.dtype)

    s_ref[0] = scores(0)

    def pair(p, carry):
        r = 2 * p
        s_ref[1] = scores(r + 1)
        finish(r, s_ref[0])
        s_ref[0] = scores(jnp.minimum(r + 2, rb - 1))
        finish(r + 1, s_ref[1])
        return carry

    lax.fori_loop(0, rb // 2, pair, 0, unroll=ATTN_UNROLL)


def _attention(q, k, v, rpb):
    t = q.shape[0]
    rows = t // GRID_W
    rb = min(ATTN_RB, rows)
    npairs = D_NA // LANES
    cb = _column_bias(rpb)
    kern = functools.partial(_attn_kernel, rb=rb, rows=rows)
    return pl.pallas_call(
        kern,
        grid=(npairs, rows // rb),
        in_specs=[
            pl.BlockSpec((rb * GRID_W, LANES), lambda p, j: (j, p)),
            pl.BlockSpec((t, LANES), lambda p, j: (0, p)),
            pl.BlockSpec((t, LANES), lambda p, j: (0, p)),
            pl.BlockSpec((2, 2 * WIN_H - 1, GRID_W, 2 * GRID_W), lambda p, j: (p, 0, 0, 0)),
        ],
        out_specs=pl.BlockSpec((rb * GRID_W, LANES), lambda p, j: (j, p)),
        out_shape=jax.ShapeDtypeStruct((t, D_NA), BF16),
        scratch_shapes=[pltpu.VMEM((2, 2 * GRID_W, WIN_H * GRID_W), F32),
                        pltpu.VMEM((2, WIN_H, GRID_W, WIN_H * GRID_W), F32)],
        compiler_params=pltpu.CompilerParams(
            dimension_semantics=("arbitrary", "arbitrary"), vmem_limit_bytes=_vmem_limit(40 << 20)),
        name="attention",
    )(q, k, v, cb)


def _outproj_kernel(conv_ref, na_ref, x_ref, wo_ref, bo_ref, g_ref, wr_ref, br_ref,
                    h_ref, xn_ref, idx_ref, gate_ref):
    for s in range(x_ref.shape[0] // ROUTER_SUB):
        rows = slice(s * ROUTER_SUB, (s + 1) * ROUTER_SUB)
        mixed = jnp.concatenate([conv_ref[rows, :], na_ref[rows, :]], axis=-1)
        h = x_ref[rows, :] + jnp.dot(mixed, wo_ref[...], preferred_element_type=F32) + bo_ref[...]
        h_ref[rows, :] = h
        ms = jnp.mean(h * h, axis=-1, keepdims=True)
        xn = (h * lax.rsqrt(ms + EPS) * g_ref[...]).astype(BF16)
        xn_ref[rows, :] = xn
        logits = jnp.transpose(jnp.dot(xn, wr_ref[...], preferred_element_type=F32) + br_ref[...])[:N_EXPERTS]
        ids = lax.broadcasted_iota(jnp.int32, logits.shape, 0)
        vals, sels = [], []
        l = logits
        for _ in range(TOP_K):
            m = jnp.max(l, axis=0, keepdims=True)
            sel = jnp.min(jnp.where(l == m, ids, N_EXPERTS), axis=0, keepdims=True)
            vals.append(m)
            sels.append(sel)
            l = jnp.where(ids == sel, -jnp.inf, l)
        es = [jnp.exp(vk - vals[0]) for vk in vals]
        tot = es[0] + es[1] + es[2] + es[3]
        idx_ref[:, rows] = jnp.concatenate(sels, axis=0)
        gate_ref[:, rows] = jnp.concatenate([e / tot for e in es], axis=0)


def _outproj_router(conv_out, na_out, x2, w_out, b_out, g_ffn, w_router, b_router):
    t = x2.shape[0]
    tm = min(PROJ_TM, t)
    tok = lambda i: (i, 0)
    const = lambda i: (0, 0)
    return pl.pallas_call(
        _outproj_kernel,
        grid=(t // tm,),
        in_specs=[
            pl.BlockSpec((tm, D_CONV), tok),
            pl.BlockSpec((tm, D_NA), tok),
            pl.BlockSpec((tm, D_MODEL), tok),
            pl.BlockSpec((D_MODEL, D_MODEL), const),
            pl.BlockSpec((1, D_MODEL), const),
            pl.BlockSpec((1, D_MODEL), const),
            pl.BlockSpec((D_MODEL, LANES), const),
            pl.BlockSpec((1, LANES), const),
        ],
        out_specs=[
            pl.BlockSpec((tm, D_MODEL), tok),
            pl.BlockSpec((tm, D_MODEL), tok),
            pl.BlockSpec((TOP_K, tm), lambda i: (0, i)),
            pl.BlockSpec((TOP_K, tm), lambda i: (0, i)),
        ],
        out_shape=[
            jax.ShapeDtypeStruct((t, D_MODEL), F32),
            jax.ShapeDtypeStruct((t, D_MODEL), BF16),
            jax.ShapeDtypeStruct((TOP_K, t), jnp.int32),
            jax.ShapeDtypeStruct((TOP_K, t), F32),
        ],
        compiler_params=pltpu.CompilerParams(
            dimension_semantics=("arbitrary",), vmem_limit_bytes=_vmem_limit(48 << 20)),
        name="outproj_router",
    )(conv_out, na_out, x2, w_out.astype(BF16), b_out.reshape(1, D_MODEL), g_ffn.reshape(1, D_MODEL),
      jnp.pad(w_router.astype(BF16), ((0, 0), (0, LANES - N_EXPERTS))),
      jnp.pad(b_router.reshape(1, N_EXPERTS), ((0, 0), (0, LANES - N_EXPERTS))))


def _route_plan(idx_t, t):
    nt = t // ROUTE_TM
    experts = jnp.arange(N_EXPERTS, dtype=I32)
    onehot = idx_t.reshape(TOP_K, nt, ROUTE_TM, 1) == experts
    cnt = jnp.sum(onehot, axis=(0, 2), dtype=I32)
    sizes = jnp.sum(cnt, axis=0)
    padded = (sizes + ROUTE_RP + EXP_BM - 1) // EXP_BM * EXP_BM
    pad_ends = jnp.cumsum(padded)
    pad_off = pad_ends - padded
    tbase = pad_off[None, :] + jnp.cumsum(cnt, axis=0) - cnt
    cap = _sorted_rows(t)
    nb = cap // EXP_BM
    n_used = pad_ends[N_EXPERTS - 1] // EXP_BM
    blk_src = jnp.minimum(jnp.arange(nb, dtype=I32), n_used - 1)
    blk_exp = jnp.minimum(
        jnp.sum(blk_src[:, None] * EXP_BM >= pad_ends[None, :], axis=1, dtype=I32), N_EXPERTS - 1)
    row_end = jnp.sum(jnp.where(blk_exp[:, None] == experts[None, :], (pad_off + sizes)[None, :], 0), axis=1)
    blk_live = jnp.clip(row_end - blk_src * EXP_BM, 0, EXP_BM).astype(I32)
    multi = jnp.any(tbase % SUBLANES + cnt > ROUTE_RP, axis=1).astype(I32)
    zero_from = jnp.concatenate([(pad_off + sizes) // ZERO_ROWS * ZERO_ROWS, pad_ends[N_EXPERTS - 1:]])
    zero_to = jnp.concatenate([pad_ends, jnp.full((1,), cap, I32)])
    return dict(cnt=cnt.reshape(-1), tbase=tbase.reshape(-1).astype(I32), multi=multi,
                zero_start=zero_from.astype(I32), zero_cnt=((zero_to - zero_from) // ZERO_ROWS).astype(I32),
                blk_src=blk_src, blk_exp=blk_exp, blk_live=blk_live, n_used=n_used.reshape(1).astype(I32))


def _sorted_rows(t):
    cap = t * TOP_K + N_EXPERTS * (EXP_BM + ROUTE_RP)
    return (cap + EXP_BM - 1) // EXP_BM * EXP_BM


def _chunk_geometry(tb, n):
    head = tb & (SUBLANES - 1)
    start = pl.multiple_of(tb - head, SUBLANES)
    nchunks = (head + n + ROUTE_RP - 1) // ROUTE_RP
    return start, head, nchunks


def _dispatch_kernel(tb_ref, cnt_ref, mt_ref, zs_ref, zc_ref, idx_ref, tbv_ref, xn_ref, xs_hbm, loc_ref,
                     stage, ostage, zbuf, head_ref, cs_ref, ms_ref, sem, osem, zsem):
    i = pl.program_id(0)
    nt = pl.num_programs(0)
    slot = i % 2
    tm = idx_ref.shape[1]
    groups = ROUTE_RP // SUBLANES

    def geometry(e, step=None):
        step = i if step is None else step
        return _chunk_geometry(tb_ref[step * N_EXPERTS + e], cnt_ref[step * N_EXPERTS + e])

    def chunk_copy(step, e):
        start, _, _ = geometry(e, step)
        return pltpu.make_async_copy(stage.at[step % 2, pl.ds(e * ROUTE_RP, ROUTE_RP)],
                                     xs_hbm.at[pl.ds(start, ROUTE_RP)], sem.at[step % 2])

    def all_chunks(step, phase):
        for e in range(N_EXPERTS):
            getattr(chunk_copy(step, e), phase)()

    @pl.when(i >= 2)
    def _():
        all_chunks(i - 2, "wait")

    @pl.when(i >= 1)
    def _():
        all_chunks(i - 1, "start")

    def zero_copy(start):
        return pltpu.make_async_copy(zbuf, xs_hbm.at[pl.ds(pl.multiple_of(start, ZERO_ROWS), ZERO_ROWS)], zsem)

    @pl.when(i == 0)
    def _():
        zbuf[...] = jnp.zeros(zbuf.shape, U32)
        head_ref[...] = jnp.zeros(head_ref.shape, U32)
        for phase in ("start", "wait"):
            for j in range(N_EXPERTS + 1):
                def piece(q, carry, j=j, phase=phase):
                    getattr(zero_copy(zs_ref[j] + q * ZERO_ROWS), phase)()
                    return carry
                lax.fori_loop(0, zc_ref[j], piece, 0)

    idx = idx_ref[...]
    eio = lax.broadcasted_iota(I32, (N_EXPERTS, tm), 0)
    member = jnp.zeros((N_EXPERTS, tm), F32)
    for k in range(TOP_K):
        member = member + (idx[k:k + 1, :] == eio).astype(F32)
    tri = (lax.broadcasted_iota(I32, (tm, tm), 0) < lax.broadcasted_iota(I32, (tm, tm), 1)).astype(BF16)
    rank = jnp.dot(member.astype(BF16), tri, preferred_element_type=F32)
    pos = rank + tbv_ref[0][:, 0:1].astype(F32)
    loc_ref[...] = jnp.concatenate(
        [jnp.sum(jnp.where(idx[k:k + 1, :] == eio, pos, 0.0), axis=0, keepdims=True) for k in range(TOP_K)],
        axis=0).astype(I32)
    cs_ref[...] = pos
    ms_ref[...] = member

    jio = lax.broadcasted_iota(I32, (ROUTE_RP, tm), 0).astype(F32)
    sel = [jnp.where((jio == pos[e:e + 1, :]) & (member[e:e + 1, :] > 0.0), 1.0, 0.0).astype(BF16)
           for e in range(N_EXPERTS)]
    rows = jnp.dot(jnp.concatenate(sel, axis=0), xn_ref[...], preferred_element_type=F32)

    stage[slot] = _pack_rows(rows)
    sub = lax.broadcasted_iota(I32, (SUBLANES, HALF), 0)
    for e in range(N_EXPERTS):
        _, head, _ = geometry(e)
        first = pl.ds(e * ROUTE_RP, SUBLANES)
        stage[slot, first, :] = jnp.where(sub < head, head_ref[e], stage[slot, first, :])
        g = jnp.minimum(lax.shift_right_logical(head + cnt_ref[i * N_EXPERTS + e], SUBLANES.bit_length() - 1),
                        groups - 1)
        head_ref[e] = stage[slot, pl.ds(pl.multiple_of(e * ROUTE_RP + g * SUBLANES, SUBLANES), SUBLANES), :]

    def per_expert(e, carry):
        start, head, nchunks = geometry(e)
        end = head + cnt_ref[i * N_EXPERTS + e]

        def per_chunk(c, carry2):
            want = jio + (c * ROUTE_RP).astype(F32)
            pick = jnp.where((want == cs_ref[pl.ds(e, 1), :]) & (ms_ref[pl.ds(e, 1), :] > 0.0), 1.0, 0.0)
            ostage[...] = _pack_rows(jnp.dot(pick.astype(BF16), xn_ref[...], preferred_element_type=F32))
            cp = pltpu.make_async_copy(
                ostage, xs_hbm.at[pl.ds(pl.multiple_of(start + c * ROUTE_RP, SUBLANES), ROUTE_RP)], osem)
            cp.start()
            cp.wait()

            @pl.when(end // ROUTE_RP == c)
            def _():
                g = (end - c * ROUTE_RP) // SUBLANES
                head_ref[e] = ostage[pl.ds(pl.multiple_of(g * SUBLANES, SUBLANES), SUBLANES), :]
            return carry2
        return lax.fori_loop(1, nchunks, per_chunk, carry)

    @pl.when(mt_ref[i] > 0)
    def _():
        lax.fori_loop(0, N_EXPERTS, per_expert, 0)

    @pl.when(i == nt - 1)
    def _():
        @pl.when(i >= 1)
        def _():
            all_chunks(i - 1, "wait")
        all_chunks(i, "start")
        all_chunks(i, "wait")


def _dispatch(plan, idx_t, xn2):
    t = xn2.shape[0]
    tm = ROUTE_TM
    cap = _sorted_rows(t)
    grid_spec = pltpu.PrefetchScalarGridSpec(
        num_scalar_prefetch=5,
        grid=(t // tm,),
        in_specs=[
            pl.BlockSpec((TOP_K, tm), lambda i, *_: (0, i)),
            pl.BlockSpec((1, N_EXPERTS, LANES), lambda i, *_: (i, 0, 0)),
            pl.BlockSpec((tm, D_MODEL), lambda i, *_: (i, 0)),
        ],
        out_specs=[
            pl.BlockSpec(memory_space=pl.ANY),
            pl.BlockSpec((TOP_K, tm), lambda i, *_: (0, i)),
        ],
        scratch_shapes=[
            pltpu.VMEM((2, N_EXPERTS * ROUTE_RP, HALF), U32),
            pltpu.VMEM((ROUTE_RP, HALF), U32),
            pltpu.VMEM((ZERO_ROWS, HALF), U32),
            pltpu.VMEM((N_EXPERTS, SUBLANES, HALF), U32),
            pltpu.VMEM((N_EXPERTS, tm), F32),
            pltpu.VMEM((N_EXPERTS, tm), F32),
            pltpu.SemaphoreType.DMA((2,)),
            pltpu.SemaphoreType.DMA(()),
            pltpu.SemaphoreType.DMA(()),
        ],
    )
    return pl.pallas_call(
        _dispatch_kernel,
        grid_spec=grid_spec,
        out_shape=[
            jax.ShapeDtypeStruct((cap, HALF), U32),
            jax.ShapeDtypeStruct((TOP_K, t), I32),
        ],
        compiler_params=pltpu.CompilerParams(
            dimension_semantics=("arbitrary",), vmem_limit_bytes=_vmem_limit(32 << 20)),
        name="dispatch",
    )(plan["tbase"], plan["cnt"], plan["multi"], plan["zero_start"], plan["zero_cnt"], idx_t,
      jnp.broadcast_to((plan["tbase"] % SUBLANES).reshape(t // tm, N_EXPERTS, 1), (t // tm, N_EXPERTS, LANES)),
      xn2)


def _expert_kernel(be_ref, src_ref, nu_ref, live_ref, xs_ref, wg_hbm, wu_hbm, wd_hbm, bg_ref, bu_ref, bd_ref,
                   ys_ref, wf32, wbf, wsem):
    b = pl.program_id(0)
    e = be_ref[b]

    def weight_copies(expert, par):
        return [pltpu.make_async_copy(w.at[expert], wf32.at[par, m], wsem.at[par])
                for m, w in enumerate((wg_hbm, wu_hbm, wd_hbm))]

    @pl.when(b == 0)
    def _():
        for cp in weight_copies(e, e % 2):
            cp.start()

    @pl.when(b < nu_ref[0])
    def _():
        @pl.when((b == 0) | (e != be_ref[jnp.maximum(b - 1, 0)]))
        def _():
            par = e % 2
            for cp in weight_copies(e, par):
                cp.wait()

            @pl.when(e + 1 < N_EXPERTS)
            def _():
                for cp in weight_copies(e + 1, 1 - par):
                    cp.start()

            for m in range(3):
                wbf[m] = wf32[par, m].astype(BF16)

        def ffn(rows):
            x = _unpack_rows(xs_ref[rows, :])
            gt = jnp.minimum(jnp.dot(x, wbf[0], preferred_element_type=F32) + bg_ref[0], SWIGLU_LIMIT)
            up = jnp.clip(jnp.dot(x, wbf[1], preferred_element_type=F32) + bu_ref[0],
                          -SWIGLU_LIMIT, SWIGLU_LIMIT)
            hdn = (up + 1.0) * (gt * jax.nn.sigmoid(SWIGLU_ALPHA * gt))
            y = jnp.dot(hdn.astype(BF16), wbf[2], preferred_element_type=F32) + bd_ref[0]
            ys_ref[rows, :] = _pack_rows(y.astype(BF16).astype(F32))

        live = live_ref[b]
        for p, part in enumerate(EXP_PARTS):
            smaller = EXP_PARTS[p + 1] if p + 1 < len(EXP_PARTS) else -1
            cond = live > smaller
            if p > 0:
                cond = cond & (live <= part)

            @pl.when(cond)
            def _(part=part):
                if part > 0:
                    ffn(slice(0, part))
                if part < EXP_BM:
                    ys_ref[part:, :] = jnp.zeros((EXP_BM - part, HALF), U32)

    @pl.when(b >= nu_ref[0])
    def _():
        ys_ref[...] = jnp.zeros(ys_ref.shape, U32)


def _experts(plan, xs, w_gate, b_gate, w_up, b_up, w_down, b_down):
    cap = xs.shape[0]
    nb = cap // EXP_BM
    rows = lambda b, be, src, nu, live: (src[b], 0)
    wsel = lambda b, be, src, nu, live: (be[b], 0, 0)
    wspec = pl.BlockSpec(memory_space=pl.ANY)
    bspec = pl.BlockSpec((1, 1, D_FF), wsel)
    grid_spec = pltpu.PrefetchScalarGridSpec(
        num_scalar_prefetch=4,
        grid=(nb,),
        in_specs=[pl.BlockSpec((EXP_BM, HALF), rows), wspec, wspec, wspec, bspec, bspec, bspec],
        out_specs=pl.BlockSpec((EXP_BM, HALF), lambda b, be, src, nu, live: (b, 0)),
        scratch_shapes=[pltpu.VMEM((2, 3, D_MODEL, D_FF), F32),
                        pltpu.VMEM((3, D_MODEL, D_FF), BF16),
                        pltpu.SemaphoreType.DMA((2,))],
    )
    return pl.pallas_call(
        _expert_kernel,
        grid_spec=grid_spec,
        out_shape=jax.ShapeDtypeStruct((cap, HALF), U32),
        compiler_params=pltpu.CompilerParams(
            dimension_semantics=("arbitrary",), vmem_limit_bytes=_vmem_limit(56 << 20)),
        name="experts",
    )(plan["blk_exp"], plan["blk_src"], plan["n_used"], plan["blk_live"], xs, w_gate, w_up, w_down,
      b_gate.reshape(N_EXPERTS, 1, D_FF), b_up.reshape(N_EXPERTS, 1, D_FF),
      b_down.reshape(N_EXPERTS, 1, D_MODEL))


def _combine_kernel(tb_ref, cnt_ref, mt_ref, h_ref, idx_ref, loc_ref, gate_ref, ys_hbm, g_ref, o_ref,
                    ybuf, obuf, acc_ref, sem, osem):
    i = pl.program_id(0)
    nt = pl.num_programs(0)
    slot = i % 2
    tm = h_ref.shape[0]

    def geometry(step, e):
        return _chunk_geometry(tb_ref[step * N_EXPERTS + e], cnt_ref[step * N_EXPERTS + e])

    def chunk_copy(step, e, s):
        start, _, _ = geometry(step, e)
        return pltpu.make_async_copy(ys_hbm.at[pl.ds(start, ROUTE_RP)],
                                     ybuf.at[s, pl.ds(e * ROUTE_RP, ROUTE_RP)], sem.at[s])

    @pl.when(i == 0)
    def _():
        for e in range(N_EXPERTS):
            chunk_copy(0, e, 0).start()

    @pl.when(i + 1 < nt)
    def _():
        for e in range(N_EXPERTS):
            chunk_copy(i + 1, e, 1 - slot).start()

    idx_loc = jnp.transpose(jnp.concatenate([idx_ref[...], loc_ref[...]], axis=0))
    idx, loc = idx_loc[:, :TOP_K], idx_loc[:, TOP_K:]
    gate = jnp.transpose(jnp.concatenate([gate_ref[...], gate_ref[...]], axis=0))[:, :TOP_K]

    def gate_matrix(ncols, col_of):
        out = []
        for rows in _row_blocks(tm):
            colio = lax.broadcasted_iota(I32, (rows.stop - rows.start, ncols), 1)
            g = jnp.zeros(colio.shape, F32)
            for k in range(TOP_K):
                g = jnp.where(colio == col_of[rows, k:k + 1], gate[rows, k:k + 1], g)
            out.append(g.astype(BF16))
        return jnp.concatenate(out, axis=0)

    for e in range(N_EXPERTS):
        chunk_copy(i, e, slot).wait()
    col = jnp.where(loc < ROUTE_RP, idx * ROUTE_RP + loc, -1)
    acc_ref[...] = jnp.dot(gate_matrix(N_EXPERTS * ROUTE_RP, col), _unpack_rows(ybuf[slot]),
                           preferred_element_type=F32)

    def per_expert(e, carry):
        start, _, nchunks = geometry(i, e)

        def per_chunk(c, carry2):
            cp = pltpu.make_async_copy(
                ys_hbm.at[pl.ds(pl.multiple_of(start + c * ROUTE_RP, SUBLANES), ROUTE_RP)], obuf, osem)
            cp.start()
            cp.wait()
            ccol = jnp.where(idx == e, loc - c * ROUTE_RP, -1)
            acc_ref[...] += jnp.dot(gate_matrix(ROUTE_RP, ccol), _unpack_rows(obuf[...]),
                                    preferred_element_type=F32)
            return carry2
        return lax.fori_loop(1, nchunks, per_chunk, carry)

    @pl.when(mt_ref[i] > 0)
    def _():
        lax.fori_loop(0, N_EXPERTS, per_expert, 0)

    out = h_ref[...] + acc_ref[...]
    ms = jnp.mean(out * out, axis=-1, keepdims=True)
    o_ref[...] = out * lax.rsqrt(ms + EPS) * g_ref[...]


def _combine(plan, h, ys, idx_t, loc_t, gate_t, g_final):
    t = h.shape[0]
    tm = ROUTE_TM
    tok = lambda i, *_: (i, 0)
    lane = lambda i, *_: (0, i)
    grid_spec = pltpu.PrefetchScalarGridSpec(
        num_scalar_prefetch=3,
        grid=(t // tm,),
        in_specs=[
            pl.BlockSpec((tm, D_MODEL), tok),
            pl.BlockSpec((TOP_K, tm), lane),
            pl.BlockSpec((TOP_K, tm), lane),
            pl.BlockSpec((TOP_K, tm), lane),
            pl.BlockSpec(memory_space=pl.ANY),
            pl.BlockSpec((1, D_MODEL), lambda i, *_: (0, 0)),
        ],
        out_specs=pl.BlockSpec((tm, D_MODEL), tok),
        scratch_shapes=[
            pltpu.VMEM((2, N_EXPERTS * ROUTE_RP, HALF), U32),
            pltpu.VMEM((ROUTE_RP, HALF), U32),
            pltpu.VMEM((tm, D_MODEL), F32),
            pltpu.SemaphoreType.DMA((2,)),
            pltpu.SemaphoreType.DMA(()),
        ],
    )
    return pl.pallas_call(
        _combine_kernel,
        grid_spec=grid_spec,
        out_shape=jax.ShapeDtypeStruct((t, D_MODEL), F32),
        compiler_params=pltpu.CompilerParams(
            dimension_semantics=("arbitrary",), vmem_limit_bytes=_vmem_limit(32 << 20)),
        name="combine",
    )(plan["tbase"], plan["cnt"], plan["multi"], h, idx_t, loc_t, gate_t, ys,
      g_final.reshape(1, D_MODEL))


def _moe(h, xn2, idx_t, gate_t, w_gate, b_gate, w_up, b_up, w_down, b_down, g_final):
    t = h.shape[0]
    plan = _route_plan(idx_t, t)
    xs, loc_t = _dispatch(plan, idx_t, xn2)
    ys = _experts(plan, xs, w_gate, b_gate, w_up, b_up, w_down, b_down)
    return _combine(plan, h, ys, idx_t, loc_t, gate_t, g_final)


def kernel(x, g_mix, w_in, b_in, conv_w, conv_b, ln_g, ln_b, rpb, w_out, b_out, g_ffn, w_router,
           b_router, w_gate, b_gate, w_up, b_up, w_down, b_down, g_final):
    bsz, seq, d = x.shape
    assert bsz == 1 and d == D_MODEL and g_mix.shape[0] == 1
    assert seq % (GRID_W * WIN_H) == 0
    x2 = x.reshape(seq, d)
    q, k, v, conv_out = _inproj_conv(x2, g_mix[0], w_in[0], b_in[0], conv_w[0], conv_b[0], ln_g[0], ln_b[0])
    na_out = _attention(q, k, v, rpb[0])
    h, xn2, idx_t, gate_t = _outproj_router(conv_out, na_out, x2, w_out[0], b_out[0], g_ffn[0],
                                            w_router[0], b_router[0])
    out = _moe(h, xn2, idx_t, gate_t, w_gate[0], b_gate[0], w_up[0], b_up[0], w_down[0], b_down[0],
               g_final)
    return out.reshape(bsz, seq, d)
```

```python
import functools

import numpy as np
import jax
import jax.numpy as jnp
from jax import lax
from jax.experimental import pallas as pl
from jax.experimental.pallas import tpu as pltpu

F32 = jnp.float32
BF16 = jnp.bfloat16
U32 = jnp.uint32
I32 = jnp.int32

D_MODEL = 1024
GRID_W = 64
D_CONV = 512
CONV_K = 31
NA_HEADS = 8
NA_HEAD_DIM = 64
D_NA = NA_HEADS * NA_HEAD_DIM
D_IN = 2 * D_CONV + 3 * D_NA
WIN_H = 8
WIN_W = 16
N_EXPERTS = 32
TOP_K = 4
D_FF = D_MODEL
SWIGLU_LIMIT = 7.0
SWIGLU_ALPHA = 1.702
EPS = 1e-5
MASK_VALUE = -1e30

V7X_VMEM_BYTES = 64 * 1024 * 1024
LANES = 128

PROJ_TM = 512
ROUTER_SUB = 256
CONV_HALO = 16
CONV_CH = 32
ATTN_RB = 32
ATTN_UNROLL = 8
ROUTE_TM = 256
SUBLANES = 8
ROUTE_RP = 56
EXP_BM = 512
EXP_PARTS = (512, 256, 128, 0)
ZERO_ROWS = 256
HALF = D_MODEL // 2
ROW_BLOCK = 16


def _vmem_limit(nbytes):
    return int(min(nbytes, V7X_VMEM_BYTES - 6 * 1024 * 1024))


def _row_blocks(n):
    return [slice(r, min(r + ROW_BLOCK, n)) for r in range(0, n, ROW_BLOCK)]


def _pack_rows(x):
    out = []
    for rows in _row_blocks(x.shape[0]):
        lo = lax.bitcast_convert_type(x[rows, :HALF], U32) >> 16
        hi = lax.bitcast_convert_type(x[rows, HALF:], U32) & jnp.uint32(0xFFFF0000)
        out.append(lo | hi)
    return jnp.concatenate(out, axis=0)


def _unpack_rows(w):
    out = []
    for rows in _row_blocks(w.shape[0]):
        lo = lax.bitcast_convert_type(w[rows] << 16, F32)
        hi = lax.bitcast_convert_type(w[rows] & jnp.uint32(0xFFFF0000), F32)
        out.append(jnp.concatenate([lo, hi], axis=1).astype(BF16))
    return jnp.concatenate(out, axis=0)


def _inproj_conv_kernel(x_ref, g_ref, w_ref, b_ref, cw_ref, cb_ref, lg_ref, lb_ref,
                        q_ref, k_ref, v_ref, o_ref, ext_ref, u_ref, sh_ref):
    i = pl.program_id(0)
    nt = pl.num_programs(0) - 1
    tm = x_ref.shape[0]

    @pl.when(i == 0)
    def _():
        ext_ref[...] = jnp.zeros(ext_ref.shape, F32)

    x = x_ref[...]
    ms = jnp.mean(x * x, axis=-1, keepdims=True)
    xn = (x * lax.rsqrt(ms + EPS) * g_ref[...]).astype(BF16)

    def proj(c):
        sl = slice(c * D_CONV, (c + 1) * D_CONV)
        return jnp.dot(xn, w_ref[:, sl], preferred_element_type=F32) + b_ref[:, sl]

    u_ref[...] = proj(0) * jax.nn.sigmoid(proj(1))
    q_ref[...] = (proj(2) * (NA_HEAD_DIM ** -0.5)).astype(BF16)
    k_ref[...] = proj(3).astype(BF16)
    v_ref[...] = proj(4).astype(BF16)

    ext_ref[CONV_HALO + tm:, :] = jnp.where(i < nt, u_ref[0:CONV_HALO, :], 0.0)
    base = CONV_HALO - CONV_K // 2
    span = (CONV_K - 1) // SUBLANES * SUBLANES
    for b in range(SUBLANES):
        sh_ref[b] = ext_ref[base + b:base + b + tm + span, :]
    for c in range(tm // CONV_CH):
        acc = jnp.zeros((CONV_CH, D_CONV), F32)
        for b in range(SUBLANES):
            for a in range((CONV_K - 1 - b) // SUBLANES + 1):
                r = c * CONV_CH + SUBLANES * a
                acc = acc + sh_ref[b, r:r + CONV_CH, :] * cw_ref[SUBLANES * a + b:SUBLANES * a + b + 1, :]
        acc = acc + cb_ref[...]
        mu = jnp.mean(acc, axis=-1, keepdims=True)
        d = acc - mu
        var = jnp.mean(d * d, axis=-1, keepdims=True)
        un = d * lax.rsqrt(var + EPS) * lg_ref[...] + lb_ref[...]
        o_ref[c * CONV_CH:(c + 1) * CONV_CH, :] = (un * jax.nn.sigmoid(un)).astype(o_ref.dtype)

    ext_ref[0:CONV_HALO, :] = ext_ref[tm:tm + CONV_HALO, :]
    ext_ref[CONV_HALO:CONV_HALO + tm, :] = u_ref[...]


def _inproj_conv(x2, g_mix, w_in, b_in, conv_w, conv_b, ln_g, ln_b):
    t = x2.shape[0]
    tm = min(PROJ_TM, t)
    nt = t // tm
    tok = lambda i: (jnp.minimum(i, nt - 1), 0)
    prev = lambda i: (jnp.maximum(i - 1, 0), 0)
    const = lambda i: (0, 0)
    w = jnp.zeros((CONV_K + 1, D_CONV), F32).at[:CONV_K].set(conv_w)
    span = (CONV_K - 1) // SUBLANES * SUBLANES
    return pl.pallas_call(
        _inproj_conv_kernel,
        grid=(nt + 1,),
        in_specs=[
            pl.BlockSpec((tm, D_MODEL), tok),
            pl.BlockSpec((1, D_MODEL), const),
            pl.BlockSpec((D_MODEL, D_IN), const),
            pl.BlockSpec((1, D_IN), const),
            pl.BlockSpec((CONV_K + 1, D_CONV), const),
            pl.BlockSpec((1, D_CONV), const),
            pl.BlockSpec((1, D_CONV), const),
            pl.BlockSpec((1, D_CONV), const),
        ],
        out_specs=[
            pl.BlockSpec((tm, D_NA), tok),
            pl.BlockSpec((tm, D_NA), tok),
            pl.BlockSpec((tm, D_NA), tok),
            pl.BlockSpec((tm, D_CONV), prev),
        ],
        out_shape=[
            jax.ShapeDtypeStruct((t, D_NA), BF16),
            jax.ShapeDtypeStruct((t, D_NA), BF16),
            jax.ShapeDtypeStruct((t, D_NA), BF16),
            jax.ShapeDtypeStruct((t, D_CONV), BF16),
        ],
        scratch_shapes=[pltpu.VMEM((tm + 2 * CONV_HALO, D_CONV), F32),
                        pltpu.VMEM((tm, D_CONV), F32),
                        pltpu.VMEM((SUBLANES, tm + span, D_CONV), F32)],
        compiler_params=pltpu.CompilerParams(
            dimension_semantics=("arbitrary",), vmem_limit_bytes=_vmem_limit(56 << 20)),
        name="inproj_conv",
    )(x2, g_mix.reshape(1, D_MODEL), w_in.astype(BF16), b_in.reshape(1, D_IN), w,
      conv_b.reshape(1, D_CONV), ln_g.reshape(1, D_CONV), ln_b.reshape(1, D_CONV))


def _column_bias(rpb):
    col = np.arange(GRID_W)
    c0 = np.clip(col - WIN_W // 2, 0, GRID_W - WIN_W)
    valid = (col[None, :] >= c0[:, None]) & (col[None, :] < c0[:, None] + WIN_W)
    dcol = np.clip(col[None, :] - col[:, None], -(WIN_W - 1), WIN_W - 1) + (WIN_W - 1)
    col_sel = ((dcol[:, :, None] == np.arange(2 * WIN_W - 1)) & valid[:, :, None]).astype(np.float32)
    b = jnp.einsum("hdj,qkj->hdqk", rpb.astype(F32), col_sel, precision=lax.Precision.HIGHEST)
    b = jnp.where(valid[None, None], b, MASK_VALUE)
    return jnp.concatenate([b, b], axis=-1)


def _attn_kernel(q_ref, k_ref, v_ref, cb_ref, o_ref, s_ref, bt_ref, *, rb, rows):
    jb = pl.program_id(1)
    lane = lax.broadcasted_iota(jnp.int32, (GRID_W, LANES), 1)
    first_head = lane < NA_HEAD_DIM
    nkeys = WIN_H * GRID_W

    @pl.when(jb == 0)
    def _():
        for h in range(2):
            for c in range(WIN_H):
                for rr in range(0, WIN_H, 2):
                    even = cb_ref[h, rr - c + WIN_H - 1]
                    odd = cb_ref[h, rr + 1 - c + WIN_H - 1]
                    bt_ref[h, c, :, rr * GRID_W:(rr + 2) * GRID_W] = jnp.where(first_head, even, odd)

    def window(r):
        i = jb * rb + r
        r0 = jnp.clip(i - WIN_H // 2, 0, rows - WIN_H)
        return i - r0, pl.multiple_of(r0 * GRID_W, GRID_W)

    def scores(r):
        _, start = window(r)
        q2 = q_ref[pl.ds(pl.multiple_of(r * GRID_W, GRID_W), GRID_W), :]
        zero = jnp.zeros_like(q2)
        qst = jnp.concatenate([jnp.where(first_head, q2, zero), jnp.where(first_head, zero, q2)], axis=0)
        ks = k_ref[pl.ds(start, nkeys), :]
        return lax.dot_general(qst, ks, (((1,), (1,)), ((), ())), preferred_element_type=F32)

    def finish(r, s):
        c, start = window(r)
        vs = v_ref[pl.ds(start, nkeys), :]
        bias = jnp.concatenate([bt_ref[0, c], bt_ref[1, c]], axis=0)
        s = jnp.where(bias > 0.5 * MASK_VALUE, s + bias, MASK_VALUE)
        m = jnp.max(s, axis=-1, keepdims=True)
        p = jnp.exp(s - m)
        l = jnp.sum(p, axis=-1, keepdims=True)
        o = jnp.dot(p.astype(BF16), vs, preferred_element_type=F32) / l
        o2 = jnp.where(first_head, o[:GRID_W], o[GRID_W:])
        o_ref[pl.ds(pl.multiple_of(r * GRID_W, GRID_W), GRID_W), :] = o2.astype(o_ref.dtype)

    s_ref[0] = scores(0)

    def pair(p, carry):
        r = 2 * p
        s_ref[1] = scores(r + 1)
        finish(r, s_ref[0])
        s_ref[0] = scores(jnp.minimum(r + 2, rb - 1))
        finish(r + 1, s_ref[1])
        return carry

    lax.fori_loop(0, rb // 2, pair, 0, unroll=ATTN_UNROLL)


def _attention(q, k, v, rpb):
    t = q.shape[0]
    rows = t // GRID_W
    rb = min(ATTN_RB, rows)
    npairs = D_NA // LANES
    cb = _column_bias(rpb)
    kern = functools.partial(_attn_kernel, rb=rb, rows=rows)
    return pl.pallas_call(
        kern,
        grid=(npairs, rows // rb),
        in_specs=[
            pl.BlockSpec((rb * GRID_W, LANES), lambda p, j: (j, p)),
            pl.BlockSpec((t, LANES), lambda p, j: (0, p)),
            pl.BlockSpec((t, LANES), lambda p, j: (0, p)),
            pl.BlockSpec((2, 2 * WIN_H - 1, GRID_W, 2 * GRID_W), lambda p, j: (p, 0, 0, 0)),
        ],
        out_specs=pl.BlockSpec((rb * GRID_W, LANES), lambda p, j: (j, p)),
        out_shape=jax.ShapeDtypeStruct((t, D_NA), BF16),
        scratch_shapes=[pltpu.VMEM((2, 2 * GRID_W, WIN_H * GRID_W), F32),
                        pltpu.VMEM((2, WIN_H, GRID_W, WIN_H * GRID_W), F32)],
        compiler_params=pltpu.CompilerParams(
            dimension_semantics=("arbitrary", "arbitrary"), vmem_limit_bytes=_vmem_limit(40 << 20)),
        name="attention",
    )(q, k, v, cb)


def _outproj_kernel(conv_ref, na_ref, x_ref, wo_ref, bo_ref, g_ref, wr_ref, br_ref,
                    h_ref, xn_ref, idx_ref, gate_ref):
    for s in range(x_ref.shape[0] // ROUTER_SUB):
        rows = slice(s * ROUTER_SUB, (s + 1) * ROUTER_SUB)
        mixed = jnp.concatenate([conv_ref[rows, :], na_ref[rows, :]], axis=-1)
        h = x_ref[rows, :] + jnp.dot(mixed, wo_ref[...], preferred_element_type=F32) + bo_ref[...]
        h_ref[rows, :] = h
        ms = jnp.mean(h * h, axis=-1, keepdims=True)
        xn = (h * lax.rsqrt(ms + EPS) * g_ref[...]).astype(BF16)
        xn_ref[rows, :] = xn
        logits = jnp.transpose(jnp.dot(xn, wr_ref[...], preferred_element_type=F32) + br_ref[...])[:N_EXPERTS]
        ids = lax.broadcasted_iota(jnp.int32, logits.shape, 0)
        vals, sels = [], []
        l = logits
        for _ in range(TOP_K):
            m = jnp.max(l, axis=0, keepdims=True)
            sel = jnp.min(jnp.where(l == m, ids, N_EXPERTS), axis=0, keepdims=True)
            vals.append(m)
            sels.append(sel)
            l = jnp.where(ids == sel, -jnp.inf, l)
        es = [jnp.exp(vk - vals[0]) for vk in vals]
        tot = es[0] + es[1] + es[2] + es[3]
        idx_ref[:, rows] = jnp.concatenate(sels, axis=0)
        gate_ref[:, rows] = jnp.concatenate([e / tot for e in es], axis=0)


def _outproj_router(conv_out, na_out, x2, w_out, b_out, g_ffn, w_router, b_router):
    t = x2.shape[0]
    tm = min(PROJ_TM, t)
    tok = lambda i: (i, 0)
    const = lambda i: (0, 0)
    return pl.pallas_call(
        _outproj_kernel,
        grid=(t // tm,),
        in_specs=[
            pl.BlockSpec((tm, D_CONV), tok),
            pl.BlockSpec((tm, D_NA), tok),
            pl.BlockSpec((tm, D_MODEL), tok),
            pl.BlockSpec((D_MODEL, D_MODEL), const),
            pl.BlockSpec((1, D_MODEL), const),
            pl.BlockSpec((1, D_MODEL), const),
            pl.BlockSpec((D_MODEL, LANES), const),
            pl.BlockSpec((1, LANES), const),
        ],
        out_specs=[
            pl.BlockSpec((tm, D_MODEL), tok),
            pl.BlockSpec((tm, D_MODEL), tok),
            pl.BlockSpec((TOP_K, tm), lambda i: (0, i)),
            pl.BlockSpec((TOP_K, tm), lambda i: (0, i)),
        ],
        out_shape=[
            jax.ShapeDtypeStruct((t, D_MODEL), F32),
            jax.ShapeDtypeStruct((t, D_MODEL), BF16),
            jax.ShapeDtypeStruct((TOP_K, t), jnp.int32),
            jax.ShapeDtypeStruct((TOP_K, t), F32),
        ],
        compiler_params=pltpu.CompilerParams(
            dimension_semantics=("arbitrary",), vmem_limit_bytes=_vmem_limit(48 << 20)),
        name="outproj_router",
    )(conv_out, na_out, x2, w_out.astype(BF16), b_out.reshape(1, D_MODEL), g_ffn.reshape(1, D_MODEL),
      jnp.pad(w_router.astype(BF16), ((0, 0), (0, LANES - N_EXPERTS))),
      jnp.pad(b_router.reshape(1, N_EXPERTS), ((0, 0), (0, LANES - N_EXPERTS))))


def _route_plan(idx_t, t):
    nt = t // ROUTE_TM
    experts = jnp.arange(N_EXPERTS, dtype=I32)
    onehot = idx_t.reshape(TOP_K, nt, ROUTE_TM, 1) == experts
    cnt = jnp.sum(onehot, axis=(0, 2), dtype=I32)
    sizes = jnp.sum(cnt, axis=0)
    padded = (sizes + ROUTE_RP + EXP_BM - 1) // EXP_BM * EXP_BM
    pad_ends = jnp.cumsum(padded)
    pad_off = pad_ends - padded
    tbase = pad_off[None, :] + jnp.cumsum(cnt, axis=0) - cnt
    cap = _sorted_rows(t)
    nb = cap // EXP_BM
    n_used = pad_ends[N_EXPERTS - 1] // EXP_BM
    blk_src = jnp.minimum(jnp.arange(nb, dtype=I32), n_used - 1)
    blk_exp = jnp.minimum(
        jnp.sum(blk_src[:, None] * EXP_BM >= pad_ends[None, :], axis=1, dtype=I32), N_EXPERTS - 1)
    row_end = jnp.sum(jnp.where(blk_exp[:, None] == experts[None, :], (pad_off + sizes)[None, :], 0), axis=1)
    blk_live = jnp.clip(row_end - blk_src * EXP_BM, 0, EXP_BM).astype(I32)
    multi = jnp.any(tbase % SUBLANES + cnt > ROUTE_RP, axis=1).astype(I32)
    zero_from = jnp.concatenate([(pad_off + sizes) // ZERO_ROWS * ZERO_ROWS, pad_ends[N_EXPERTS - 1:]])
    zero_to = jnp.concatenate([pad_ends, jnp.full((1,), cap, I32)])
    return dict(cnt=cnt.reshape(-1), tbase=tbase.reshape(-1).astype(I32), multi=multi,
                zero_start=zero_from.astype(I32), zero_cnt=((zero_to - zero_from) // ZERO_ROWS).astype(I32),
                blk_src=blk_src, blk_exp=blk_exp, blk_live=blk_live, n_used=n_used.reshape(1).astype(I32))


def _sorted_rows(t):
    cap = t * TOP_K + N_EXPERTS * (EXP_BM + ROUTE_RP)
    return (cap + EXP_BM - 1) // EXP_BM * EXP_BM


def _chunk_geometry(tb, n):
    head = tb & (SUBLANES - 1)
    start = pl.multiple_of(tb - head, SUBLANES)
    nchunks = (head + n + ROUTE_RP - 1) // ROUTE_RP
    return start, head, nchunks


def _dispatch_kernel(tb_ref, cnt_ref, mt_ref, zs_ref, zc_ref, idx_ref, tbv_ref, xn_ref, xs_hbm, loc_ref,
                     stage, ostage, zbuf, head_ref, cs_ref, ms_ref, sem, osem, zsem):
    i = pl.program_id(0)
    nt = pl.num_programs(0)
    slot = i % 2
    tm = idx_ref.shape[1]
    groups = ROUTE_RP // SUBLANES

    def geometry(e, step=None):
        step = i if step is None else step
        return _chunk_geometry(tb_ref[step * N_EXPERTS + e], cnt_ref[step * N_EXPERTS + e])

    def chunk_copy(step, e):
        start, _, _ = geometry(e, step)
        return pltpu.make_async_copy(stage.at[step % 2, pl.ds(e * ROUTE_RP, ROUTE_RP)],
                                     xs_hbm.at[pl.ds(start, ROUTE_RP)], sem.at[step % 2])

    def all_chunks(step, phase):
        if phase == "wait":
            pltpu.make_async_copy(stage.at[0], xs_hbm.at[pl.ds(0, N_EXPERTS * ROUTE_RP)], sem.at[step % 2]).wait()
            return
        for e in range(N_EXPERTS):
            chunk_copy(step, e).start()

    @pl.when(i >= 2)
    def _():
        all_chunks(i - 2, "wait")

    @pl.when(i >= 1)
    def _():
        all_chunks(i - 1, "start")

    def zero_copy(start):
        return pltpu.make_async_copy(zbuf, xs_hbm.at[pl.ds(pl.multiple_of(start, ZERO_ROWS), ZERO_ROWS)], zsem)

    @pl.when(i == 0)
    def _():
        zbuf[...] = jnp.zeros(zbuf.shape, U32)
        head_ref[...] = jnp.zeros(head_ref.shape, U32)
        for phase in ("start", "wait"):
            for j in range(N_EXPERTS + 1):
                def piece(q, carry, j=j, phase=phase):
                    getattr(zero_copy(zs_ref[j] + q * ZERO_ROWS), phase)()
                    return carry
                lax.fori_loop(0, zc_ref[j], piece, 0)

    idx = idx_ref[...]
    eio = lax.broadcasted_iota(I32, (N_EXPERTS, tm), 0)
    member = jnp.zeros((N_EXPERTS, tm), F32)
    for k in range(TOP_K):
        member = member + (idx[k:k + 1, :] == eio).astype(F32)
    tri = (lax.broadcasted_iota(I32, (tm, tm), 0) < lax.broadcasted_iota(I32, (tm, tm), 1)).astype(BF16)
    rank = jnp.dot(member.astype(BF16), tri, preferred_element_type=F32)
    pos = rank + tbv_ref[0][:, 0:1].astype(F32)
    loc_ref[...] = jnp.concatenate(
        [jnp.sum(jnp.where(idx[k:k + 1, :] == eio, pos, 0.0), axis=0, keepdims=True) for k in range(TOP_K)],
        axis=0).astype(I32)
    cs_ref[...] = pos
    ms_ref[...] = member

    jio = lax.broadcasted_iota(I32, (ROUTE_RP, tm), 0).astype(F32)
    sel = [jnp.where((jio == pos[e:e + 1, :]) & (member[e:e + 1, :] > 0.0), 1.0, 0.0).astype(BF16)
           for e in range(N_EXPERTS)]
    rows = jnp.dot(jnp.concatenate(sel, axis=0), xn_ref[...], preferred_element_type=F32)

    stage[slot] = _pack_rows(rows)
    sub = lax.broadcasted_iota(I32, (SUBLANES, HALF), 0)
    for e in range(N_EXPERTS):
        _, head, _ = geometry(e)
        first = pl.ds(e * ROUTE_RP, SUBLANES)
        stage[slot, first, :] = jnp.where(sub < head, head_ref[e], stage[slot, first, :])
        g = jnp.minimum(lax.shift_right_logical(head + cnt_ref[i * N_EXPERTS + e], SUBLANES.bit_length() - 1),
                        groups - 1)
        head_ref[e] = stage[slot, pl.ds(pl.multiple_of(e * ROUTE_RP + g * SUBLANES, SUBLANES), SUBLANES), :]

    def per_expert(e, carry):
        start, head, nchunks = geometry(e)
        end = head + cnt_ref[i * N_EXPERTS + e]

        def per_chunk(c, carry2):
            want = jio + (c * ROUTE_RP).astype(F32)
            pick = jnp.where((want == cs_ref[pl.ds(e, 1), :]) & (ms_ref[pl.ds(e, 1), :] > 0.0), 1.0, 0.0)
            ostage[...] = _pack_rows(jnp.dot(pick.astype(BF16), xn_ref[...], preferred_element_type=F32))
            cp = pltpu.make_async_copy(
                ostage, xs_hbm.at[pl.ds(pl.multiple_of(start + c * ROUTE_RP, SUBLANES), ROUTE_RP)], osem)
            cp.start()
            cp.wait()

            @pl.when(end // ROUTE_RP == c)
            def _():
                g = (end - c * ROUTE_RP) // SUBLANES
                head_ref[e] = ostage[pl.ds(pl.multiple_of(g * SUBLANES, SUBLANES), SUBLANES), :]
            return carry2
        return lax.fori_loop(1, nchunks, per_chunk, carry)

    @pl.when(mt_ref[i] > 0)
    def _():
        lax.fori_loop(0, N_EXPERTS, per_expert, 0)

    @pl.when(i == nt - 1)
    def _():
        @pl.when(i >= 1)
        def _():
            all_chunks(i - 1, "wait")
        all_chunks(i, "start")
        all_chunks(i, "wait")


def _dispatch(plan, idx_t, xn2):
    t = xn2.shape[0]
    tm = ROUTE_TM
    cap = _sorted_rows(t)
    grid_spec = pltpu.PrefetchScalarGridSpec(
        num_scalar_prefetch=5,
        grid=(t // tm,),
        in_specs=[
            pl.BlockSpec((TOP_K, tm), lambda i, *_: (0, i)),
            pl.BlockSpec((1, N_EXPERTS, LANES), lambda i, *_: (i, 0, 0)),
            pl.BlockSpec((tm, D_MODEL), lambda i, *_: (i, 0)),
        ],
        out_specs=[
            pl.BlockSpec(memory_space=pl.ANY),
            pl.BlockSpec((TOP_K, tm), lambda i, *_: (0, i)),
        ],
        scratch_shapes=[
            pltpu.VMEM((2, N_EXPERTS * ROUTE_RP, HALF), U32),
            pltpu.VMEM((ROUTE_RP, HALF), U32),
            pltpu.VMEM((ZERO_ROWS, HALF), U32),
            pltpu.VMEM((N_EXPERTS, SUBLANES, HALF), U32),
            pltpu.VMEM((N_EXPERTS, tm), F32),
            pltpu.VMEM((N_EXPERTS, tm), F32),
            pltpu.SemaphoreType.DMA((2,)),
            pltpu.SemaphoreType.DMA(()),
            pltpu.SemaphoreType.DMA(()),
        ],
    )
    return pl.pallas_call(
        _dispatch_kernel,
        grid_spec=grid_spec,
        out_shape=[
            jax.ShapeDtypeStruct((cap, HALF), U32),
            jax.ShapeDtypeStruct((TOP_K, t), I32),
        ],
        compiler_params=pltpu.CompilerParams(
            dimension_semantics=("arbitrary",), vmem_limit_bytes=_vmem_limit(32 << 20)),
        name="dispatch",
    )(plan["tbase"], plan["cnt"], plan["multi"], plan["zero_start"], plan["zero_cnt"], idx_t,
      jnp.broadcast_to((plan["tbase"] % SUBLANES).reshape(t // tm, N_EXPERTS, 1), (t // tm, N_EXPERTS, LANES)),
      xn2)


def _expert_kernel(be_ref, src_ref, nu_ref, live_ref, xs_ref, wg_hbm, wu_hbm, wd_hbm, bg_ref, bu_ref, bd_ref,
                   ys_ref, wf32, wbf, wsem):
    b = pl.program_id(0)
    e = be_ref[b]

    def weight_copies(expert, par):
        return [pltpu.make_async_copy(w.at[expert], wf32.at[par, m], wsem.at[par])
                for m, w in enumerate((wg_hbm, wu_hbm, wd_hbm))]

    @pl.when(b == 0)
    def _():
        for cp in weight_copies(e, e % 2):
            cp.start()

    @pl.when(b < nu_ref[0])
    def _():
        @pl.when((b == 0) | (e != be_ref[jnp.maximum(b - 1, 0)]))
        def _():
            par = e % 2
            for cp in weight_copies(e, par):
                cp.wait()

            @pl.when(e + 1 < N_EXPERTS)
            def _():
                for cp in weight_copies(e + 1, 1 - par):
                    cp.start()

            for m in range(3):
                wbf[m] = wf32[par, m].astype(BF16)

        def ffn(rows):
            x = _unpack_rows(xs_ref[rows, :])
            gt = jnp.minimum(jnp.dot(x, wbf[0], preferred_element_type=F32) + bg_ref[0], SWIGLU_LIMIT)
            up = jnp.clip(jnp.dot(x, wbf[1], preferred_element_type=F32) + bu_ref[0],
                          -SWIGLU_LIMIT, SWIGLU_LIMIT)
            hdn = (up + 1.0) * (gt * jax.nn.sigmoid(SWIGLU_ALPHA * gt))
            y = jnp.dot(hdn.astype(BF16), wbf[2], preferred_element_type=F32) + bd_ref[0]
            ys_ref[rows, :] = _pack_rows(y.astype(BF16).astype(F32))

        live = live_ref[b]
        for p, part in enumerate(EXP_PARTS):
            smaller = EXP_PARTS[p + 1] if p + 1 < len(EXP_PARTS) else -1
            cond = live > smaller
            if p > 0:
                cond = cond & (live <= part)

            @pl.when(cond)
            def _(part=part):
                if part > 0:
                    ffn(slice(0, part))
                if part < EXP_BM:
                    ys_ref[part:, :] = jnp.zeros((EXP_BM - part, HALF), U32)

    @pl.when(b >= nu_ref[0])
    def _():
        ys_ref[...] = jnp.zeros(ys_ref.shape, U32)


def _experts(plan, xs, w_gate, b_gate, w_up, b_up, w_down, b_down):
    cap = xs.shape[0]
    nb = cap // EXP_BM
    rows = lambda b, be, src, nu, live: (src[b], 0)
    wsel = lambda b, be, src, nu, live: (be[b], 0, 0)
    wspec = pl.BlockSpec(memory_space=pl.ANY)
    bspec = pl.BlockSpec((1, 1, D_FF), wsel)
    grid_spec = pltpu.PrefetchScalarGridSpec(
        num_scalar_prefetch=4,
        grid=(nb,),
        in_specs=[pl.BlockSpec((EXP_BM, HALF), rows), wspec, wspec, wspec, bspec, bspec, bspec],
        out_specs=pl.BlockSpec((EXP_BM, HALF), lambda b, be, src, nu, live: (b, 0)),
        scratch_shapes=[pltpu.VMEM((2, 3, D_MODEL, D_FF), F32),
                        pltpu.VMEM((3, D_MODEL, D_FF), BF16),
                        pltpu.SemaphoreType.DMA((2,))],
    )
    return pl.pallas_call(
        _expert_kernel,
        grid_spec=grid_spec,
        out_shape=jax.ShapeDtypeStruct((cap, HALF), U32),
        compiler_params=pltpu.CompilerParams(
            dimension_semantics=("arbitrary",), vmem_limit_bytes=_vmem_limit(56 << 20)),
        name="experts",
    )(plan["blk_exp"], plan["blk_src"], plan["n_used"], plan["blk_live"], xs, w_gate, w_up, w_down,
      b_gate.reshape(N_EXPERTS, 1, D_FF), b_up.reshape(N_EXPERTS, 1, D_FF),
      b_down.reshape(N_EXPERTS, 1, D_MODEL))


def _combine_kernel(tb_ref, cnt_ref, mt_ref, h_ref, idx_ref, loc_ref, gate_ref, ys_hbm, g_ref, o_ref,
                    ybuf, obuf, acc_ref, sem, osem):
    i = pl.program_id(0)
    nt = pl.num_programs(0)
    slot = i % 2
    tm = h_ref.shape[0]

    def geometry(step, e):
        return _chunk_geometry(tb_ref[step * N_EXPERTS + e], cnt_ref[step * N_EXPERTS + e])

    def chunk_copy(step, e, s):
        start, _, _ = geometry(step, e)
        return pltpu.make_async_copy(ys_hbm.at[pl.ds(start, ROUTE_RP)],
                                     ybuf.at[s, pl.ds(e * ROUTE_RP, ROUTE_RP)], sem.at[s])

    @pl.when(i == 0)
    def _():
        for e in range(N_EXPERTS):
            chunk_copy(0, e, 0).start()

    for e in range(N_EXPERTS):
        chunk_copy(jnp.minimum(i + 1, nt - 1), e, 1 - slot).start()

    idx_loc = jnp.transpose(jnp.concatenate([idx_ref[...], loc_ref[...]], axis=0))
    idx, loc = idx_loc[:, :TOP_K], idx_loc[:, TOP_K:]
    gate = jnp.transpose(jnp.concatenate([gate_ref[...], gate_ref[...]], axis=0))[:, :TOP_K]

    def gate_matrix(ncols, col_of):
        out = []
        for rows in _row_blocks(tm):
            colio = lax.broadcasted_iota(I32, (rows.stop - rows.start, ncols), 1)
            g = jnp.zeros(colio.shape, F32)
            for k in range(TOP_K):
                g = jnp.where(colio == col_of[rows, k:k + 1], gate[rows, k:k + 1], g)
            out.append(g.astype(BF16))
        return jnp.concatenate(out, axis=0)

    pltpu.make_async_copy(ys_hbm.at[pl.ds(0, N_EXPERTS * ROUTE_RP)], ybuf.at[0], sem.at[slot]).wait()
    col = jnp.where(loc < ROUTE_RP, idx * ROUTE_RP + loc, -1)
    acc_ref[...] = jnp.dot(gate_matrix(N_EXPERTS * ROUTE_RP, col), _unpack_rows(ybuf[slot]),
                           preferred_element_type=F32)

    def per_expert(e, carry):
        start, _, nchunks = geometry(i, e)

        def per_chunk(c, carry2):
            cp = pltpu.make_async_copy(
                ys_hbm.at[pl.ds(pl.multiple_of(start + c * ROUTE_RP, SUBLANES), ROUTE_RP)], obuf, osem)
            cp.start()
            cp.wait()
            ccol = jnp.where(idx == e, loc - c * ROUTE_RP, -1)
            acc_ref[...] += jnp.dot(gate_matrix(ROUTE_RP, ccol), _unpack_rows(obuf[...]),
                                    preferred_element_type=F32)
            return carry2
        return lax.fori_loop(1, nchunks, per_chunk, carry)

    @pl.when(mt_ref[i] > 0)
    def _():
        lax.fori_loop(0, N_EXPERTS, per_expert, 0)

    out = h_ref[...] + acc_ref[...]
    ms = jnp.mean(out * out, axis=-1, keepdims=True)
    o_ref[...] = out * lax.rsqrt(ms + EPS) * g_ref[...]

    @pl.when(i == nt - 1)
    def _():
        pltpu.make_async_copy(ys_hbm.at[pl.ds(0, N_EXPERTS * ROUTE_RP)], ybuf.at[0], sem.at[1 - slot]).wait()


def _combine(plan, h, ys, idx_t, loc_t, gate_t, g_final):
    t = h.shape[0]
    tm = ROUTE_TM
    tok = lambda i, *_: (i, 0)
    lane = lambda i, *_: (0, i)
    grid_spec = pltpu.PrefetchScalarGridSpec(
        num_scalar_prefetch=3,
        grid=(t // tm,),
        in_specs=[
            pl.BlockSpec((tm, D_MODEL), tok),
            pl.BlockSpec((TOP_K, tm), lane),
            pl.BlockSpec((TOP_K, tm), lane),
            pl.BlockSpec((TOP_K, tm), lane),
            pl.BlockSpec(memory_space=pl.ANY),
            pl.BlockSpec((1, D_MODEL), lambda i, *_: (0, 0)),
        ],
        out_specs=pl.BlockSpec((tm, D_MODEL), tok),
        scratch_shapes=[
            pltpu.VMEM((2, N_EXPERTS * ROUTE_RP, HALF), U32),
            pltpu.VMEM((ROUTE_RP, HALF), U32),
            pltpu.VMEM((tm, D_MODEL), F32),
            pltpu.SemaphoreType.DMA((2,)),
            pltpu.SemaphoreType.DMA(()),
        ],
    )
    return pl.pallas_call(
        _combine_kernel,
        grid_spec=grid_spec,
        out_shape=jax.ShapeDtypeStruct((t, D_MODEL), F32),
        compiler_params=pltpu.CompilerParams(
            dimension_semantics=("arbitrary",), vmem_limit_bytes=_vmem_limit(32 << 20)),
        name="combine",
    )(plan["tbase"], plan["cnt"], plan["multi"], h, idx_t, loc_t, gate_t, ys,
      g_final.reshape(1, D_MODEL))


def _moe(h, xn2, idx_t, gate_t, w_gate, b_gate, w_up, b_up, w_down, b_down, g_final):
    t = h.shape[0]
    plan = _route_plan(idx_t, t)
    xs, loc_t = _dispatch(plan, idx_t, xn2)
    ys = _experts(plan, xs, w_gate, b_gate, w_up, b_up, w_down, b_down)
    return _combine(plan, h, ys, idx_t, loc_t, gate_t, g_final)


def kernel(x, g_mix, w_in, b_in, conv_w, conv_b, ln_g, ln_b, rpb, w_out, b_out, g_ffn, w_router,
           b_router, w_gate, b_gate, w_up, b_up, w_down, b_down, g_final):
    bsz, seq, d = x.shape
    assert bsz == 1 and d == D_MODEL and g_mix.shape[0] == 1
    assert seq % (GRID_W * WIN_H) == 0
    x2 = x.reshape(seq, d)
    q, k, v, conv_out = _inproj_conv(x2, g_mix[0], w_in[0], b_in[0], conv_w[0], conv_b[0], ln_g[0], ln_b[0])
    na_out = _attention(q, k, v, rpb[0])
    h, xn2, idx_t, gate_t = _outproj_router(conv_out, na_out, x2, w_out[0], b_out[0], g_ffn[0],
                                            w_router[0], b_router[0])
    out = _moe(h, xn2, idx_t, gate_t, w_gate[0], b_gate[0], w_up[0], b_up[0], w_down[0], b_down[0],
               g_final)
    return out.reshape(bsz, seq, d)
```

```python
import functools

import numpy as np
import jax
import jax.numpy as jnp
from jax import lax
from jax.experimental import pallas as pl
from jax.experimental.pallas import tpu as pltpu

F32 = jnp.float32
BF16 = jnp.bfloat16
U32 = jnp.uint32
I32 = jnp.int32

D_MODEL = 1024
GRID_W = 64
D_CONV = 512
CONV_K = 31
NA_HEADS = 8
NA_HEAD_DIM = 64
D_NA = NA_HEADS * NA_HEAD_DIM
D_IN = 2 * D_CONV + 3 * D_NA
WIN_H = 8
WIN_W = 16
N_EXPERTS = 32
TOP_K = 4
D_FF = D_MODEL
SWIGLU_LIMIT = 7.0
SWIGLU_ALPHA = 1.702
EPS = 1e-5
MASK_VALUE = -1e30

V7X_VMEM_BYTES = 64 * 1024 * 1024
LANES = 128

PROJ_TM = 512
ROUTER_SUB = 256
CONV_HALO = 16
CONV_CH = 32
ATTN_RB = 32
ATTN_UNROLL = 8
ROUTE_TM = 256
SUBLANES = 8
ROUTE_RP = 56
EXP_BM = 512
EXP_PARTS = (512, 256, 128, 0)
ZERO_ROWS = 256
HALF = D_MODEL // 2
BF16_BITS = 16
HIGH_HALF = 0xFFFF0000
ROW_BLOCK = 16


def _vmem_limit(nbytes):
    return int(min(nbytes, V7X_VMEM_BYTES - 6 * 1024 * 1024))


def _row_blocks(n):
    return [slice(r, min(r + ROW_BLOCK, n)) for r in range(0, n, ROW_BLOCK)]


def _pack_rows(x):
    out = []
    for rows in _row_blocks(x.shape[0]):
        lo = lax.bitcast_convert_type(x[rows, :HALF], U32) >> BF16_BITS
        hi = lax.bitcast_convert_type(x[rows, HALF:], U32) & jnp.uint32(HIGH_HALF)
        out.append(lo | hi)
    return jnp.concatenate(out, axis=0)


def _unpack_rows(w):
    out = []
    for rows in _row_blocks(w.shape[0]):
        lo = lax.bitcast_convert_type(w[rows] << BF16_BITS, F32)
        hi = lax.bitcast_convert_type(w[rows] & jnp.uint32(HIGH_HALF), F32)
        out.append(jnp.concatenate([lo, hi], axis=1).astype(BF16))
    return jnp.concatenate(out, axis=0)


def _inproj_conv_kernel(x_ref, g_ref, w_ref, b_ref, cw_ref, cb_ref, lg_ref, lb_ref,
                        q_ref, k_ref, v_ref, o_ref, ext_ref, u_ref, sh_ref):
    i = pl.program_id(0)
    nt = pl.num_programs(0) - 1
    tm = x_ref.shape[0]

    @pl.when(i == 0)
    def _():
        ext_ref[...] = jnp.zeros(ext_ref.shape, F32)

    x = x_ref[...]
    ms = jnp.mean(x * x, axis=-1, keepdims=True)
    xn = (x * lax.rsqrt(ms + EPS) * g_ref[...]).astype(BF16)

    def proj(c):
        sl = slice(c * D_CONV, (c + 1) * D_CONV)
        return jnp.dot(xn, w_ref[:, sl], preferred_element_type=F32) + b_ref[:, sl]

    u_ref[...] = proj(0) * jax.nn.sigmoid(proj(1))
    q_ref[...] = (proj(2) * (NA_HEAD_DIM ** -0.5)).astype(BF16)
    k_ref[...] = proj(3).astype(BF16)
    v_ref[...] = proj(4).astype(BF16)

    ext_ref[CONV_HALO + tm:, :] = jnp.where(i < nt, u_ref[0:CONV_HALO, :], 0.0)
    base = CONV_HALO - CONV_K // 2
    span = (CONV_K - 1) // SUBLANES * SUBLANES
    for b in range(SUBLANES):
        sh_ref[b] = ext_ref[base + b:base + b + tm + span, :]
    for c in range(tm // CONV_CH):
        acc = jnp.zeros((CONV_CH, D_CONV), F32)
        for b in range(SUBLANES):
            for a in range((CONV_K - 1 - b) // SUBLANES + 1):
                r = c * CONV_CH + SUBLANES * a
                acc = acc + sh_ref[b, r:r + CONV_CH, :] * cw_ref[SUBLANES * a + b:SUBLANES * a + b + 1, :]
        acc = acc + cb_ref[...]
        mu = jnp.mean(acc, axis=-1, keepdims=True)
        d = acc - mu
        var = jnp.mean(d * d, axis=-1, keepdims=True)
        un = d * lax.rsqrt(var + EPS) * lg_ref[...] + lb_ref[...]
        o_ref[c * CONV_CH:(c + 1) * CONV_CH, :] = (un * jax.nn.sigmoid(un)).astype(o_ref.dtype)

    ext_ref[0:CONV_HALO, :] = ext_ref[tm:tm + CONV_HALO, :]
    ext_ref[CONV_HALO:CONV_HALO + tm, :] = u_ref[...]


def _inproj_conv(x2, g_mix, w_in, b_in, conv_w, conv_b, ln_g, ln_b):
    t = x2.shape[0]
    tm = min(PROJ_TM, t)
    nt = t // tm
    tok = lambda i: (jnp.minimum(i, nt - 1), 0)
    prev = lambda i: (jnp.maximum(i - 1, 0), 0)
    const = lambda i: (0, 0)
    w = jnp.zeros((CONV_K + 1, D_CONV), F32).at[:CONV_K].set(conv_w)
    span = (CONV_K - 1) // SUBLANES * SUBLANES
    return pl.pallas_call(
        _inproj_conv_kernel,
        grid=(nt + 1,),
        in_specs=[
            pl.BlockSpec((tm, D_MODEL), tok),
            pl.BlockSpec((1, D_MODEL), const),
            pl.BlockSpec((D_MODEL, D_IN), const),
            pl.BlockSpec((1, D_IN), const),
            pl.BlockSpec((CONV_K + 1, D_CONV), const),
            pl.BlockSpec((1, D_CONV), const),
            pl.BlockSpec((1, D_CONV), const),
            pl.BlockSpec((1, D_CONV), const),
        ],
        out_specs=[
            pl.BlockSpec((tm, D_NA), tok),
            pl.BlockSpec((tm, D_NA), tok),
            pl.BlockSpec((tm, D_NA), tok),
            pl.BlockSpec((tm, D_CONV), prev),
        ],
        out_shape=[
            jax.ShapeDtypeStruct((t, D_NA), BF16),
            jax.ShapeDtypeStruct((t, D_NA), BF16),
            jax.ShapeDtypeStruct((t, D_NA), BF16),
            jax.ShapeDtypeStruct((t, D_CONV), BF16),
        ],
        scratch_shapes=[pltpu.VMEM((tm + 2 * CONV_HALO, D_CONV), F32),
                        pltpu.VMEM((tm, D_CONV), F32),
                        pltpu.VMEM((SUBLANES, tm + span, D_CONV), F32)],
        compiler_params=pltpu.CompilerParams(
            dimension_semantics=("arbitrary",), vmem_limit_bytes=_vmem_limit(56 << 20)),
        name="inproj_conv",
    )(x2, g_mix.reshape(1, D_MODEL), w_in.astype(BF16), b_in.reshape(1, D_IN), w,
      conv_b.reshape(1, D_CONV), ln_g.reshape(1, D_CONV), ln_b.reshape(1, D_CONV))


def _column_bias(rpb):
    col = np.arange(GRID_W)
    c0 = np.clip(col - WIN_W // 2, 0, GRID_W - WIN_W)
    valid = (col[None, :] >= c0[:, None]) & (col[None, :] < c0[:, None] + WIN_W)
    dcol = np.clip(col[None, :] - col[:, None], -(WIN_W - 1), WIN_W - 1) + (WIN_W - 1)
    col_sel = ((dcol[:, :, None] == np.arange(2 * WIN_W - 1)) & valid[:, :, None]).astype(np.float32)
    b = jnp.einsum("hdj,qkj->hdqk", rpb.astype(F32), col_sel, precision=lax.Precision.HIGHEST)
    b = jnp.where(valid[None, None], b, MASK_VALUE)
    return jnp.concatenate([b, b], axis=-1)


def _attn_kernel(q_ref, k_ref, v_ref, cb_ref, o_ref, s_ref, bt_ref, *, rb, rows):
    jb = pl.program_id(1)
    lane = lax.broadcasted_iota(jnp.int32, (GRID_W, LANES), 1)
    first_head = lane < NA_HEAD_DIM
    nkeys = WIN_H * GRID_W

    @pl.when(jb == 0)
    def _():
        for h in range(2):
            for c in range(WIN_H):
                for rr in range(0, WIN_H, 2):
                    even = cb_ref[h, rr - c + WIN_H - 1]
                    odd = cb_ref[h, rr + 1 - c + WIN_H - 1]
                    bt_ref[h, c, :, rr * GRID_W:(rr + 2) * GRID_W] = jnp.where(first_head, even, odd)

    def window(r):
        i = jb * rb + r
        r0 = jnp.clip(i - WIN_H // 2, 0, rows - WIN_H)
        return i - r0, pl.multiple_of(r0 * GRID_W, GRID_W)

    def scores(r):
        _, start = window(r)
        q2 = q_ref[pl.ds(pl.multiple_of(r * GRID_W, GRID_W), GRID_W), :]
        zero = jnp.zeros_like(q2)
        qst = jnp.concatenate([jnp.where(first_head, q2, zero), jnp.where(first_head, zero, q2)], axis=0)
        ks = k_ref[pl.ds(start, nkeys), :]
        return lax.dot_general(qst, ks, (((1,), (1,)), ((), ())), preferred_element_type=F32)

    def finish(r, s):
        c, start = window(r)
        vs = v_ref[pl.ds(start, nkeys), :]
        bias = jnp.concatenate([bt_ref[0, c], bt_ref[1, c]], axis=0)
        s = jnp.where(bias > 0.5 * MASK_VALUE, s + bias, MASK_VALUE)
        m = jnp.max(s, axis=-1, keepdims=True)
        p = jnp.exp(s - m)
        l = jnp.sum(p, axis=-1, keepdims=True)
        o = jnp.dot(p.astype(BF16), vs, preferred_element_type=F32) / l
        o2 = jnp.where(first_head, o[:GRID_W], o[GRID_W:])
        o_ref[pl.ds(pl.multiple_of(r * GRID_W, GRID_W), GRID_W), :] = o2.astype(o_ref.dtype)

    s_ref[0] = scores(0)

    def pair(p, carry):
        r = 2 * p
        s_ref[1] = scores(r + 1)
        finish(r, s_ref[0])
        s_ref[0] = scores(jnp.minimum(r + 2, rb - 1))
        finish(r + 1, s_ref[1])
        return carry

    lax.fori_loop(0, rb // 2, pair, 0, unroll=ATTN_UNROLL)


def _attention(q, k, v, rpb):
    t = q.shape[0]
    rows = t // GRID_W
    rb = min(ATTN_RB, rows)
    npairs = D_NA // LANES
    cb = _column_bias(rpb)
    kern = functools.partial(_attn_kernel, rb=rb, rows=rows)
    return pl.pallas_call(
        kern,
        grid=(npairs, rows // rb),
        in_specs=[
            pl.BlockSpec((rb * GRID_W, LANES), lambda p, j: (j, p)),
            pl.BlockSpec((t, LANES), lambda p, j: (0, p)),
            pl.BlockSpec((t, LANES), lambda p, j: (0, p)),
            pl.BlockSpec((2, 2 * WIN_H - 1, GRID_W, 2 * GRID_W), lambda p, j: (p, 0, 0, 0)),
        ],
        out_specs=pl.BlockSpec((rb * GRID_W, LANES), lambda p, j: (j, p)),
        out_shape=jax.ShapeDtypeStruct((t, D_NA), BF16),
        scratch_shapes=[pltpu.VMEM((2, 2 * GRID_W, WIN_H * GRID_W), F32),
                        pltpu.VMEM((2, WIN_H, GRID_W, WIN_H * GRID_W), F32)],
        compiler_params=pltpu.CompilerParams(
            dimension_semantics=("arbitrary", "arbitrary"), vmem_limit_bytes=_vmem_limit(40 << 20)),
        name="attention",
    )(q, k, v, cb)


def _outproj_kernel(conv_ref, na_ref, x_ref, wo_ref, bo_ref, g_ref, wr_ref, br_ref,
                    h_ref, xn_ref, idx_ref, gate_ref):
    for s in range(x_ref.shape[0] // ROUTER_SUB):
        rows = slice(s * ROUTER_SUB, (s + 1) * ROUTER_SUB)
        mixed = jnp.concatenate([conv_ref[rows, :], na_ref[rows, :]], axis=-1)
        h = x_ref[rows, :] + jnp.dot(mixed, wo_ref[...], preferred_element_type=F32) + bo_ref[...]
        h_ref[rows, :] = h
        ms = jnp.mean(h * h, axis=-1, keepdims=True)
        xn = (h * lax.rsqrt(ms + EPS) * g_ref[...]).astype(BF16)
        xn_ref[rows, :] = xn
        logits = jnp.transpose(jnp.dot(xn, wr_ref[...], preferred_element_type=F32) + br_ref[...])[:N_EXPERTS]
        ids = lax.broadcasted_iota(jnp.int32, logits.shape, 0)
        vals, sels = [], []
        l = logits
        for _ in range(TOP_K):
            m = jnp.max(l, axis=0, keepdims=True)
            sel = jnp.min(jnp.where(l == m, ids, N_EXPERTS), axis=0, keepdims=True)
            vals.append(m)
            sels.append(sel)
            l = jnp.where(ids == sel, -jnp.inf, l)
        es = [jnp.exp(vk - vals[0]) for vk in vals]
        tot = es[0] + es[1] + es[2] + es[3]
        idx_ref[:, rows] = jnp.concatenate(sels, axis=0)
        gate_ref[:, rows] = jnp.concatenate([e / tot for e in es], axis=0)


def _outproj_router(conv_out, na_out, x2, w_out, b_out, g_ffn, w_router, b_router):
    t = x2.shape[0]
    tm = min(PROJ_TM, t)
    tok = lambda i: (i, 0)
    const = lambda i: (0, 0)
    return pl.pallas_call(
        _outproj_kernel,
        grid=(t // tm,),
        in_specs=[
            pl.BlockSpec((tm, D_CONV), tok),
            pl.BlockSpec((tm, D_NA), tok),
            pl.BlockSpec((tm, D_MODEL), tok),
            pl.BlockSpec((D_MODEL, D_MODEL), const),
            pl.BlockSpec((1, D_MODEL), const),
            pl.BlockSpec((1, D_MODEL), const),
            pl.BlockSpec((D_MODEL, LANES), const),
            pl.BlockSpec((1, LANES), const),
        ],
        out_specs=[
            pl.BlockSpec((tm, D_MODEL), tok),
            pl.BlockSpec((tm, D_MODEL), tok),
            pl.BlockSpec((TOP_K, tm), lambda i: (0, i)),
            pl.BlockSpec((TOP_K, tm), lambda i: (0, i)),
        ],
        out_shape=[
            jax.ShapeDtypeStruct((t, D_MODEL), F32),
            jax.ShapeDtypeStruct((t, D_MODEL), BF16),
            jax.ShapeDtypeStruct((TOP_K, t), jnp.int32),
            jax.ShapeDtypeStruct((TOP_K, t), F32),
        ],
        compiler_params=pltpu.CompilerParams(
            dimension_semantics=("arbitrary",), vmem_limit_bytes=_vmem_limit(48 << 20)),
        name="outproj_router",
    )(conv_out, na_out, x2, w_out.astype(BF16), b_out.reshape(1, D_MODEL), g_ffn.reshape(1, D_MODEL),
      jnp.pad(w_router.astype(BF16), ((0, 0), (0, LANES - N_EXPERTS))),
      jnp.pad(b_router.reshape(1, N_EXPERTS), ((0, 0), (0, LANES - N_EXPERTS))))


def _route_plan(idx_t, t):
    nt = t // ROUTE_TM
    experts = jnp.arange(N_EXPERTS, dtype=I32)
    onehot = idx_t.reshape(TOP_K, nt, ROUTE_TM, 1) == experts
    cnt = jnp.sum(onehot, axis=(0, 2), dtype=I32)
    sizes = jnp.sum(cnt, axis=0)
    padded = (sizes + ROUTE_RP + EXP_BM - 1) // EXP_BM * EXP_BM
    pad_ends = jnp.cumsum(padded)
    pad_off = pad_ends - padded
    tbase = pad_off[None, :] + jnp.cumsum(cnt, axis=0) - cnt
    cap = _sorted_rows(t)
    nb = cap // EXP_BM
    n_used = pad_ends[N_EXPERTS - 1] // EXP_BM
    blk_src = jnp.minimum(jnp.arange(nb, dtype=I32), n_used - 1)
    blk_exp = jnp.minimum(
        jnp.sum(blk_src[:, None] * EXP_BM >= pad_ends[None, :], axis=1, dtype=I32), N_EXPERTS - 1)
    row_end = jnp.sum(jnp.where(blk_exp[:, None] == experts[None, :], (pad_off + sizes)[None, :], 0), axis=1)
    blk_live = jnp.clip(row_end - blk_src * EXP_BM, 0, EXP_BM).astype(I32)
    multi = jnp.any(tbase % SUBLANES + cnt > ROUTE_RP, axis=1).astype(I32)
    zero_from = jnp.concatenate([(pad_off + sizes) // ZERO_ROWS * ZERO_ROWS, pad_ends[N_EXPERTS - 1:]])
    zero_to = jnp.concatenate([pad_ends, jnp.full((1,), cap, I32)])
    return dict(cnt=cnt.reshape(-1), tbase=tbase.reshape(-1).astype(I32), multi=multi,
                zero_start=zero_from.astype(I32), zero_cnt=((zero_to - zero_from) // ZERO_ROWS).astype(I32),
                blk_src=blk_src, blk_exp=blk_exp, blk_live=blk_live, n_used=n_used.reshape(1).astype(I32))


def _sorted_rows(t):
    cap = t * TOP_K + N_EXPERTS * (EXP_BM + ROUTE_RP)
    return (cap + EXP_BM - 1) // EXP_BM * EXP_BM


def _chunk_geometry(tb, n):
    head = tb & (SUBLANES - 1)
    start = pl.multiple_of(tb - head, SUBLANES)
    nchunks = (head + n + ROUTE_RP - 1) // ROUTE_RP
    return start, head, nchunks


def _dispatch_kernel(tb_ref, cnt_ref, mt_ref, zs_ref, zc_ref, idx_ref, tbv_ref, xn_ref, xs_hbm, loc_ref,
                     stage, ostage, zbuf, head_ref, cs_ref, ms_ref, sem, osem, zsem):
    i = pl.program_id(0)
    nt = pl.num_programs(0)
    slot = i % 2
    tm = idx_ref.shape[1]
    groups = ROUTE_RP // SUBLANES

    def geometry(e, step=None):
        step = i if step is None else step
        return _chunk_geometry(tb_ref[step * N_EXPERTS + e], cnt_ref[step * N_EXPERTS + e])

    def chunk_copy(step, e):
        start, _, _ = geometry(e, step)
        return pltpu.make_async_copy(stage.at[step % 2, pl.ds(e * ROUTE_RP, ROUTE_RP)],
                                     xs_hbm.at[pl.ds(start, ROUTE_RP)], sem.at[step % 2])

    def all_chunks(step, phase):
        if phase == "wait":
            pltpu.make_async_copy(stage.at[0], xs_hbm.at[pl.ds(0, N_EXPERTS * ROUTE_RP)], sem.at[step % 2]).wait()
            return
        for e in range(N_EXPERTS):
            chunk_copy(step, e).start()

    @pl.when(i >= 2)
    def _():
        all_chunks(i - 2, "wait")

    @pl.when(i >= 1)
    def _():
        all_chunks(i - 1, "start")

    def zero_copy(start):
        return pltpu.make_async_copy(zbuf, xs_hbm.at[pl.ds(pl.multiple_of(start, ZERO_ROWS), ZERO_ROWS)], zsem)

    @pl.when(i == 0)
    def _():
        zbuf[...] = jnp.zeros(zbuf.shape, U32)
        head_ref[...] = jnp.zeros(head_ref.shape, U32)
        for phase in ("start", "wait"):
            for j in range(N_EXPERTS + 1):
                def piece(q, carry, j=j, phase=phase):
                    getattr(zero_copy(zs_ref[j] + q * ZERO_ROWS), phase)()
                    return carry
                lax.fori_loop(0, zc_ref[j], piece, 0)

    idx = idx_ref[...]
    eio = lax.broadcasted_iota(I32, (N_EXPERTS, tm), 0)
    member = jnp.zeros((N_EXPERTS, tm), F32)
    for k in range(TOP_K):
        member = member + (idx[k:k + 1, :] == eio).astype(F32)
    tri = (lax.broadcasted_iota(I32, (tm, tm), 0) < lax.broadcasted_iota(I32, (tm, tm), 1)).astype(BF16)
    rank = jnp.dot(member.astype(BF16), tri, preferred_element_type=F32)
    pos = rank + tbv_ref[0][:, 0:1].astype(F32)
    loc_ref[...] = jnp.concatenate(
        [jnp.sum(jnp.where(idx[k:k + 1, :] == eio, pos, 0.0), axis=0, keepdims=True) for k in range(TOP_K)],
        axis=0).astype(I32)
    cs_ref[...] = pos
    ms_ref[...] = member

    jio = lax.broadcasted_iota(I32, (ROUTE_RP, tm), 0).astype(F32)
    sel = [jnp.where((jio == pos[e:e + 1, :]) & (member[e:e + 1, :] > 0.0), 1.0, 0.0).astype(BF16)
           for e in range(N_EXPERTS)]
    rows = jnp.dot(jnp.concatenate(sel, axis=0), xn_ref[...], preferred_element_type=F32)

    stage[slot] = _pack_rows(rows)
    sub = lax.broadcasted_iota(I32, (SUBLANES, HALF), 0)
    for e in range(N_EXPERTS):
        _, head, _ = geometry(e)
        first = pl.ds(e * ROUTE_RP, SUBLANES)
        stage[slot, first, :] = jnp.where(sub < head, head_ref[e], stage[slot, first, :])
        g = jnp.minimum(lax.shift_right_logical(head + cnt_ref[i * N_EXPERTS + e], SUBLANES.bit_length() - 1),
                        groups - 1)
        head_ref[e] = stage[slot, pl.ds(pl.multiple_of(e * ROUTE_RP + g * SUBLANES, SUBLANES), SUBLANES), :]

    def per_expert(e, carry):
        start, head, nchunks = geometry(e)
        end = head + cnt_ref[i * N_EXPERTS + e]

        def per_chunk(c, carry2):
            want = jio + (c * ROUTE_RP).astype(F32)
            pick = jnp.where((want == cs_ref[pl.ds(e, 1), :]) & (ms_ref[pl.ds(e, 1), :] > 0.0), 1.0, 0.0)
            ostage[...] = _pack_rows(jnp.dot(pick.astype(BF16), xn_ref[...], preferred_element_type=F32))
            cp = pltpu.make_async_copy(
                ostage, xs_hbm.at[pl.ds(pl.multiple_of(start + c * ROUTE_RP, SUBLANES), ROUTE_RP)], osem)
            cp.start()
            cp.wait()

            @pl.when(end // ROUTE_RP == c)
            def _():
                g = (end - c * ROUTE_RP) // SUBLANES
                head_ref[e] = ostage[pl.ds(pl.multiple_of(g * SUBLANES, SUBLANES), SUBLANES), :]
            return carry2
        return lax.fori_loop(1, nchunks, per_chunk, carry)

    @pl.when(mt_ref[i] > 0)
    def _():
        lax.fori_loop(0, N_EXPERTS, per_expert, 0)

    @pl.when(i == nt - 1)
    def _():
        @pl.when(i >= 1)
        def _():
            all_chunks(i - 1, "wait")
        all_chunks(i, "start")
        all_chunks(i, "wait")


def _dispatch(plan, idx_t, xn2):
    t = xn2.shape[0]
    tm = ROUTE_TM
    cap = _sorted_rows(t)
    grid_spec = pltpu.PrefetchScalarGridSpec(
        num_scalar_prefetch=5,
        grid=(t // tm,),
        in_specs=[
            pl.BlockSpec((TOP_K, tm), lambda i, *_: (0, i)),
            pl.BlockSpec((1, N_EXPERTS, LANES), lambda i, *_: (i, 0, 0)),
            pl.BlockSpec((tm, D_MODEL), lambda i, *_: (i, 0)),
        ],
        out_specs=[
            pl.BlockSpec(memory_space=pl.ANY),
            pl.BlockSpec((TOP_K, tm), lambda i, *_: (0, i)),
        ],
        scratch_shapes=[
            pltpu.VMEM((2, N_EXPERTS * ROUTE_RP, HALF), U32),
            pltpu.VMEM((ROUTE_RP, HALF), U32),
            pltpu.VMEM((ZERO_ROWS, HALF), U32),
            pltpu.VMEM((N_EXPERTS, SUBLANES, HALF), U32),
            pltpu.VMEM((N_EXPERTS, tm), F32),
            pltpu.VMEM((N_EXPERTS, tm), F32),
            pltpu.SemaphoreType.DMA((2,)),
            pltpu.SemaphoreType.DMA(()),
            pltpu.SemaphoreType.DMA(()),
        ],
    )
    return pl.pallas_call(
        _dispatch_kernel,
        grid_spec=grid_spec,
        out_shape=[
            jax.ShapeDtypeStruct((cap, HALF), U32),
            jax.ShapeDtypeStruct((TOP_K, t), I32),
        ],
        compiler_params=pltpu.CompilerParams(
            dimension_semantics=("arbitrary",), vmem_limit_bytes=_vmem_limit(32 << 20)),
        name="dispatch",
    )(plan["tbase"], plan["cnt"], plan["multi"], plan["zero_start"], plan["zero_cnt"], idx_t,
      jnp.broadcast_to((plan["tbase"] % SUBLANES).reshape(t // tm, N_EXPERTS, 1), (t // tm, N_EXPERTS, LANES)),
      xn2)


def _expert_kernel(be_ref, src_ref, nu_ref, live_ref, xs_ref, wg_hbm, wu_hbm, wd_hbm, bg_ref, bu_ref, bd_ref,
                   ys_ref, wf32, wbf, wsem):
    b = pl.program_id(0)
    e = be_ref[b]

    def weight_copies(expert, par):
        return [pltpu.make_async_copy(w.at[expert], wf32.at[par, m], wsem.at[par])
                for m, w in enumerate((wg_hbm, wu_hbm, wd_hbm))]

    @pl.when(b == 0)
    def _():
        for cp in weight_copies(e, e % 2):
            cp.start()

    @pl.when(b < nu_ref[0])
    def _():
        @pl.when((b == 0) | (e != be_ref[jnp.maximum(b - 1, 0)]))
        def _():
            par = e % 2
            for cp in weight_copies(e, par):
                cp.wait()

            @pl.when(e + 1 < N_EXPERTS)
            def _():
                for cp in weight_copies(e + 1, 1 - par):
                    cp.start()

            for m in range(3):
                wbf[m] = wf32[par, m].astype(BF16)

        def ffn(rows):
            x = _unpack_rows(xs_ref[rows, :])
            gt = jnp.minimum(jnp.dot(x, wbf[0], preferred_element_type=F32) + bg_ref[0], SWIGLU_LIMIT)
            up = jnp.clip(jnp.dot(x, wbf[1], preferred_element_type=F32) + bu_ref[0],
                          -SWIGLU_LIMIT, SWIGLU_LIMIT)
            hdn = (up + 1.0) * (gt * jax.nn.sigmoid(SWIGLU_ALPHA * gt))
            y = jnp.dot(hdn.astype(BF16), wbf[2], preferred_element_type=F32) + bd_ref[0]
            ys_ref[rows, :] = _pack_rows(y.astype(BF16).astype(F32))

        live = live_ref[b]
        for p, part in enumerate(EXP_PARTS):
            smaller = EXP_PARTS[p + 1] if p + 1 < len(EXP_PARTS) else -1
            cond = live > smaller
            if p > 0:
                cond = cond & (live <= part)

            @pl.when(cond)
            def _(part=part):
                if part > 0:
                    ffn(slice(0, part))
                if part < EXP_BM:
                    ys_ref[part:, :] = jnp.zeros((EXP_BM - part, HALF), U32)

    @pl.when(b >= nu_ref[0])
    def _():
        ys_ref[...] = jnp.zeros(ys_ref.shape, U32)


def _experts(plan, xs, w_gate, b_gate, w_up, b_up, w_down, b_down):
    cap = xs.shape[0]
    nb = cap // EXP_BM
    rows = lambda b, be, src, nu, live: (src[b], 0)
    wsel = lambda b, be, src, nu, live: (be[b], 0, 0)
    wspec = pl.BlockSpec(memory_space=pl.ANY)
    bspec = pl.BlockSpec((1, 1, D_FF), wsel)
    grid_spec = pltpu.PrefetchScalarGridSpec(
        num_scalar_prefetch=4,
        grid=(nb,),
        in_specs=[pl.BlockSpec((EXP_BM, HALF), rows), wspec, wspec, wspec, bspec, bspec, bspec],
        out_specs=pl.BlockSpec((EXP_BM, HALF), lambda b, be, src, nu, live: (b, 0)),
        scratch_shapes=[pltpu.VMEM((2, 3, D_MODEL, D_FF), F32),
                        pltpu.VMEM((3, D_MODEL, D_FF), BF16),
                        pltpu.SemaphoreType.DMA((2,))],
    )
    return pl.pallas_call(
        _expert_kernel,
        grid_spec=grid_spec,
        out_shape=jax.ShapeDtypeStruct((cap, HALF), U32),
        compiler_params=pltpu.CompilerParams(
            dimension_semantics=("arbitrary",), vmem_limit_bytes=_vmem_limit(56 << 20)),
        name="experts",
    )(plan["blk_exp"], plan["blk_src"], plan["n_used"], plan["blk_live"], xs, w_gate, w_up, w_down,
      b_gate.reshape(N_EXPERTS, 1, D_FF), b_up.reshape(N_EXPERTS, 1, D_FF),
      b_down.reshape(N_EXPERTS, 1, D_MODEL))


def _combine_kernel(tb_ref, cnt_ref, mt_ref, h_ref, idx_ref, loc_ref, gate_ref, ys_hbm, g_ref, o_ref,
                    ybuf, obuf, acc_ref, sem, osem):
    i = pl.program_id(0)
    nt = pl.num_programs(0)
    slot = i % 2
    tm = h_ref.shape[0]

    def geometry(step, e):
        return _chunk_geometry(tb_ref[step * N_EXPERTS + e], cnt_ref[step * N_EXPERTS + e])

    def chunk_copy(step, e, s):
        start, _, _ = geometry(step, e)
        return pltpu.make_async_copy(ys_hbm.at[pl.ds(start, ROUTE_RP)],
                                     ybuf.at[s, pl.ds(e * ROUTE_RP, ROUTE_RP)], sem.at[s])

    @pl.when(i == 0)
    def _():
        for e in range(N_EXPERTS):
            chunk_copy(0, e, 0).start()

    for e in range(N_EXPERTS):
        chunk_copy(jnp.minimum(i + 1, nt - 1), e, 1 - slot).start()

    idx_loc = jnp.transpose(jnp.concatenate([idx_ref[...], loc_ref[...]], axis=0))
    idx, loc = idx_loc[:, :TOP_K], idx_loc[:, TOP_K:]
    gate = jnp.transpose(jnp.concatenate([gate_ref[...], gate_ref[...]], axis=0))[:, :TOP_K]

    def gate_matrix(ncols, col_of):
        out = []
        for rows in _row_blocks(tm):
            colio = lax.broadcasted_iota(I32, (rows.stop - rows.start, ncols), 1)
            g = jnp.zeros(colio.shape, F32)
            for k in range(TOP_K):
                g = jnp.where(colio == col_of[rows, k:k + 1], gate[rows, k:k + 1], g)
            out.append(g.astype(BF16))
        return jnp.concatenate(out, axis=0)

    pltpu.make_async_copy(ys_hbm.at[pl.ds(0, N_EXPERTS * ROUTE_RP)], ybuf.at[0], sem.at[slot]).wait()
    col = jnp.where(loc < ROUTE_RP, idx * ROUTE_RP + loc, -1)
    acc_ref[...] = jnp.dot(gate_matrix(N_EXPERTS * ROUTE_RP, col), _unpack_rows(ybuf[slot]),
                           preferred_element_type=F32)

    def per_expert(e, carry):
        start, _, nchunks = geometry(i, e)

        def per_chunk(c, carry2):
            cp = pltpu.make_async_copy(
                ys_hbm.at[pl.ds(pl.multiple_of(start + c * ROUTE_RP, SUBLANES), ROUTE_RP)], obuf, osem)
            cp.start()
            cp.wait()
            ccol = jnp.where(idx == e, loc - c * ROUTE_RP, -1)
            acc_ref[...] += jnp.dot(gate_matrix(ROUTE_RP, ccol), _unpack_rows(obuf[...]),
                                    preferred_element_type=F32)
            return carry2
        return lax.fori_loop(1, nchunks, per_chunk, carry)

    @pl.when(mt_ref[i] > 0)
    def _():
        lax.fori_loop(0, N_EXPERTS, per_expert, 0)

    out = h_ref[...] + acc_ref[...]
    ms = jnp.mean(out * out, axis=-1, keepdims=True)
    o_ref[...] = out * lax.rsqrt(ms + EPS) * g_ref[...]

    @pl.when(i == nt - 1)
    def _():
        pltpu.make_async_copy(ys_hbm.at[pl.ds(0, N_EXPERTS * ROUTE_RP)], ybuf.at[0], sem.at[1 - slot]).wait()


def _combine(plan, h, ys, idx_t, loc_t, gate_t, g_final):
    t = h.shape[0]
    tm = ROUTE_TM
    tok = lambda i, *_: (i, 0)
    lane = lambda i, *_: (0, i)
    grid_spec = pltpu.PrefetchScalarGridSpec(
        num_scalar_prefetch=3,
        grid=(t // tm,),
        in_specs=[
            pl.BlockSpec((tm, D_MODEL), tok),
            pl.BlockSpec((TOP_K, tm), lane),
            pl.BlockSpec((TOP_K, tm), lane),
            pl.BlockSpec((TOP_K, tm), lane),
            pl.BlockSpec(memory_space=pl.ANY),
            pl.BlockSpec((1, D_MODEL), lambda i, *_: (0, 0)),
        ],
        out_specs=pl.BlockSpec((tm, D_MODEL), tok),
        scratch_shapes=[
            pltpu.VMEM((2, N_EXPERTS * ROUTE_RP, HALF), U32),
            pltpu.VMEM((ROUTE_RP, HALF), U32),
            pltpu.VMEM((tm, D_MODEL), F32),
            pltpu.SemaphoreType.DMA((2,)),
            pltpu.SemaphoreType.DMA(()),
        ],
    )
    return pl.pallas_call(
        _combine_kernel,
        grid_spec=grid_spec,
        out_shape=jax.ShapeDtypeStruct((t, D_MODEL), F32),
        compiler_params=pltpu.CompilerParams(
            dimension_semantics=("arbitrary",), vmem_limit_bytes=_vmem_limit(32 << 20)),
        name="combine",
    )(plan["tbase"], plan["cnt"], plan["multi"], h, idx_t, loc_t, gate_t, ys,
      g_final.reshape(1, D_MODEL))


def _moe(h, xn2, idx_t, gate_t, w_gate, b_gate, w_up, b_up, w_down, b_down, g_final):
    t = h.shape[0]
    plan = _route_plan(idx_t, t)
    xs, loc_t = _dispatch(plan, idx_t, xn2)
    ys = _experts(plan, xs, w_gate, b_gate, w_up, b_up, w_down, b_down)
    return _combine(plan, h, ys, idx_t, loc_t, gate_t, g_final)


def kernel(x, g_mix, w_in, b_in, conv_w, conv_b, ln_g, ln_b, rpb, w_out, b_out, g_ffn, w_router,
           b_router, w_gate, b_gate, w_up, b_up, w_down, b_down, g_final):
    bsz, seq, d = x.shape
    assert bsz == 1 and d == D_MODEL and g_mix.shape[0] == 1
    assert seq % (GRID_W * WIN_H) == 0
    x2 = x.reshape(seq, d)
    q, k, v, conv_out = _inproj_conv(x2, g_mix[0], w_in[0], b_in[0], conv_w[0], conv_b[0], ln_g[0], ln_b[0])
    na_out = _attention(q, k, v, rpb[0])
    h, xn2, idx_t, gate_t = _outproj_router(conv_out, na_out, x2, w_out[0], b_out[0], g_ffn[0],
                                            w_router[0], b_router[0])
    out = _moe(h, xn2, idx_t, gate_t, w_gate[0], b_gate[0], w_up[0], b_up[0], w_down[0], b_down[0],
               g_final)
    return out.reshape(bsz, seq, d)
```

```python
import functools

import numpy as np
import jax
import jax.numpy as jnp
from jax import lax
from jax.experimental import pallas as pl
from jax.experimental.pallas import tpu as pltpu

F32 = jnp.float32
BF16 = jnp.bfloat16
U32 = jnp.uint32
I32 = jnp.int32

D_MODEL = 1024
GRID_W = 64
D_CONV = 512
CONV_K = 31
NA_HEADS = 8
NA_HEAD_DIM = 64
D_NA = NA_HEADS * NA_HEAD_DIM
D_IN = 2 * D_CONV + 3 * D_NA
WIN_H = 8
WIN_W = 16
N_EXPERTS = 32
TOP_K = 4
D_FF = D_MODEL
SWIGLU_LIMIT = 7.0
SWIGLU_ALPHA = 1.702
EPS = 1e-5
MASK_VALUE = -1e30

V7X_VMEM_BYTES = 64 * 1024 * 1024
LANES = 128

PROJ_TM = 512
ROUTER_SUB = 512
CONV_HALO = 16
ATTN_RB = 32
ATTN_UNROLL = 16
ROUTE_TM = 256
SUBLANES = 8
ROUTE_RP = 56
EXP_BM = 512
EXP_PARTS = (512, 256, 128, 0)
ZERO_ROWS = 256
HALF = D_MODEL // 2
BF16_BITS = 16
HIGH_HALF = 0xFFFF0000


def _vmem_limit(nbytes):
    return int(min(nbytes, V7X_VMEM_BYTES - 6 * 1024 * 1024))


def _pack_rows(x):
    lo = lax.bitcast_convert_type(x[:, :HALF], U32) >> BF16_BITS
    hi = lax.bitcast_convert_type(x[:, HALF:], U32) & jnp.uint32(HIGH_HALF)
    return lo | hi


def _unpack_rows(w):
    lo = lax.bitcast_convert_type(w << BF16_BITS, F32)
    hi = lax.bitcast_convert_type(w & jnp.uint32(HIGH_HALF), F32)
    return jnp.concatenate([lo, hi], axis=1).astype(BF16)


def _inproj_conv_kernel(x_ref, g_ref, w_ref, b_ref, cw_ref, cb_ref, lg_ref, lb_ref,
                        q_ref, k_ref, v_ref, o_ref, ext_ref, u_ref, sh_ref):
    i = pl.program_id(0)
    nt = pl.num_programs(0) - 1
    tm = x_ref.shape[0]

    @pl.when(i == 0)
    def _():
        ext_ref[...] = jnp.zeros(ext_ref.shape, F32)

    x = x_ref[...]
    ms = jnp.mean(x * x, axis=-1, keepdims=True)
    xn = (x * lax.rsqrt(ms + EPS) * g_ref[...]).astype(BF16)

    def proj(c):
        sl = slice(c * D_CONV, (c + 1) * D_CONV)
        return jnp.dot(xn, w_ref[:, sl], preferred_element_type=F32) + b_ref[:, sl]

    u_ref[...] = proj(0) * jax.nn.sigmoid(proj(1))
    q_ref[...] = (proj(2) * (NA_HEAD_DIM ** -0.5)).astype(BF16)
    k_ref[...] = proj(3).astype(BF16)
    v_ref[...] = proj(4).astype(BF16)

    ext_ref[CONV_HALO + tm:, :] = jnp.where(i < nt, u_ref[0:CONV_HALO, :], 0.0)
    base = CONV_HALO - CONV_K // 2
    span = (CONV_K - 1) // SUBLANES * SUBLANES
    for b in range(SUBLANES):
        sh_ref[b] = ext_ref[base + b:base + b + tm + span, :]
    acc = jnp.zeros((tm, D_CONV), F32)
    for b in range(SUBLANES):
        for a in range((CONV_K - 1 - b) // SUBLANES + 1):
            r = SUBLANES * a
            acc = acc + sh_ref[b, r:r + tm, :] * cw_ref[r + b:r + b + 1, :]
    acc = acc + cb_ref[...]
    mu = jnp.mean(acc, axis=-1, keepdims=True)
    d = acc - mu
    var = jnp.mean(d * d, axis=-1, keepdims=True)
    un = d * lax.rsqrt(var + EPS) * lg_ref[...] + lb_ref[...]
    o_ref[...] = (un * jax.nn.sigmoid(un)).astype(o_ref.dtype)

    ext_ref[0:CONV_HALO, :] = ext_ref[tm:tm + CONV_HALO, :]
    ext_ref[CONV_HALO:CONV_HALO + tm, :] = u_ref[...]


def _inproj_conv(x2, g_mix, w_in, b_in, conv_w, conv_b, ln_g, ln_b):
    t = x2.shape[0]
    tm = min(PROJ_TM, t)
    nt = t // tm
    tok = lambda i: (jnp.minimum(i, nt - 1), 0)
    prev = lambda i: (jnp.maximum(i - 1, 0), 0)
    const = lambda i: (0, 0)
    w = jnp.zeros((CONV_K + 1, D_CONV), F32).at[:CONV_K].set(conv_w)
    span = (CONV_K - 1) // SUBLANES * SUBLANES
    return pl.pallas_call(
        _inproj_conv_kernel,
        grid=(nt + 1,),
        in_specs=[
            pl.BlockSpec((tm, D_MODEL), tok),
            pl.BlockSpec((1, D_MODEL), const),
            pl.BlockSpec((D_MODEL, D_IN), const),
            pl.BlockSpec((1, D_IN), const),
            pl.BlockSpec((CONV_K + 1, D_CONV), const),
            pl.BlockSpec((1, D_CONV), const),
            pl.BlockSpec((1, D_CONV), const),
            pl.BlockSpec((1, D_CONV), const),
        ],
        out_specs=[
            pl.BlockSpec((tm, D_NA), tok),
            pl.BlockSpec((tm, D_NA), tok),
            pl.BlockSpec((tm, D_NA), tok),
            pl.BlockSpec((tm, D_CONV), prev),
        ],
        out_shape=[
            jax.ShapeDtypeStruct((t, D_NA), BF16),
            jax.ShapeDtypeStruct((t, D_NA), BF16),
            jax.ShapeDtypeStruct((t, D_NA), BF16),
            jax.ShapeDtypeStruct((t, D_CONV), BF16),
        ],
        scratch_shapes=[pltpu.VMEM((tm + 2 * CONV_HALO, D_CONV), F32),
                        pltpu.VMEM((tm, D_CONV), F32),
                        pltpu.VMEM((SUBLANES, tm + span, D_CONV), F32)],
        compiler_params=pltpu.CompilerParams(
            dimension_semantics=("arbitrary",), vmem_limit_bytes=_vmem_limit(56 << 20)),
        name="inproj_conv",
    )(x2, g_mix.reshape(1, D_MODEL), w_in.astype(BF16), b_in.reshape(1, D_IN), w,
      conv_b.reshape(1, D_CONV), ln_g.reshape(1, D_CONV), ln_b.reshape(1, D_CONV))


def _column_bias(rpb):
    col = np.arange(GRID_W)
    c0 = np.clip(col - WIN_W // 2, 0, GRID_W - WIN_W)
    valid = (col[None, :] >= c0[:, None]) & (col[None, :] < c0[:, None] + WIN_W)
    dcol = np.clip(col[None, :] - col[:, None], -(WIN_W - 1), WIN_W - 1) + (WIN_W - 1)
    col_sel = ((dcol[:, :, None] == np.arange(2 * WIN_W - 1)) & valid[:, :, None]).astype(np.float32)
    b = jnp.einsum("hdj,qkj->hdqk", rpb.astype(F32), col_sel, precision=lax.Precision.HIGHEST)
    b = jnp.where(valid[None, None], b, MASK_VALUE)
    return jnp.concatenate([b, b], axis=-1)


def _attn_kernel(q_ref, k_ref, v_ref, cb_ref, o_ref, s_ref, bt_ref, *, rb, rows):
    jb = pl.program_id(1)
    lane = lax.broadcasted_iota(jnp.int32, (GRID_W, LANES), 1)
    first_head = lane < NA_HEAD_DIM
    nkeys = WIN_H * GRID_W

    @pl.when(jb == 0)
    def _():
        for h in range(2):
            for c in range(WIN_H):
                for rr in range(0, WIN_H, 2):
                    even = cb_ref[h, rr - c + WIN_H - 1]
                    odd = cb_ref[h, rr + 1 - c + WIN_H - 1]
                    bt_ref[h, c, :, rr * GRID_W:(rr + 2) * GRID_W] = jnp.where(first_head, even, odd)

    def window(r):
        i = jb * rb + r
        r0 = jnp.clip(i - WIN_H // 2, 0, rows - WIN_H)
        return i - r0, pl.multiple_of(r0 * GRID_W, GRID_W)

    def scores(r):
        _, start = window(r)
        q2 = q_ref[pl.ds(pl.multiple_of(r * GRID_W, GRID_W), GRID_W), :]
        zero = jnp.zeros_like(q2)
        qst = jnp.concatenate([jnp.where(first_head, q2, zero), jnp.where(first_head, zero, q2)], axis=0)
        ks = k_ref[pl.ds(start, nkeys), :]
        return lax.dot_general(qst, ks, (((1,), (1,)), ((), ())), preferred_element_type=F32)

    def finish(r, s):
        c, start = window(r)
        vs = v_ref[pl.ds(start, nkeys), :]
        bias = jnp.concatenate([bt_ref[0, c], bt_ref[1, c]], axis=0)
        s = jnp.where(bias > 0.5 * MASK_VALUE, s + bias, MASK_VALUE)
        m = jnp.max(s, axis=-1, keepdims=True)
        p = jnp.exp(s - m)
        l = jnp.sum(p, axis=-1, keepdims=True)
        o = jnp.dot(p.astype(BF16), vs, preferred_element_type=F32) / l
        o2 = jnp.where(first_head, o[:GRID_W], o[GRID_W:])
        o_ref[pl.ds(pl.multiple_of(r * GRID_W, GRID_W), GRID_W), :] = o2.astype(o_ref.dtype)

    s_ref[0] = scores(0)

    def pair(p, carry):
        r = 2 * p
        s_ref[1] = scores(r + 1)
        finish(r, s_ref[0])
        s_ref[0] = scores(jnp.minimum(r + 2, rb - 1))
        finish(r + 1, s_ref[1])
        return carry

    lax.fori_loop(0, rb // 2, pair, 0, unroll=ATTN_UNROLL)


def _attention(q, k, v, rpb):
    t = q.shape[0]
    rows = t // GRID_W
    rb = min(ATTN_RB, rows)
    npairs = D_NA // LANES
    cb = _column_bias(rpb)
    kern = functools.partial(_attn_kernel, rb=rb, rows=rows)
    return pl.pallas_call(
        kern,
        grid=(npairs, rows // rb),
        in_specs=[
            pl.BlockSpec((rb * GRID_W, LANES), lambda p, j: (j, p)),
            pl.BlockSpec((t, LANES), lambda p, j: (0, p)),
            pl.BlockSpec((t, LANES), lambda p, j: (0, p)),
            pl.BlockSpec((2, 2 * WIN_H - 1, GRID_W, 2 * GRID_W), lambda p, j: (p, 0, 0, 0)),
        ],
        out_specs=pl.BlockSpec((rb * GRID_W, LANES), lambda p, j: (j, p)),
        out_shape=jax.ShapeDtypeStruct((t, D_NA), BF16),
        scratch_shapes=[pltpu.VMEM((2, 2 * GRID_W, WIN_H * GRID_W), F32),
                        pltpu.VMEM((2, WIN_H, GRID_W, WIN_H * GRID_W), F32)],
        compiler_params=pltpu.CompilerParams(
            dimension_semantics=("arbitrary", "arbitrary"), vmem_limit_bytes=_vmem_limit(40 << 20)),
        name="attention",
    )(q, k, v, cb)


def _outproj_kernel(conv_ref, na_ref, x_ref, wo_ref, bo_ref, g_ref, wr_ref, br_ref,
                    h_ref, xn_ref, idx_ref, gate_ref):
    for s in range(x_ref.shape[0] // ROUTER_SUB):
        rows = slice(s * ROUTER_SUB, (s + 1) * ROUTER_SUB)
        mixed = jnp.concatenate([conv_ref[rows, :], na_ref[rows, :]], axis=-1)
        h = x_ref[rows, :] + jnp.dot(mixed, wo_ref[...], preferred_element_type=F32) + bo_ref[...]
        h_ref[rows, :] = h
        ms = jnp.mean(h * h, axis=-1, keepdims=True)
        xn = (h * lax.rsqrt(ms + EPS) * g_ref[...]).astype(BF16)
        xn_ref[rows, :] = xn
        logits = jnp.transpose(jnp.dot(xn, wr_ref[...], preferred_element_type=F32) + br_ref[...])[:N_EXPERTS]
        ids = lax.broadcasted_iota(jnp.int32, logits.shape, 0)
        vals, sels = [], []
        l = logits
        for _ in range(TOP_K):
            m = jnp.max(l, axis=0, keepdims=True)
            sel = jnp.min(jnp.where(l == m, ids, N_EXPERTS), axis=0, keepdims=True)
            vals.append(m)
            sels.append(sel)
            l = jnp.where(ids == sel, -jnp.inf, l)
        es = [jnp.exp(vk - vals[0]) for vk in vals]
        tot = es[0] + es[1] + es[2] + es[3]
        idx_ref[:, rows] = jnp.concatenate(sels, axis=0)
        gate_ref[:, rows] = jnp.concatenate([e / tot for e in es], axis=0)


def _outproj_router(conv_out, na_out, x2, w_out, b_out, g_ffn, w_router, b_router):
    t = x2.shape[0]
    tm = min(PROJ_TM, t)
    tok = lambda i: (i, 0)
    const = lambda i: (0, 0)
    return pl.pallas_call(
        _outproj_kernel,
        grid=(t // tm,),
        in_specs=[
            pl.BlockSpec((tm, D_CONV), tok),
            pl.BlockSpec((tm, D_NA), tok),
            pl.BlockSpec((tm, D_MODEL), tok),
            pl.BlockSpec((D_MODEL, D_MODEL), const),
            pl.BlockSpec((1, D_MODEL), const),
            pl.BlockSpec((1, D_MODEL), const),
            pl.BlockSpec((D_MODEL, LANES), const),
            pl.BlockSpec((1, LANES), const),
        ],
        out_specs=[
            pl.BlockSpec((tm, D_MODEL), tok),
            pl.BlockSpec((tm, D_MODEL), tok),
            pl.BlockSpec((TOP_K, tm), lambda i: (0, i)),
            pl.BlockSpec((TOP_K, tm), lambda i: (0, i)),
        ],
        out_shape=[
            jax.ShapeDtypeStruct((t, D_MODEL), F32),
            jax.ShapeDtypeStruct((t, D_MODEL), BF16),
            jax.ShapeDtypeStruct((TOP_K, t), jnp.int32),
            jax.ShapeDtypeStruct((TOP_K, t), F32),
        ],
        compiler_params=pltpu.CompilerParams(
            dimension_semantics=("arbitrary",), vmem_limit_bytes=_vmem_limit(48 << 20)),
        name="outproj_router",
    )(conv_out, na_out, x2, w_out.astype(BF16), b_out.reshape(1, D_MODEL), g_ffn.reshape(1, D_MODEL),
      jnp.pad(w_router.astype(BF16), ((0, 0), (0, LANES - N_EXPERTS))),
      jnp.pad(b_router.reshape(1, N_EXPERTS), ((0, 0), (0, LANES - N_EXPERTS))))


def _route_plan(idx_t, t):
    nt = t // ROUTE_TM
    experts = jnp.arange(N_EXPERTS, dtype=I32)
    onehot = idx_t.reshape(TOP_K, nt, ROUTE_TM, 1) == experts
    cnt = jnp.sum(onehot, axis=(0, 2), dtype=I32)
    sizes = jnp.sum(cnt, axis=0)
    padded = (sizes + ROUTE_RP + EXP_BM - 1) // EXP_BM * EXP_BM
    pad_ends = jnp.cumsum(padded)
    pad_off = pad_ends - padded
    tbase = pad_off[None, :] + jnp.cumsum(cnt, axis=0) - cnt
    cap = _sorted_rows(t)
    nb = cap // EXP_BM
    n_used = pad_ends[N_EXPERTS - 1] // EXP_BM
    blk_src = jnp.minimum(jnp.arange(nb, dtype=I32), n_used - 1)
    blk_exp = jnp.minimum(
        jnp.sum(blk_src[:, None] * EXP_BM >= pad_ends[None, :], axis=1, dtype=I32), N_EXPERTS - 1)
    row_end = jnp.sum(jnp.where(blk_exp[:, None] == experts[None, :], (pad_off + sizes)[None, :], 0), axis=1)
    blk_live = jnp.clip(row_end - blk_src * EXP_BM, 0, EXP_BM).astype(I32)
    multi = jnp.any(tbase % SUBLANES + cnt > ROUTE_RP, axis=1).astype(I32)
    zero_from = jnp.concatenate([(pad_off + sizes) // ZERO_ROWS * ZERO_ROWS, pad_ends[N_EXPERTS - 1:]])
    zero_to = jnp.concatenate([pad_ends, jnp.full((1,), cap, I32)])
    return dict(cnt=cnt.reshape(-1), tbase=tbase.reshape(-1).astype(I32), multi=multi,
                zero_start=zero_from.astype(I32), zero_cnt=((zero_to - zero_from) // ZERO_ROWS).astype(I32),
                blk_src=blk_src, blk_exp=blk_exp, blk_live=blk_live, n_used=n_used.reshape(1).astype(I32))


def _sorted_rows(t):
    cap = t * TOP_K + N_EXPERTS * (EXP_BM + ROUTE_RP)
    return (cap + EXP_BM - 1) // EXP_BM * EXP_BM


def _chunk_geometry(tb, n):
    head = tb & (SUBLANES - 1)
    start = pl.multiple_of(tb - head, SUBLANES)
    nchunks = (head + n + ROUTE_RP - 1) // ROUTE_RP
    return start, head, nchunks


def _dispatch_kernel(tb_ref, cnt_ref, mt_ref, zs_ref, zc_ref, idx_ref, tbv_ref, xn_ref, xs_hbm, loc_ref,
                     stage, ostage, zbuf, head_ref, cs_ref, ms_ref, sem, osem, zsem):
    i = pl.program_id(0)
    nt = pl.num_programs(0)
    slot = i % 2
    tm = idx_ref.shape[1]
    groups = ROUTE_RP // SUBLANES

    def geometry(e, step=None):
        step = i if step is None else step
        return _chunk_geometry(tb_ref[step * N_EXPERTS + e], cnt_ref[step * N_EXPERTS + e])

    def chunk_copy(step, e):
        start, _, _ = geometry(e, step)
        return pltpu.make_async_copy(stage.at[step % 2, pl.ds(e * ROUTE_RP, ROUTE_RP)],
                                     xs_hbm.at[pl.ds(start, ROUTE_RP)], sem.at[step % 2])

    def all_chunks(step, phase):
        if phase == "wait":
            pltpu.make_async_copy(stage.at[0], xs_hbm.at[pl.ds(0, N_EXPERTS * ROUTE_RP)], sem.at[step % 2]).wait()
            return
        for e in range(N_EXPERTS):
            chunk_copy(step, e).start()

    @pl.when(i >= 2)
    def _():
        all_chunks(i - 2, "wait")

    @pl.when(i >= 1)
    def _():
        all_chunks(i - 1, "start")

    def zero_copy(start):
        return pltpu.make_async_copy(zbuf, xs_hbm.at[pl.ds(pl.multiple_of(start, ZERO_ROWS), ZERO_ROWS)], zsem)

    @pl.when(i == 0)
    def _():
        zbuf[...] = jnp.zeros(zbuf.shape, U32)
        head_ref[...] = jnp.zeros(head_ref.shape, U32)
        for phase in ("start", "wait"):
            for j in range(N_EXPERTS + 1):
                def piece(q, carry, j=j, phase=phase):
                    getattr(zero_copy(zs_ref[j] + q * ZERO_ROWS), phase)()
                    return carry
                lax.fori_loop(0, zc_ref[j], piece, 0)

    idx = idx_ref[...]
    eio = lax.broadcasted_iota(I32, (N_EXPERTS, tm), 0)
    member = jnp.zeros((N_EXPERTS, tm), F32)
    for k in range(TOP_K):
        member = member + (idx[k:k + 1, :] == eio).astype(F32)
    tri = (lax.broadcasted_iota(I32, (tm, tm), 0) < lax.broadcasted_iota(I32, (tm, tm), 1)).astype(BF16)
    rank = jnp.dot(member.astype(BF16), tri, preferred_element_type=F32)
    pos = rank + tbv_ref[0][:, 0:1].astype(F32)
    loc_ref[...] = jnp.concatenate(
        [jnp.sum(jnp.where(idx[k:k + 1, :] == eio, pos, 0.0), axis=0, keepdims=True) for k in range(TOP_K)],
        axis=0).astype(I32)
    cs_ref[...] = pos
    ms_ref[...] = member

    jio = lax.broadcasted_iota(I32, (ROUTE_RP, tm), 0).astype(F32)
    sel = [jnp.where((jio == pos[e:e + 1, :]) & (member[e:e + 1, :] > 0.0), 1.0, 0.0).astype(BF16)
           for e in range(N_EXPERTS)]
    rows = jnp.dot(jnp.concatenate(sel, axis=0), xn_ref[...], preferred_element_type=F32)

    stage[slot] = _pack_rows(rows)
    sub = lax.broadcasted_iota(I32, (SUBLANES, HALF), 0)
    for e in range(N_EXPERTS):
        _, head, _ = geometry(e)
        first = pl.ds(e * ROUTE_RP, SUBLANES)
        stage[slot, first, :] = jnp.where(sub < head, head_ref[e], stage[slot, first, :])
        g = jnp.minimum(lax.shift_right_logical(head + cnt_ref[i * N_EXPERTS + e], SUBLANES.bit_length() - 1),
                        groups - 1)
        head_ref[e] = stage[slot, pl.ds(pl.multiple_of(e * ROUTE_RP + g * SUBLANES, SUBLANES), SUBLANES), :]

    def per_expert(e, carry):
        start, head, nchunks = geometry(e)
        end = head + cnt_ref[i * N_EXPERTS + e]

        def per_chunk(c, carry2):
            want = jio + (c * ROUTE_RP).astype(F32)
            pick = jnp.where((want == cs_ref[pl.ds(e, 1), :]) & (ms_ref[pl.ds(e, 1), :] > 0.0), 1.0, 0.0)
            ostage[...] = _pack_rows(jnp.dot(pick.astype(BF16), xn_ref[...], preferred_element_type=F32))
            cp = pltpu.make_async_copy(
                ostage, xs_hbm.at[pl.ds(pl.multiple_of(start + c * ROUTE_RP, SUBLANES), ROUTE_RP)], osem)
            cp.start()
            cp.wait()

            @pl.when(end // ROUTE_RP == c)
            def _():
                g = (end - c * ROUTE_RP) // SUBLANES
                head_ref[e] = ostage[pl.ds(pl.multiple_of(g * SUBLANES, SUBLANES), SUBLANES), :]
            return carry2
        return lax.fori_loop(1, nchunks, per_chunk, carry)

    @pl.when(mt_ref[i] > 0)
    def _():
        lax.fori_loop(0, N_EXPERTS, per_expert, 0)

    @pl.when(i == nt - 1)
    def _():
        @pl.when(i >= 1)
        def _():
            all_chunks(i - 1, "wait")
        all_chunks(i, "start")
        all_chunks(i, "wait")


def _dispatch(plan, idx_t, xn2):
    t = xn2.shape[0]
    tm = ROUTE_TM
    cap = _sorted_rows(t)
    grid_spec = pltpu.PrefetchScalarGridSpec(
        num_scalar_prefetch=5,
        grid=(t // tm,),
        in_specs=[
            pl.BlockSpec((TOP_K, tm), lambda i, *_: (0, i)),
            pl.BlockSpec((1, N_EXPERTS, LANES), lambda i, *_: (i, 0, 0)),
            pl.BlockSpec((tm, D_MODEL), lambda i, *_: (i, 0)),
        ],
        out_specs=[
            pl.BlockSpec(memory_space=pl.ANY),
            pl.BlockSpec((TOP_K, tm), lambda i, *_: (0, i)),
        ],
        scratch_shapes=[
            pltpu.VMEM((2, N_EXPERTS * ROUTE_RP, HALF), U32),
            pltpu.VMEM((ROUTE_RP, HALF), U32),
            pltpu.VMEM((ZERO_ROWS, HALF), U32),
            pltpu.VMEM((N_EXPERTS, SUBLANES, HALF), U32),
            pltpu.VMEM((N_EXPERTS, tm), F32),
            pltpu.VMEM((N_EXPERTS, tm), F32),
            pltpu.SemaphoreType.DMA((2,)),
            pltpu.SemaphoreType.DMA(()),
            pltpu.SemaphoreType.DMA(()),
        ],
    )
    return pl.pallas_call(
        _dispatch_kernel,
        grid_spec=grid_spec,
        out_shape=[
            jax.ShapeDtypeStruct((cap, HALF), U32),
            jax.ShapeDtypeStruct((TOP_K, t), I32),
        ],
        compiler_params=pltpu.CompilerParams(
            dimension_semantics=("arbitrary",), vmem_limit_bytes=_vmem_limit(32 << 20)),
        name="dispatch",
    )(plan["tbase"], plan["cnt"], plan["multi"], plan["zero_start"], plan["zero_cnt"], idx_t,
      jnp.broadcast_to((plan["tbase"] % SUBLANES).reshape(t // tm, N_EXPERTS, 1), (t // tm, N_EXPERTS, LANES)),
      xn2)


def _expert_kernel(be_ref, src_ref, nu_ref, live_ref, xs_ref, wg_hbm, wu_hbm, wd_hbm, bg_ref, bu_ref, bd_ref,
                   ys_ref, wf32, wbf, wsem):
    b = pl.program_id(0)
    e = be_ref[b]

    def weight_copies(expert, par):
        return [pltpu.make_async_copy(w.at[expert], wf32.at[par, m], wsem.at[par])
                for m, w in enumerate((wg_hbm, wu_hbm, wd_hbm))]

    @pl.when(b == 0)
    def _():
        for cp in weight_copies(e, e % 2):
            cp.start()

    @pl.when(b < nu_ref[0])
    def _():
        @pl.when((b == 0) | (e != be_ref[jnp.maximum(b - 1, 0)]))
        def _():
            par = e % 2
            for cp in weight_copies(e, par):
                cp.wait()

            @pl.when(e + 1 < N_EXPERTS)
            def _():
                for cp in weight_copies(e + 1, 1 - par):
                    cp.start()

            for m in range(3):
                wbf[m] = wf32[par, m].astype(BF16)

        def ffn(rows):
            x = _unpack_rows(xs_ref[rows, :])
            gt = jnp.minimum(jnp.dot(x, wbf[0], preferred_element_type=F32) + bg_ref[0], SWIGLU_LIMIT)
            up = jnp.clip(jnp.dot(x, wbf[1], preferred_element_type=F32) + bu_ref[0],
                          -SWIGLU_LIMIT, SWIGLU_LIMIT)
            hdn = (up + 1.0) * (gt * jax.nn.sigmoid(SWIGLU_ALPHA * gt))
            y = jnp.dot(hdn.astype(BF16), wbf[2], preferred_element_type=F32) + bd_ref[0]
            ys_ref[rows, :] = _pack_rows(y.astype(BF16).astype(F32))

        live = live_ref[b]
        for p, part in enumerate(EXP_PARTS):
            smaller = EXP_PARTS[p + 1] if p + 1 < len(EXP_PARTS) else -1
            cond = live > smaller
            if p > 0:
                cond = cond & (live <= part)

            @pl.when(cond)
            def _(part=part):
                if part > 0:
                    ffn(slice(0, part))
                if part < EXP_BM:
                    ys_ref[part:, :] = jnp.zeros((EXP_BM - part, HALF), U32)

    @pl.when(b >= nu_ref[0])
    def _():
        ys_ref[...] = jnp.zeros(ys_ref.shape, U32)


def _experts(plan, xs, w_gate, b_gate, w_up, b_up, w_down, b_down):
    cap = xs.shape[0]
    nb = cap // EXP_BM
    rows = lambda b, be, src, nu, live: (src[b], 0)
    wsel = lambda b, be, src, nu, live: (be[b], 0, 0)
    wspec = pl.BlockSpec(memory_space=pl.ANY)
    bspec = pl.BlockSpec((1, 1, D_FF), wsel)
    grid_spec = pltpu.PrefetchScalarGridSpec(
        num_scalar_prefetch=4,
        grid=(nb,),
        in_specs=[pl.BlockSpec((EXP_BM, HALF), rows), wspec, wspec, wspec, bspec, bspec, bspec],
        out_specs=pl.BlockSpec((EXP_BM, HALF), lambda b, be, src, nu, live: (b, 0)),
        scratch_shapes=[pltpu.VMEM((2, 3, D_MODEL, D_FF), F32),
                        pltpu.VMEM((3, D_MODEL, D_FF), BF16),
                        pltpu.SemaphoreType.DMA((2,))],
    )
    return pl.pallas_call(
        _expert_kernel,
        grid_spec=grid_spec,
        out_shape=jax.ShapeDtypeStruct((cap, HALF), U32),
        compiler_params=pltpu.CompilerParams(
            dimension_semantics=("arbitrary",), vmem_limit_bytes=_vmem_limit(56 << 20)),
        name="experts",
    )(plan["blk_exp"], plan["blk_src"], plan["n_used"], plan["blk_live"], xs, w_gate, w_up, w_down,
      b_gate.reshape(N_EXPERTS, 1, D_FF), b_up.reshape(N_EXPERTS, 1, D_FF),
      b_down.reshape(N_EXPERTS, 1, D_MODEL))


def _combine_kernel(tb_ref, cnt_ref, mt_ref, h_ref, idx_ref, loc_ref, gate_ref, ys_hbm, g_ref, o_ref,
                    ybuf, obuf, acc_ref, sem, osem):
    i = pl.program_id(0)
    nt = pl.num_programs(0)
    slot = i % 2
    tm = h_ref.shape[0]

    def geometry(step, e):
        return _chunk_geometry(tb_ref[step * N_EXPERTS + e], cnt_ref[step * N_EXPERTS + e])

    def chunk_copy(step, e, s):
        start, _, _ = geometry(step, e)
        return pltpu.make_async_copy(ys_hbm.at[pl.ds(start, ROUTE_RP)],
                                     ybuf.at[s, pl.ds(e * ROUTE_RP, ROUTE_RP)], sem.at[s])

    @pl.when(i == 0)
    def _():
        for e in range(N_EXPERTS):
            chunk_copy(0, e, 0).start()

    for e in range(N_EXPERTS):
        chunk_copy(jnp.minimum(i + 1, nt - 1), e, 1 - slot).start()

    idx_loc = jnp.transpose(jnp.concatenate([idx_ref[...], loc_ref[...]], axis=0))
    idx, loc = idx_loc[:, :TOP_K], idx_loc[:, TOP_K:]
    gate = jnp.transpose(jnp.concatenate([gate_ref[...], gate_ref[...]], axis=0))[:, :TOP_K]

    def gate_matrix(ncols, col_of):
        colio = lax.broadcasted_iota(I32, (tm, ncols), 1)
        g = jnp.zeros((tm, ncols), F32)
        for k in range(TOP_K):
            g = jnp.where(colio == col_of[:, k:k + 1], gate[:, k:k + 1], g)
        return g.astype(BF16)

    pltpu.make_async_copy(ys_hbm.at[pl.ds(0, N_EXPERTS * ROUTE_RP)], ybuf.at[0], sem.at[slot]).wait()
    col = jnp.where(loc < ROUTE_RP, idx * ROUTE_RP + loc, -1)
    acc_ref[...] = jnp.dot(gate_matrix(N_EXPERTS * ROUTE_RP, col), _unpack_rows(ybuf[slot]),
                           preferred_element_type=F32)

    def per_expert(e, carry):
        start, _, nchunks = geometry(i, e)

        def per_chunk(c, carry2):
            cp = pltpu.make_async_copy(
                ys_hbm.at[pl.ds(pl.multiple_of(start + c * ROUTE_RP, SUBLANES), ROUTE_RP)], obuf, osem)
            cp.start()
            cp.wait()
            ccol = jnp.where(idx == e, loc - c * ROUTE_RP, -1)
            acc_ref[...] += jnp.dot(gate_matrix(ROUTE_RP, ccol), _unpack_rows(obuf[...]),
                                    preferred_element_type=F32)
            return carry2
        return lax.fori_loop(1, nchunks, per_chunk, carry)

    @pl.when(mt_ref[i] > 0)
    def _():
        lax.fori_loop(0, N_EXPERTS, per_expert, 0)

    out = h_ref[...] + acc_ref[...]
    ms = jnp.mean(out * out, axis=-1, keepdims=True)
    o_ref[...] = out * lax.rsqrt(ms + EPS) * g_ref[...]

    @pl.when(i == nt - 1)
    def _():
        pltpu.make_async_copy(ys_hbm.at[pl.ds(0, N_EXPERTS * ROUTE_RP)], ybuf.at[0], sem.at[1 - slot]).wait()


def _combine(plan, h, ys, idx_t, loc_t, gate_t, g_final):
    t = h.shape[0]
    tm = ROUTE_TM
    tok = lambda i, *_: (i, 0)
    lane = lambda i, *_: (0, i)
    grid_spec = pltpu.PrefetchScalarGridSpec(
        num_scalar_prefetch=3,
        grid=(t // tm,),
        in_specs=[
            pl.BlockSpec((tm, D_MODEL), tok),
            pl.BlockSpec((TOP_K, tm), lane),
            pl.BlockSpec((TOP_K, tm), lane),
            pl.BlockSpec((TOP_K, tm), lane),
            pl.BlockSpec(memory_space=pl.ANY),
            pl.BlockSpec((1, D_MODEL), lambda i, *_: (0, 0)),
        ],
        out_specs=pl.BlockSpec((tm, D_MODEL), tok),
        scratch_shapes=[
            pltpu.VMEM((2, N_EXPERTS * ROUTE_RP, HALF), U32),
            pltpu.VMEM((ROUTE_RP, HALF), U32),
            pltpu.VMEM((tm, D_MODEL), F32),
            pltpu.SemaphoreType.DMA((2,)),
            pltpu.SemaphoreType.DMA(()),
        ],
    )
    return pl.pallas_call(
        _combine_kernel,
        grid_spec=grid_spec,
        out_shape=jax.ShapeDtypeStruct((t, D_MODEL), F32),
        compiler_params=pltpu.CompilerParams(
            dimension_semantics=("arbitrary",), vmem_limit_bytes=_vmem_limit(32 << 20)),
        name="combine",
    )(plan["tbase"], plan["cnt"], plan["multi"], h, idx_t, loc_t, gate_t, ys,
      g_final.reshape(1, D_MODEL))


def _moe(h, xn2, idx_t, gate_t, w_gate, b_gate, w_up, b_up, w_down, b_down, g_final):
    t = h.shape[0]
    plan = _route_plan(idx_t, t)
    xs, loc_t = _dispatch(plan, idx_t, xn2)
    ys = _experts(plan, xs, w_gate, b_gate, w_up, b_up, w_down, b_down)
    return _combine(plan, h, ys, idx_t, loc_t, gate_t, g_final)


def kernel(x, g_mix, w_in, b_in, conv_w, conv_b, ln_g, ln_b, rpb, w_out, b_out, g_ffn, w_router,
           b_router, w_gate, b_gate, w_up, b_up, w_down, b_down, g_final):
    bsz, seq, d = x.shape
    assert bsz == 1 and d == D_MODEL and g_mix.shape[0] == 1
    assert seq % (GRID_W * WIN_H) == 0
    x2 = x.reshape(seq, d)
    q, k, v, conv_out = _inproj_conv(x2, g_mix[0], w_in[0], b_in[0], conv_w[0], conv_b[0], ln_g[0], ln_b[0])
    na_out = _attention(q, k, v, rpb[0])
    h, xn2, idx_t, gate_t = _outproj_router(conv_out, na_out, x2, w_out[0], b_out[0], g_ffn[0],
                                            w_router[0], b_router[0])
    out = _moe(h, xn2, idx_t, gate_t, w_gate[0], b_gate[0], w_up[0], b_up[0], w_down[0], b_down[0],
               g_final)
    return out.reshape(bsz, seq, d)
```

```python
import functools

import numpy as np
import jax
import jax.numpy as jnp
from jax import lax
from jax.experimental import pallas as pl
from jax.experimental.pallas import tpu as pltpu

F32 = jnp.float32
BF16 = jnp.bfloat16
U32 = jnp.uint32
I32 = jnp.int32

D_MODEL = 1024
GRID_W = 64
D_CONV = 512
CONV_K = 31
NA_HEADS = 8
NA_HEAD_DIM = 64
D_NA = NA_HEADS * NA_HEAD_DIM
D_IN = 2 * D_CONV + 3 * D_NA
WIN_H = 8
WIN_W = 16
N_EXPERTS = 32
TOP_K = 4
D_FF = D_MODEL
SWIGLU_LIMIT = 7.0
SWIGLU_ALPHA = 1.702
EPS = 1e-5
MASK_VALUE = -1e30

V7X_VMEM_BYTES = 64 * 1024 * 1024
LANES = 128

PROJ_TM = 1024
OUTPROJ_TM = 1024
ROUTER_SUB = 512
CONV_HALO = 16
ATTN_RB = 32
ATTN_UNROLL = 16
ROUTE_TM = 256
SUBLANES = 8
ROUTE_RP = 56
EXP_BM = 512
EXP_PARTS = (512, 256, 128, 0)
ZERO_ROWS = 256
HALF = D_MODEL // 2
BF16_BITS = 16
HIGH_HALF = 0xFFFF0000


def _vmem_limit(nbytes):
    return int(min(nbytes, V7X_VMEM_BYTES - 6 * 1024 * 1024))


def _pack_rows(x):
    lo = lax.bitcast_convert_type(x[:, :HALF], U32) >> BF16_BITS
    hi = lax.bitcast_convert_type(x[:, HALF:], U32) & jnp.uint32(HIGH_HALF)
    return lo | hi


def _unpack_rows(w):
    lo = lax.bitcast_convert_type(w << BF16_BITS, F32)
    hi = lax.bitcast_convert_type(w & jnp.uint32(HIGH_HALF), F32)
    return jnp.concatenate([lo, hi], axis=1).astype(BF16)


def _inproj_conv_kernel(x_ref, g_ref, w_ref, b_ref, cw_ref, cb_ref, lg_ref, lb_ref,
                        q_ref, k_ref, v_ref, o_ref, ext_ref, u_ref, sh_ref):
    i = pl.program_id(0)
    nt = pl.num_programs(0) - 1
    tm = x_ref.shape[0]

    @pl.when(i == 0)
    def _():
        ext_ref[...] = jnp.zeros(ext_ref.shape, F32)

    @pl.when(i < nt)
    def _():
        x = x_ref[...]
        ms = jnp.mean(x * x, axis=-1, keepdims=True)
        xn = (x * lax.rsqrt(ms + EPS) * g_ref[...]).astype(BF16)

        def proj(c):
            sl = slice(c * D_CONV, (c + 1) * D_CONV)
            return jnp.dot(xn, w_ref[:, sl], preferred_element_type=F32) + b_ref[:, sl]

        u_ref[...] = proj(0) * jax.nn.sigmoid(proj(1))
        q_ref[...] = (proj(2) * (NA_HEAD_DIM ** -0.5)).astype(BF16)
        k_ref[...] = proj(3).astype(BF16)
        v_ref[...] = proj(4).astype(BF16)

    ext_ref[CONV_HALO + tm:, :] = jnp.where(i < nt, u_ref[0:CONV_HALO, :], 0.0)

    @pl.when(i >= 1)
    def _():
        base = CONV_HALO - CONV_K // 2
        span = (CONV_K - 1) // SUBLANES * SUBLANES
        for b in range(SUBLANES):
            sh_ref[b] = ext_ref[base + b:base + b + tm + span, :]
        acc = jnp.zeros((tm, D_CONV), F32)
        for b in range(SUBLANES):
            for a in range((CONV_K - 1 - b) // SUBLANES + 1):
                r = SUBLANES * a
                acc = acc + sh_ref[b, r:r + tm, :] * cw_ref[r + b:r + b + 1, :]
        acc = acc + cb_ref[...]
        mu = jnp.mean(acc, axis=-1, keepdims=True)
        d = acc - mu
        var = jnp.mean(d * d, axis=-1, keepdims=True)
        un = d * lax.rsqrt(var + EPS) * lg_ref[...] + lb_ref[...]
        o_ref[...] = (un * jax.nn.sigmoid(un)).astype(o_ref.dtype)

    ext_ref[0:CONV_HALO, :] = ext_ref[tm:tm + CONV_HALO, :]
    ext_ref[CONV_HALO:CONV_HALO + tm, :] = u_ref[...]


def _inproj_conv(x2, g_mix, w_in, b_in, conv_w, conv_b, ln_g, ln_b):
    t = x2.shape[0]
    tm = min(PROJ_TM, t)
    nt = t // tm
    tok = lambda i: (jnp.minimum(i, nt - 1), 0)
    prev = lambda i: (jnp.maximum(i - 1, 0), 0)
    const = lambda i: (0, 0)
    w = jnp.zeros((CONV_K + 1, D_CONV), F32).at[:CONV_K].set(conv_w)
    span = (CONV_K - 1) // SUBLANES * SUBLANES
    return pl.pallas_call(
        _inproj_conv_kernel,
        grid=(nt + 1,),
        in_specs=[
            pl.BlockSpec((tm, D_MODEL), tok),
            pl.BlockSpec((1, D_MODEL), const),
            pl.BlockSpec((D_MODEL, D_IN), const),
            pl.BlockSpec((1, D_IN), const),
            pl.BlockSpec((CONV_K + 1, D_CONV), const),
            pl.BlockSpec((1, D_CONV), const),
            pl.BlockSpec((1, D_CONV), const),
            pl.BlockSpec((1, D_CONV), const),
        ],
        out_specs=[
            pl.BlockSpec((tm, D_NA), tok),
            pl.BlockSpec((tm, D_NA), tok),
            pl.BlockSpec((tm, D_NA), tok),
            pl.BlockSpec((tm, D_CONV), prev),
        ],
        out_shape=[
            jax.ShapeDtypeStruct((t, D_NA), BF16),
            jax.ShapeDtypeStruct((t, D_NA), BF16),
            jax.ShapeDtypeStruct((t, D_NA), BF16),
            jax.ShapeDtypeStruct((t, D_CONV), BF16),
        ],
        scratch_shapes=[pltpu.VMEM((tm + 2 * CONV_HALO, D_CONV), F32),
                        pltpu.VMEM((tm, D_CONV), F32),
                        pltpu.VMEM((SUBLANES, tm + span, D_CONV), F32)],
        compiler_params=pltpu.CompilerParams(
            dimension_semantics=("arbitrary",), vmem_limit_bytes=_vmem_limit(56 << 20)),
        name="inproj_conv",
    )(x2, g_mix.reshape(1, D_MODEL), w_in.astype(BF16), b_in.reshape(1, D_IN), w,
      conv_b.reshape(1, D_CONV), ln_g.reshape(1, D_CONV), ln_b.reshape(1, D_CONV))


def _column_bias(rpb):
    col = np.arange(GRID_W)
    c0 = np.clip(col - WIN_W // 2, 0, GRID_W - WIN_W)
    valid = (col[None, :] >= c0[:, None]) & (col[None, :] < c0[:, None] + WIN_W)
    dcol = np.clip(col[None, :] - col[:, None], -(WIN_W - 1), WIN_W - 1) + (WIN_W - 1)
    col_sel = ((dcol[:, :, None] == np.arange(2 * WIN_W - 1)) & valid[:, :, None]).astype(np.float32)
    b = jnp.einsum("hdj,qkj->hdqk", rpb.astype(F32), col_sel, precision=lax.Precision.HIGHEST)
    b = jnp.where(valid[None, None], b, MASK_VALUE)
    return jnp.concatenate([b, b], axis=-1)


def _attn_kernel(q_ref, k_ref, v_ref, cb_ref, o_ref, s_ref, bt_ref, *, rb, rows):
    jb = pl.program_id(1)
    lane = lax.broadcasted_iota(jnp.int32, (GRID_W, LANES), 1)
    first_head = lane < NA_HEAD_DIM
    nkeys = WIN_H * GRID_W

    @pl.when(jb == 0)
    def _():
        for h in range(2):
            for c in range(WIN_H):
                for rr in range(0, WIN_H, 2):
                    even = cb_ref[h, rr - c + WIN_H - 1]
                    odd = cb_ref[h, rr + 1 - c + WIN_H - 1]
                    bt_ref[h, c, :, rr * GRID_W:(rr + 2) * GRID_W] = jnp.where(first_head, even, odd)

    def window(r):
        i = jb * rb + r
        r0 = jnp.clip(i - WIN_H // 2, 0, rows - WIN_H)
        return i - r0, pl.multiple_of(r0 * GRID_W, GRID_W)

    def scores(r):
        _, start = window(r)
        q2 = q_ref[pl.ds(pl.multiple_of(r * GRID_W, GRID_W), GRID_W), :]
        zero = jnp.zeros_like(q2)
        qst = jnp.concatenate([jnp.where(first_head, q2, zero), jnp.where(first_head, zero, q2)], axis=0)
        ks = k_ref[pl.ds(start, nkeys), :]
        return lax.dot_general(qst, ks, (((1,), (1,)), ((), ())), preferred_element_type=F32)

    def finish(r, s):
        c, start = window(r)
        vs = v_ref[pl.ds(start, nkeys), :]
        bias = jnp.concatenate([bt_ref[0, c], bt_ref[1, c]], axis=0)
        s = jnp.where(bias > 0.5 * MASK_VALUE, s + bias, MASK_VALUE)
        m = jnp.max(s, axis=-1, keepdims=True)
        p = jnp.exp(s - m)
        l = jnp.sum(p, axis=-1, keepdims=True)
        o = jnp.dot(p.astype(BF16), vs, preferred_element_type=F32) / l
        o2 = jnp.where(first_head, o[:GRID_W], o[GRID_W:])
        o_ref[pl.ds(pl.multiple_of(r * GRID_W, GRID_W), GRID_W), :] = o2.astype(o_ref.dtype)

    s_ref[0] = scores(0)

    def pair(p, carry):
        r = 2 * p
        s_ref[1] = scores(r + 1)
        finish(r, s_ref[0])
        s_ref[0] = scores(jnp.minimum(r + 2, rb - 1))
        finish(r + 1, s_ref[1])
        return carry

    lax.fori_loop(0, rb // 2, pair, 0, unroll=ATTN_UNROLL)


def _attention(q, k, v, rpb):
    t = q.shape[0]
    rows = t // GRID_W
    rb = min(ATTN_RB, rows)
    npairs = D_NA // LANES
    cb = _column_bias(rpb)
    kern = functools.partial(_attn_kernel, rb=rb, rows=rows)
    return pl.pallas_call(
        kern,
        grid=(npairs, rows // rb),
        in_specs=[
            pl.BlockSpec((rb * GRID_W, LANES), lambda p, j: (j, p)),
            pl.BlockSpec((t, LANES), lambda p, j: (0, p)),
            pl.BlockSpec((t, LANES), lambda p, j: (0, p)),
            pl.BlockSpec((2, 2 * WIN_H - 1, GRID_W, 2 * GRID_W), lambda p, j: (p, 0, 0, 0)),
        ],
        out_specs=pl.BlockSpec((rb * GRID_W, LANES), lambda p, j: (j, p)),
        out_shape=jax.ShapeDtypeStruct((t, D_NA), BF16),
        scratch_shapes=[pltpu.VMEM((2, 2 * GRID_W, WIN_H * GRID_W), F32),
                        pltpu.VMEM((2, WIN_H, GRID_W, WIN_H * GRID_W), F32)],
        compiler_params=pltpu.CompilerParams(
            dimension_semantics=("arbitrary", "arbitrary"), vmem_limit_bytes=_vmem_limit(40 << 20)),
        name="attention",
    )(q, k, v, cb)


def _outproj_kernel(conv_ref, na_ref, x_ref, wo_ref, bo_ref, g_ref, wr_ref, br_ref,
                    h_ref, xn_ref, idx_ref, gate_ref):
    for s in range(x_ref.shape[0] // ROUTER_SUB):
        rows = slice(s * ROUTER_SUB, (s + 1) * ROUTER_SUB)
        mixed = jnp.concatenate([conv_ref[rows, :], na_ref[rows, :]], axis=-1)
        h = x_ref[rows, :] + jnp.dot(mixed, wo_ref[...], preferred_element_type=F32) + bo_ref[...]
        h_ref[rows, :] = h
        ms = jnp.mean(h * h, axis=-1, keepdims=True)
        xn = (h * lax.rsqrt(ms + EPS) * g_ref[...]).astype(BF16)
        xn_ref[rows, :] = xn
        logits = jnp.transpose(jnp.dot(xn, wr_ref[...], preferred_element_type=F32) + br_ref[...])[:N_EXPERTS]
        ids = lax.broadcasted_iota(jnp.int32, logits.shape, 0)
        vals, sels = [], []
        l = logits
        for _ in range(TOP_K):
            m = jnp.max(l, axis=0, keepdims=True)
            sel = jnp.min(jnp.where(l == m, ids, N_EXPERTS), axis=0, keepdims=True)
            vals.append(m)
            sels.append(sel)
            l = jnp.where(ids == sel, -jnp.inf, l)
        es = [jnp.exp(vk - vals[0]) for vk in vals]
        tot = es[0] + es[1] + es[2] + es[3]
        idx_ref[:, rows] = jnp.concatenate(sels, axis=0)
        gate_ref[:, rows] = jnp.concatenate([e / tot for e in es], axis=0)


def _outproj_router(conv_out, na_out, x2, w_out, b_out, g_ffn, w_router, b_router):
    t = x2.shape[0]
    tm = min(OUTPROJ_TM, t)
    tok = lambda i: (i, 0)
    const = lambda i: (0, 0)
    return pl.pallas_call(
        _outproj_kernel,
        grid=(t // tm,),
        in_specs=[
            pl.BlockSpec((tm, D_CONV), tok),
            pl.BlockSpec((tm, D_NA), tok),
            pl.BlockSpec((tm, D_MODEL), tok),
            pl.BlockSpec((D_MODEL, D_MODEL), const),
            pl.BlockSpec((1, D_MODEL), const),
            pl.BlockSpec((1, D_MODEL), const),
            pl.BlockSpec((D_MODEL, LANES), const),
            pl.BlockSpec((1, LANES), const),
        ],
        out_specs=[
            pl.BlockSpec((tm, D_MODEL), tok),
            pl.BlockSpec((tm, D_MODEL), tok),
            pl.BlockSpec((TOP_K, tm), lambda i: (0, i)),
            pl.BlockSpec((TOP_K, tm), lambda i: (0, i)),
        ],
        out_shape=[
            jax.ShapeDtypeStruct((t, D_MODEL), F32),
            jax.ShapeDtypeStruct((t, D_MODEL), BF16),
            jax.ShapeDtypeStruct((TOP_K, t), jnp.int32),
            jax.ShapeDtypeStruct((TOP_K, t), F32),
        ],
        compiler_params=pltpu.CompilerParams(
            dimension_semantics=("arbitrary",), vmem_limit_bytes=_vmem_limit(48 << 20)),
        name="outproj_router",
    )(conv_out, na_out, x2, w_out.astype(BF16), b_out.reshape(1, D_MODEL), g_ffn.reshape(1, D_MODEL),
      jnp.pad(w_router.astype(BF16), ((0, 0), (0, LANES - N_EXPERTS))),
      jnp.pad(b_router.reshape(1, N_EXPERTS), ((0, 0), (0, LANES - N_EXPERTS))))


def _route_plan(idx_t, t):
    nt = t // ROUTE_TM
    experts = jnp.arange(N_EXPERTS, dtype=I32)
    onehot = idx_t.reshape(TOP_K, nt, ROUTE_TM, 1) == experts
    cnt = jnp.sum(onehot, axis=(0, 2), dtype=I32)
    sizes = jnp.sum(cnt, axis=0)
    padded = (sizes + ROUTE_RP + EXP_BM - 1) // EXP_BM * EXP_BM
    pad_ends = jnp.cumsum(padded)
    pad_off = pad_ends - padded
    tbase = pad_off[None, :] + jnp.cumsum(cnt, axis=0) - cnt
    cap = _sorted_rows(t)
    nb = cap // EXP_BM
    n_used = pad_ends[N_EXPERTS - 1] // EXP_BM
    blk_src = jnp.minimum(jnp.arange(nb, dtype=I32), n_used - 1)
    blk_exp = jnp.minimum(
        jnp.sum(blk_src[:, None] * EXP_BM >= pad_ends[None, :], axis=1, dtype=I32), N_EXPERTS - 1)
    row_end = jnp.sum(jnp.where(blk_exp[:, None] == experts[None, :], (pad_off + sizes)[None, :], 0), axis=1)
    blk_live = jnp.clip(row_end - blk_src * EXP_BM, 0, EXP_BM).astype(I32)
    multi = jnp.any(tbase % SUBLANES + cnt > ROUTE_RP, axis=1).astype(I32)
    zero_from = jnp.concatenate([(pad_off + sizes) // ZERO_ROWS * ZERO_ROWS, pad_ends[N_EXPERTS - 1:]])
    zero_to = jnp.concatenate([pad_ends, jnp.full((1,), cap, I32)])
    return dict(cnt=cnt.reshape(-1), tbase=tbase.reshape(-1).astype(I32), multi=multi,
                zero_start=zero_from.astype(I32), zero_cnt=((zero_to - zero_from) // ZERO_ROWS).astype(I32),
                blk_src=blk_src, blk_exp=blk_exp, blk_live=blk_live, n_used=n_used.reshape(1).astype(I32))


def _sorted_rows(t):
    cap = t * TOP_K + N_EXPERTS * (EXP_BM + ROUTE_RP)
    return (cap + EXP_BM - 1) // EXP_BM * EXP_BM


def _chunk_geometry(tb, n):
    head = tb & (SUBLANES - 1)
    start = pl.multiple_of(tb - head, SUBLANES)
    nchunks = (head + n + ROUTE_RP - 1) // ROUTE_RP
    return start, head, nchunks


def _dispatch_kernel(tb_ref, cnt_ref, mt_ref, zs_ref, zc_ref, idx_ref, tbv_ref, xn_ref, xs_hbm, loc_ref,
                     stage, ostage, zbuf, head_ref, cs_ref, ms_ref, sem, osem, zsem):
    i = pl.program_id(0)
    nt = pl.num_programs(0)
    slot = i % 2
    tm = idx_ref.shape[1]
    groups = ROUTE_RP // SUBLANES

    def geometry(e, step=None):
        step = i if step is None else step
        return _chunk_geometry(tb_ref[step * N_EXPERTS + e], cnt_ref[step * N_EXPERTS + e])

    def chunk_copy(step, e):
        start, _, _ = geometry(e, step)
        return pltpu.make_async_copy(stage.at[step % 2, pl.ds(e * ROUTE_RP, ROUTE_RP)],
                                     xs_hbm.at[pl.ds(start, ROUTE_RP)], sem.at[step % 2])

    def all_chunks(step, phase):
        if phase == "wait":
            pltpu.make_async_copy(stage.at[0], xs_hbm.at[pl.ds(0, N_EXPERTS * ROUTE_RP)], sem.at[step % 2]).wait()
            return
        for e in range(N_EXPERTS):
            chunk_copy(step, e).start()

    @pl.when(i >= 2)
    def _():
        all_chunks(i - 2, "wait")

    @pl.when(i >= 1)
    def _():
        all_chunks(i - 1, "start")

    def zero_copy(start):
        return pltpu.make_async_copy(zbuf, xs_hbm.at[pl.ds(pl.multiple_of(start, ZERO_ROWS), ZERO_ROWS)], zsem)

    @pl.when(i == 0)
    def _():
        zbuf[...] = jnp.zeros(zbuf.shape, U32)
        head_ref[...] = jnp.zeros(head_ref.shape, U32)
        for phase in ("start", "wait"):
            for j in range(N_EXPERTS + 1):
                def piece(q, carry, j=j, phase=phase):
                    getattr(zero_copy(zs_ref[j] + q * ZERO_ROWS), phase)()
                    return carry
                lax.fori_loop(0, zc_ref[j], piece, 0)

    idx = idx_ref[...]
    eio = lax.broadcasted_iota(I32, (N_EXPERTS, tm), 0)
    member = jnp.zeros((N_EXPERTS, tm), F32)
    for k in range(TOP_K):
        member = member + (idx[k:k + 1, :] == eio).astype(F32)
    tri = (lax.broadcasted_iota(I32, (tm, tm), 0) < lax.broadcasted_iota(I32, (tm, tm), 1)).astype(BF16)
    rank = jnp.dot(member.astype(BF16), tri, preferred_element_type=F32)
    pos = rank + tbv_ref[0][:, 0:1].astype(F32)
    loc_ref[...] = jnp.concatenate(
        [jnp.sum(jnp.where(idx[k:k + 1, :] == eio, pos, 0.0), axis=0, keepdims=True) for k in range(TOP_K)],
        axis=0).astype(I32)
    cs_ref[...] = pos
    ms_ref[...] = member

    jio = lax.broadcasted_iota(I32, (ROUTE_RP, tm), 0).astype(F32)
    sel = [jnp.where((jio == pos[e:e + 1, :]) & (member[e:e + 1, :] > 0.0), 1.0, 0.0).astype(BF16)
           for e in range(N_EXPERTS)]
    rows = jnp.dot(jnp.concatenate(sel, axis=0), xn_ref[...], preferred_element_type=F32)

    stage[slot] = _pack_rows(rows)
    sub = lax.broadcasted_iota(I32, (SUBLANES, HALF), 0)
    for e in range(N_EXPERTS):
        _, head, _ = geometry(e)
        first = pl.ds(e * ROUTE_RP, SUBLANES)
        stage[slot, first, :] = jnp.where(sub < head, head_ref[e], stage[slot, first, :])
        g = jnp.minimum(lax.shift_right_logical(head + cnt_ref[i * N_EXPERTS + e], SUBLANES.bit_length() - 1),
                        groups - 1)
        head_ref[e] = stage[slot, pl.ds(pl.multiple_of(e * ROUTE_RP + g * SUBLANES, SUBLANES), SUBLANES), :]

    def per_expert(e, carry):
        start, head, nchunks = geometry(e)
        end = head + cnt_ref[i * N_EXPERTS + e]

        def per_chunk(c, carry2):
            want = jio + (c * ROUTE_RP).astype(F32)
            pick = jnp.where((want == cs_ref[pl.ds(e, 1), :]) & (ms_ref[pl.ds(e, 1), :] > 0.0), 1.0, 0.0)
            ostage[...] = _pack_rows(jnp.dot(pick.astype(BF16), xn_ref[...], preferred_element_type=F32))
            cp = pltpu.make_async_copy(
                ostage, xs_hbm.at[pl.ds(pl.multiple_of(start + c * ROUTE_RP, SUBLANES), ROUTE_RP)], osem)
            cp.start()
            cp.wait()

            @pl.when(end // ROUTE_RP == c)
            def _():
                g = (end - c * ROUTE_RP) // SUBLANES
                head_ref[e] = ostage[pl.ds(pl.multiple_of(g * SUBLANES, SUBLANES), SUBLANES), :]
            return carry2
        return lax.fori_loop(1, nchunks, per_chunk, carry)

    @pl.when(mt_ref[i] > 0)
    def _():
        lax.fori_loop(0, N_EXPERTS, per_expert, 0)

    @pl.when(i == nt - 1)
    def _():
        @pl.when(i >= 1)
        def _():
            all_chunks(i - 1, "wait")
        all_chunks(i, "start")
        all_chunks(i, "wait")


def _dispatch(plan, idx_t, xn2):
    t = xn2.shape[0]
    tm = ROUTE_TM
    cap = _sorted_rows(t)
    grid_spec = pltpu.PrefetchScalarGridSpec(
        num_scalar_prefetch=5,
        grid=(t // tm,),
        in_specs=[
            pl.BlockSpec((TOP_K, tm), lambda i, *_: (0, i)),
            pl.BlockSpec((1, N_EXPERTS, LANES), lambda i, *_: (i, 0, 0)),
            pl.BlockSpec((tm, D_MODEL), lambda i, *_: (i, 0)),
        ],
        out_specs=[
            pl.BlockSpec(memory_space=pl.ANY),
            pl.BlockSpec((TOP_K, tm), lambda i, *_: (0, i)),
        ],
        scratch_shapes=[
            pltpu.VMEM((2, N_EXPERTS * ROUTE_RP, HALF), U32),
            pltpu.VMEM((ROUTE_RP, HALF), U32),
            pltpu.VMEM((ZERO_ROWS, HALF), U32),
            pltpu.VMEM((N_EXPERTS, SUBLANES, HALF), U32),
            pltpu.VMEM((N_EXPERTS, tm), F32),
            pltpu.VMEM((N_EXPERTS, tm), F32),
            pltpu.SemaphoreType.DMA((2,)),
            pltpu.SemaphoreType.DMA(()),
            pltpu.SemaphoreType.DMA(()),
        ],
    )
    return pl.pallas_call(
        _dispatch_kernel,
        grid_spec=grid_spec,
        out_shape=[
            jax.ShapeDtypeStruct((cap, HALF), U32),
            jax.ShapeDtypeStruct((TOP_K, t), I32),
        ],
        compiler_params=pltpu.CompilerParams(
            dimension_semantics=("arbitrary",), vmem_limit_bytes=_vmem_limit(32 << 20)),
        name="dispatch",
    )(plan["tbase"], plan["cnt"], plan["multi"], plan["zero_start"], plan["zero_cnt"], idx_t,
      jnp.broadcast_to((plan["tbase"] % SUBLANES).reshape(t // tm, N_EXPERTS, 1), (t // tm, N_EXPERTS, LANES)),
      xn2)


def _expert_kernel(be_ref, src_ref, nu_ref, live_ref, xs_ref, wg_hbm, wu_hbm, wd_hbm, bg_ref, bu_ref, bd_ref,
                   ys_ref, wf32, wbf, wsem):
    b = pl.program_id(0)
    e = be_ref[b]

    def weight_copies(expert, par):
        return [pltpu.make_async_copy(w.at[expert], wf32.at[par, m], wsem.at[par])
                for m, w in enumerate((wg_hbm, wu_hbm, wd_hbm))]

    @pl.when(b == 0)
    def _():
        for cp in weight_copies(e, e % 2):
            cp.start()

    @pl.when(b < nu_ref[0])
    def _():
        @pl.when((b == 0) | (e != be_ref[jnp.maximum(b - 1, 0)]))
        def _():
            par = e % 2
            for cp in weight_copies(e, par):
                cp.wait()

            @pl.when(e + 1 < N_EXPERTS)
            def _():
                for cp in weight_copies(e + 1, 1 - par):
                    cp.start()

            for m in range(3):
                wbf[m] = wf32[par, m].astype(BF16)

        def ffn(rows):
            x = _unpack_rows(xs_ref[rows, :])
            gt = jnp.minimum(jnp.dot(x, wbf[0], preferred_element_type=F32) + bg_ref[0], SWIGLU_LIMIT)
            up = jnp.clip(jnp.dot(x, wbf[1], preferred_element_type=F32) + bu_ref[0],
                          -SWIGLU_LIMIT, SWIGLU_LIMIT)
            hdn = (up + 1.0) * (gt * jax.nn.sigmoid(SWIGLU_ALPHA * gt))
            y = jnp.dot(hdn.astype(BF16), wbf[2], preferred_element_type=F32) + bd_ref[0]
            ys_ref[rows, :] = _pack_rows(y.astype(BF16).astype(F32))

        live = live_ref[b]
        for p, part in enumerate(EXP_PARTS):
            smaller = EXP_PARTS[p + 1] if p + 1 < len(EXP_PARTS) else -1
            cond = live > smaller
            if p > 0:
                cond = cond & (live <= part)

            @pl.when(cond)
            def _(part=part):
                if part > 0:
                    ffn(slice(0, part))
                if part < EXP_BM:
                    ys_ref[part:, :] = jnp.zeros((EXP_BM - part, HALF), U32)

    @pl.when(b >= nu_ref[0])
    def _():
        ys_ref[...] = jnp.zeros(ys_ref.shape, U32)


def _experts(plan, xs, w_gate, b_gate, w_up, b_up, w_down, b_down):
    cap = xs.shape[0]
    nb = cap // EXP_BM
    rows = lambda b, be, src, nu, live: (src[b], 0)
    wsel = lambda b, be, src, nu, live: (be[b], 0, 0)
    wspec = pl.BlockSpec(memory_space=pl.ANY)
    bspec = pl.BlockSpec((1, 1, D_FF), wsel)
    grid_spec = pltpu.PrefetchScalarGridSpec(
        num_scalar_prefetch=4,
        grid=(nb,),
        in_specs=[pl.BlockSpec((EXP_BM, HALF), rows), wspec, wspec, wspec, bspec, bspec, bspec],
        out_specs=pl.BlockSpec((EXP_BM, HALF), lambda b, be, src, nu, live: (b, 0)),
        scratch_shapes=[pltpu.VMEM((2, 3, D_MODEL, D_FF), F32),
                        pltpu.VMEM((3, D_MODEL, D_FF), BF16),
                        pltpu.SemaphoreType.DMA((2,))],
    )
    return pl.pallas_call(
        _expert_kernel,
        grid_spec=grid_spec,
        out_shape=jax.ShapeDtypeStruct((cap, HALF), U32),
        compiler_params=pltpu.CompilerParams(
            dimension_semantics=("arbitrary",), vmem_limit_bytes=_vmem_limit(56 << 20)),
        name="experts",
    )(plan["blk_exp"], plan["blk_src"], plan["n_used"], plan["blk_live"], xs, w_gate, w_up, w_down,
      b_gate.reshape(N_EXPERTS, 1, D_FF), b_up.reshape(N_EXPERTS, 1, D_FF),
      b_down.reshape(N_EXPERTS, 1, D_MODEL))


def _combine_kernel(tb_ref, cnt_ref, mt_ref, h_ref, idx_ref, loc_ref, gate_ref, ys_hbm, g_ref, o_ref,
                    ybuf, obuf, acc_ref, sem, osem):
    i = pl.program_id(0)
    nt = pl.num_programs(0)
    slot = i % 2
    tm = h_ref.shape[0]

    def geometry(step, e):
        return _chunk_geometry(tb_ref[step * N_EXPERTS + e], cnt_ref[step * N_EXPERTS + e])

    def chunk_copy(step, e, s):
        start, _, _ = geometry(step, e)
        return pltpu.make_async_copy(ys_hbm.at[pl.ds(start, ROUTE_RP)],
                                     ybuf.at[s, pl.ds(e * ROUTE_RP, ROUTE_RP)], sem.at[s])

    @pl.when(i == 0)
    def _():
        for e in range(N_EXPERTS):
            chunk_copy(0, e, 0).start()

    for e in range(N_EXPERTS):
        chunk_copy(jnp.minimum(i + 1, nt - 1), e, 1 - slot).start()

    idx_loc = jnp.transpose(jnp.concatenate([idx_ref[...], loc_ref[...]], axis=0))
    idx, loc = idx_loc[:, :TOP_K], idx_loc[:, TOP_K:]
    gate = jnp.transpose(jnp.concatenate([gate_ref[...], gate_ref[...]], axis=0))[:, :TOP_K]

    def gate_matrix(ncols, col_of):
        colio = lax.broadcasted_iota(I32, (tm, ncols), 1)
        g = jnp.zeros((tm, ncols), F32)
        for k in range(TOP_K):
            g = jnp.where(colio == col_of[:, k:k + 1], gate[:, k:k + 1], g)
        return g.astype(BF16)

    pltpu.make_async_copy(ys_hbm.at[pl.ds(0, N_EXPERTS * ROUTE_RP)], ybuf.at[0], sem.at[slot]).wait()
    col = jnp.where(loc < ROUTE_RP, idx * ROUTE_RP + loc, -1)
    acc_ref[...] = jnp.dot(gate_matrix(N_EXPERTS * ROUTE_RP, col), _unpack_rows(ybuf[slot]),
                           preferred_element_type=F32)

    def per_expert(e, carry):
        start, _, nchunks = geometry(i, e)

        def per_chunk(c, carry2):
            cp = pltpu.make_async_copy(
                ys_hbm.at[pl.ds(pl.multiple_of(start + c * ROUTE_RP, SUBLANES), ROUTE_RP)], obuf, osem)
            cp.start()
            cp.wait()
            ccol = jnp.where(idx == e, loc - c * ROUTE_RP, -1)
            acc_ref[...] += jnp.dot(gate_matrix(ROUTE_RP, ccol), _unpack_rows(obuf[...]),
                                    preferred_element_type=F32)
            return carry2
        return lax.fori_loop(1, nchunks, per_chunk, carry)

    @pl.when(mt_ref[i] > 0)
    def _():
        lax.fori_loop(0, N_EXPERTS, per_expert, 0)

    out = h_ref[...] + acc_ref[...]
    ms = jnp.mean(out * out, axis=-1, keepdims=True)
    o_ref[...] = out * lax.rsqrt(ms + EPS) * g_ref[...]

    @pl.when(i == nt - 1)
    def _():
        pltpu.make_async_copy(ys_hbm.at[pl.ds(0, N_EXPERTS * ROUTE_RP)], ybuf.at[0], sem.at[1 - slot]).wait()


def _combine(plan, h, ys, idx_t, loc_t, gate_t, g_final):
    t = h.shape[0]
    tm = ROUTE_TM
    tok = lambda i, *_: (i, 0)
    lane = lambda i, *_: (0, i)
    grid_spec = pltpu.PrefetchScalarGridSpec(
        num_scalar_prefetch=3,
        grid=(t // tm,),
        in_specs=[
            pl.BlockSpec((tm, D_MODEL), tok),
            pl.BlockSpec((TOP_K, tm), lane),
            pl.BlockSpec((TOP_K, tm), lane),
            pl.BlockSpec((TOP_K, tm), lane),
            pl.BlockSpec(memory_space=pl.ANY),
            pl.BlockSpec((1, D_MODEL), lambda i, *_: (0, 0)),
        ],
        out_specs=pl.BlockSpec((tm, D_MODEL), tok),
        scratch_shapes=[
            pltpu.VMEM((2, N_EXPERTS * ROUTE_RP, HALF), U32),
            pltpu.VMEM((ROUTE_RP, HALF), U32),
            pltpu.VMEM((tm, D_MODEL), F32),
            pltpu.SemaphoreType.DMA((2,)),
            pltpu.SemaphoreType.DMA(()),
        ],
    )
    return pl.pallas_call(
        _combine_kernel,
        grid_spec=grid_spec,
        out_shape=jax.ShapeDtypeStruct((t, D_MODEL), F32),
        compiler_params=pltpu.CompilerParams(
            dimension_semantics=("arbitrary",), vmem_limit_bytes=_vmem_limit(32 << 20)),
        name="combine",
    )(plan["tbase"], plan["cnt"], plan["multi"], h, idx_t, loc_t, gate_t, ys,
      g_final.reshape(1, D_MODEL))


def _moe(h, xn2, idx_t, gate_t, w_gate, b_gate, w_up, b_up, w_down, b_down, g_final):
    t = h.shape[0]
    plan = _route_plan(idx_t, t)
    xs, loc_t = _dispatch(plan, idx_t, xn2)
    ys = _experts(plan, xs, w_gate, b_gate, w_up, b_up, w_down, b_down)
    return _combine(plan, h, ys, idx_t, loc_t, gate_t, g_final)


def kernel(x, g_mix, w_in, b_in, conv_w, conv_b, ln_g, ln_b, rpb, w_out, b_out, g_ffn, w_router,
           b_router, w_gate, b_gate, w_up, b_up, w_down, b_down, g_final):
    bsz, seq, d = x.shape
    assert bsz == 1 and d == D_MODEL and g_mix.shape[0] == 1
    assert seq % (GRID_W * WIN_H) == 0
    x2 = x.reshape(seq, d)
    q, k, v, conv_out = _inproj_conv(x2, g_mix[0], w_in[0], b_in[0], conv_w[0], conv_b[0], ln_g[0], ln_b[0])
    na_out = _attention(q, k, v, rpb[0])
    h, xn2, idx_t, gate_t = _outproj_router(conv_out, na_out, x2, w_out[0], b_out[0], g_ffn[0],
                                            w_router[0], b_router[0])
    out = _moe(h, xn2, idx_t, gate_t, w_gate[0], b_gate[0], w_up[0], b_up[0], w_down[0], b_down[0],
               g_final)
    return out.reshape(bsz, seq, d)
```

```python
import functools

import numpy as np
import jax
import jax.numpy as jnp
from jax import lax
from jax.experimental import pallas as pl
from jax.experimental.pallas import tpu as pltpu

F32 = jnp.float32
BF16 = jnp.bfloat16
U32 = jnp.uint32
I32 = jnp.int32

D_MODEL = 1024
GRID_W = 64
D_CONV = 512
CONV_K = 31
NA_HEADS = 8
NA_HEAD_DIM = 64
D_NA = NA_HEADS * NA_HEAD_DIM
D_IN = 2 * D_CONV + 3 * D_NA
WIN_H = 8
WIN_W = 16
N_EXPERTS = 32
TOP_K = 4
D_FF = D_MODEL
SWIGLU_LIMIT = 7.0
SWIGLU_ALPHA = 1.702
EPS = 1e-5
MASK_VALUE = -1e30

V7X_VMEM_BYTES = 64 * 1024 * 1024
LANES = 128

PROJ_TM = 1024
OUTPROJ_TM = 1024
ROUTER_SUB = 512
CONV_HALO = 16
ATTN_RB = 32
ATTN_UNROLL = 16
ROUTE_TM = 256
SUBLANES = 8
ROUTE_RP = 64
EXP_BM = 512
EXP_PARTS = (512, 256, 128, 0)
ZERO_ROWS = 256
HALF = D_MODEL // 2
BF16_BITS = 16
HIGH_HALF = 0xFFFF0000


def _vmem_limit(nbytes):
    return int(min(nbytes, V7X_VMEM_BYTES - 6 * 1024 * 1024))


def _pack_rows(x):
    lo = lax.bitcast_convert_type(x[:, :HALF], U32) >> BF16_BITS
    hi = lax.bitcast_convert_type(x[:, HALF:], U32) & jnp.uint32(HIGH_HALF)
    return lo | hi


def _unpack_rows(w):
    lo = lax.bitcast_convert_type(w << BF16_BITS, F32)
    hi = lax.bitcast_convert_type(w & jnp.uint32(HIGH_HALF), F32)
    return jnp.concatenate([lo, hi], axis=1).astype(BF16)


def _inproj_conv_kernel(x_ref, g_ref, w_ref, b_ref, cw_ref, cb_ref, lg_ref, lb_ref,
                        q_ref, k_ref, v_ref, o_ref, ext_ref, u_ref, sh_ref):
    i = pl.program_id(0)
    nt = pl.num_programs(0) - 1
    tm = x_ref.shape[0]

    @pl.when(i == 0)
    def _():
        ext_ref[...] = jnp.zeros(ext_ref.shape, F32)

    @pl.when(i < nt)
    def _():
        x = x_ref[...]
        ms = jnp.mean(x * x, axis=-1, keepdims=True)
        xn = (x * lax.rsqrt(ms + EPS) * g_ref[...]).astype(BF16)

        def proj(c):
            sl = slice(c * D_CONV, (c + 1) * D_CONV)
            return jnp.dot(xn, w_ref[:, sl], preferred_element_type=F32) + b_ref[:, sl]

        u_ref[...] = proj(0) * jax.nn.sigmoid(proj(1))
        q_ref[...] = (proj(2) * (NA_HEAD_DIM ** -0.5)).astype(BF16)
        k_ref[...] = proj(3).astype(BF16)
        v_ref[...] = proj(4).astype(BF16)

    ext_ref[CONV_HALO + tm:, :] = jnp.where(i < nt, u_ref[0:CONV_HALO, :], 0.0)

    @pl.when(i >= 1)
    def _():
        base = CONV_HALO - CONV_K // 2
        span = (CONV_K - 1) // SUBLANES * SUBLANES
        for b in range(SUBLANES):
            sh_ref[b] = ext_ref[base + b:base + b + tm + span, :]
        acc = jnp.zeros((tm, D_CONV), F32)
        for b in range(SUBLANES):
            for a in range((CONV_K - 1 - b) // SUBLANES + 1):
                r = SUBLANES * a
                acc = acc + sh_ref[b, r:r + tm, :] * cw_ref[r + b:r + b + 1, :]
        acc = acc + cb_ref[...]
        mu = jnp.mean(acc, axis=-1, keepdims=True)
        d = acc - mu
        var = jnp.mean(d * d, axis=-1, keepdims=True)
        un = d * lax.rsqrt(var + EPS) * lg_ref[...] + lb_ref[...]
        o_ref[...] = (un * jax.nn.sigmoid(un)).astype(o_ref.dtype)

    ext_ref[0:CONV_HALO, :] = ext_ref[tm:tm + CONV_HALO, :]
    ext_ref[CONV_HALO:CONV_HALO + tm, :] = u_ref[...]


def _inproj_conv(x2, g_mix, w_in, b_in, conv_w, conv_b, ln_g, ln_b):
    t = x2.shape[0]
    tm = min(PROJ_TM, t)
    nt = t // tm
    tok = lambda i: (jnp.minimum(i, nt - 1), 0)
    prev = lambda i: (jnp.maximum(i - 1, 0), 0)
    const = lambda i: (0, 0)
    w = jnp.zeros((CONV_K + 1, D_CONV), F32).at[:CONV_K].set(conv_w)
    span = (CONV_K - 1) // SUBLANES * SUBLANES
    return pl.pallas_call(
        _inproj_conv_kernel,
        grid=(nt + 1,),
        in_specs=[
            pl.BlockSpec((tm, D_MODEL), tok),
            pl.BlockSpec((1, D_MODEL), const),
            pl.BlockSpec((D_MODEL, D_IN), const),
            pl.BlockSpec((1, D_IN), const),
            pl.BlockSpec((CONV_K + 1, D_CONV), const),
            pl.BlockSpec((1, D_CONV), const),
            pl.BlockSpec((1, D_CONV), const),
            pl.BlockSpec((1, D_CONV), const),
        ],
        out_specs=[
            pl.BlockSpec((tm, D_NA), tok),
            pl.BlockSpec((tm, D_NA), tok),
            pl.BlockSpec((tm, D_NA), tok),
            pl.BlockSpec((tm, D_CONV), prev),
        ],
        out_shape=[
            jax.ShapeDtypeStruct((t, D_NA), BF16),
            jax.ShapeDtypeStruct((t, D_NA), BF16),
            jax.ShapeDtypeStruct((t, D_NA), BF16),
            jax.ShapeDtypeStruct((t, D_CONV), BF16),
        ],
        scratch_shapes=[pltpu.VMEM((tm + 2 * CONV_HALO, D_CONV), F32),
                        pltpu.VMEM((tm, D_CONV), F32),
                        pltpu.VMEM((SUBLANES, tm + span, D_CONV), F32)],
        compiler_params=pltpu.CompilerParams(
            dimension_semantics=("arbitrary",), vmem_limit_bytes=_vmem_limit(56 << 20)),
        name="inproj_conv",
    )(x2, g_mix.reshape(1, D_MODEL), w_in.astype(BF16), b_in.reshape(1, D_IN), w,
      conv_b.reshape(1, D_CONV), ln_g.reshape(1, D_CONV), ln_b.reshape(1, D_CONV))


def _column_bias(rpb):
    col = np.arange(GRID_W)
    c0 = np.clip(col - WIN_W // 2, 0, GRID_W - WIN_W)
    valid = (col[None, :] >= c0[:, None]) & (col[None, :] < c0[:, None] + WIN_W)
    dcol = np.clip(col[None, :] - col[:, None], -(WIN_W - 1), WIN_W - 1) + (WIN_W - 1)
    col_sel = ((dcol[:, :, None] == np.arange(2 * WIN_W - 1)) & valid[:, :, None]).astype(np.float32)
    b = jnp.einsum("hdj,qkj->hdqk", rpb.astype(F32), col_sel, precision=lax.Precision.HIGHEST)
    b = jnp.where(valid[None, None], b, MASK_VALUE)
    return jnp.concatenate([b, b], axis=-1)


def _attn_kernel(q_ref, k_ref, v_ref, cb_ref, o_ref, s_ref, bt_ref, *, rb, rows):
    jb = pl.program_id(1)
    lane = lax.broadcasted_iota(jnp.int32, (GRID_W, LANES), 1)
    first_head = lane < NA_HEAD_DIM
    nkeys = WIN_H * GRID_W

    @pl.when(jb == 0)
    def _():
        for h in range(2):
            for c in range(WIN_H):
                for rr in range(0, WIN_H, 2):
                    even = cb_ref[h, rr - c + WIN_H - 1]
                    odd = cb_ref[h, rr + 1 - c + WIN_H - 1]
                    bt_ref[h, c, :, rr * GRID_W:(rr + 2) * GRID_W] = jnp.where(first_head, even, odd)

    def window(r):
        i = jb * rb + r
        r0 = jnp.clip(i - WIN_H // 2, 0, rows - WIN_H)
        return i - r0, pl.multiple_of(r0 * GRID_W, GRID_W)

    def scores(r):
        _, start = window(r)
        q2 = q_ref[pl.ds(pl.multiple_of(r * GRID_W, GRID_W), GRID_W), :]
        zero = jnp.zeros_like(q2)
        qst = jnp.concatenate([jnp.where(first_head, q2, zero), jnp.where(first_head, zero, q2)], axis=0)
        ks = k_ref[pl.ds(start, nkeys), :]
        return lax.dot_general(qst, ks, (((1,), (1,)), ((), ())), preferred_element_type=F32)

    def finish(r, s):
        c, start = window(r)
        vs = v_ref[pl.ds(start, nkeys), :]
        bias = jnp.concatenate([bt_ref[0, c], bt_ref[1, c]], axis=0)
        s = jnp.where(bias > 0.5 * MASK_VALUE, s + bias, MASK_VALUE)
        m = jnp.max(s, axis=-1, keepdims=True)
        p = jnp.exp(s - m)
        l = jnp.sum(p, axis=-1, keepdims=True)
        o = jnp.dot(p.astype(BF16), vs, preferred_element_type=F32) / l
        o2 = jnp.where(first_head, o[:GRID_W], o[GRID_W:])
        o_ref[pl.ds(pl.multiple_of(r * GRID_W, GRID_W), GRID_W), :] = o2.astype(o_ref.dtype)

    s_ref[0] = scores(0)

    def pair(p, carry):
        r = 2 * p
        s_ref[1] = scores(r + 1)
        finish(r, s_ref[0])
        s_ref[0] = scores(jnp.minimum(r + 2, rb - 1))
        finish(r + 1, s_ref[1])
        return carry

    lax.fori_loop(0, rb // 2, pair, 0, unroll=ATTN_UNROLL)


def _attention(q, k, v, rpb):
    t = q.shape[0]
    rows = t // GRID_W
    rb = min(ATTN_RB, rows)
    npairs = D_NA // LANES
    cb = _column_bias(rpb)
    kern = functools.partial(_attn_kernel, rb=rb, rows=rows)
    return pl.pallas_call(
        kern,
        grid=(npairs, rows // rb),
        in_specs=[
            pl.BlockSpec((rb * GRID_W, LANES), lambda p, j: (j, p)),
            pl.BlockSpec((t, LANES), lambda p, j: (0, p)),
            pl.BlockSpec((t, LANES), lambda p, j: (0, p)),
            pl.BlockSpec((2, 2 * WIN_H - 1, GRID_W, 2 * GRID_W), lambda p, j: (p, 0, 0, 0)),
        ],
        out_specs=pl.BlockSpec((rb * GRID_W, LANES), lambda p, j: (j, p)),
        out_shape=jax.ShapeDtypeStruct((t, D_NA), BF16),
        scratch_shapes=[pltpu.VMEM((2, 2 * GRID_W, WIN_H * GRID_W), F32),
                        pltpu.VMEM((2, WIN_H, GRID_W, WIN_H * GRID_W), F32)],
        compiler_params=pltpu.CompilerParams(
            dimension_semantics=("arbitrary", "arbitrary"), vmem_limit_bytes=_vmem_limit(40 << 20)),
        name="attention",
    )(q, k, v, cb)


def _outproj_kernel(conv_ref, na_ref, x_ref, wo_ref, bo_ref, g_ref, wr_ref, br_ref,
                    h_ref, xn_ref, idx_ref, gate_ref):
    for s in range(x_ref.shape[0] // ROUTER_SUB):
        rows = slice(s * ROUTER_SUB, (s + 1) * ROUTER_SUB)
        mixed = jnp.concatenate([conv_ref[rows, :], na_ref[rows, :]], axis=-1)
        h = x_ref[rows, :] + jnp.dot(mixed, wo_ref[...], preferred_element_type=F32) + bo_ref[...]
        h_ref[rows, :] = h
        ms = jnp.mean(h * h, axis=-1, keepdims=True)
        xn = (h * lax.rsqrt(ms + EPS) * g_ref[...]).astype(BF16)
        xn_ref[rows, :] = xn
        logits = jnp.transpose(jnp.dot(xn, wr_ref[...], preferred_element_type=F32) + br_ref[...])[:N_EXPERTS]
        ids = lax.broadcasted_iota(jnp.int32, logits.shape, 0)
        vals, sels = [], []
        l = logits
        for _ in range(TOP_K):
            m = jnp.max(l, axis=0, keepdims=True)
            sel = jnp.min(jnp.where(l == m, ids, N_EXPERTS), axis=0, keepdims=True)
            vals.append(m)
            sels.append(sel)
            l = jnp.where(ids == sel, -jnp.inf, l)
        es = [jnp.exp(vk - vals[0]) for vk in vals]
        tot = es[0] + es[1] + es[2] + es[3]
        idx_ref[:, rows] = jnp.concatenate(sels, axis=0)
        gate_ref[:, rows] = jnp.concatenate([e / tot for e in es], axis=0)


def _outproj_router(conv_out, na_out, x2, w_out, b_out, g_ffn, w_router, b_router):
    t = x2.shape[0]
    tm = min(OUTPROJ_TM, t)
    tok = lambda i: (i, 0)
    const = lambda i: (0, 0)
    return pl.pallas_call(
        _outproj_kernel,
        grid=(t // tm,),
        in_specs=[
            pl.BlockSpec((tm, D_CONV), tok),
            pl.BlockSpec((tm, D_NA), tok),
            pl.BlockSpec((tm, D_MODEL), tok),
            pl.BlockSpec((D_MODEL, D_MODEL), const),
            pl.BlockSpec((1, D_MODEL), const),
            pl.BlockSpec((1, D_MODEL), const),
            pl.BlockSpec((D_MODEL, LANES), const),
            pl.BlockSpec((1, LANES), const),
        ],
        out_specs=[
            pl.BlockSpec((tm, D_MODEL), tok),
            pl.BlockSpec((tm, D_MODEL), tok),
            pl.BlockSpec((TOP_K, tm), lambda i: (0, i)),
            pl.BlockSpec((TOP_K, tm), lambda i: (0, i)),
        ],
        out_shape=[
            jax.ShapeDtypeStruct((t, D_MODEL), F32),
            jax.ShapeDtypeStruct((t, D_MODEL), BF16),
            jax.ShapeDtypeStruct((TOP_K, t), jnp.int32),
            jax.ShapeDtypeStruct((TOP_K, t), F32),
        ],
        compiler_params=pltpu.CompilerParams(
            dimension_semantics=("arbitrary",), vmem_limit_bytes=_vmem_limit(48 << 20)),
        name="outproj_router",
    )(conv_out, na_out, x2, w_out.astype(BF16), b_out.reshape(1, D_MODEL), g_ffn.reshape(1, D_MODEL),
      jnp.pad(w_router.astype(BF16), ((0, 0), (0, LANES - N_EXPERTS))),
      jnp.pad(b_router.reshape(1, N_EXPERTS), ((0, 0), (0, LANES - N_EXPERTS))))


def _route_plan(idx_t, t):
    nt = t // ROUTE_TM
    experts = jnp.arange(N_EXPERTS, dtype=I32)
    onehot = idx_t.reshape(TOP_K, nt, ROUTE_TM, 1) == experts
    cnt = jnp.sum(onehot, axis=(0, 2), dtype=I32)
    sizes = jnp.sum(cnt, axis=0)
    padded = (sizes + ROUTE_RP + EXP_BM - 1) // EXP_BM * EXP_BM
    pad_ends = jnp.cumsum(padded)
    pad_off = pad_ends - padded
    tbase = pad_off[None, :] + jnp.cumsum(cnt, axis=0) - cnt
    cap = _sorted_rows(t)
    nb = cap // EXP_BM
    n_used = pad_ends[N_EXPERTS - 1] // EXP_BM
    blk_src = jnp.minimum(jnp.arange(nb, dtype=I32), n_used - 1)
    blk_exp = jnp.minimum(
        jnp.sum(blk_src[:, None] * EXP_BM >= pad_ends[None, :], axis=1, dtype=I32), N_EXPERTS - 1)
    row_end = jnp.sum(jnp.where(blk_exp[:, None] == experts[None, :], (pad_off + sizes)[None, :], 0), axis=1)
    blk_live = jnp.clip(row_end - blk_src * EXP_BM, 0, EXP_BM).astype(I32)
    multi = jnp.any(tbase % SUBLANES + cnt > ROUTE_RP, axis=1).astype(I32)
    zero_from = jnp.concatenate([(pad_off + sizes) // ZERO_ROWS * ZERO_ROWS, pad_ends[N_EXPERTS - 1:]])
    zero_to = jnp.concatenate([pad_ends, jnp.full((1,), cap, I32)])
    return dict(cnt=cnt.reshape(-1), tbase=tbase.reshape(-1).astype(I32), multi=multi,
                zero_start=zero_from.astype(I32), zero_cnt=((zero_to - zero_from) // ZERO_ROWS).astype(I32),
                blk_src=blk_src, blk_exp=blk_exp, blk_live=blk_live, n_used=n_used.reshape(1).astype(I32))


def _sorted_rows(t):
    cap = t * TOP_K + N_EXPERTS * (EXP_BM + ROUTE_RP)
    return (cap + EXP_BM - 1) // EXP_BM * EXP_BM


def _chunk_geometry(tb, n):
    head = tb & (SUBLANES - 1)
    start = pl.multiple_of(tb - head, SUBLANES)
    nchunks = (head + n + ROUTE_RP - 1) // ROUTE_RP
    return start, head, nchunks


def _dispatch_kernel(tb_ref, cnt_ref, mt_ref, zs_ref, zc_ref, idx_ref, tbv_ref, xn_ref, xs_hbm, loc_ref,
                     stage, ostage, zbuf, head_ref, cs_ref, ms_ref, sem, osem, zsem):
    i = pl.program_id(0)
    nt = pl.num_programs(0)
    slot = i % 2
    tm = idx_ref.shape[1]
    groups = ROUTE_RP // SUBLANES

    def geometry(e, step=None):
        step = i if step is None else step
        return _chunk_geometry(tb_ref[step * N_EXPERTS + e], cnt_ref[step * N_EXPERTS + e])

    def chunk_copy(step, e):
        start, _, _ = geometry(e, step)
        return pltpu.make_async_copy(stage.at[step % 2, pl.ds(e * ROUTE_RP, ROUTE_RP)],
                                     xs_hbm.at[pl.ds(start, ROUTE_RP)], sem.at[step % 2])

    def all_chunks(step, phase):
        if phase == "wait":
            pltpu.make_async_copy(stage.at[0], xs_hbm.at[pl.ds(0, N_EXPERTS * ROUTE_RP)], sem.at[step % 2]).wait()
            return
        for e in range(N_EXPERTS):
            chunk_copy(step, e).start()

    @pl.when(i >= 2)
    def _():
        all_chunks(i - 2, "wait")

    @pl.when(i >= 1)
    def _():
        all_chunks(i - 1, "start")

    def zero_copy(start):
        return pltpu.make_async_copy(zbuf, xs_hbm.at[pl.ds(pl.multiple_of(start, ZERO_ROWS), ZERO_ROWS)], zsem)

    @pl.when(i == 0)
    def _():
        zbuf[...] = jnp.zeros(zbuf.shape, U32)
        head_ref[...] = jnp.zeros(head_ref.shape, U32)
        for phase in ("start", "wait"):
            for j in range(N_EXPERTS + 1):
                def piece(q, carry, j=j, phase=phase):
                    getattr(zero_copy(zs_ref[j] + q * ZERO_ROWS), phase)()
                    return carry
                lax.fori_loop(0, zc_ref[j], piece, 0)

    idx = idx_ref[...]
    eio = lax.broadcasted_iota(I32, (N_EXPERTS, tm), 0)
    member = jnp.zeros((N_EXPERTS, tm), F32)
    for k in range(TOP_K):
        member = member + (idx[k:k + 1, :] == eio).astype(F32)
    tri = (lax.broadcasted_iota(I32, (tm, tm), 0) < lax.broadcasted_iota(I32, (tm, tm), 1)).astype(BF16)
    rank = jnp.dot(member.astype(BF16), tri, preferred_element_type=F32)
    pos = rank + tbv_ref[0][:, 0:1].astype(F32)
    loc_ref[...] = jnp.concatenate(
        [jnp.sum(jnp.where(idx[k:k + 1, :] == eio, pos, 0.0), axis=0, keepdims=True) for k in range(TOP_K)],
        axis=0).astype(I32)
    cs_ref[...] = pos
    ms_ref[...] = member

    jio = lax.broadcasted_iota(I32, (ROUTE_RP, tm), 0).astype(F32)
    sel = [jnp.where((jio == pos[e:e + 1, :]) & (member[e:e + 1, :] > 0.0), 1.0, 0.0).astype(BF16)
           for e in range(N_EXPERTS)]
    rows = jnp.dot(jnp.concatenate(sel, axis=0), xn_ref[...], preferred_element_type=F32)

    stage[slot] = _pack_rows(rows)
    sub = lax.broadcasted_iota(I32, (SUBLANES, HALF), 0)
    for e in range(N_EXPERTS):
        _, head, _ = geometry(e)
        first = pl.ds(e * ROUTE_RP, SUBLANES)
        stage[slot, first, :] = jnp.where(sub < head, head_ref[e], stage[slot, first, :])
        g = jnp.minimum(lax.shift_right_logical(head + cnt_ref[i * N_EXPERTS + e], SUBLANES.bit_length() - 1),
                        groups - 1)
        head_ref[e] = stage[slot, pl.ds(pl.multiple_of(e * ROUTE_RP + g * SUBLANES, SUBLANES), SUBLANES), :]

    def per_expert(e, carry):
        start, head, nchunks = geometry(e)
        end = head + cnt_ref[i * N_EXPERTS + e]

        def per_chunk(c, carry2):
            want = jio + (c * ROUTE_RP).astype(F32)
            pick = jnp.where((want == cs_ref[pl.ds(e, 1), :]) & (ms_ref[pl.ds(e, 1), :] > 0.0), 1.0, 0.0)
            ostage[...] = _pack_rows(jnp.dot(pick.astype(BF16), xn_ref[...], preferred_element_type=F32))
            cp = pltpu.make_async_copy(
                ostage, xs_hbm.at[pl.ds(pl.multiple_of(start + c * ROUTE_RP, SUBLANES), ROUTE_RP)], osem)
            cp.start()
            cp.wait()

            @pl.when(end // ROUTE_RP == c)
            def _():
                g = (end - c * ROUTE_RP) // SUBLANES
                head_ref[e] = ostage[pl.ds(pl.multiple_of(g * SUBLANES, SUBLANES), SUBLANES), :]
            return carry2
        return lax.fori_loop(1, nchunks, per_chunk, carry)

    @pl.when(mt_ref[i] > 0)
    def _():
        lax.fori_loop(0, N_EXPERTS, per_expert, 0)

    @pl.when(i == nt - 1)
    def _():
        @pl.when(i >= 1)
        def _():
            all_chunks(i - 1, "wait")
        all_chunks(i, "start")
        all_chunks(i, "wait")


def _dispatch(plan, idx_t, xn2):
    t = xn2.shape[0]
    tm = ROUTE_TM
    cap = _sorted_rows(t)
    grid_spec = pltpu.PrefetchScalarGridSpec(
        num_scalar_prefetch=5,
        grid=(t // tm,),
        in_specs=[
            pl.BlockSpec((TOP_K, tm), lambda i, *_: (0, i)),
            pl.BlockSpec((1, N_EXPERTS, LANES), lambda i, *_: (i, 0, 0)),
            pl.BlockSpec((tm, D_MODEL), lambda i, *_: (i, 0)),
        ],
        out_specs=[
            pl.BlockSpec(memory_space=pl.ANY),
            pl.BlockSpec((TOP_K, tm), lambda i, *_: (0, i)),
        ],
        scratch_shapes=[
            pltpu.VMEM((2, N_EXPERTS * ROUTE_RP, HALF), U32),
            pltpu.VMEM((ROUTE_RP, HALF), U32),
            pltpu.VMEM((ZERO_ROWS, HALF), U32),
            pltpu.VMEM((N_EXPERTS, SUBLANES, HALF), U32),
            pltpu.VMEM((N_EXPERTS, tm), F32),
            pltpu.VMEM((N_EXPERTS, tm), F32),
            pltpu.SemaphoreType.DMA((2,)),
            pltpu.SemaphoreType.DMA(()),
            pltpu.SemaphoreType.DMA(()),
        ],
    )
    return pl.pallas_call(
        _dispatch_kernel,
        grid_spec=grid_spec,
        out_shape=[
            jax.ShapeDtypeStruct((cap, HALF), U32),
            jax.ShapeDtypeStruct((TOP_K, t), I32),
        ],
        compiler_params=pltpu.CompilerParams(
            dimension_semantics=("arbitrary",), vmem_limit_bytes=_vmem_limit(32 << 20)),
        name="dispatch",
    )(plan["tbase"], plan["cnt"], plan["multi"], plan["zero_start"], plan["zero_cnt"], idx_t,
      jnp.broadcast_to((plan["tbase"] % SUBLANES).reshape(t // tm, N_EXPERTS, 1), (t // tm, N_EXPERTS, LANES)),
      xn2)


def _expert_kernel(be_ref, src_ref, nu_ref, live_ref, xs_ref, wg_hbm, wu_hbm, wd_hbm, bg_ref, bu_ref, bd_ref,
                   ys_ref, wf32, wbf, wsem):
    b = pl.program_id(0)
    e = be_ref[b]

    def weight_copies(expert, par):
        return [pltpu.make_async_copy(w.at[expert], wf32.at[par, m], wsem.at[par])
                for m, w in enumerate((wg_hbm, wu_hbm, wd_hbm))]

    @pl.when(b == 0)
    def _():
        for cp in weight_copies(e, e % 2):
            cp.start()

    @pl.when(b < nu_ref[0])
    def _():
        @pl.when((b == 0) | (e != be_ref[jnp.maximum(b - 1, 0)]))
        def _():
            par = e % 2
            for cp in weight_copies(e, par):
                cp.wait()

            @pl.when(e + 1 < N_EXPERTS)
            def _():
                for cp in weight_copies(e + 1, 1 - par):
                    cp.start()

            for m in range(3):
                wbf[m] = wf32[par, m].astype(BF16)

        def ffn(rows):
            x = _unpack_rows(xs_ref[rows, :])
            gt = jnp.minimum(jnp.dot(x, wbf[0], preferred_element_type=F32) + bg_ref[0], SWIGLU_LIMIT)
            up = jnp.clip(jnp.dot(x, wbf[1], preferred_element_type=F32) + bu_ref[0],
                          -SWIGLU_LIMIT, SWIGLU_LIMIT)
            hdn = (up + 1.0) * (gt * jax.nn.sigmoid(SWIGLU_ALPHA * gt))
            y = jnp.dot(hdn.astype(BF16), wbf[2], preferred_element_type=F32) + bd_ref[0]
            ys_ref[rows, :] = _pack_rows(y.astype(BF16).astype(F32))

        live = live_ref[b]
        for p, part in enumerate(EXP_PARTS):
            smaller = EXP_PARTS[p + 1] if p + 1 < len(EXP_PARTS) else -1
            cond = live > smaller
            if p > 0:
                cond = cond & (live <= part)

            @pl.when(cond)
            def _(part=part):
                if part > 0:
                    ffn(slice(0, part))
                if part < EXP_BM:
                    ys_ref[part:, :] = jnp.zeros((EXP_BM - part, HALF), U32)

    @pl.when(b >= nu_ref[0])
    def _():
        ys_ref[...] = jnp.zeros(ys_ref.shape, U32)


def _experts(plan, xs, w_gate, b_gate, w_up, b_up, w_down, b_down):
    cap = xs.shape[0]
    nb = cap // EXP_BM
    rows = lambda b, be, src, nu, live: (src[b], 0)
    wsel = lambda b, be, src, nu, live: (be[b], 0, 0)
    wspec = pl.BlockSpec(memory_space=pl.ANY)
    bspec = pl.BlockSpec((1, 1, D_FF), wsel)
    grid_spec = pltpu.PrefetchScalarGridSpec(
        num_scalar_prefetch=4,
        grid=(nb,),
        in_specs=[pl.BlockSpec((EXP_BM, HALF), rows), wspec, wspec, wspec, bspec, bspec, bspec],
        out_specs=pl.BlockSpec((EXP_BM, HALF), lambda b, be, src, nu, live: (b, 0)),
        scratch_shapes=[pltpu.VMEM((2, 3, D_MODEL, D_FF), F32),
                        pltpu.VMEM((3, D_MODEL, D_FF), BF16),
                        pltpu.SemaphoreType.DMA((2,))],
    )
    return pl.pallas_call(
        _expert_kernel,
        grid_spec=grid_spec,
        out_shape=jax.ShapeDtypeStruct((cap, HALF), U32),
        compiler_params=pltpu.CompilerParams(
            dimension_semantics=("arbitrary",), vmem_limit_bytes=_vmem_limit(56 << 20)),
        name="experts",
    )(plan["blk_exp"], plan["blk_src"], plan["n_used"], plan["blk_live"], xs, w_gate, w_up, w_down,
      b_gate.reshape(N_EXPERTS, 1, D_FF), b_up.reshape(N_EXPERTS, 1, D_FF),
      b_down.reshape(N_EXPERTS, 1, D_MODEL))


def _combine_kernel(tb_ref, cnt_ref, mt_ref, h_ref, idx_ref, loc_ref, gate_ref, ys_hbm, g_ref, o_ref,
                    ybuf, obuf, acc_ref, sem, osem):
    i = pl.program_id(0)
    nt = pl.num_programs(0)
    slot = i % 2
    tm = h_ref.shape[0]

    def geometry(step, e):
        return _chunk_geometry(tb_ref[step * N_EXPERTS + e], cnt_ref[step * N_EXPERTS + e])

    def chunk_copy(step, e, s):
        start, _, _ = geometry(step, e)
        return pltpu.make_async_copy(ys_hbm.at[pl.ds(start, ROUTE_RP)],
                                     ybuf.at[s, pl.ds(e * ROUTE_RP, ROUTE_RP)], sem.at[s])

    @pl.when(i == 0)
    def _():
        for e in range(N_EXPERTS):
            chunk_copy(0, e, 0).start()

    for e in range(N_EXPERTS):
        chunk_copy(jnp.minimum(i + 1, nt - 1), e, 1 - slot).start()

    idx_loc = jnp.transpose(jnp.concatenate([idx_ref[...], loc_ref[...]], axis=0))
    idx, loc = idx_loc[:, :TOP_K], idx_loc[:, TOP_K:]
    gate = jnp.transpose(jnp.concatenate([gate_ref[...], gate_ref[...]], axis=0))[:, :TOP_K]

    def gate_matrix(ncols, col_of):
        colio = lax.broadcasted_iota(I32, (tm, ncols), 1)
        g = jnp.zeros((tm, ncols), F32)
        for k in range(TOP_K):
            g = jnp.where(colio == col_of[:, k:k + 1], gate[:, k:k + 1], g)
        return g.astype(BF16)

    pltpu.make_async_copy(ys_hbm.at[pl.ds(0, N_EXPERTS * ROUTE_RP)], ybuf.at[0], sem.at[slot]).wait()
    col = jnp.where(loc < ROUTE_RP, idx * ROUTE_RP + loc, -1)
    acc_ref[...] = jnp.dot(gate_matrix(N_EXPERTS * ROUTE_RP, col), _unpack_rows(ybuf[slot]),
                           preferred_element_type=F32)

    def per_expert(e, carry):
        start, _, nchunks = geometry(i, e)

        def per_chunk(c, carry2):
            cp = pltpu.make_async_copy(
                ys_hbm.at[pl.ds(pl.multiple_of(start + c * ROUTE_RP, SUBLANES), ROUTE_RP)], obuf, osem)
            cp.start()
            cp.wait()
            ccol = jnp.where(idx == e, loc - c * ROUTE_RP, -1)
            acc_ref[...] += jnp.dot(gate_matrix(ROUTE_RP, ccol), _unpack_rows(obuf[...]),
                                    preferred_element_type=F32)
            return carry2
        return lax.fori_loop(1, nchunks, per_chunk, carry)

    @pl.when(mt_ref[i] > 0)
    def _():
        lax.fori_loop(0, N_EXPERTS, per_expert, 0)

    out = h_ref[...] + acc_ref[...]
    ms = jnp.mean(out * out, axis=-1, keepdims=True)
    o_ref[...] = out * lax.rsqrt(ms + EPS) * g_ref[...]

    @pl.when(i == nt - 1)
    def _():
        pltpu.make_async_copy(ys_hbm.at[pl.ds(0, N_EXPERTS * ROUTE_RP)], ybuf.at[0], sem.at[1 - slot]).wait()


def _combine(plan, h, ys, idx_t, loc_t, gate_t, g_final):
    t = h.shape[0]
    tm = ROUTE_TM
    tok = lambda i, *_: (i, 0)
    lane = lambda i, *_: (0, i)
    grid_spec = pltpu.PrefetchScalarGridSpec(
        num_scalar_prefetch=3,
        grid=(t // tm,),
        in_specs=[
            pl.BlockSpec((tm, D_MODEL), tok),
            pl.BlockSpec((TOP_K, tm), lane),
            pl.BlockSpec((TOP_K, tm), lane),
            pl.BlockSpec((TOP_K, tm), lane),
            pl.BlockSpec(memory_space=pl.ANY),
            pl.BlockSpec((1, D_MODEL), lambda i, *_: (0, 0)),
        ],
        out_specs=pl.BlockSpec((tm, D_MODEL), tok),
        scratch_shapes=[
            pltpu.VMEM((2, N_EXPERTS * ROUTE_RP, HALF), U32),
            pltpu.VMEM((ROUTE_RP, HALF), U32),
            pltpu.VMEM((tm, D_MODEL), F32),
            pltpu.SemaphoreType.DMA((2,)),
            pltpu.SemaphoreType.DMA(()),
        ],
    )
    return pl.pallas_call(
        _combine_kernel,
        grid_spec=grid_spec,
        out_shape=jax.ShapeDtypeStruct((t, D_MODEL), F32),
        compiler_params=pltpu.CompilerParams(
            dimension_semantics=("arbitrary",), vmem_limit_bytes=_vmem_limit(32 << 20)),
        name="combine",
    )(plan["tbase"], plan["cnt"], plan["multi"], h, idx_t, loc_t, gate_t, ys,
      g_final.reshape(1, D_MODEL))


def _moe(h, xn2, idx_t, gate_t, w_gate, b_gate, w_up, b_up, w_down, b_down, g_final):
    t = h.shape[0]
    plan = _route_plan(idx_t, t)
    xs, loc_t = _dispatch(plan, idx_t, xn2)
    ys = _experts(plan, xs, w_gate, b_gate, w_up, b_up, w_down, b_down)
    return _combine(plan, h, ys, idx_t, loc_t, gate_t, g_final)


def kernel(x, g_mix, w_in, b_in, conv_w, conv_b, ln_g, ln_b, rpb, w_out, b_out, g_ffn, w_router,
           b_router, w_gate, b_gate, w_up, b_up, w_down, b_down, g_final):
    bsz, seq, d = x.shape
    assert bsz == 1 and d == D_MODEL and g_mix.shape[0] == 1
    assert seq % (GRID_W * WIN_H) == 0
    x2 = x.reshape(seq, d)
    q, k, v, conv_out = _inproj_conv(x2, g_mix[0], w_in[0], b_in[0], conv_w[0], conv_b[0], ln_g[0], ln_b[0])
    na_out = _attention(q, k, v, rpb[0])
    h, xn2, idx_t, gate_t = _outproj_router(conv_out, na_out, x2, w_out[0], b_out[0], g_ffn[0],
                                            w_router[0], b_router[0])
    out = _moe(h, xn2, idx_t, gate_t, w_gate[0], b_gate[0], w_up[0], b_up[0], w_down[0], b_down[0],
               g_final)
    return out.reshape(bsz, seq, d)
```

```python
import functools

import numpy as np
import jax
import jax.numpy as jnp
from jax import lax
from jax.experimental import pallas as pl
from jax.experimental.pallas import tpu as pltpu

F32 = jnp.float32
BF16 = jnp.bfloat16
U32 = jnp.uint32
I32 = jnp.int32

D_MODEL = 1024
GRID_W = 64
D_CONV = 512
CONV_K = 31
NA_HEADS = 8
NA_HEAD_DIM = 64
D_NA = NA_HEADS * NA_HEAD_DIM
D_IN = 2 * D_CONV + 3 * D_NA
WIN_H = 8
WIN_W = 16
N_EXPERTS = 32
TOP_K = 4
D_FF = D_MODEL
SWIGLU_LIMIT = 7.0
SWIGLU_ALPHA = 1.702
EPS = 1e-5
MASK_VALUE = -1e30

V7X_VMEM_BYTES = 64 * 1024 * 1024
LANES = 128

PROJ_TM = 1024
OUTPROJ_TM = 1024
ROUTER_SUB = 512
CONV_HALO = 16
ATTN_RB = 32
ATTN_UNROLL = 16
ROUTE_TM = 256
SUBLANES = 8
ROUTE_RP = 64
EXP_BM = 512
EXP_PARTS = (512, 256, 128, 0)
FFN_CHUNK = 512
ZERO_ROWS = 256
HALF = D_MODEL // 2
BF16_BITS = 16
HIGH_HALF = 0xFFFF0000


def _vmem_limit(nbytes):
    return int(min(nbytes, V7X_VMEM_BYTES - 6 * 1024 * 1024))


def _pack_rows(x):
    lo = lax.bitcast_convert_type(x[:, :HALF], U32) >> BF16_BITS
    hi = lax.bitcast_convert_type(x[:, HALF:], U32) & jnp.uint32(HIGH_HALF)
    return lo | hi


def _unpack_rows(w):
    lo = lax.bitcast_convert_type(w << BF16_BITS, F32)
    hi = lax.bitcast_convert_type(w & jnp.uint32(HIGH_HALF), F32)
    return jnp.concatenate([lo, hi], axis=1).astype(BF16)


def _inproj_conv_kernel(x_ref, g_ref, w_ref, b_ref, cw_ref, cb_ref, lg_ref, lb_ref,
                        q_ref, k_ref, v_ref, o_ref, ext_ref, u_ref, sh_ref):
    i = pl.program_id(0)
    nt = pl.num_programs(0) - 1
    tm = x_ref.shape[0]

    @pl.when(i == 0)
    def _():
        ext_ref[...] = jnp.zeros(ext_ref.shape, F32)

    @pl.when(i < nt)
    def _():
        x = x_ref[...]
        ms = jnp.mean(x * x, axis=-1, keepdims=True)
        xn = (x * lax.rsqrt(ms + EPS) * g_ref[...]).astype(BF16)

        def proj(c):
            sl = slice(c * D_CONV, (c + 1) * D_CONV)
            return jnp.dot(xn, w_ref[:, sl], preferred_element_type=F32) + b_ref[:, sl]

        u_ref[...] = proj(0) * jax.nn.sigmoid(proj(1))
        q_ref[...] = (proj(2) * (NA_HEAD_DIM ** -0.5)).astype(BF16)
        k_ref[...] = proj(3).astype(BF16)
        v_ref[...] = proj(4).astype(BF16)

    ext_ref[CONV_HALO + tm:, :] = jnp.where(i < nt, u_ref[0:CONV_HALO, :], 0.0)

    @pl.when(i >= 1)
    def _():
        base = CONV_HALO - CONV_K // 2
        span = (CONV_K - 1) // SUBLANES * SUBLANES
        for b in range(SUBLANES):
            sh_ref[b] = ext_ref[base + b:base + b + tm + span, :]
        acc = jnp.zeros((tm, D_CONV), F32)
        for b in range(SUBLANES):
            for a in range((CONV_K - 1 - b) // SUBLANES + 1):
                r = SUBLANES * a
                acc = acc + sh_ref[b, r:r + tm, :] * cw_ref[r + b:r + b + 1, :]
        acc = acc + cb_ref[...]
        mu = jnp.mean(acc, axis=-1, keepdims=True)
        d = acc - mu
        var = jnp.mean(d * d, axis=-1, keepdims=True)
        un = d * lax.rsqrt(var + EPS) * lg_ref[...] + lb_ref[...]
        o_ref[...] = (un * jax.nn.sigmoid(un)).astype(o_ref.dtype)

    ext_ref[0:CONV_HALO, :] = ext_ref[tm:tm + CONV_HALO, :]
    ext_ref[CONV_HALO:CONV_HALO + tm, :] = u_ref[...]


def _inproj_conv(x2, g_mix, w_in, b_in, conv_w, conv_b, ln_g, ln_b):
    t = x2.shape[0]
    tm = min(PROJ_TM, t)
    nt = t // tm
    tok = lambda i: (jnp.minimum(i, nt - 1), 0)
    prev = lambda i: (jnp.maximum(i - 1, 0), 0)
    const = lambda i: (0, 0)
    w = jnp.zeros((CONV_K + 1, D_CONV), F32).at[:CONV_K].set(conv_w)
    span = (CONV_K - 1) // SUBLANES * SUBLANES
    return pl.pallas_call(
        _inproj_conv_kernel,
        grid=(nt + 1,),
        in_specs=[
            pl.BlockSpec((tm, D_MODEL), tok),
            pl.BlockSpec((1, D_MODEL), const),
            pl.BlockSpec((D_MODEL, D_IN), const),
            pl.BlockSpec((1, D_IN), const),
            pl.BlockSpec((CONV_K + 1, D_CONV), const),
            pl.BlockSpec((1, D_CONV), const),
            pl.BlockSpec((1, D_CONV), const),
            pl.BlockSpec((1, D_CONV), const),
        ],
        out_specs=[
            pl.BlockSpec((tm, D_NA), tok),
            pl.BlockSpec((tm, D_NA), tok),
            pl.BlockSpec((tm, D_NA), tok),
            pl.BlockSpec((tm, D_CONV), prev),
        ],
        out_shape=[
            jax.ShapeDtypeStruct((t, D_NA), BF16),
            jax.ShapeDtypeStruct((t, D_NA), BF16),
            jax.ShapeDtypeStruct((t, D_NA), BF16),
            jax.ShapeDtypeStruct((t, D_CONV), BF16),
        ],
        scratch_shapes=[pltpu.VMEM((tm + 2 * CONV_HALO, D_CONV), F32),
                        pltpu.VMEM((tm, D_CONV), F32),
                        pltpu.VMEM((SUBLANES, tm + span, D_CONV), F32)],
        compiler_params=pltpu.CompilerParams(
            dimension_semantics=("arbitrary",), vmem_limit_bytes=_vmem_limit(56 << 20)),
        name="inproj_conv",
    )(x2, g_mix.reshape(1, D_MODEL), w_in.astype(BF16), b_in.reshape(1, D_IN), w,
      conv_b.reshape(1, D_CONV), ln_g.reshape(1, D_CONV), ln_b.reshape(1, D_CONV))


def _column_bias(rpb):
    col = np.arange(GRID_W)
    c0 = np.clip(col - WIN_W // 2, 0, GRID_W - WIN_W)
    valid = (col[None, :] >= c0[:, None]) & (col[None, :] < c0[:, None] + WIN_W)
    dcol = np.clip(col[None, :] - col[:, None], -(WIN_W - 1), WIN_W - 1) + (WIN_W - 1)
    col_sel = ((dcol[:, :, None] == np.arange(2 * WIN_W - 1)) & valid[:, :, None]).astype(np.float32)
    b = jnp.einsum("hdj,qkj->hdqk", rpb.astype(F32), col_sel, precision=lax.Precision.HIGHEST)
    b = jnp.where(valid[None, None], b, MASK_VALUE)
    return jnp.concatenate([b, b], axis=-1)


def _attn_kernel(q_ref, k_ref, v_ref, cb_ref, o_ref, s_ref, bt_ref, *, rb, rows):
    jb = pl.program_id(1)
    lane = lax.broadcasted_iota(jnp.int32, (GRID_W, LANES), 1)
    first_head = lane < NA_HEAD_DIM
    nkeys = WIN_H * GRID_W

    @pl.when(jb == 0)
    def _():
        for h in range(2):
            for c in range(WIN_H):
                for rr in range(0, WIN_H, 2):
                    even = cb_ref[h, rr - c + WIN_H - 1]
                    odd = cb_ref[h, rr + 1 - c + WIN_H - 1]
                    bt_ref[h, c, :, rr * GRID_W:(rr + 2) * GRID_W] = jnp.where(first_head, even, odd)

    def window(r):
        i = jb * rb + r
        r0 = jnp.clip(i - WIN_H // 2, 0, rows - WIN_H)
        return i - r0, pl.multiple_of(r0 * GRID_W, GRID_W)

    def scores(r):
        _, start = window(r)
        q2 = q_ref[pl.ds(pl.multiple_of(r * GRID_W, GRID_W), GRID_W), :]
        zero = jnp.zeros_like(q2)
        qst = jnp.concatenate([jnp.where(first_head, q2, zero), jnp.where(first_head, zero, q2)], axis=0)
        ks = k_ref[pl.ds(start, nkeys), :]
        return lax.dot_general(qst, ks, (((1,), (1,)), ((), ())), preferred_element_type=F32)

    def finish(r, s):
        c, start = window(r)
        vs = v_ref[pl.ds(start, nkeys), :]
        bias = jnp.concatenate([bt_ref[0, c], bt_ref[1, c]], axis=0)
        s = jnp.where(bias > 0.5 * MASK_VALUE, s + bias, MASK_VALUE)
        m = jnp.max(s, axis=-1, keepdims=True)
        p = jnp.exp(s - m)
        l = jnp.sum(p, axis=-1, keepdims=True)
        o = jnp.dot(p.astype(BF16), vs, preferred_element_type=F32) / l
        o2 = jnp.where(first_head, o[:GRID_W], o[GRID_W:])
        o_ref[pl.ds(pl.multiple_of(r * GRID_W, GRID_W), GRID_W), :] = o2.astype(o_ref.dtype)

    s_ref[0] = scores(0)

    def pair(p, carry):
        r = 2 * p
        s_ref[1] = scores(r + 1)
        finish(r, s_ref[0])
        s_ref[0] = scores(jnp.minimum(r + 2, rb - 1))
        finish(r + 1, s_ref[1])
        return carry

    lax.fori_loop(0, rb // 2, pair, 0, unroll=ATTN_UNROLL)


def _attention(q, k, v, rpb):
    t = q.shape[0]
    rows = t // GRID_W
    rb = min(ATTN_RB, rows)
    npairs = D_NA // LANES
    cb = _column_bias(rpb)
    kern = functools.partial(_attn_kernel, rb=rb, rows=rows)
    return pl.pallas_call(
        kern,
        grid=(npairs, rows // rb),
        in_specs=[
            pl.BlockSpec((rb * GRID_W, LANES), lambda p, j: (j, p)),
            pl.BlockSpec((t, LANES), lambda p, j: (0, p)),
            pl.BlockSpec((t, LANES), lambda p, j: (0, p)),
            pl.BlockSpec((2, 2 * WIN_H - 1, GRID_W, 2 * GRID_W), lambda p, j: (p, 0, 0, 0)),
        ],
        out_specs=pl.BlockSpec((rb * GRID_W, LANES), lambda p, j: (j, p)),
        out_shape=jax.ShapeDtypeStruct((t, D_NA), BF16),
        scratch_shapes=[pltpu.VMEM((2, 2 * GRID_W, WIN_H * GRID_W), F32),
                        pltpu.VMEM((2, WIN_H, GRID_W, WIN_H * GRID_W), F32)],
        compiler_params=pltpu.CompilerParams(
            dimension_semantics=("arbitrary", "arbitrary"), vmem_limit_bytes=_vmem_limit(40 << 20)),
        name="attention",
    )(q, k, v, cb)


def _outproj_kernel(conv_ref, na_ref, x_ref, wo_ref, bo_ref, g_ref, wr_ref, br_ref,
                    h_ref, xn_ref, idx_ref, gate_ref):
    for s in range(x_ref.shape[0] // ROUTER_SUB):
        rows = slice(s * ROUTER_SUB, (s + 1) * ROUTER_SUB)
        mixed = jnp.concatenate([conv_ref[rows, :], na_ref[rows, :]], axis=-1)
        h = x_ref[rows, :] + jnp.dot(mixed, wo_ref[...], preferred_element_type=F32) + bo_ref[...]
        h_ref[rows, :] = h
        ms = jnp.mean(h * h, axis=-1, keepdims=True)
        xn = (h * lax.rsqrt(ms + EPS) * g_ref[...]).astype(BF16)
        xn_ref[rows, :] = xn
        logits = jnp.transpose(jnp.dot(xn, wr_ref[...], preferred_element_type=F32) + br_ref[...])[:N_EXPERTS]
        ids = lax.broadcasted_iota(jnp.int32, logits.shape, 0)
        vals, sels = [], []
        l = logits
        for _ in range(TOP_K):
            m = jnp.max(l, axis=0, keepdims=True)
            sel = jnp.min(jnp.where(l == m, ids, N_EXPERTS), axis=0, keepdims=True)
            vals.append(m)
            sels.append(sel)
            l = jnp.where(ids == sel, -jnp.inf, l)
        es = [jnp.exp(vk - vals[0]) for vk in vals]
        tot = es[0] + es[1] + es[2] + es[3]
        idx_ref[:, rows] = jnp.concatenate(sels, axis=0)
        gate_ref[:, rows] = jnp.concatenate([e / tot for e in es], axis=0)


def _outproj_router(conv_out, na_out, x2, w_out, b_out, g_ffn, w_router, b_router):
    t = x2.shape[0]
    tm = min(OUTPROJ_TM, t)
    tok = lambda i: (i, 0)
    const = lambda i: (0, 0)
    return pl.pallas_call(
        _outproj_kernel,
        grid=(t // tm,),
        in_specs=[
            pl.BlockSpec((tm, D_CONV), tok),
            pl.BlockSpec((tm, D_NA), tok),
            pl.BlockSpec((tm, D_MODEL), tok),
            pl.BlockSpec((D_MODEL, D_MODEL), const),
            pl.BlockSpec((1, D_MODEL), const),
            pl.BlockSpec((1, D_MODEL), const),
            pl.BlockSpec((D_MODEL, LANES), const),
            pl.BlockSpec((1, LANES), const),
        ],
        out_specs=[
            pl.BlockSpec((tm, D_MODEL), tok),
            pl.BlockSpec((tm, D_MODEL), tok),
            pl.BlockSpec((TOP_K, tm), lambda i: (0, i)),
            pl.BlockSpec((TOP_K, tm), lambda i: (0, i)),
        ],
        out_shape=[
            jax.ShapeDtypeStruct((t, D_MODEL), F32),
            jax.ShapeDtypeStruct((t, D_MODEL), BF16),
            jax.ShapeDtypeStruct((TOP_K, t), jnp.int32),
            jax.ShapeDtypeStruct((TOP_K, t), F32),
        ],
        compiler_params=pltpu.CompilerParams(
            dimension_semantics=("arbitrary",), vmem_limit_bytes=_vmem_limit(48 << 20)),
        name="outproj_router",
    )(conv_out, na_out, x2, w_out.astype(BF16), b_out.reshape(1, D_MODEL), g_ffn.reshape(1, D_MODEL),
      jnp.pad(w_router.astype(BF16), ((0, 0), (0, LANES - N_EXPERTS))),
      jnp.pad(b_router.reshape(1, N_EXPERTS), ((0, 0), (0, LANES - N_EXPERTS))))


def _route_plan(idx_t, t):
    nt = t // ROUTE_TM
    experts = jnp.arange(N_EXPERTS, dtype=I32)
    onehot = idx_t.reshape(TOP_K, nt, ROUTE_TM, 1) == experts
    cnt = jnp.sum(onehot, axis=(0, 2), dtype=I32)
    sizes = jnp.sum(cnt, axis=0)
    padded = (sizes + ROUTE_RP + EXP_BM - 1) // EXP_BM * EXP_BM
    pad_ends = jnp.cumsum(padded)
    pad_off = pad_ends - padded
    tbase = pad_off[None, :] + jnp.cumsum(cnt, axis=0) - cnt
    cap = _sorted_rows(t)
    nb = cap // EXP_BM
    n_used = pad_ends[N_EXPERTS - 1] // EXP_BM
    blk_src = jnp.minimum(jnp.arange(nb, dtype=I32), n_used - 1)
    blk_exp = jnp.minimum(
        jnp.sum(blk_src[:, None] * EXP_BM >= pad_ends[None, :], axis=1, dtype=I32), N_EXPERTS - 1)
    row_end = jnp.sum(jnp.where(blk_exp[:, None] == experts[None, :], (pad_off + sizes)[None, :], 0), axis=1)
    blk_live = jnp.clip(row_end - blk_src * EXP_BM, 0, EXP_BM).astype(I32)
    multi = jnp.any(tbase % SUBLANES + cnt > ROUTE_RP, axis=1).astype(I32)
    zero_from = jnp.concatenate([(pad_off + sizes) // ZERO_ROWS * ZERO_ROWS, pad_ends[N_EXPERTS - 1:]])
    zero_to = jnp.concatenate([pad_ends, jnp.full((1,), cap, I32)])
    return dict(cnt=cnt.reshape(-1), tbase=tbase.reshape(-1).astype(I32), multi=multi,
                zero_start=zero_from.astype(I32), zero_cnt=((zero_to - zero_from) // ZERO_ROWS).astype(I32),
                blk_src=blk_src, blk_exp=blk_exp, blk_live=blk_live, n_used=n_used.reshape(1).astype(I32))


def _sorted_rows(t):
    cap = t * TOP_K + N_EXPERTS * (EXP_BM + ROUTE_RP)
    return (cap + EXP_BM - 1) // EXP_BM * EXP_BM


def _chunk_geometry(tb, n):
    head = tb & (SUBLANES - 1)
    start = pl.multiple_of(tb - head, SUBLANES)
    nchunks = (head + n + ROUTE_RP - 1) // ROUTE_RP
    return start, head, nchunks


def _dispatch_kernel(tb_ref, cnt_ref, mt_ref, zs_ref, zc_ref, idx_ref, tbv_ref, xn_ref, xs_hbm, loc_ref,
                     stage, ostage, zbuf, head_ref, cs_ref, ms_ref, sem, osem, zsem):
    i = pl.program_id(0)
    nt = pl.num_programs(0)
    slot = i % 2
    tm = idx_ref.shape[1]
    groups = ROUTE_RP // SUBLANES

    def geometry(e, step=None):
        step = i if step is None else step
        return _chunk_geometry(tb_ref[step * N_EXPERTS + e], cnt_ref[step * N_EXPERTS + e])

    def chunk_copy(step, e):
        start, _, _ = geometry(e, step)
        return pltpu.make_async_copy(stage.at[step % 2, pl.ds(e * ROUTE_RP, ROUTE_RP)],
                                     xs_hbm.at[pl.ds(start, ROUTE_RP)], sem.at[step % 2])

    def all_chunks(step, phase):
        if phase == "wait":
            pltpu.make_async_copy(stage.at[0], xs_hbm.at[pl.ds(0, N_EXPERTS * ROUTE_RP)], sem.at[step % 2]).wait()
            return
        for e in range(N_EXPERTS):
            chunk_copy(step, e).start()

    @pl.when(i >= 2)
    def _():
        all_chunks(i - 2, "wait")

    @pl.when(i >= 1)
    def _():
        all_chunks(i - 1, "start")

    def zero_copy(start):
        return pltpu.make_async_copy(zbuf, xs_hbm.at[pl.ds(pl.multiple_of(start, ZERO_ROWS), ZERO_ROWS)], zsem)

    @pl.when(i == 0)
    def _():
        zbuf[...] = jnp.zeros(zbuf.shape, U32)
        head_ref[...] = jnp.zeros(head_ref.shape, U32)
        for phase in ("start", "wait"):
            for j in range(N_EXPERTS + 1):
                def piece(q, carry, j=j, phase=phase):
                    getattr(zero_copy(zs_ref[j] + q * ZERO_ROWS), phase)()
                    return carry
                lax.fori_loop(0, zc_ref[j], piece, 0)

    idx = idx_ref[...]
    eio = lax.broadcasted_iota(I32, (N_EXPERTS, tm), 0)
    member = jnp.zeros((N_EXPERTS, tm), F32)
    for k in range(TOP_K):
        member = member + (idx[k:k + 1, :] == eio).astype(F32)
    tri = (lax.broadcasted_iota(I32, (tm, tm), 0) < lax.broadcasted_iota(I32, (tm, tm), 1)).astype(BF16)
    rank = jnp.dot(member.astype(BF16), tri, preferred_element_type=F32)
    pos = rank + tbv_ref[0][:, 0:1].astype(F32)
    loc_ref[...] = jnp.concatenate(
        [jnp.sum(jnp.where(idx[k:k + 1, :] == eio, pos, 0.0), axis=0, keepdims=True) for k in range(TOP_K)],
        axis=0).astype(I32)
    cs_ref[...] = pos
    ms_ref[...] = member

    jio = lax.broadcasted_iota(I32, (ROUTE_RP, tm), 0).astype(F32)
    sel = [jnp.where((jio == pos[e:e + 1, :]) & (member[e:e + 1, :] > 0.0), 1.0, 0.0).astype(BF16)
           for e in range(N_EXPERTS)]
    rows = jnp.dot(jnp.concatenate(sel, axis=0), xn_ref[...], preferred_element_type=F32)

    stage[slot] = _pack_rows(rows)
    sub = lax.broadcasted_iota(I32, (SUBLANES, HALF), 0)
    for e in range(N_EXPERTS):
        _, head, _ = geometry(e)
        first = pl.ds(e * ROUTE_RP, SUBLANES)
        stage[slot, first, :] = jnp.where(sub < head, head_ref[e], stage[slot, first, :])
        g = jnp.minimum(lax.shift_right_logical(head + cnt_ref[i * N_EXPERTS + e], SUBLANES.bit_length() - 1),
                        groups - 1)
        head_ref[e] = stage[slot, pl.ds(pl.multiple_of(e * ROUTE_RP + g * SUBLANES, SUBLANES), SUBLANES), :]

    def per_expert(e, carry):
        start, head, nchunks = geometry(e)
        end = head + cnt_ref[i * N_EXPERTS + e]

        def per_chunk(c, carry2):
            want = jio + (c * ROUTE_RP).astype(F32)
            pick = jnp.where((want == cs_ref[pl.ds(e, 1), :]) & (ms_ref[pl.ds(e, 1), :] > 0.0), 1.0, 0.0)
            ostage[...] = _pack_rows(jnp.dot(pick.astype(BF16), xn_ref[...], preferred_element_type=F32))
            cp = pltpu.make_async_copy(
                ostage, xs_hbm.at[pl.ds(pl.multiple_of(start + c * ROUTE_RP, SUBLANES), ROUTE_RP)], osem)
            cp.start()
            cp.wait()

            @pl.when(end // ROUTE_RP == c)
            def _():
                g = (end - c * ROUTE_RP) // SUBLANES
                head_ref[e] = ostage[pl.ds(pl.multiple_of(g * SUBLANES, SUBLANES), SUBLANES), :]
            return carry2
        return lax.fori_loop(1, nchunks, per_chunk, carry)

    @pl.when(mt_ref[i] > 0)
    def _():
        lax.fori_loop(0, N_EXPERTS, per_expert, 0)

    @pl.when(i == nt - 1)
    def _():
        @pl.when(i >= 1)
        def _():
            all_chunks(i - 1, "wait")
        all_chunks(i, "start")
        all_chunks(i, "wait")


def _dispatch(plan, idx_t, xn2):
    t = xn2.shape[0]
    tm = ROUTE_TM
    cap = _sorted_rows(t)
    grid_spec = pltpu.PrefetchScalarGridSpec(
        num_scalar_prefetch=5,
        grid=(t // tm,),
        in_specs=[
            pl.BlockSpec((TOP_K, tm), lambda i, *_: (0, i)),
            pl.BlockSpec((1, N_EXPERTS, LANES), lambda i, *_: (i, 0, 0)),
            pl.BlockSpec((tm, D_MODEL), lambda i, *_: (i, 0)),
        ],
        out_specs=[
            pl.BlockSpec(memory_space=pl.ANY),
            pl.BlockSpec((TOP_K, tm), lambda i, *_: (0, i)),
        ],
        scratch_shapes=[
            pltpu.VMEM((2, N_EXPERTS * ROUTE_RP, HALF), U32),
            pltpu.VMEM((ROUTE_RP, HALF), U32),
            pltpu.VMEM((ZERO_ROWS, HALF), U32),
            pltpu.VMEM((N_EXPERTS, SUBLANES, HALF), U32),
            pltpu.VMEM((N_EXPERTS, tm), F32),
            pltpu.VMEM((N_EXPERTS, tm), F32),
            pltpu.SemaphoreType.DMA((2,)),
            pltpu.SemaphoreType.DMA(()),
            pltpu.SemaphoreType.DMA(()),
        ],
    )
    return pl.pallas_call(
        _dispatch_kernel,
        grid_spec=grid_spec,
        out_shape=[
            jax.ShapeDtypeStruct((cap, HALF), U32),
            jax.ShapeDtypeStruct((TOP_K, t), I32),
        ],
        compiler_params=pltpu.CompilerParams(
            dimension_semantics=("arbitrary",), vmem_limit_bytes=_vmem_limit(32 << 20)),
        name="dispatch",
    )(plan["tbase"], plan["cnt"], plan["multi"], plan["zero_start"], plan["zero_cnt"], idx_t,
      jnp.broadcast_to((plan["tbase"] % SUBLANES).reshape(t // tm, N_EXPERTS, 1), (t // tm, N_EXPERTS, LANES)),
      xn2)


def _expert_kernel(be_ref, src_ref, nu_ref, live_ref, xs_ref, wg_hbm, wu_hbm, wd_hbm, bg_ref, bu_ref, bd_ref,
                   ys_ref, wf32, wbf, wsem):
    b = pl.program_id(0)
    e = be_ref[b]

    def weight_copies(expert, par):
        return [pltpu.make_async_copy(w.at[expert], wf32.at[par, m], wsem.at[par])
                for m, w in enumerate((wg_hbm, wu_hbm, wd_hbm))]

    @pl.when(b == 0)
    def _():
        for cp in weight_copies(e, e % 2):
            cp.start()

    @pl.when(b < nu_ref[0])
    def _():
        @pl.when((b == 0) | (e != be_ref[jnp.maximum(b - 1, 0)]))
        def _():
            par = e % 2
            for cp in weight_copies(e, par):
                cp.wait()

            @pl.when(e + 1 < N_EXPERTS)
            def _():
                for cp in weight_copies(e + 1, 1 - par):
                    cp.start()

            for m in range(3):
                wbf[m] = wf32[par, m].astype(BF16)

        def ffn(rows):
            x = _unpack_rows(xs_ref[rows, :])
            hidden = []
            for c in range(D_FF // FFN_CHUNK):
                cols = slice(c * FFN_CHUNK, (c + 1) * FFN_CHUNK)
                gt = jnp.minimum(jnp.dot(x, wbf[0, :, cols], preferred_element_type=F32)
                                 + bg_ref[0, :, cols], SWIGLU_LIMIT)
                up = jnp.clip(jnp.dot(x, wbf[1, :, cols], preferred_element_type=F32)
                              + bu_ref[0, :, cols], -SWIGLU_LIMIT, SWIGLU_LIMIT)
                hidden.append(((up + 1.0) * (gt * jax.nn.sigmoid(SWIGLU_ALPHA * gt))).astype(BF16))
            y = jnp.dot(jnp.concatenate(hidden, axis=1), wbf[2], preferred_element_type=F32) + bd_ref[0]
            ys_ref[rows, :] = _pack_rows(y.astype(BF16).astype(F32))

        live = live_ref[b]
        for p, part in enumerate(EXP_PARTS):
            smaller = EXP_PARTS[p + 1] if p + 1 < len(EXP_PARTS) else -1
            cond = live > smaller
            if p > 0:
                cond = cond & (live <= part)

            @pl.when(cond)
            def _(part=part):
                if part > 0:
                    ffn(slice(0, part))
                if part < EXP_BM:
                    ys_ref[part:, :] = jnp.zeros((EXP_BM - part, HALF), U32)

    @pl.when(b >= nu_ref[0])
    def _():
        ys_ref[...] = jnp.zeros(ys_ref.shape, U32)


def _experts(plan, xs, w_gate, b_gate, w_up, b_up, w_down, b_down):
    cap = xs.shape[0]
    nb = cap // EXP_BM
    rows = lambda b, be, src, nu, live: (src[b], 0)
    wsel = lambda b, be, src, nu, live: (be[b], 0, 0)
    wspec = pl.BlockSpec(memory_space=pl.ANY)
    bspec = pl.BlockSpec((1, 1, D_FF), wsel)
    grid_spec = pltpu.PrefetchScalarGridSpec(
        num_scalar_prefetch=4,
        grid=(nb,),
        in_specs=[pl.BlockSpec((EXP_BM, HALF), rows), wspec, wspec, wspec, bspec, bspec, bspec],
        out_specs=pl.BlockSpec((EXP_BM, HALF), lambda b, be, src, nu, live: (b, 0)),
        scratch_shapes=[pltpu.VMEM((2, 3, D_MODEL, D_FF), F32),
                        pltpu.VMEM((3, D_MODEL, D_FF), BF16),
                        pltpu.SemaphoreType.DMA((2,))],
    )
    return pl.pallas_call(
        _expert_kernel,
        grid_spec=grid_spec,
        out_shape=jax.ShapeDtypeStruct((cap, HALF), U32),
        compiler_params=pltpu.CompilerParams(
            dimension_semantics=("arbitrary",), vmem_limit_bytes=_vmem_limit(56 << 20)),
        name="experts",
    )(plan["blk_exp"], plan["blk_src"], plan["n_used"], plan["blk_live"], xs, w_gate, w_up, w_down,
      b_gate.reshape(N_EXPERTS, 1, D_FF), b_up.reshape(N_EXPERTS, 1, D_FF),
      b_down.reshape(N_EXPERTS, 1, D_MODEL))


def _combine_kernel(tb_ref, cnt_ref, mt_ref, h_ref, idx_ref, loc_ref, gate_ref, ys_hbm, g_ref, o_ref,
                    ybuf, obuf, acc_ref, sem, osem):
    i = pl.program_id(0)
    nt = pl.num_programs(0)
    slot = i % 2
    tm = h_ref.shape[0]

    def geometry(step, e):
        return _chunk_geometry(tb_ref[step * N_EXPERTS + e], cnt_ref[step * N_EXPERTS + e])

    def chunk_copy(step, e, s):
        start, _, _ = geometry(step, e)
        return pltpu.make_async_copy(ys_hbm.at[pl.ds(start, ROUTE_RP)],
                                     ybuf.at[s, pl.ds(e * ROUTE_RP, ROUTE_RP)], sem.at[s])

    @pl.when(i == 0)
    def _():
        for e in range(N_EXPERTS):
            chunk_copy(0, e, 0).start()

    for e in range(N_EXPERTS):
        chunk_copy(jnp.minimum(i + 1, nt - 1), e, 1 - slot).start()

    idx_loc = jnp.transpose(jnp.concatenate([idx_ref[...], loc_ref[...]], axis=0))
    idx, loc = idx_loc[:, :TOP_K], idx_loc[:, TOP_K:]
    gate = jnp.transpose(jnp.concatenate([gate_ref[...], gate_ref[...]], axis=0))[:, :TOP_K]

    def gate_matrix(ncols, col_of):
        colio = lax.broadcasted_iota(I32, (tm, ncols), 1)
        g = jnp.zeros((tm, ncols), F32)
        for k in range(TOP_K):
            g = jnp.where(colio == col_of[:, k:k + 1], gate[:, k:k + 1], g)
        return g.astype(BF16)

    pltpu.make_async_copy(ys_hbm.at[pl.ds(0, N_EXPERTS * ROUTE_RP)], ybuf.at[0], sem.at[slot]).wait()
    col = jnp.where(loc < ROUTE_RP, idx * ROUTE_RP + loc, -1)
    acc_ref[...] = jnp.dot(gate_matrix(N_EXPERTS * ROUTE_RP, col), _unpack_rows(ybuf[slot]),
                           preferred_element_type=F32)

    def per_expert(e, carry):
        start, _, nchunks = geometry(i, e)

        def per_chunk(c, carry2):
            cp = pltpu.make_async_copy(
                ys_hbm.at[pl.ds(pl.multiple_of(start + c * ROUTE_RP, SUBLANES), ROUTE_RP)], obuf, osem)
            cp.start()
            cp.wait()
            ccol = jnp.where(idx == e, loc - c * ROUTE_RP, -1)
            acc_ref[...] += jnp.dot(gate_matrix(ROUTE_RP, ccol), _unpack_rows(obuf[...]),
                                    preferred_element_type=F32)
            return carry2
        return lax.fori_loop(1, nchunks, per_chunk, carry)

    @pl.when(mt_ref[i] > 0)
    def _():
        lax.fori_loop(0, N_EXPERTS, per_expert, 0)

    out = h_ref[...] + acc_ref[...]
    ms = jnp.mean(out * out, axis=-1, keepdims=True)
    o_ref[...] = out * lax.rsqrt(ms + EPS) * g_ref[...]

    @pl.when(i == nt - 1)
    def _():
        pltpu.make_async_copy(ys_hbm.at[pl.ds(0, N_EXPERTS * ROUTE_RP)], ybuf.at[0], sem.at[1 - slot]).wait()


def _combine(plan, h, ys, idx_t, loc_t, gate_t, g_final):
    t = h.shape[0]
    tm = ROUTE_TM
    tok = lambda i, *_: (i, 0)
    lane = lambda i, *_: (0, i)
    grid_spec = pltpu.PrefetchScalarGridSpec(
        num_scalar_prefetch=3,
        grid=(t // tm,),
        in_specs=[
            pl.BlockSpec((tm, D_MODEL), tok),
            pl.BlockSpec((TOP_K, tm), lane),
            pl.BlockSpec((TOP_K, tm), lane),
            pl.BlockSpec((TOP_K, tm), lane),
            pl.BlockSpec(memory_space=pl.ANY),
            pl.BlockSpec((1, D_MODEL), lambda i, *_: (0, 0)),
        ],
        out_specs=pl.BlockSpec((tm, D_MODEL), tok),
        scratch_shapes=[
            pltpu.VMEM((2, N_EXPERTS * ROUTE_RP, HALF), U32),
            pltpu.VMEM((ROUTE_RP, HALF), U32),
            pltpu.VMEM((tm, D_MODEL), F32),
            pltpu.SemaphoreType.DMA((2,)),
            pltpu.SemaphoreType.DMA(()),
        ],
    )
    return pl.pallas_call(
        _combine_kernel,
        grid_spec=grid_spec,
        out_shape=jax.ShapeDtypeStruct((t, D_MODEL), F32),
        compiler_params=pltpu.CompilerParams(
            dimension_semantics=("arbitrary",), vmem_limit_bytes=_vmem_limit(32 << 20)),
        name="combine",
    )(plan["tbase"], plan["cnt"], plan["multi"], h, idx_t, loc_t, gate_t, ys,
      g_final.reshape(1, D_MODEL))


def _moe(h, xn2, idx_t, gate_t, w_gate, b_gate, w_up, b_up, w_down, b_down, g_final):
    t = h.shape[0]
    plan = _route_plan(idx_t, t)
    xs, loc_t = _dispatch(plan, idx_t, xn2)
    ys = _experts(plan, xs, w_gate, b_gate, w_up, b_up, w_down, b_down)
    return _combine(plan, h, ys, idx_t, loc_t, gate_t, g_final)


def kernel(x, g_mix, w_in, b_in, conv_w, conv_b, ln_g, ln_b, rpb, w_out, b_out, g_ffn, w_router,
           b_router, w_gate, b_gate, w_up, b_up, w_down, b_down, g_final):
    bsz, seq, d = x.shape
    assert bsz == 1 and d == D_MODEL and g_mix.shape[0] == 1
    assert seq % (GRID_W * WIN_H) == 0
    x2 = x.reshape(seq, d)
    q, k, v, conv_out = _inproj_conv(x2, g_mix[0], w_in[0], b_in[0], conv_w[0], conv_b[0], ln_g[0], ln_b[0])
    na_out = _attention(q, k, v, rpb[0])
    h, xn2, idx_t, gate_t = _outproj_router(conv_out, na_out, x2, w_out[0], b_out[0], g_ffn[0],
                                            w_router[0], b_router[0])
    out = _moe(h, xn2, idx_t, gate_t, w_gate[0], b_gate[0], w_up[0], b_up[0], w_down[0], b_down[0],
               g_final)
    return out.reshape(bsz, seq, d)
```

```python
import functools

import numpy as np
import jax
import jax.numpy as jnp
from jax import lax
from jax.experimental import pallas as pl
from jax.experimental.pallas import tpu as pltpu

F32 = jnp.float32
BF16 = jnp.bfloat16
U32 = jnp.uint32
I32 = jnp.int32

D_MODEL = 1024
GRID_W = 64
D_CONV = 512
CONV_K = 31
NA_HEADS = 8
NA_HEAD_DIM = 64
D_NA = NA_HEADS * NA_HEAD_DIM
D_IN = 2 * D_CONV + 3 * D_NA
WIN_H = 8
WIN_W = 16
N_EXPERTS = 32
TOP_K = 4
D_FF = D_MODEL
SWIGLU_LIMIT = 7.0
SWIGLU_ALPHA = 1.702
EPS = 1e-5
MASK_VALUE = -1e30

V7X_VMEM_BYTES = 64 * 1024 * 1024
LANES = 128

PROJ_TM = 1024
OUTPROJ_TM = 1024
ROUTER_SUB = 512
CONV_HALO = 16
ATTN_RB = 32
ATTN_UNROLL = 16
ROUTE_TM = 256
SUBLANES = 8
ROUTE_RP = 64
COMBINE_SLOTS = 3
EXP_BM = 512
EXP_PARTS = (512, 256, 128, 0)
ZERO_ROWS = 256
HALF = D_MODEL // 2
BF16_BITS = 16
HIGH_HALF = 0xFFFF0000


def _vmem_limit(nbytes):
    return int(min(nbytes, V7X_VMEM_BYTES - 6 * 1024 * 1024))


def _pack_rows(x):
    lo = lax.bitcast_convert_type(x[:, :HALF], U32) >> BF16_BITS
    hi = lax.bitcast_convert_type(x[:, HALF:], U32) & jnp.uint32(HIGH_HALF)
    return lo | hi


def _unpack_rows(w):
    lo = lax.bitcast_convert_type(w << BF16_BITS, F32)
    hi = lax.bitcast_convert_type(w & jnp.uint32(HIGH_HALF), F32)
    return jnp.concatenate([lo, hi], axis=1).astype(BF16)


def _inproj_conv_kernel(x_ref, g_ref, w_ref, b_ref, cw_ref, cb_ref, lg_ref, lb_ref,
                        q_ref, k_ref, v_ref, o_ref, ext_ref, u_ref, sh_ref):
    i = pl.program_id(0)
    nt = pl.num_programs(0) - 1
    tm = x_ref.shape[0]

    @pl.when(i == 0)
    def _():
        ext_ref[...] = jnp.zeros(ext_ref.shape, F32)

    @pl.when(i < nt)
    def _():
        x = x_ref[...]
        ms = jnp.mean(x * x, axis=-1, keepdims=True)
        xn = (x * lax.rsqrt(ms + EPS) * g_ref[...]).astype(BF16)

        def proj(c):
            sl = slice(c * D_CONV, (c + 1) * D_CONV)
            return jnp.dot(xn, w_ref[:, sl], preferred_element_type=F32) + b_ref[:, sl]

        u_ref[...] = proj(0) * jax.nn.sigmoid(proj(1))
        q_ref[...] = (proj(2) * (NA_HEAD_DIM ** -0.5)).astype(BF16)
        k_ref[...] = proj(3).astype(BF16)
        v_ref[...] = proj(4).astype(BF16)

    ext_ref[CONV_HALO + tm:, :] = jnp.where(i < nt, u_ref[0:CONV_HALO, :], 0.0)

    @pl.when(i >= 1)
    def _():
        base = CONV_HALO - CONV_K // 2
        span = (CONV_K - 1) // SUBLANES * SUBLANES
        for b in range(SUBLANES):
            sh_ref[b] = ext_ref[base + b:base + b + tm + span, :]
        acc = jnp.zeros((tm, D_CONV), F32)
        for b in range(SUBLANES):
            for a in range((CONV_K - 1 - b) // SUBLANES + 1):
                r = SUBLANES * a
                acc = acc + sh_ref[b, r:r + tm, :] * cw_ref[r + b:r + b + 1, :]
        acc = acc + cb_ref[...]
        mu = jnp.mean(acc, axis=-1, keepdims=True)
        d = acc - mu
        var = jnp.mean(d * d, axis=-1, keepdims=True)
        un = d * lax.rsqrt(var + EPS) * lg_ref[...] + lb_ref[...]
        o_ref[...] = (un * jax.nn.sigmoid(un)).astype(o_ref.dtype)

    ext_ref[0:CONV_HALO, :] = ext_ref[tm:tm + CONV_HALO, :]
    ext_ref[CONV_HALO:CONV_HALO + tm, :] = u_ref[...]


def _inproj_conv(x2, g_mix, w_in, b_in, conv_w, conv_b, ln_g, ln_b):
    t = x2.shape[0]
    tm = min(PROJ_TM, t)
    nt = t // tm
    tok = lambda i: (jnp.minimum(i, nt - 1), 0)
    prev = lambda i: (jnp.maximum(i - 1, 0), 0)
    const = lambda i: (0, 0)
    w = jnp.zeros((CONV_K + 1, D_CONV), F32).at[:CONV_K].set(conv_w)
    span = (CONV_K - 1) // SUBLANES * SUBLANES
    return pl.pallas_call(
        _inproj_conv_kernel,
        grid=(nt + 1,),
        in_specs=[
            pl.BlockSpec((tm, D_MODEL), tok),
            pl.BlockSpec((1, D_MODEL), const),
            pl.BlockSpec((D_MODEL, D_IN), const),
            pl.BlockSpec((1, D_IN), const),
            pl.BlockSpec((CONV_K + 1, D_CONV), const),
            pl.BlockSpec((1, D_CONV), const),
            pl.BlockSpec((1, D_CONV), const),
            pl.BlockSpec((1, D_CONV), const),
        ],
        out_specs=[
            pl.BlockSpec((tm, D_NA), tok),
            pl.BlockSpec((tm, D_NA), tok),
            pl.BlockSpec((tm, D_NA), tok),
            pl.BlockSpec((tm, D_CONV), prev),
        ],
        out_shape=[
            jax.ShapeDtypeStruct((t, D_NA), BF16),
            jax.ShapeDtypeStruct((t, D_NA), BF16),
            jax.ShapeDtypeStruct((t, D_NA), BF16),
            jax.ShapeDtypeStruct((t, D_CONV), BF16),
        ],
        scratch_shapes=[pltpu.VMEM((tm + 2 * CONV_HALO, D_CONV), F32),
                        pltpu.VMEM((tm, D_CONV), F32),
                        pltpu.VMEM((SUBLANES, tm + span, D_CONV), F32)],
        compiler_params=pltpu.CompilerParams(
            dimension_semantics=("arbitrary",), vmem_limit_bytes=_vmem_limit(56 << 20)),
        name="inproj_conv",
    )(x2, g_mix.reshape(1, D_MODEL), w_in.astype(BF16), b_in.reshape(1, D_IN), w,
      conv_b.reshape(1, D_CONV), ln_g.reshape(1, D_CONV), ln_b.reshape(1, D_CONV))


def _column_bias(rpb):
    col = np.arange(GRID_W)
    c0 = np.clip(col - WIN_W // 2, 0, GRID_W - WIN_W)
    valid = (col[None, :] >= c0[:, None]) & (col[None, :] < c0[:, None] + WIN_W)
    dcol = np.clip(col[None, :] - col[:, None], -(WIN_W - 1), WIN_W - 1) + (WIN_W - 1)
    col_sel = ((dcol[:, :, None] == np.arange(2 * WIN_W - 1)) & valid[:, :, None]).astype(np.float32)
    b = jnp.einsum("hdj,qkj->hdqk", rpb.astype(F32), col_sel, precision=lax.Precision.HIGHEST)
    b = jnp.where(valid[None, None], b, MASK_VALUE)
    return jnp.concatenate([b, b], axis=-1)


def _attn_kernel(q_ref, k_ref, v_ref, cb_ref, o_ref, s_ref, bt_ref, *, rb, rows):
    jb = pl.program_id(1)
    lane = lax.broadcasted_iota(jnp.int32, (GRID_W, LANES), 1)
    first_head = lane < NA_HEAD_DIM
    nkeys = WIN_H * GRID_W

    @pl.when(jb == 0)
    def _():
        for h in range(2):
            for c in range(WIN_H):
                for rr in range(0, WIN_H, 2):
                    even = cb_ref[h, rr - c + WIN_H - 1]
                    odd = cb_ref[h, rr + 1 - c + WIN_H - 1]
                    bt_ref[h, c, :, rr * GRID_W:(rr + 2) * GRID_W] = jnp.where(first_head, even, odd)

    def window(r):
        i = jb * rb + r
        r0 = jnp.clip(i - WIN_H // 2, 0, rows - WIN_H)
        return i - r0, pl.multiple_of(r0 * GRID_W, GRID_W)

    def scores(r):
        _, start = window(r)
        q2 = q_ref[pl.ds(pl.multiple_of(r * GRID_W, GRID_W), GRID_W), :]
        zero = jnp.zeros_like(q2)
        qst = jnp.concatenate([jnp.where(first_head, q2, zero), jnp.where(first_head, zero, q2)], axis=0)
        ks = k_ref[pl.ds(start, nkeys), :]
        return lax.dot_general(qst, ks, (((1,), (1,)), ((), ())), preferred_element_type=F32)

    def finish(r, s):
        c, start = window(r)
        vs = v_ref[pl.ds(start, nkeys), :]
        bias = jnp.concatenate([bt_ref[0, c], bt_ref[1, c]], axis=0)
        s = jnp.where(bias > 0.5 * MASK_VALUE, s + bias, MASK_VALUE)
        m = jnp.max(s, axis=-1, keepdims=True)
        p = jnp.exp(s - m)
        l = jnp.sum(p, axis=-1, keepdims=True)
        o = jnp.dot(p.astype(BF16), vs, preferred_element_type=F32) / l
        o2 = jnp.where(first_head, o[:GRID_W], o[GRID_W:])
        o_ref[pl.ds(pl.multiple_of(r * GRID_W, GRID_W), GRID_W), :] = o2.astype(o_ref.dtype)

    s_ref[0] = scores(0)

    def pair(p, carry):
        r = 2 * p
        s_ref[1] = scores(r + 1)
        finish(r, s_ref[0])
        s_ref[0] = scores(jnp.minimum(r + 2, rb - 1))
        finish(r + 1, s_ref[1])
        return carry

    lax.fori_loop(0, rb // 2, pair, 0, unroll=ATTN_UNROLL)


def _attention(q, k, v, rpb):
    t = q.shape[0]
    rows = t // GRID_W
    rb = min(ATTN_RB, rows)
    npairs = D_NA // LANES
    cb = _column_bias(rpb)
    kern = functools.partial(_attn_kernel, rb=rb, rows=rows)
    return pl.pallas_call(
        kern,
        grid=(npairs, rows // rb),
        in_specs=[
            pl.BlockSpec((rb * GRID_W, LANES), lambda p, j: (j, p)),
            pl.BlockSpec((t, LANES), lambda p, j: (0, p)),
            pl.BlockSpec((t, LANES), lambda p, j: (0, p)),
            pl.BlockSpec((2, 2 * WIN_H - 1, GRID_W, 2 * GRID_W), lambda p, j: (p, 0, 0, 0)),
        ],
        out_specs=pl.BlockSpec((rb * GRID_W, LANES), lambda p, j: (j, p)),
        out_shape=jax.ShapeDtypeStruct((t, D_NA), BF16),
        scratch_shapes=[pltpu.VMEM((2, 2 * GRID_W, WIN_H * GRID_W), F32),
                        pltpu.VMEM((2, WIN_H, GRID_W, WIN_H * GRID_W), F32)],
        compiler_params=pltpu.CompilerParams(
            dimension_semantics=("arbitrary", "arbitrary"), vmem_limit_bytes=_vmem_limit(40 << 20)),
        name="attention",
    )(q, k, v, cb)


def _outproj_kernel(conv_ref, na_ref, x_ref, wo_ref, bo_ref, g_ref, wr_ref, br_ref,
                    h_ref, xn_ref, idx_ref, gate_ref):
    for s in range(x_ref.shape[0] // ROUTER_SUB):
        rows = slice(s * ROUTER_SUB, (s + 1) * ROUTER_SUB)
        mixed = jnp.concatenate([conv_ref[rows, :], na_ref[rows, :]], axis=-1)
        h = x_ref[rows, :] + jnp.dot(mixed, wo_ref[...], preferred_element_type=F32) + bo_ref[...]
        h_ref[rows, :] = h
        ms = jnp.mean(h * h, axis=-1, keepdims=True)
        xn = (h * lax.rsqrt(ms + EPS) * g_ref[...]).astype(BF16)
        xn_ref[rows, :] = xn
        logits = jnp.transpose(jnp.dot(xn, wr_ref[...], preferred_element_type=F32) + br_ref[...])[:N_EXPERTS]
        ids = lax.broadcasted_iota(jnp.int32, logits.shape, 0)
        vals, sels = [], []
        l = logits
        for _ in range(TOP_K):
            m = jnp.max(l, axis=0, keepdims=True)
            sel = jnp.min(jnp.where(l == m, ids, N_EXPERTS), axis=0, keepdims=True)
            vals.append(m)
            sels.append(sel)
            l = jnp.where(ids == sel, -jnp.inf, l)
        es = [jnp.exp(vk - vals[0]) for vk in vals]
        tot = es[0] + es[1] + es[2] + es[3]
        idx_ref[:, rows] = jnp.concatenate(sels, axis=0)
        gate_ref[:, rows] = jnp.concatenate([e / tot for e in es], axis=0)


def _outproj_router(conv_out, na_out, x2, w_out, b_out, g_ffn, w_router, b_router):
    t = x2.shape[0]
    tm = min(OUTPROJ_TM, t)
    tok = lambda i: (i, 0)
    const = lambda i: (0, 0)
    return pl.pallas_call(
        _outproj_kernel,
        grid=(t // tm,),
        in_specs=[
            pl.BlockSpec((tm, D_CONV), tok),
            pl.BlockSpec((tm, D_NA), tok),
            pl.BlockSpec((tm, D_MODEL), tok),
            pl.BlockSpec((D_MODEL, D_MODEL), const),
            pl.BlockSpec((1, D_MODEL), const),
            pl.BlockSpec((1, D_MODEL), const),
            pl.BlockSpec((D_MODEL, LANES), const),
            pl.BlockSpec((1, LANES), const),
        ],
        out_specs=[
            pl.BlockSpec((tm, D_MODEL), tok),
            pl.BlockSpec((tm, D_MODEL), tok),
            pl.BlockSpec((TOP_K, tm), lambda i: (0, i)),
            pl.BlockSpec((TOP_K, tm), lambda i: (0, i)),
        ],
        out_shape=[
            jax.ShapeDtypeStruct((t, D_MODEL), F32),
            jax.ShapeDtypeStruct((t, D_MODEL), BF16),
            jax.ShapeDtypeStruct((TOP_K, t), jnp.int32),
            jax.ShapeDtypeStruct((TOP_K, t), F32),
        ],
        compiler_params=pltpu.CompilerParams(
            dimension_semantics=("arbitrary",), vmem_limit_bytes=_vmem_limit(48 << 20)),
        name="outproj_router",
    )(conv_out, na_out, x2, w_out.astype(BF16), b_out.reshape(1, D_MODEL), g_ffn.reshape(1, D_MODEL),
      jnp.pad(w_router.astype(BF16), ((0, 0), (0, LANES - N_EXPERTS))),
      jnp.pad(b_router.reshape(1, N_EXPERTS), ((0, 0), (0, LANES - N_EXPERTS))))


def _route_plan(idx_t, t):
    nt = t // ROUTE_TM
    experts = jnp.arange(N_EXPERTS, dtype=I32)
    onehot = idx_t.reshape(TOP_K, nt, ROUTE_TM, 1) == experts
    cnt = jnp.sum(onehot, axis=(0, 2), dtype=I32)
    sizes = jnp.sum(cnt, axis=0)
    padded = (sizes + ROUTE_RP + EXP_BM - 1) // EXP_BM * EXP_BM
    pad_ends = jnp.cumsum(padded)
    pad_off = pad_ends - padded
    tbase = pad_off[None, :] + jnp.cumsum(cnt, axis=0) - cnt
    cap = _sorted_rows(t)
    nb = cap // EXP_BM
    n_used = pad_ends[N_EXPERTS - 1] // EXP_BM
    blk_src = jnp.minimum(jnp.arange(nb, dtype=I32), n_used - 1)
    blk_exp = jnp.minimum(
        jnp.sum(blk_src[:, None] * EXP_BM >= pad_ends[None, :], axis=1, dtype=I32), N_EXPERTS - 1)
    row_end = jnp.sum(jnp.where(blk_exp[:, None] == experts[None, :], (pad_off + sizes)[None, :], 0), axis=1)
    blk_live = jnp.clip(row_end - blk_src * EXP_BM, 0, EXP_BM).astype(I32)
    multi = jnp.any(tbase % SUBLANES + cnt > ROUTE_RP, axis=1).astype(I32)
    zero_from = jnp.concatenate([(pad_off + sizes) // ZERO_ROWS * ZERO_ROWS, pad_ends[N_EXPERTS - 1:]])
    zero_to = jnp.concatenate([pad_ends, jnp.full((1,), cap, I32)])
    return dict(cnt=cnt.reshape(-1), tbase=tbase.reshape(-1).astype(I32), multi=multi,
                zero_start=zero_from.astype(I32), zero_cnt=((zero_to - zero_from) // ZERO_ROWS).astype(I32),
                blk_src=blk_src, blk_exp=blk_exp, blk_live=blk_live, n_used=n_used.reshape(1).astype(I32))


def _sorted_rows(t):
    cap = t * TOP_K + N_EXPERTS * (EXP_BM + ROUTE_RP)
    return (cap + EXP_BM - 1) // EXP_BM * EXP_BM


def _chunk_geometry(tb, n):
    head = tb & (SUBLANES - 1)
    start = pl.multiple_of(tb - head, SUBLANES)
    nchunks = (head + n + ROUTE_RP - 1) // ROUTE_RP
    return start, head, nchunks


def _dispatch_kernel(tb_ref, cnt_ref, mt_ref, zs_ref, zc_ref, idx_ref, tbv_ref, xn_ref, xs_hbm, loc_ref,
                     stage, ostage, zbuf, head_ref, cs_ref, ms_ref, sem, osem, zsem):
    i = pl.program_id(0)
    nt = pl.num_programs(0)
    slot = i % 2
    tm = idx_ref.shape[1]
    groups = ROUTE_RP // SUBLANES

    def geometry(e, step=None):
        step = i if step is None else step
        return _chunk_geometry(tb_ref[step * N_EXPERTS + e], cnt_ref[step * N_EXPERTS + e])

    def chunk_copy(step, e):
        start, _, _ = geometry(e, step)
        return pltpu.make_async_copy(stage.at[step % 2, pl.ds(e * ROUTE_RP, ROUTE_RP)],
                                     xs_hbm.at[pl.ds(start, ROUTE_RP)], sem.at[step % 2])

    def all_chunks(step, phase):
        if phase == "wait":
            pltpu.make_async_copy(stage.at[0], xs_hbm.at[pl.ds(0, N_EXPERTS * ROUTE_RP)], sem.at[step % 2]).wait()
            return
        for e in range(N_EXPERTS):
            chunk_copy(step, e).start()

    @pl.when(i >= 2)
    def _():
        all_chunks(i - 2, "wait")

    @pl.when(i >= 1)
    def _():
        all_chunks(i - 1, "start")

    def zero_copy(start):
        return pltpu.make_async_copy(zbuf, xs_hbm.at[pl.ds(pl.multiple_of(start, ZERO_ROWS), ZERO_ROWS)], zsem)

    @pl.when(i == 0)
    def _():
        zbuf[...] = jnp.zeros(zbuf.shape, U32)
        head_ref[...] = jnp.zeros(head_ref.shape, U32)
        for phase in ("start", "wait"):
            for j in range(N_EXPERTS + 1):
                def piece(q, carry, j=j, phase=phase):
                    getattr(zero_copy(zs_ref[j] + q * ZERO_ROWS), phase)()
                    return carry
                lax.fori_loop(0, zc_ref[j], piece, 0)

    idx = idx_ref[...]
    eio = lax.broadcasted_iota(I32, (N_EXPERTS, tm), 0)
    member = jnp.zeros((N_EXPERTS, tm), F32)
    for k in range(TOP_K):
        member = member + (idx[k:k + 1, :] == eio).astype(F32)
    tri = (lax.broadcasted_iota(I32, (tm, tm), 0) < lax.broadcasted_iota(I32, (tm, tm), 1)).astype(BF16)
    rank = jnp.dot(member.astype(BF16), tri, preferred_element_type=F32)
    pos = rank + tbv_ref[0][:, 0:1].astype(F32)
    loc_ref[...] = jnp.concatenate(
        [jnp.sum(jnp.where(idx[k:k + 1, :] == eio, pos, 0.0), axis=0, keepdims=True) for k in range(TOP_K)],
        axis=0).astype(I32)
    cs_ref[...] = pos
    ms_ref[...] = member

    jio = lax.broadcasted_iota(I32, (ROUTE_RP, tm), 0).astype(F32)
    sel = [jnp.where((jio == pos[e:e + 1, :]) & (member[e:e + 1, :] > 0.0), 1.0, 0.0).astype(BF16)
           for e in range(N_EXPERTS)]
    rows = jnp.dot(jnp.concatenate(sel, axis=0), xn_ref[...], preferred_element_type=F32)

    stage[slot] = _pack_rows(rows)
    sub = lax.broadcasted_iota(I32, (SUBLANES, HALF), 0)
    for e in range(N_EXPERTS):
        _, head, _ = geometry(e)
        first = pl.ds(e * ROUTE_RP, SUBLANES)
        stage[slot, first, :] = jnp.where(sub < head, head_ref[e], stage[slot, first, :])
        g = jnp.minimum(lax.shift_right_logical(head + cnt_ref[i * N_EXPERTS + e], SUBLANES.bit_length() - 1),
                        groups - 1)
        head_ref[e] = stage[slot, pl.ds(pl.multiple_of(e * ROUTE_RP + g * SUBLANES, SUBLANES), SUBLANES), :]

    def per_expert(e, carry):
        start, head, nchunks = geometry(e)
        end = head + cnt_ref[i * N_EXPERTS + e]

        def per_chunk(c, carry2):
            want = jio + (c * ROUTE_RP).astype(F32)
            pick = jnp.where((want == cs_ref[pl.ds(e, 1), :]) & (ms_ref[pl.ds(e, 1), :] > 0.0), 1.0, 0.0)
            ostage[...] = _pack_rows(jnp.dot(pick.astype(BF16), xn_ref[...], preferred_element_type=F32))
            cp = pltpu.make_async_copy(
                ostage, xs_hbm.at[pl.ds(pl.multiple_of(start + c * ROUTE_RP, SUBLANES), ROUTE_RP)], osem)
            cp.start()
            cp.wait()

            @pl.when(end // ROUTE_RP == c)
            def _():
                g = (end - c * ROUTE_RP) // SUBLANES
                head_ref[e] = ostage[pl.ds(pl.multiple_of(g * SUBLANES, SUBLANES), SUBLANES), :]
            return carry2
        return lax.fori_loop(1, nchunks, per_chunk, carry)

    @pl.when(mt_ref[i] > 0)
    def _():
        lax.fori_loop(0, N_EXPERTS, per_expert, 0)

    @pl.when(i == nt - 1)
    def _():
        @pl.when(i >= 1)
        def _():
            all_chunks(i - 1, "wait")
        all_chunks(i, "start")
        all_chunks(i, "wait")


def _dispatch(plan, idx_t, xn2):
    t = xn2.shape[0]
    tm = ROUTE_TM
    cap = _sorted_rows(t)
    grid_spec = pltpu.PrefetchScalarGridSpec(
        num_scalar_prefetch=5,
        grid=(t // tm,),
        in_specs=[
            pl.BlockSpec((TOP_K, tm), lambda i, *_: (0, i)),
            pl.BlockSpec((1, N_EXPERTS, LANES), lambda i, *_: (i, 0, 0)),
            pl.BlockSpec((tm, D_MODEL), lambda i, *_: (i, 0)),
        ],
        out_specs=[
            pl.BlockSpec(memory_space=pl.ANY),
            pl.BlockSpec((TOP_K, tm), lambda i, *_: (0, i)),
        ],
        scratch_shapes=[
            pltpu.VMEM((2, N_EXPERTS * ROUTE_RP, HALF), U32),
            pltpu.VMEM((ROUTE_RP, HALF), U32),
            pltpu.VMEM((ZERO_ROWS, HALF), U32),
            pltpu.VMEM((N_EXPERTS, SUBLANES, HALF), U32),
            pltpu.VMEM((N_EXPERTS, tm), F32),
            pltpu.VMEM((N_EXPERTS, tm), F32),
            pltpu.SemaphoreType.DMA((2,)),
            pltpu.SemaphoreType.DMA(()),
            pltpu.SemaphoreType.DMA(()),
        ],
    )
    return pl.pallas_call(
        _dispatch_kernel,
        grid_spec=grid_spec,
        out_shape=[
            jax.ShapeDtypeStruct((cap, HALF), U32),
            jax.ShapeDtypeStruct((TOP_K, t), I32),
        ],
        compiler_params=pltpu.CompilerParams(
            dimension_semantics=("arbitrary",), vmem_limit_bytes=_vmem_limit(32 << 20)),
        name="dispatch",
    )(plan["tbase"], plan["cnt"], plan["multi"], plan["zero_start"], plan["zero_cnt"], idx_t,
      jnp.broadcast_to((plan["tbase"] % SUBLANES).reshape(t // tm, N_EXPERTS, 1), (t // tm, N_EXPERTS, LANES)),
      xn2)


def _expert_kernel(be_ref, src_ref, nu_ref, live_ref, xs_ref, wg_hbm, wu_hbm, wd_hbm, bg_ref, bu_ref, bd_ref,
                   ys_ref, wf32, wbf, wsem):
    b = pl.program_id(0)
    e = be_ref[b]

    def weight_copies(expert, par):
        return [pltpu.make_async_copy(w.at[expert], wf32.at[par, m], wsem.at[par])
                for m, w in enumerate((wg_hbm, wu_hbm, wd_hbm))]

    @pl.when(b == 0)
    def _():
        for cp in weight_copies(e, e % 2):
            cp.start()

    @pl.when(b < nu_ref[0])
    def _():
        @pl.when((b == 0) | (e != be_ref[jnp.maximum(b - 1, 0)]))
        def _():
            par = e % 2
            for cp in weight_copies(e, par):
                cp.wait()

            @pl.when(e + 1 < N_EXPERTS)
            def _():
                for cp in weight_copies(e + 1, 1 - par):
                    cp.start()

            for m in range(3):
                wbf[m] = wf32[par, m].astype(BF16)

        def ffn(rows):
            x = _unpack_rows(xs_ref[rows, :])
            gt = jnp.minimum(jnp.dot(x, wbf[0], preferred_element_type=F32) + bg_ref[0], SWIGLU_LIMIT)
            up = jnp.clip(jnp.dot(x, wbf[1], preferred_element_type=F32) + bu_ref[0],
                          -SWIGLU_LIMIT, SWIGLU_LIMIT)
            hdn = (up + 1.0) * (gt * jax.nn.sigmoid(SWIGLU_ALPHA * gt))
            y = jnp.dot(hdn.astype(BF16), wbf[2], preferred_element_type=F32) + bd_ref[0]
            ys_ref[rows, :] = _pack_rows(y.astype(BF16).astype(F32))

        live = live_ref[b]
        for p, part in enumerate(EXP_PARTS):
            smaller = EXP_PARTS[p + 1] if p + 1 < len(EXP_PARTS) else -1
            cond = live > smaller
            if p > 0:
                cond = cond & (live <= part)

            @pl.when(cond)
            def _(part=part):
                if part > 0:
                    ffn(slice(0, part))
                if part < EXP_BM:
                    ys_ref[part:, :] = jnp.zeros((EXP_BM - part, HALF), U32)

    @pl.when(b >= nu_ref[0])
    def _():
        ys_ref[...] = jnp.zeros(ys_ref.shape, U32)


def _experts(plan, xs, w_gate, b_gate, w_up, b_up, w_down, b_down):
    cap = xs.shape[0]
    nb = cap // EXP_BM
    rows = lambda b, be, src, nu, live: (src[b], 0)
    wsel = lambda b, be, src, nu, live: (be[b], 0, 0)
    wspec = pl.BlockSpec(memory_space=pl.ANY)
    bspec = pl.BlockSpec((1, 1, D_FF), wsel)
    grid_spec = pltpu.PrefetchScalarGridSpec(
        num_scalar_prefetch=4,
        grid=(nb,),
        in_specs=[pl.BlockSpec((EXP_BM, HALF), rows), wspec, wspec, wspec, bspec, bspec, bspec],
        out_specs=pl.BlockSpec((EXP_BM, HALF), lambda b, be, src, nu, live: (b, 0)),
        scratch_shapes=[pltpu.VMEM((2, 3, D_MODEL, D_FF), F32),
                        pltpu.VMEM((3, D_MODEL, D_FF), BF16),
                        pltpu.SemaphoreType.DMA((2,))],
    )
    return pl.pallas_call(
        _expert_kernel,
        grid_spec=grid_spec,
        out_shape=jax.ShapeDtypeStruct((cap, HALF), U32),
        compiler_params=pltpu.CompilerParams(
            dimension_semantics=("arbitrary",), vmem_limit_bytes=_vmem_limit(56 << 20)),
        name="experts",
    )(plan["blk_exp"], plan["blk_src"], plan["n_used"], plan["blk_live"], xs, w_gate, w_up, w_down,
      b_gate.reshape(N_EXPERTS, 1, D_FF), b_up.reshape(N_EXPERTS, 1, D_FF),
      b_down.reshape(N_EXPERTS, 1, D_MODEL))


def _combine_kernel(tb_ref, cnt_ref, mt_ref, h_ref, idx_ref, loc_ref, gate_ref, ys_hbm, g_ref, o_ref,
                    ybuf, obuf, acc_ref, sem, osem):
    i = pl.program_id(0)
    nt = pl.num_programs(0)
    slot = i % COMBINE_SLOTS
    tm = h_ref.shape[0]

    def geometry(step, e):
        return _chunk_geometry(tb_ref[step * N_EXPERTS + e], cnt_ref[step * N_EXPERTS + e])

    def tile_wait(s):
        pltpu.make_async_copy(ys_hbm.at[pl.ds(0, N_EXPERTS * ROUTE_RP)], ybuf.at[0], sem.at[s]).wait()

    def chunk_copy(step, e, s):
        start, _, _ = geometry(step, e)
        return pltpu.make_async_copy(ys_hbm.at[pl.ds(start, ROUTE_RP)],
                                     ybuf.at[s, pl.ds(e * ROUTE_RP, ROUTE_RP)], sem.at[s])

    ahead = COMBINE_SLOTS - 1

    @pl.when(i == 0)
    def _():
        for a in range(ahead):
            for e in range(N_EXPERTS):
                chunk_copy(jnp.minimum(a, nt - 1), e, a).start()

    for e in range(N_EXPERTS):
        chunk_copy(jnp.minimum(i + ahead, nt - 1), e, (i + ahead) % COMBINE_SLOTS).start()

    idx_loc = jnp.transpose(jnp.concatenate([idx_ref[...], loc_ref[...]], axis=0))
    idx, loc = idx_loc[:, :TOP_K], idx_loc[:, TOP_K:]
    gate = jnp.transpose(jnp.concatenate([gate_ref[...], gate_ref[...]], axis=0))[:, :TOP_K]

    def gate_matrix(ncols, col_of):
        colio = lax.broadcasted_iota(I32, (tm, ncols), 1)
        g = jnp.zeros((tm, ncols), F32)
        for k in range(TOP_K):
            g = jnp.where(colio == col_of[:, k:k + 1], gate[:, k:k + 1], g)
        return g.astype(BF16)

    tile_wait(slot)
    col = jnp.where(loc < ROUTE_RP, idx * ROUTE_RP + loc, -1)
    acc_ref[...] = jnp.dot(gate_matrix(N_EXPERTS * ROUTE_RP, col), _unpack_rows(ybuf[slot]),
                           preferred_element_type=F32)

    def per_expert(e, carry):
        start, _, nchunks = geometry(i, e)

        def per_chunk(c, carry2):
            cp = pltpu.make_async_copy(
                ys_hbm.at[pl.ds(pl.multiple_of(start + c * ROUTE_RP, SUBLANES), ROUTE_RP)], obuf, osem)
            cp.start()
            cp.wait()
            ccol = jnp.where(idx == e, loc - c * ROUTE_RP, -1)
            acc_ref[...] += jnp.dot(gate_matrix(ROUTE_RP, ccol), _unpack_rows(obuf[...]),
                                    preferred_element_type=F32)
            return carry2
        return lax.fori_loop(1, nchunks, per_chunk, carry)

    @pl.when(mt_ref[i] > 0)
    def _():
        lax.fori_loop(0, N_EXPERTS, per_expert, 0)

    out = h_ref[...] + acc_ref[...]
    ms = jnp.mean(out * out, axis=-1, keepdims=True)
    o_ref[...] = out * lax.rsqrt(ms + EPS) * g_ref[...]

    @pl.when(i == nt - 1)
    def _():
        for a in range(1, COMBINE_SLOTS):
            tile_wait((i + a) % COMBINE_SLOTS)


def _combine(plan, h, ys, idx_t, loc_t, gate_t, g_final):
    t = h.shape[0]
    tm = ROUTE_TM
    tok = lambda i, *_: (i, 0)
    lane = lambda i, *_: (0, i)
    grid_spec = pltpu.PrefetchScalarGridSpec(
        num_scalar_prefetch=3,
        grid=(t // tm,),
        in_specs=[
            pl.BlockSpec((tm, D_MODEL), tok),
            pl.BlockSpec((TOP_K, tm), lane),
            pl.BlockSpec((TOP_K, tm), lane),
            pl.BlockSpec((TOP_K, tm), lane),
            pl.BlockSpec(memory_space=pl.ANY),
            pl.BlockSpec((1, D_MODEL), lambda i, *_: (0, 0)),
        ],
        out_specs=pl.BlockSpec((tm, D_MODEL), tok),
        scratch_shapes=[
            pltpu.VMEM((COMBINE_SLOTS, N_EXPERTS * ROUTE_RP, HALF), U32),
            pltpu.VMEM((ROUTE_RP, HALF), U32),
            pltpu.VMEM((tm, D_MODEL), F32),
            pltpu.SemaphoreType.DMA((COMBINE_SLOTS,)),
            pltpu.SemaphoreType.DMA(()),
        ],
    )
    return pl.pallas_call(
        _combine_kernel,
        grid_spec=grid_spec,
        out_shape=jax.ShapeDtypeStruct((t, D_MODEL), F32),
        compiler_params=pltpu.CompilerParams(
            dimension_semantics=("arbitrary",), vmem_limit_bytes=_vmem_limit(32 << 20)),
        name="combine",
    )(plan["tbase"], plan["cnt"], plan["multi"], h, idx_t, loc_t, gate_t, ys,
      g_final.reshape(1, D_MODEL))


def _moe(h, xn2, idx_t, gate_t, w_gate, b_gate, w_up, b_up, w_down, b_down, g_final):
    t = h.shape[0]
    plan = _route_plan(idx_t, t)
    xs, loc_t = _dispatch(plan, idx_t, xn2)
    ys = _experts(plan, xs, w_gate, b_gate, w_up, b_up, w_down, b_down)
    return _combine(plan, h, ys, idx_t, loc_t, gate_t, g_final)


def kernel(x, g_mix, w_in, b_in, conv_w, conv_b, ln_g, ln_b, rpb, w_out, b_out, g_ffn, w_router,
           b_router, w_gate, b_gate, w_up, b_up, w_down, b_down, g_final):
    bsz, seq, d = x.shape
    assert bsz == 1 and d == D_MODEL and g_mix.shape[0] == 1
    assert seq % (GRID_W * WIN_H) == 0
    x2 = x.reshape(seq, d)
    q, k, v, conv_out = _inproj_conv(x2, g_mix[0], w_in[0], b_in[0], conv_w[0], conv_b[0], ln_g[0], ln_b[0])
    na_out = _attention(q, k, v, rpb[0])
    h, xn2, idx_t, gate_t = _outproj_router(conv_out, na_out, x2, w_out[0], b_out[0], g_ffn[0],
                                            w_router[0], b_router[0])
    out = _moe(h, xn2, idx_t, gate_t, w_gate[0], b_gate[0], w_up[0], b_up[0], w_down[0], b_down[0],
               g_final)
    return out.reshape(bsz, seq, d)
```
